```python
import jax, jax.numpy as jnp
from jax import lax
import numpy as np

D_MODEL = 1024
BATCH = 16
SEQ = 2048
DEPTH = 1
DEC_BATCH = 128
DEC_SEQ = 1
PAST_LEN = 16384
PAGE_SIZE = 128

BRANCH_W = D_MODEL // 2
SWA_HEAD_DIM = 64
SWA_HEADS = BRANCH_W // SWA_HEAD_DIM
SWA_KV_HEADS = 2
WINDOW = 128
SWA_BLOCK = 128
GDN_HEAD_DIM = 128
GDN_HEADS = BRANCH_W // GDN_HEAD_DIM
CONV_WIDTH = 4
GDN_CHUNK = 64
MEM_LEN = 256
MEM_HEADS = 4
MEM_HEAD_DIM = BRANCH_W // MEM_HEADS
N_BRANCH = 3
N_EXPERTS = 32
TOP_K = 4
D_FF = D_MODEL
SWIGLU_LIMIT = 7.0
SWIGLU_ALPHA = 1.702
MOE_BLOCK = 128
EPS = 1e-6

SWA_Q_W = SWA_HEADS * SWA_HEAD_DIM
SWA_KV_W = SWA_KV_HEADS * SWA_HEAD_DIM
GDN_W = GDN_HEADS * GDN_HEAD_DIM
GDN_CONV_W = 3 * GDN_W
MEM_W = MEM_HEADS * MEM_HEAD_DIM
GATE_W = N_BRANCH * D_MODEL
IN_SPLITS = (SWA_Q_W, SWA_KV_W, SWA_KV_W, GDN_W, GDN_W, GDN_W, GDN_W, GDN_HEADS, GDN_HEADS, MEM_W, GATE_W)
IN_WIDTH = sum(IN_SPLITS)
IN_OFFSETS = tuple(int(o) for o in np.cumsum(IN_SPLITS)[:-1])

kernel_name = 'hybrid_swa_gdn_mem_moe_step'

F32 = jnp.float32


def rms_norm(x, gain):
    xf = x.astype(F32)
    y = xf * lax.rsqrt(jnp.mean(xf * xf, axis=-1, keepdims=True) + EPS) * gain.astype(F32)
    return y.astype(x.dtype)


def l2_norm(x):
    xf = x.astype(F32)
    return xf * lax.rsqrt(jnp.sum(xf * xf, axis=-1, keepdims=True) + EPS)


def alibi_slopes(n):
    return jnp.asarray(2.0 ** (-8.0 * np.arange(1, n + 1) / n), F32)


def sliding_window_attention(q, k_all, v_all, pos0, sinks):
    B, L, H, Dh = q.shape
    KVH = k_all.shape[2]
    G = H // KVH
    QB = SWA_BLOCK if L % SWA_BLOCK == 0 else L
    NB = L // QB
    KS = QB + WINDOW
    idx = np.arange(NB)[:, None] * QB + np.arange(KS)[None, :]
    kb = k_all[:, idx]
    vb = v_all[:, idx]
    qb = q.reshape(B, NB, QB, KVH, G, Dh)
    q_pos = pos0 + np.arange(L).reshape(NB, QB)
    k_pos = pos0 - WINDOW + idx
    dist = q_pos[:, :, None] - k_pos[:, None, :]
    valid = (dist >= 0) & (dist <= WINDOW) & (k_pos[:, None, :] >= 0)
    slopes = alibi_slopes(H).reshape(KVH, G)
    s = jnp.einsum('bnqkgd,bnskd->bnkgqs', qb, kb).astype(F32) * (Dh ** -0.5)
    s = s - slopes[None, None, :, :, None, None] * dist.astype(np.float32)[None, :, None, None, :, :]
    s = jnp.where(valid[None, :, None, None, :, :], s, -jnp.inf)
    sink = jnp.broadcast_to(sinks.astype(F32).reshape(1, 1, KVH, G, 1, 1), s.shape[:-1] + (1,))
    p = jax.nn.softmax(jnp.concatenate([s, sink], axis=-1), axis=-1)[..., :-1]
    o = jnp.einsum('bnkgqs,bnskd->bnqkgd', p.astype(vb.dtype), vb)
    return o.reshape(B, L, H * Dh)


def causal_conv_silu(x, prev, w):
    L = x.shape[1]
    xp = jnp.concatenate([prev.astype(x.dtype), x], axis=1)
    y = xp[:, 0:L] * w[0]
    for j in range(1, CONV_WIDTH):
        y = y + xp[:, j:j + L] * w[j]
    return jax.nn.silu(y), xp[:, -(CONV_WIDTH - 1):]


def _to_chunks(a, n, c):
    b, h = a.shape[0], a.shape[2]
    a = a.reshape((b, n, c, h) + a.shape[3:])
    return jnp.swapaxes(jnp.swapaxes(a, 0, 1), 2, 3)


def gated_delta_rule(q, k, v, g, beta, s0):
    B, L, H, DK = q.shape
    DV = v.shape[-1]
    C = GDN_CHUNK
    pad = (-L) % C
    n = (L + pad) // C

    def prep(a):
        a = a.astype(F32)
        a = jnp.pad(a, [(0, 0), (0, pad)] + [(0, 0)] * (a.ndim - 2))
        return _to_chunks(a, n, C)

    qc = prep(q) * (DK ** -0.5)
    kc, vc, gc, bc = prep(k), prep(v), prep(g), prep(beta)
    gcum = jnp.cumsum(gc, axis=-1)
    incl = np.tril(np.ones((C, C), bool))
    strict = np.tril(np.ones((C, C), bool), -1)
    gdiff = gcum[..., :, None] - gcum[..., None, :]
    decay = jnp.where(incl, jnp.exp(jnp.where(incl, gdiff, 0.0)), 0.0)
    kb = kc * bc[..., None]
    a_mat = jnp.where(strict, jnp.einsum('nbhik,nbhjk->nbhij', kb, kc) * decay, 0.0)
    eye = jnp.eye(C, dtype=F32)
    t_mat = lax.linalg.triangular_solve(eye + a_mat, jnp.broadcast_to(eye, a_mat.shape),
                                        left_side=True, lower=True, unit_diagonal=True)
    u = jnp.einsum('nbhij,nbhjv->nbhiv', t_mat, vc * bc[..., None])
    w = jnp.einsum('nbhij,nbhjk->nbhik', t_mat, kb * jnp.exp(gcum)[..., None])
    intra = jnp.where(incl, jnp.einsum('nbhik,nbhjk->nbhij', qc, kc) * decay, 0.0)

    def step(s, inp):
        q_i, k_i, u_i, w_i, g_i, a_i = inp
        v_new = u_i - jnp.einsum('bhck,bhkv->bhcv', w_i, s)
        o = (jnp.einsum('bhck,bhkv->bhcv', q_i * jnp.exp(g_i)[..., None], s)
             + jnp.einsum('bhij,bhjv->bhiv', a_i, v_new))
        g_last = g_i[..., -1:]
        s = (s * jnp.exp(g_last)[..., None]
             + jnp.einsum('bhck,bhcv->bhkv', k_i * jnp.exp(g_last - g_i)[..., None], v_new))
        return s, o

    s_fin, o = lax.scan(step, s0.astype(F32), (qc, kc, u, w, gcum, intra))
    o = jnp.transpose(o, (1, 0, 3, 2, 4)).reshape(B, n * C, H, DV)[:, :L]
    return o, s_fin


def memory_kv(mem, mem_norm, w_mem_kv, k_norm_mem):
    B, M, _ = mem.shape
    kv = rms_norm(mem, mem_norm) @ w_mem_kv
    k, v = jnp.split(kv, 2, axis=-1)
    k = rms_norm(k.reshape(B, M, MEM_HEADS, MEM_HEAD_DIM), k_norm_mem)
    return k, v.reshape(B, M, MEM_HEADS, MEM_HEAD_DIM)


def memory_attention(q, mem_k, mem_v):
    s = jnp.einsum('blhd,bmhd->bhlm', q, mem_k.astype(q.dtype)).astype(F32) * (q.shape[-1] ** -0.5)
    p = jax.nn.softmax(s, axis=-1)
    o = jnp.einsum('bhlm,bmhd->blhd', p.astype(q.dtype), mem_v.astype(q.dtype))
    return o.reshape(q.shape[0], q.shape[1], -1)


def moe_ffn(x2, w_router, b_router, w_mlp1, b_mlp1, w_mlp2, b_mlp2):
    T, D = x2.shape
    logits = x2.astype(F32) @ w_router.astype(F32) + b_router.astype(F32)
    top_val, top_idx = lax.top_k(logits, TOP_K)
    gate = jax.nn.softmax(top_val, axis=-1)
    n_rows = T * TOP_K
    flat_e = top_idx.reshape(-1).astype(jnp.int32)
    flat_tok = jnp.repeat(jnp.arange(T, dtype=jnp.int32), TOP_K)
    flat_g = gate.reshape(-1)
    order = jnp.argsort(flat_e)
    se, stok, sg = flat_e[order], flat_tok[order], flat_g[order]
    counts = jax.ops.segment_sum(jnp.ones((n_rows,), jnp.int32), flat_e, num_segments=N_EXPERTS)
    starts = jnp.cumsum(counts) - counts
    padded = (counts + MOE_BLOCK - 1) // MOE_BLOCK * MOE_BLOCK
    pends = jnp.cumsum(padded)
    pstarts = pends - padded
    dest = pstarts[se] + jnp.arange(n_rows, dtype=jnp.int32) - starts[se]
    n_blocks = -(-n_rows // MOE_BLOCK) + N_EXPERTS
    n_slots = n_blocks * MOE_BLOCK
    slot_tok = jnp.zeros((n_slots,), jnp.int32).at[dest].set(stok)
    slot_gate = jnp.zeros((n_slots,), F32).at[dest].set(sg)
    block_e = jnp.minimum(jnp.searchsorted(pends, jnp.arange(n_blocks, dtype=jnp.int32) * MOE_BLOCK, side='right'),
                          N_EXPERTS - 1)

    def expert_block(args):
        tok, gb, e = args
        hmid = x2[tok] @ w_mlp1[e] + b_mlp1[e]
        glu = jnp.minimum(hmid[:, :D_FF], SWIGLU_LIMIT)
        lin = jnp.clip(hmid[:, D_FF:], -SWIGLU_LIMIT, SWIGLU_LIMIT)
        act = glu * jax.nn.sigmoid(SWIGLU_ALPHA * glu) * (lin + 1.0)
        y = act @ w_mlp2[e] + b_mlp2[e]
        return y.astype(F32) * gb[:, None]

    ys = lax.map(expert_block, (slot_tok.reshape(n_blocks, MOE_BLOCK), slot_gate.reshape(n_blocks, MOE_BLOCK), block_e))
    out = jnp.zeros((T, D), F32).at[slot_tok].add(ys.reshape(n_slots, D))
    return out.astype(x2.dtype)


def hybrid_layer(x, pos0, win_k, win_v, gdn_state, conv_state, mem_k, mem_v,
                 ln1_gain, w_in, q_norm_swa, k_norm_swa, swa_sinks, conv_w, a_log, dt_bias, gdn_norm,
                 q_norm_mem, w_branch, w_out, ln2_gain, w_router, b_router, w_mlp1, b_mlp1, w_mlp2, b_mlp2):
    B, L, D = x.shape
    h = rms_norm(x, ln1_gain)
    proj = h @ w_in
    (q_s, k_s, v_s, q_g, k_g, v_g, z_g, a_g, b_g, q_m, gate_logits) = jnp.split(proj, IN_OFFSETS, axis=-1)

    q_s = rms_norm(q_s.reshape(B, L, SWA_HEADS, SWA_HEAD_DIM), q_norm_swa)
    k_s = rms_norm(k_s.reshape(B, L, SWA_KV_HEADS, SWA_HEAD_DIM), k_norm_swa)
    v_s = v_s.reshape(B, L, SWA_KV_HEADS, SWA_HEAD_DIM)
    k_all = jnp.concatenate([win_k.astype(k_s.dtype), k_s], axis=1)
    v_all = jnp.concatenate([win_v.astype(v_s.dtype), v_s], axis=1)
    o_swa = sliding_window_attention(q_s, k_all, v_all, pos0, swa_sinks)
    new_win_k, new_win_v = k_all[:, -WINDOW:], v_all[:, -WINDOW:]

    qkv, new_conv = causal_conv_silu(jnp.concatenate([q_g, k_g, v_g], axis=-1), conv_state, conv_w)
    qg, kg, vg = jnp.split(qkv, [GDN_W, 2 * GDN_W], axis=-1)
    qg = l2_norm(qg.reshape(B, L, GDN_HEADS, GDN_HEAD_DIM))
    kg = l2_norm(kg.reshape(B, L, GDN_HEADS, GDN_HEAD_DIM))
    vg = vg.reshape(B, L, GDN_HEADS, GDN_HEAD_DIM)
    beta = jax.nn.sigmoid(b_g.astype(F32))
    g = -jnp.exp(a_log.astype(F32)) * jax.nn.softplus(a_g.astype(F32) + dt_bias.astype(F32))
    o_g, new_s = gated_delta_rule(qg, kg, vg, g, beta, gdn_state)
    o_g = rms_norm(o_g.astype(x.dtype), gdn_norm) * jax.nn.silu(z_g.reshape(B, L, GDN_HEADS, GDN_HEAD_DIM))

    q_m = rms_norm(q_m.reshape(B, L, MEM_HEADS, MEM_HEAD_DIM), q_norm_mem)
    o_m = memory_attention(q_m, mem_k, mem_v)

    branches = jnp.stack([o_swa, o_g.reshape(B, L, GDN_W), o_m], axis=2)
    gates = jax.nn.sigmoid(gate_logits.reshape(B, L, N_BRANCH, D))
    mixed = jnp.einsum('blnc,ncd->blnd', branches, w_branch)
    x = x + jnp.sum(gates * mixed, axis=2) @ w_out

    h2 = rms_norm(x, ln2_gain)
    x = x + moe_ffn(h2.reshape(B * L, D), w_router, b_router, w_mlp1, b_mlp1, w_mlp2, b_mlp2).reshape(B, L, D)
    return x, new_win_k, new_win_v, new_s.astype(gdn_state.dtype), new_conv


def setup_inputs(seed: int = 0) -> dict:
    key = jax.random.key(seed)
    ks = iter(jax.random.split(key, 40))

    def nrm(shape, scale):
        return scale * jax.random.normal(next(ks), shape, F32)

    def gain(shape):
        return 1.0 + nrm(shape, 0.02)

    a_log = jnp.log(jax.random.uniform(next(ks), (DEPTH, GDN_HEADS), F32, 1.0, 16.0))
    dt = jnp.exp(jax.random.uniform(next(ks), (DEPTH, GDN_HEADS), F32, float(np.log(1e-3)), float(np.log(1e-1))))
    dt_bias = dt + jnp.log(-jnp.expm1(-dt))
    return {
        'x_prompt': nrm((BATCH, SEQ, D_MODEL), 1.0),
        'x_sample': nrm((DEC_BATCH, DEC_SEQ, D_MODEL), 1.0),
        'cache_swa_k': nrm((DEPTH, DEC_BATCH, WINDOW, SWA_KV_HEADS, SWA_HEAD_DIM), 1.0),
        'cache_swa_v': nrm((DEPTH, DEC_BATCH, WINDOW, SWA_KV_HEADS, SWA_HEAD_DIM), 1.0),
        'state_gdn': nrm((DEPTH, DEC_BATCH, GDN_HEADS, GDN_HEAD_DIM, GDN_HEAD_DIM), 0.1),
        'state_conv': nrm((DEPTH, DEC_BATCH, CONV_WIDTH - 1, GDN_CONV_W), 1.0),
        'cache_mem_k': nrm((DEPTH, DEC_BATCH, MEM_LEN, MEM_HEADS, MEM_HEAD_DIM), 1.0),
        'cache_mem_v': nrm((DEPTH, DEC_BATCH, MEM_LEN, MEM_HEADS, MEM_HEAD_DIM), 1.0),
        'mem_prompt': nrm((BATCH, MEM_LEN, D_MODEL), 1.0),
        'ln1_gain': gain((DEPTH, D_MODEL)),
        'w_in': nrm((DEPTH, D_MODEL, IN_WIDTH), D_MODEL ** -0.5),
        'q_norm_swa': gain((DEPTH, SWA_HEAD_DIM)),
        'k_norm_swa': gain((DEPTH, SWA_HEAD_DIM)),
        'swa_sinks': nrm((DEPTH, SWA_HEADS), 0.5),
        'conv_w': nrm((DEPTH, CONV_WIDTH, GDN_CONV_W), CONV_WIDTH ** -0.5),
        'a_log': a_log,
        'dt_bias': dt_bias,
        'gdn_norm': gain((DEPTH, GDN_HEAD_DIM)),
        'q_norm_mem': gain((DEPTH, MEM_HEAD_DIM)),
        'k_norm_mem': gain((DEPTH, MEM_HEAD_DIM)),
        'mem_norm': gain((DEPTH, D_MODEL)),
        'w_mem_kv': nrm((DEPTH, D_MODEL, 2 * MEM_W), D_MODEL ** -0.5),
        'w_branch': nrm((DEPTH, N_BRANCH, BRANCH_W, D_MODEL), BRANCH_W ** -0.5),
        'w_out': nrm((DEPTH, D_MODEL, D_MODEL), D_MODEL ** -0.5),
        'ln2_gain': gain((DEPTH, D_MODEL)),
        'w_router': nrm((DEPTH, D_MODEL, N_EXPERTS), D_MODEL ** -0.5),
        'b_router': nrm((DEPTH, N_EXPERTS), 0.01),
        'w_mlp1': nrm((DEPTH, N_EXPERTS, D_MODEL, 2 * D_FF), D_MODEL ** -0.5),
        'b_mlp1': nrm((DEPTH, N_EXPERTS, 2 * D_FF), 0.01),
        'w_mlp2': nrm((DEPTH, N_EXPERTS, D_FF, D_MODEL), D_FF ** -0.5),
        'b_mlp2': nrm((DEPTH, N_EXPERTS, D_MODEL), 0.01),
    }


def reference(x_prompt, x_sample, cache_swa_k, cache_swa_v, state_gdn, state_conv, cache_mem_k, cache_mem_v,
              mem_prompt, ln1_gain, w_in, q_norm_swa, k_norm_swa, swa_sinks, conv_w, a_log, dt_bias, gdn_norm,
              q_norm_mem, k_norm_mem, mem_norm, w_mem_kv, w_branch, w_out, ln2_gain, w_router, b_router,
              w_mlp1, b_mlp1, w_mlp2, b_mlp2):
    xp, xs = x_prompt, x_sample
    Bp = xp.shape[0]
    swa_kp, swa_vp, swa_ks, swa_vs = [], [], [], []
    gdn_p, gdn_s, conv_p, conv_s, mem_kp, mem_vp = [], [], [], [], [], []
    for l in range(DEPTH):
        lw = (ln1_gain[l], w_in[l], q_norm_swa[l], k_norm_swa[l], swa_sinks[l], conv_w[l], a_log[l], dt_bias[l],
              gdn_norm[l], q_norm_mem[l], w_branch[l], w_out[l], ln2_gain[l], w_router[l], b_router[l],
              w_mlp1[l], b_mlp1[l], w_mlp2[l], b_mlp2[l])
        mk, mv = memory_kv(mem_prompt, mem_norm[l], w_mem_kv[l], k_norm_mem[l])
        zero_win = jnp.zeros((Bp, WINDOW, SWA_KV_HEADS, SWA_HEAD_DIM), xp.dtype)
        zero_s = jnp.zeros((Bp, GDN_HEADS, GDN_HEAD_DIM, GDN_HEAD_DIM), xp.dtype)
        zero_conv = jnp.zeros((Bp, CONV_WIDTH - 1, GDN_CONV_W), xp.dtype)
        xp, pk, pv, ps, pc = hybrid_layer(xp, 0, zero_win, zero_win, zero_s, zero_conv, mk, mv, *lw)
        xs, sk, sv, ss, sc = hybrid_layer(xs, PAST_LEN, cache_swa_k[l], cache_swa_v[l], state_gdn[l], state_conv[l],
                                          cache_mem_k[l], cache_mem_v[l], *lw)
        swa_kp.append(pk); swa_vp.append(pv); swa_ks.append(sk); swa_vs.append(sv)
        gdn_p.append(ps); gdn_s.append(ss); conv_p.append(pc); conv_s.append(sc)
        mem_kp.append(mk); mem_vp.append(mv)
    return (xp, xs,
            jnp.stack(swa_kp), jnp.stack(swa_vp), jnp.stack(swa_ks), jnp.stack(swa_vs),
            jnp.stack(gdn_p), jnp.stack(gdn_s), jnp.stack(conv_p), jnp.stack(conv_s),
            jnp.stack(mem_kp), jnp.stack(mem_vp))
```

```python
import functools

import numpy as np
import jax
import jax.numpy as jnp
from jax import lax
from jax.experimental import pallas as pl
from jax.experimental.pallas import tpu as pltpu

F32 = jnp.float32
BF16 = jnp.bfloat16
I32 = jnp.int32

D_MODEL = 1024
BRANCH_W = 512
SWA_HEADS = 8
SWA_KV_HEADS = 2
SWA_HEAD_DIM = 64
SWA_GROUP = SWA_HEADS // SWA_KV_HEADS
SWA_KV_W = SWA_KV_HEADS * SWA_HEAD_DIM
WINDOW = 128
GDN_HEADS = 4
GDN_HEAD_DIM = 128
GDN_W = GDN_HEADS * GDN_HEAD_DIM
GDN_CONV_W = 3 * GDN_W
GDN_CHUNK = 64
CONV_WIDTH = 4
MEM_LEN = 256
MEM_HEADS = 4
MEM_HEAD_DIM = 128
N_BRANCH = 3
N_EXPERTS = 32
TOP_K = 4
D_FF = 1024
SWIGLU_LIMIT = 7.0
SWIGLU_ALPHA = 1.702
EPS = 1e-6
PAST_LEN = 16384

LANES = 128
MOE_BM = 256
TOK_TILE = 128
VMEM_LIMIT = 56 * 1024 * 1024

_SEG_QS = (0, 512)
_SEG_KS = (512, 128)
_SEG_VS = (640, 128)
_SEG_QKVG = (768, 1536)
_SEG_Z = (2304, 512)
_SEG_AB = (2816, 128)
_SEG_QM = (2944, 512)
_SEG_GATE = (3456, 3072)
_PACKED_W = 6528
_IN_MAIN = 2816
_IN_AB = 8

_NEG = -1e30
_HI = lax.Precision.HIGHEST


def _cparams(sem, vmem=VMEM_LIMIT):
    return pltpu.CompilerParams(dimension_semantics=sem, vmem_limit_bytes=vmem)


def _bdot(a, b):
    return jnp.dot(a.astype(BF16), b.astype(BF16), preferred_element_type=F32)


def _bdot_nt(a, b):
    return lax.dot_general(a.astype(BF16), b.astype(BF16), (((1,), (1,)), ((), ())),
                           preferred_element_type=F32)


def _bdot_tn(a, b):
    return lax.dot_general(a.astype(BF16), b.astype(BF16), (((0,), (0,)), ((), ())),
                           preferred_element_type=F32)


def _hdot(a, b):
    return jnp.dot(a, b, preferred_element_type=F32, precision=_HI)


def _hdot_nt(a, b):
    return lax.dot_general(a, b, (((1,), (1,)), ((), ())), preferred_element_type=F32, precision=_HI)


def _rms(x, gain):
    return x * lax.rsqrt(jnp.mean(x * x, axis=-1, keepdims=True) + EPS) * gain


def _l2(x):
    return x * lax.rsqrt(jnp.sum(x * x, axis=-1, keepdims=True) + EPS)


def _silu(x):
    return x * jax.nn.sigmoid(x)


def _softplus(x):
    return jnp.maximum(x, 0.0) + jnp.log1p(jnp.exp(-jnp.abs(x)))


def _inproj_kernel(x_ref, g_ref, w_ref, qs, ks, vs, qkvg, z, ab, qm, gate):
    x = x_ref[...]
    hb = _rms(x, g_ref[...]).astype(BF16)
    for ref, (off, width) in ((qs, _SEG_QS), (ks, _SEG_KS), (vs, _SEG_VS), (qkvg, _SEG_QKVG), (z, _SEG_Z),
                              (ab, _SEG_AB), (qm, _SEG_QM), (gate, _SEG_GATE)):
        step = min(width, 512)
        for c0 in range(0, width, step):
            ref[:, c0:c0 + step] = jnp.dot(hb, w_ref[:, off + c0:off + c0 + step], preferred_element_type=F32)


def _inproj(x2, gain, wp, tm):
    t = x2.shape[0]
    segs = (_SEG_QS, _SEG_KS, _SEG_VS, _SEG_QKVG, _SEG_Z, _SEG_AB, _SEG_QM, _SEG_GATE)
    return pl.pallas_call(
        _inproj_kernel,
        out_shape=[jax.ShapeDtypeStruct((t, w), F32) for _, w in segs],
        grid=(t // tm,),
        in_specs=[pl.BlockSpec((tm, D_MODEL), lambda i: (i, 0)),
                  pl.BlockSpec((1, D_MODEL), lambda i: (0, 0)),
                  pl.BlockSpec((D_MODEL, _PACKED_W), lambda i: (0, 0))],
        out_specs=[pl.BlockSpec((tm, w), lambda i: (i, 0)) for _, w in segs],
        compiler_params=_cparams(("parallel",)),
        name="inproj",
    )(x2, gain, wp)


def _alibi_slopes(n):
    return [float(2.0 ** (-8.0 * (i + 1) / n)) for i in range(n)]


def _swa_prompt_kernel(q_ref, kc_ref, kp_ref, vc_ref, vp_ref, qn_ref, kn_ref, sink_ref, o_ref, kwin_ref):
    n = pl.program_id(1)
    q = q_ref[0]
    kc, kp, vc, vp = kc_ref[0], kp_ref[0], vc_ref[0], vp_ref[0]
    row = lax.broadcasted_iota(I32, (WINDOW, 2 * WINDOW), 0)
    col = lax.broadcasted_iota(I32, (WINDOW, 2 * WINDOW), 1)
    dist = row + WINDOW - col
    valid = (dist >= 0) & (dist <= WINDOW) & ((col >= WINDOW) | (n > 0))
    distf = dist.astype(F32)
    slopes = _alibi_slopes(SWA_HEADS)
    scale = SWA_HEAD_DIM ** -0.5
    for g in range(SWA_KV_HEADS):
        sl = slice(g * SWA_HEAD_DIM, (g + 1) * SWA_HEAD_DIM)
        kcn = _rms(kc[:, sl], kn_ref[...])
        kpn = _rms(kp[:, sl], kn_ref[...])
        kwin_ref[0, :, sl] = kcn
        kk = jnp.concatenate([kpn, kcn], axis=0)
        vv = jnp.concatenate([vp[:, sl], vc[:, sl]], axis=0)
        for hh in range(SWA_GROUP):
            h = g * SWA_GROUP + hh
            hs = slice(h * SWA_HEAD_DIM, (h + 1) * SWA_HEAD_DIM)
            qh = _rms(q[:, hs], qn_ref[...])
            s = _bdot_nt(qh, kk) * scale - slopes[h] * distf
            s = jnp.where(valid, s, _NEG)
            sink = sink_ref[0:1, h:h + 1]
            m = jnp.maximum(jnp.max(s, axis=-1, keepdims=True), sink)
            p = jnp.exp(s - m)
            denom = jnp.sum(p, axis=-1, keepdims=True) + jnp.exp(sink - m)
            o_ref[0, :, hs] = _bdot(p, vv) / denom


def _swa_prompt(qs, ks, vs, qn, kn, sinks):
    b, l, _ = qs.shape
    nb = l // WINDOW
    cur = lambda i, j: (i, j, 0)
    prev = lambda i, j: (i, jnp.maximum(j - 1, 0), 0)
    const2 = lambda i, j: (0, 0)
    return pl.pallas_call(
        _swa_prompt_kernel,
        out_shape=[jax.ShapeDtypeStruct((b, l, BRANCH_W), F32),
                   jax.ShapeDtypeStruct((b, WINDOW, SWA_KV_W), F32)],
        grid=(b, nb),
        in_specs=[pl.BlockSpec((1, WINDOW, BRANCH_W), cur),
                  pl.BlockSpec((1, WINDOW, SWA_KV_W), cur),
                  pl.BlockSpec((1, WINDOW, SWA_KV_W), prev),
                  pl.BlockSpec((1, WINDOW, SWA_KV_W), cur),
                  pl.BlockSpec((1, WINDOW, SWA_KV_W), prev),
                  pl.BlockSpec((1, SWA_HEAD_DIM), const2),
                  pl.BlockSpec((1, SWA_HEAD_DIM), const2),
                  pl.BlockSpec((1, SWA_HEADS), const2)],
        out_specs=[pl.BlockSpec((1, WINDOW, BRANCH_W), cur),
                   pl.BlockSpec((1, WINDOW, SWA_KV_W), lambda i, j: (i, 0, 0))],
        compiler_params=_cparams(("parallel", "arbitrary")),
        name="swa_prompt",
    )(qs, ks, ks, vs, vs, qn, kn, sinks)


def _swa_decode_kernel(bs, q_ref, k3_ref, kf_ref, v3_ref, vf_ref, ck_ref, cv_ref, qn_ref, kn_ref, kn2_ref,
                       sink_ref, slope_ref, o_ref, ok_ref, ov_ref):
    scale = SWA_HEAD_DIM ** -0.5
    lane = lax.broadcasted_iota(I32, (1, SWA_KV_W), 1)
    rowi = lax.broadcasted_iota(I32, (WINDOW, SWA_KV_W), 0)
    keyd = (WINDOW - lax.broadcasted_iota(I32, (1, WINDOW), 1)).astype(F32)
    for b in range(bs):
        qn = _rms(q_ref[b], qn_ref[...])
        kn3 = _rms(k3_ref[b], kn_ref[...])
        v3 = v3_ref[b]
        kf = kf_ref[b]
        sq = kf * kf
        ms0 = jnp.sum(jnp.where(lane < SWA_HEAD_DIM, sq, 0.0), axis=-1, keepdims=True) / SWA_HEAD_DIM
        ms1 = jnp.sum(jnp.where(lane >= SWA_HEAD_DIM, sq, 0.0), axis=-1, keepdims=True) / SWA_HEAD_DIM
        knf = kf * lax.rsqrt(jnp.where(lane < SWA_HEAD_DIM, ms0, ms1) + EPS) * kn2_ref[...]
        ck = ck_ref[b]
        cv = cv_ref[b]
        for g in range(SWA_KV_HEADS):
            sl = slice(g * SWA_HEAD_DIM, (g + 1) * SWA_HEAD_DIM)
            hs = slice(g * SWA_GROUP, (g + 1) * SWA_GROUP)
            qg = qn[hs]
            slope = slope_ref[hs]
            sink = sink_ref[hs]
            s = _bdot_nt(qg, ck[:, sl]) * scale - slope * keyd
            s_new = jnp.sum(qg * kn3[g:g + 1], axis=-1, keepdims=True) * scale
            m = jnp.maximum(jnp.maximum(jnp.max(s, axis=-1, keepdims=True), s_new), sink)
            p = jnp.exp(s - m)
            p_new = jnp.exp(s_new - m)
            denom = jnp.sum(p, axis=-1, keepdims=True) + p_new + jnp.exp(sink - m)
            o_ref[b, hs, :] = (_bdot(p, cv[:, sl]) + p_new * v3[g:g + 1]) / denom
        last = rowi == WINDOW - 1
        ok_ref[b] = jnp.where(last, knf, pltpu.roll(ck, WINDOW - 1, 0))
        ov_ref[b] = jnp.where(last, vf_ref[b], pltpu.roll(cv, WINDOW - 1, 0))


def _swa_decode(qs, ks, vs, cache_k, cache_v, qn, kn, sinks, bs=8):
    b = qs.shape[0]
    q3 = qs.reshape(b, SWA_HEADS, SWA_HEAD_DIM)
    k3 = ks.reshape(b, SWA_KV_HEADS, SWA_HEAD_DIM)
    kf = ks.reshape(b, 1, SWA_KV_W)
    v3 = vs.reshape(b, SWA_KV_HEADS, SWA_HEAD_DIM)
    vf = vs.reshape(b, 1, SWA_KV_W)
    ck = cache_k.reshape(b, WINDOW, SWA_KV_W)
    cv = cache_v.reshape(b, WINDOW, SWA_KV_W)
    kn2 = jnp.concatenate([kn, kn], axis=-1)
    sink_col = sinks.reshape(SWA_HEADS, 1)
    slope_col = jnp.asarray(np.asarray(_alibi_slopes(SWA_HEADS), np.float32).reshape(SWA_HEADS, 1))
    blk = lambda *shape: pl.BlockSpec((bs,) + shape, lambda i: (i,) + (0,) * len(shape))
    full = lambda *shape: pl.BlockSpec(shape, lambda i: (0,) * len(shape))
    o, ok, ov = pl.pallas_call(
        functools.partial(_swa_decode_kernel, bs),
        out_shape=[jax.ShapeDtypeStruct((b, SWA_HEADS, SWA_HEAD_DIM), F32),
                   jax.ShapeDtypeStruct((b, WINDOW, SWA_KV_W), F32),
                   jax.ShapeDtypeStruct((b, WINDOW, SWA_KV_W), F32)],
        grid=(b // bs,),
        in_specs=[blk(SWA_HEADS, SWA_HEAD_DIM), blk(SWA_KV_HEADS, SWA_HEAD_DIM), blk(1, SWA_KV_W),
                  blk(SWA_KV_HEADS, SWA_HEAD_DIM), blk(1, SWA_KV_W), blk(WINDOW, SWA_KV_W), blk(WINDOW, SWA_KV_W),
                  full(1, SWA_HEAD_DIM), full(1, SWA_HEAD_DIM), full(1, SWA_KV_W),
                  full(SWA_HEADS, 1), full(SWA_HEADS, 1)],
        out_specs=[blk(SWA_HEADS, SWA_HEAD_DIM), blk(WINDOW, SWA_KV_W), blk(WINDOW, SWA_KV_W)],
        compiler_params=_cparams(("parallel",)),
        name="swa_decode",
    )(q3, k3, kf, v3, vf, ck, cv, qn, kn, kn2, sink_col, slope_col)
    return o.reshape(b, BRANCH_W), ok, ov


def _gate_rows(ab, alog_ref, dtb_ref):
    g = -jnp.exp(alog_ref[...]) * _softplus(ab + dtb_ref[...])
    return g, jax.nn.sigmoid(ab)


def _unit_lower_inverse(a):
    n = a.shape[0]
    eye = (lax.broadcasted_iota(I32, (n, n), 0) == lax.broadcasted_iota(I32, (n, n), 1)).astype(F32)
    bp = -a
    p = eye + bp
    span = 2
    while span < n:
        bp = _hdot(bp, bp)
        p = p + _hdot(p, bp)
        span *= 2
    return p


def _gdn_prompt_kernel(x_ref, z_ref, ab_ref, cw_ref, alog_ref, dtb_ref, gn_ref, o_ref, s_ref, st, xp):
    c = pl.program_id(1)
    C = GDN_CHUNK
    halo = 8

    @pl.when(c == 0)
    def _():
        st[...] = jnp.zeros_like(st)
        xp[0:halo, :] = jnp.zeros((halo, GDN_CONV_W), F32)

    x = x_ref[0]
    xp[halo:halo + C, :] = x
    y = x * cw_ref[CONV_WIDTH - 1:CONV_WIDTH, :]
    for j in range(CONV_WIDTH - 1):
        sh = CONV_WIDTH - 1 - j
        y = y + xp[halo - sh:halo - sh + C, :] * cw_ref[j:j + 1, :]
    xp[0:halo, :] = x[C - halo:C, :]
    y = _silu(y)

    g_all, beta_all = _gate_rows(ab_ref[0], alog_ref, dtb_ref)
    r = lax.broadcasted_iota(I32, (C, C), 0)
    cc = lax.broadcasted_iota(I32, (C, C), 1)
    incl = r >= cc
    strict = r > cc
    z = z_ref[0]
    for h in range(GDN_HEADS):
        hs = slice(h * GDN_HEAD_DIM, (h + 1) * GDN_HEAD_DIM)
        qh = _l2(y[:, h * GDN_HEAD_DIM:(h + 1) * GDN_HEAD_DIM]) * (GDN_HEAD_DIM ** -0.5)
        kh = _l2(y[:, GDN_W + h * GDN_HEAD_DIM:GDN_W + (h + 1) * GDN_HEAD_DIM])
        vh = y[:, 2 * GDN_W + h * GDN_HEAD_DIM:2 * GDN_W + (h + 1) * GDN_HEAD_DIM]
        g = g_all[:, h:h + 1]
        beta = beta_all[:, GDN_HEADS + h:GDN_HEADS + h + 1]
        gc_row = jnp.sum(jnp.where(r <= cc, g, 0.0), axis=0, keepdims=True)
        gc_col = jnp.sum(jnp.where(r == cc, gc_row, 0.0), axis=1, keepdims=True)
        decay = jnp.where(incl, jnp.exp(jnp.where(incl, gc_col - gc_row, 0.0)), 0.0)
        kb = kh * beta
        a = jnp.where(strict, _bdot_nt(kb, kh) * decay, 0.0)
        t = _unit_lower_inverse(a)
        uw = _bdot(t, jnp.concatenate([vh * beta, kb * jnp.exp(gc_col)], axis=1))
        u, w = uw[:, :GDN_HEAD_DIM], uw[:, GDN_HEAD_DIM:]
        intra = jnp.where(incl, _bdot_nt(qh, kh) * decay, 0.0)
        s_h = st[h]
        v_new = u - _bdot(w, s_h)
        o = _bdot(qh * jnp.exp(gc_col), s_h) + _bdot(intra, v_new)
        g_last = gc_row[:, C - 1:C]
        st[h] = s_h * jnp.exp(g_last) + _bdot_tn(kh * jnp.exp(g_last - gc_col), v_new)
        o_ref[0, :, hs] = _rms(o, gn_ref[...]) * _silu(z[:, hs])
    s_ref[0] = st[...]


def _gdn_prompt(qkvg, z, ab, conv_w, alog, dtb, gnorm):
    b, l, _ = qkvg.shape
    nc = l // GDN_CHUNK
    cur = lambda i, j: (i, j, 0)
    const2 = lambda i, j: (0, 0)
    return pl.pallas_call(
        _gdn_prompt_kernel,
        out_shape=[jax.ShapeDtypeStruct((b, l, GDN_W), F32),
                   jax.ShapeDtypeStruct((b, GDN_HEADS, GDN_HEAD_DIM, GDN_HEAD_DIM), F32)],
        grid=(b, nc),
        in_specs=[pl.BlockSpec((1, GDN_CHUNK, GDN_CONV_W), cur),
                  pl.BlockSpec((1, GDN_CHUNK, GDN_W), cur),
                  pl.BlockSpec((1, GDN_CHUNK, LANES), cur),
                  pl.BlockSpec((CONV_WIDTH, GDN_CONV_W), const2),
                  pl.BlockSpec((1, LANES), const2),
                  pl.BlockSpec((1, LANES), const2),
                  pl.BlockSpec((1, GDN_HEAD_DIM), const2)],
        out_specs=[pl.BlockSpec((1, GDN_CHUNK, GDN_W), cur),
                   pl.BlockSpec((1, GDN_HEADS, GDN_HEAD_DIM, GDN_HEAD_DIM), lambda i, j: (i, 0, 0, 0))],
        scratch_shapes=[pltpu.VMEM((GDN_HEADS, GDN_HEAD_DIM, GDN_HEAD_DIM), F32),
                        pltpu.VMEM((8 + GDN_CHUNK, GDN_CONV_W), F32)],
        compiler_params=_cparams(("parallel", "arbitrary")),
        name="gdn_prompt",
    )(qkvg, z, ab, conv_w, alog, dtb, gnorm)


def _gdn_decode_kernel(bs, x_ref, cs_ref, z_ref, ab_ref, s_ref, cw_ref, alog_ref, dtb_ref, gn_ref, o_ref, so_ref):
    x = x_ref[...]
    y = x * cw_ref[CONV_WIDTH - 1:CONV_WIDTH, :]
    for j in range(CONV_WIDTH - 1):
        y = y + cs_ref[:, j, :] * cw_ref[j:j + 1, :]
    y = _silu(y)
    g_all, beta_all = _gate_rows(ab_ref[...], alog_ref, dtb_ref)
    eg_all = jnp.exp(g_all)
    z = z_ref[...]
    d = GDN_HEAD_DIM
    eye = (lax.broadcasted_iota(I32, (d, d), 0) == lax.broadcasted_iota(I32, (d, d), 1)).astype(F32)
    for h in range(GDN_HEADS):
        hs = slice(h * d, (h + 1) * d)
        qh = _l2(y[:, h * d:(h + 1) * d]) * (d ** -0.5)
        kh = _l2(y[:, GDN_W + h * d:GDN_W + (h + 1) * d])
        vh = y[:, 2 * GDN_W + h * d:2 * GDN_W + (h + 1) * d]
        kt = _hdot_nt(eye, kh)
        qt = _hdot_nt(eye, qh)
        outs = []
        for b in range(bs):
            s_b = s_ref[b, h]
            kc = kt[:, b:b + 1]
            eg = eg_all[b:b + 1, h:h + 1]
            beta = beta_all[b:b + 1, GDN_HEADS + h:GDN_HEADS + h + 1]
            sk = jnp.sum(s_b * kc, axis=0, keepdims=True)
            v_new = beta * (vh[b:b + 1] - eg * sk)
            s_new = eg * s_b + kc * v_new
            so_ref[b, h] = s_new
            outs.append(jnp.sum(s_new * qt[:, b:b + 1], axis=0, keepdims=True))
        o = jnp.concatenate(outs, axis=0)
        o_ref[:, hs] = _rms(o, gn_ref[...]) * _silu(z[:, hs])


def _gdn_decode(qkvg, z, ab, state, conv_state, conv_w, alog, dtb, gnorm, bs=8):
    b = qkvg.shape[0]
    row = lambda w: pl.BlockSpec((bs, w), lambda i: (i, 0))
    full = lambda *shape: pl.BlockSpec(shape, lambda i: (0,) * len(shape))
    sspec = pl.BlockSpec((bs, GDN_HEADS, GDN_HEAD_DIM, GDN_HEAD_DIM), lambda i: (i, 0, 0, 0))
    return pl.pallas_call(
        functools.partial(_gdn_decode_kernel, bs),
        out_shape=[jax.ShapeDtypeStruct((b, GDN_W), F32),
                   jax.ShapeDtypeStruct(state.shape, F32)],
        grid=(b // bs,),
        in_specs=[row(GDN_CONV_W),
                  pl.BlockSpec((bs, CONV_WIDTH - 1, GDN_CONV_W), lambda i: (i, 0, 0)),
                  row(GDN_W), row(LANES), sspec,
                  full(CONV_WIDTH, GDN_CONV_W), full(1, LANES), full(1, LANES), full(1, GDN_HEAD_DIM)],
        out_specs=[row(GDN_W), sspec],
        compiler_params=_cparams(("parallel",)),
        name="gdn_decode",
    )(qkvg, conv_state, z, ab, state, conv_w, alog, dtb, gnorm)


def _memkv_kernel(x_ref, g_ref, w_ref, kn_ref, k_ref, v_ref):
    hb = _rms(x_ref[...], g_ref[...]).astype(BF16)
    for hd in range(MEM_HEADS):
        hs = slice(hd * MEM_HEAD_DIM, (hd + 1) * MEM_HEAD_DIM)
        k_ref[:, hs] = _rms(jnp.dot(hb, w_ref[:, hs], preferred_element_type=F32), kn_ref[...])
    v_ref[...] = jnp.dot(hb, w_ref[:, BRANCH_W:], preferred_element_type=F32)


def _memkv(mem2, gain, w_kv, kn, tm=512):
    t = mem2.shape[0]
    return pl.pallas_call(
        _memkv_kernel,
        out_shape=[jax.ShapeDtypeStruct((t, BRANCH_W), F32)] * 2,
        grid=(t // tm,),
        in_specs=[pl.BlockSpec((tm, D_MODEL), lambda i: (i, 0)),
                  pl.BlockSpec((1, D_MODEL), lambda i: (0, 0)),
                  pl.BlockSpec((D_MODEL, 2 * BRANCH_W), lambda i: (0, 0)),
                  pl.BlockSpec((1, MEM_HEAD_DIM), lambda i: (0, 0))],
        out_specs=[pl.BlockSpec((tm, BRANCH_W), lambda i: (i, 0))] * 2,
        compiler_params=_cparams(("parallel",)),
        name="memkv",
    )(mem2, gain, w_kv, kn)


def _memattn_prompt_kernel(q_ref, k_ref, v_ref, qn_ref, o_ref):
    q = q_ref[0]
    k = k_ref[0]
    v = v_ref[0]
    scale = MEM_HEAD_DIM ** -0.5
    for hd in range(MEM_HEADS):
        hs = slice(hd * MEM_HEAD_DIM, (hd + 1) * MEM_HEAD_DIM)
        s = _bdot_nt(_rms(q[:, hs], qn_ref[...]), k[:, hs]) * scale
        p = jnp.exp(s - jnp.max(s, axis=-1, keepdims=True))
        o_ref[0, :, hs] = _bdot(p, v[:, hs]) / jnp.sum(p, axis=-1, keepdims=True)


def _memattn_prompt(qm, mk, mv, qn):
    b, l, _ = qm.shape
    tq = 512 if l % 512 == 0 else WINDOW
    return pl.pallas_call(
        _memattn_prompt_kernel,
        out_shape=jax.ShapeDtypeStruct((b, l, BRANCH_W), F32),
        grid=(b, l // tq),
        in_specs=[pl.BlockSpec((1, tq, BRANCH_W), lambda i, j: (i, j, 0)),
                  pl.BlockSpec((1, MEM_LEN, BRANCH_W), lambda i, j: (i, 0, 0)),
                  pl.BlockSpec((1, MEM_LEN, BRANCH_W), lambda i, j: (i, 0, 0)),
                  pl.BlockSpec((1, MEM_HEAD_DIM), lambda i, j: (0, 0))],
        out_specs=pl.BlockSpec((1, tq, BRANCH_W), lambda i, j: (i, j, 0)),
        compiler_params=_cparams(("parallel", "parallel")),
        name="memattn_prompt",
    )(qm, mk, mv, qn)


def _memattn_decode_kernel(bs, q_ref, k_ref, v_ref, qn_ref, o_ref):
    scale = MEM_HEAD_DIM ** -0.5
    q = q_ref[...]
    for hd in range(MEM_HEADS):
        hs = slice(hd * MEM_HEAD_DIM, (hd + 1) * MEM_HEAD_DIM)
        qn = _rms(q[:, hs], qn_ref[...])
        outs = []
        for b in range(bs):
            s = jnp.sum(k_ref[b, :, hs] * qn[b:b + 1], axis=-1, keepdims=True) * scale
            p = jnp.exp(s - jnp.max(s, axis=0, keepdims=True))
            outs.append(jnp.sum(p * v_ref[b, :, hs], axis=0, keepdims=True) / jnp.sum(p, axis=0, keepdims=True))
        o_ref[:, hs] = jnp.concatenate(outs, axis=0)


def _memattn_decode(qm, ck, cv, qn, bs=8):
    b = qm.shape[0]
    kv = pl.BlockSpec((bs, MEM_LEN, BRANCH_W), lambda i: (i, 0, 0))
    return pl.pallas_call(
        functools.partial(_memattn_decode_kernel, bs),
        out_shape=jax.ShapeDtypeStruct((b, BRANCH_W), F32),
        grid=(b // bs,),
        in_specs=[pl.BlockSpec((bs, BRANCH_W), lambda i: (i, 0)), kv, kv,
                  pl.BlockSpec((1, MEM_HEAD_DIM), lambda i: (0, 0))],
        out_specs=pl.BlockSpec((bs, BRANCH_W), lambda i: (i, 0)),
        compiler_params=_cparams(("parallel",)),
        name="memattn_decode",
    )(qm, ck, cv, qn)


def _merge_kernel(oa_ref, ob_ref, oc_ref, gate_ref, x_ref, wb_ref, wo_ref, g2_ref, wr_ref, br_ref,
                  x1_ref, h2_ref, lg_ref):
    acc = None
    for i, o_ref in enumerate((oa_ref, ob_ref, oc_ref)):
        mixed = jnp.dot(o_ref[...].astype(BF16), wb_ref[i], preferred_element_type=F32)
        term = jax.nn.sigmoid(gate_ref[:, i * D_MODEL:(i + 1) * D_MODEL]) * mixed
        acc = term if acc is None else acc + term
    x1 = x_ref[...] + jnp.dot(acc.astype(BF16), wo_ref[...], preferred_element_type=F32)
    x1_ref[...] = x1
    h2 = _rms(x1, g2_ref[...])
    h2_ref[...] = h2
    lg_ref[...] = _hdot_nt(wr_ref[...], h2) + br_ref[...]


def _merge(oa, ob, oc, gate, x2, wb, wo, g2, wr_t, br_col, tm):
    t = x2.shape[0]
    row = lambda w: pl.BlockSpec((tm, w), lambda i: (i, 0))
    full = lambda *shape: pl.BlockSpec(shape, lambda i: (0,) * len(shape))
    return pl.pallas_call(
        _merge_kernel,
        out_shape=[jax.ShapeDtypeStruct((t, D_MODEL), F32),
                   jax.ShapeDtypeStruct((t, D_MODEL), F32),
                   jax.ShapeDtypeStruct((N_EXPERTS, t), F32)],
        grid=(t // tm,),
        in_specs=[row(BRANCH_W), row(BRANCH_W), row(BRANCH_W), row(N_BRANCH * D_MODEL), row(D_MODEL),
                  full(N_BRANCH, BRANCH_W, D_MODEL), full(D_MODEL, D_MODEL), full(1, D_MODEL),
                  full(N_EXPERTS, D_MODEL), full(N_EXPERTS, 1)],
        out_specs=[row(D_MODEL), row(D_MODEL), pl.BlockSpec((N_EXPERTS, tm), lambda i: (0, i))],
        compiler_params=_cparams(("parallel",)),
        name="merge",
    )(oa, ob, oc, gate, x2, wb, wo, g2, wr_t, br_col)


def _route_topk_kernel(lg_ref, eidx_ref, gate_ref, rank_ref, cnt_ref, carry):
    i = pl.program_id(0)
    tn = lg_ref.shape[1]

    @pl.when(i == 0)
    def _():
        carry[...] = jnp.zeros_like(carry)

    l = lg_ref[...]
    eio = lax.broadcasted_iota(I32, (N_EXPERTS, tn), 0)
    hot = jnp.zeros((N_EXPERTS, tn), F32)
    vals, idxs = [], []
    for _ in range(TOP_K):
        m = jnp.max(l, axis=0, keepdims=True)
        idx = jnp.min(jnp.where(l == m, eio, N_EXPERTS), axis=0, keepdims=True)
        sel = eio == idx
        vals.append(m)
        idxs.append(idx)
        hot = hot + sel.astype(F32)
        l = jnp.where(sel, -jnp.inf, l)
    ex = [jnp.exp(v - vals[0]) for v in vals]
    tot = ex[0] + ex[1] + ex[2] + ex[3]
    before = (lax.broadcasted_iota(I32, (tn, tn), 0) < lax.broadcasted_iota(I32, (tn, tn), 1)).astype(BF16)
    rank = jnp.dot(hot.astype(BF16), before, preferred_element_type=F32) + carry[:, 0:1]
    carry[...] = carry[...] + jnp.sum(hot, axis=1, keepdims=True)
    for k in range(TOP_K):
        eidx_ref[k:k + 1, :] = idxs[k]
        gate_ref[k:k + 1, :] = ex[k] / tot
        rank_ref[k:k + 1, :] = jnp.sum(jnp.where(eio == idxs[k], rank, 0.0), axis=0, keepdims=True).astype(I32)
    cnt_ref[...] = carry[...]


def _route_topk(logits_t, tn):
    t = logits_t.shape[1]
    kt = pl.BlockSpec((TOP_K, tn), lambda i: (0, i))
    return pl.pallas_call(
        _route_topk_kernel,
        out_shape=[jax.ShapeDtypeStruct((TOP_K, t), I32),
                   jax.ShapeDtypeStruct((TOP_K, t), F32),
                   jax.ShapeDtypeStruct((TOP_K, t), I32),
                   jax.ShapeDtypeStruct((N_EXPERTS, LANES), F32)],
        grid=(t // tn,),
        in_specs=[pl.BlockSpec((N_EXPERTS, tn), lambda i: (0, i))],
        out_specs=[kt, kt, kt, pl.BlockSpec((N_EXPERTS, LANES), lambda i: (0, 0))],
        scratch_shapes=[pltpu.VMEM((N_EXPERTS, LANES), F32)],
        compiler_params=_cparams(("arbitrary",)),
        name="route_topk",
    )(logits_t)


def _expert_offsets(cnt_col):
    padded = jnp.floor((cnt_col + (MOE_BM - 1)) * (1.0 / MOE_BM)) * MOE_BM
    r = lax.broadcasted_iota(I32, (N_EXPERTS, N_EXPERTS), 0)
    c = lax.broadcasted_iota(I32, (N_EXPERTS, N_EXPERTS), 1)
    ends_row = jnp.sum(jnp.where(r <= c, padded, 0.0), axis=0, keepdims=True)
    ends_col = jnp.sum(jnp.where(r == c, ends_row, 0.0), axis=1, keepdims=True)
    return ends_col, ends_col - padded


def _route_dest_kernel(eidx_ref, rank_ref, cnt_ref, dest_ref):
    tn = eidx_ref.shape[1]
    _, starts = _expert_offsets(cnt_ref[:, 0:1])
    eio = lax.broadcasted_iota(I32, (N_EXPERTS, tn), 0)
    for k in range(TOP_K):
        base = jnp.sum(jnp.where(eio == eidx_ref[k:k + 1, :], starts, 0.0), axis=0, keepdims=True)
        d = base.astype(I32) + rank_ref[k:k + 1, :]
        for j in range(tn // TOK_TILE):
            dest_ref[j, k:k + 1, :] = d[:, j * TOK_TILE:(j + 1) * TOK_TILE]


def _route_dest(eidx, rank, cnt, tn):
    t = eidx.shape[1]
    kt = pl.BlockSpec((TOP_K, tn), lambda i: (0, i))
    return pl.pallas_call(
        _route_dest_kernel,
        out_shape=jax.ShapeDtypeStruct((t // TOK_TILE, TOP_K, TOK_TILE), I32),
        grid=(t // tn,),
        in_specs=[kt, kt, pl.BlockSpec((N_EXPERTS, LANES), lambda i: (0, 0))],
        out_specs=pl.BlockSpec((tn // TOK_TILE, TOP_K, TOK_TILE), lambda i: (i, 0, 0)),
        compiler_params=_cparams(("parallel",)),
        name="route_dest",
    )(eidx, rank, cnt)


def _route_blocks_kernel(cnt_ref, be_ref, na_ref):
    nbp = be_ref.shape[1]
    ends, _ = _expert_offsets(cnt_ref[:, 0:1])
    nact = ends[N_EXPERTS - 1:N_EXPERTS, :] * (1.0 / MOE_BM)
    blk = jnp.minimum(lax.broadcasted_iota(I32, (1, nbp), 1).astype(F32), nact - 1.0)
    be = jnp.sum((ends <= blk * MOE_BM).astype(F32), axis=0, keepdims=True)
    be_ref[...] = jnp.minimum(be, N_EXPERTS - 1.0).astype(I32)
    na_ref[...] = jnp.broadcast_to(nact, na_ref.shape).astype(I32)


def _route_blocks(cnt, nbp):
    return pl.pallas_call(
        _route_blocks_kernel,
        out_shape=[jax.ShapeDtypeStruct((1, nbp), I32), jax.ShapeDtypeStruct((1, LANES), I32)],
        name="route_blocks",
    )(cnt)


def _row_copies_wait(make_copy, n):
    def body(t, carry):
        make_copy(t).wait()
        return carry
    lax.fori_loop(0, n, body, 0)


def _dispatch_kernel(dest_ref, h_ref, xg_in, xg_ref, buf, sem):
    del xg_in
    i = pl.program_id(0)
    n = pl.num_programs(0)
    slot = i % 2

    def copy(s, t, k):
        return pltpu.make_async_copy(buf.at[s, pl.ds(t, 1)], xg_ref.at[pl.ds(dest_ref[0, k, t], 1)], sem.at[s])

    def drain(s):
        _row_copies_wait(lambda t: copy(s, 0, 0), TOP_K * TOK_TILE)

    @pl.when(i >= 2)
    def _():
        drain(slot)

    buf[slot] = h_ref[...]

    def issue(t, carry):
        for k in range(TOP_K):
            copy(slot, t, k).start()
        return carry
    lax.fori_loop(0, TOK_TILE, issue, 0)

    @pl.when(i == n - 1)
    def _():
        drain(slot)

        @pl.when(n >= 2)
        def _():
            drain(1 - slot)


def _dispatch(dest3, h2, n_slots):
    t = h2.shape[0]
    xg0 = jnp.zeros((n_slots, D_MODEL), F32)
    return pl.pallas_call(
        _dispatch_kernel,
        out_shape=jax.ShapeDtypeStruct((n_slots, D_MODEL), F32),
        grid=(t // TOK_TILE,),
        in_specs=[pl.BlockSpec((1, TOP_K, TOK_TILE), lambda i: (i, 0, 0), memory_space=pltpu.SMEM),
                  pl.BlockSpec((TOK_TILE, D_MODEL), lambda i: (i, 0)),
                  pl.BlockSpec(memory_space=pl.ANY)],
        out_specs=pl.BlockSpec(memory_space=pl.ANY),
        scratch_shapes=[pltpu.VMEM((2, TOK_TILE, D_MODEL), F32), pltpu.SemaphoreType.DMA((2,))],
        input_output_aliases={2: 0},
        compiler_params=_cparams(("arbitrary",)),
        name="moe_dispatch",
    )(dest3, h2, xg0)


def _expert_kernel(be_ref, na_ref, x_ref, w1_ref, b1_ref, w2_ref, b2_ref, y_ref, w1s, w2s):
    i = pl.program_id(0)
    e = be_ref[i]
    prev = be_ref[jnp.maximum(i - 1, 0)]

    @pl.when((i == 0) | (e != prev))
    def _():
        rows = 128
        for r0 in range(0, D_MODEL, rows):
            w1s[r0:r0 + rows, :] = w1_ref[0, r0:r0 + rows, :].astype(BF16)
        for r0 in range(0, D_FF, rows):
            w2s[r0:r0 + rows, :] = w2_ref[0, r0:r0 + rows, :].astype(BF16)

    @pl.when(i < na_ref[0])
    def _():
        hmid = jnp.dot(x_ref[...].astype(BF16), w1s[...], preferred_element_type=F32) + b1_ref[0]
        glu = jnp.minimum(hmid[:, :D_FF], SWIGLU_LIMIT)
        lin = jnp.clip(hmid[:, D_FF:], -SWIGLU_LIMIT, SWIGLU_LIMIT)
        act = glu * jax.nn.sigmoid(SWIGLU_ALPHA * glu) * (lin + 1.0)
        y_ref[...] = jnp.dot(act.astype(BF16), w2s[...], preferred_element_type=F32) + b2_ref[0]


def _experts(be, nact, xg, w1, b1, w2, b2):
    n_slots = xg.shape[0]
    nb = n_slots // MOE_BM
    blk = lambda i, be_r, na_r: (jnp.minimum(i, na_r[0] - 1), 0)
    wsel = lambda i, be_r, na_r: (be_r[i], 0, 0)
    return pl.pallas_call(
        _expert_kernel,
        out_shape=jax.ShapeDtypeStruct((n_slots, D_MODEL), F32),
        grid_spec=pltpu.PrefetchScalarGridSpec(
            num_scalar_prefetch=2,
            grid=(nb,),
            in_specs=[pl.BlockSpec((MOE_BM, D_MODEL), blk),
                      pl.BlockSpec((1, D_MODEL, 2 * D_FF), wsel),
                      pl.BlockSpec((1, 1, 2 * D_FF), wsel),
                      pl.BlockSpec((1, D_FF, D_MODEL), wsel),
                      pl.BlockSpec((1, 1, D_MODEL), wsel)],
            out_specs=pl.BlockSpec((MOE_BM, D_MODEL), blk),
            scratch_shapes=[pltpu.VMEM((D_MODEL, 2 * D_FF), BF16), pltpu.VMEM((D_FF, D_MODEL), BF16)]),
        compiler_params=_cparams(("arbitrary",)),
        name="moe_experts",
    )(be, nact, xg, w1, b1, w2, b2)


def _combine_kernel(dcur_ref, dnext_ref, gate_ref, x1_ref, yg_ref, o_ref, rows, sem):
    i = pl.program_id(0)
    n = pl.num_programs(0)
    slot = i % 2

    def copy(dref, s, t, k):
        return pltpu.make_async_copy(yg_ref.at[pl.ds(dref[0, k, t], 1)], rows.at[s, k, pl.ds(t, 1)], sem.at[s])

    def issue(dref, s):
        def body(t, carry):
            for k in range(TOP_K):
                copy(dref, s, t, k).start()
            return carry
        lax.fori_loop(0, TOK_TILE, body, 0)

    @pl.when(i == 0)
    def _():
        issue(dcur_ref, 0)

    @pl.when(i + 1 < n)
    def _():
        issue(dnext_ref, 1 - slot)

    _row_copies_wait(lambda t: copy(dcur_ref, slot, 0, 0), TOP_K * TOK_TILE)

    eye = (lax.broadcasted_iota(I32, (TOK_TILE, TOK_TILE), 0)
           == lax.broadcasted_iota(I32, (TOK_TILE, TOK_TILE), 1)).astype(F32)
    gcol = _hdot_nt(eye, gate_ref[...])
    acc = x1_ref[...]
    for k in range(TOP_K):
        acc = acc + gcol[:, k:k + 1] * rows[slot, k]
    o_ref[...] = acc


def _combine(dest3, gates, x1, yg):
    t = x1.shape[0]
    nt = t // TOK_TILE
    return pl.pallas_call(
        _combine_kernel,
        out_shape=jax.ShapeDtypeStruct((t, D_MODEL), F32),
        grid=(nt,),
        in_specs=[pl.BlockSpec((1, TOP_K, TOK_TILE), lambda i: (i, 0, 0), memory_space=pltpu.SMEM),
                  pl.BlockSpec((1, TOP_K, TOK_TILE), lambda i: (jnp.minimum(i + 1, nt - 1), 0, 0),
                               memory_space=pltpu.SMEM),
                  pl.BlockSpec((TOP_K, TOK_TILE), lambda i: (0, i)),
                  pl.BlockSpec((TOK_TILE, D_MODEL), lambda i: (i, 0)),
                  pl.BlockSpec(memory_space=pl.ANY)],
        out_specs=pl.BlockSpec((TOK_TILE, D_MODEL), lambda i: (i, 0)),
        scratch_shapes=[pltpu.VMEM((2, TOP_K, TOK_TILE, D_MODEL), F32), pltpu.SemaphoreType.DMA((2,))],
        compiler_params=_cparams(("arbitrary",)),
        name="moe_combine",
    )(dest3, dest3, gates, x1, yg)


def _moe(x1, h2, logits_t, w1, b1, w2, b2):
    t = x1.shape[0]
    tn = 512 if t % 512 == 0 else TOK_TILE
    n_blocks = t * TOP_K // MOE_BM + N_EXPERTS
    nbp = -(-n_blocks // LANES) * LANES
    eidx, gates, rank, cnt = _route_topk(logits_t, tn)
    dest3 = _route_dest(eidx, rank, cnt, tn)
    be, nact = _route_blocks(cnt, nbp)
    xg = _dispatch(dest3, h2, n_blocks * MOE_BM)
    yg = _experts(be[0, :n_blocks], nact[0, :1], xg, w1, b1.reshape(N_EXPERTS, 1, -1), w2,
                  b2.reshape(N_EXPERTS, 1, -1))
    return _combine(dest3, gates, x1, yg)


def _pack_w_in(w):
    ab = jnp.pad(w[:, _IN_MAIN:_IN_MAIN + _IN_AB], ((0, 0), (0, LANES - _IN_AB)))
    return jnp.concatenate([w[:, :_IN_MAIN], ab, w[:, _IN_MAIN + _IN_AB:]], axis=1).astype(BF16)


def _lane_row(v):
    return jnp.pad(v.astype(F32), (0, LANES - v.shape[0])).reshape(1, LANES)


def _layer_weights(ln1_gain, w_in, q_norm_swa, k_norm_swa, swa_sinks, conv_w, a_log, dt_bias, gdn_norm,
                   q_norm_mem, w_branch, w_out, ln2_gain, w_router, b_router, w_mlp1, b_mlp1, w_mlp2, b_mlp2):
    return dict(
        ln1=ln1_gain.reshape(1, -1), wp=_pack_w_in(w_in),
        qn_s=q_norm_swa.reshape(1, -1), kn_s=k_norm_swa.reshape(1, -1), sinks=swa_sinks.reshape(1, -1),
        conv_w=conv_w, alog=_lane_row(a_log), dtb=_lane_row(dt_bias), gnorm=gdn_norm.reshape(1, -1),
        qn_m=q_norm_mem.reshape(1, -1), wb=w_branch.astype(BF16), wo=w_out.astype(BF16),
        ln2=ln2_gain.reshape(1, -1), wr_t=w_router.T, br=b_router.reshape(-1, 1),
        w1=w_mlp1, b1=b_mlp1, w2=w_mlp2, b2=b_mlp2)


def _finish(lw, oa, ob, oc, gate, x2, tm):
    x1, h2, logits_t = _merge(oa, ob, oc, gate, x2, lw["wb"], lw["wo"], lw["ln2"], lw["wr_t"], lw["br"], tm)
    return _moe(x1, h2, logits_t, lw["w1"], lw["b1"], lw["w2"], lw["b2"])


def _prompt_layer(x, mem, lw, mem_norm, w_mem_kv, k_norm_mem):
    b, l, d = x.shape
    x2 = x.reshape(b * l, d)
    tm = 256
    qs, ks, vs, qkvg, z, ab, qm, gate = _inproj(x2, lw["ln1"], lw["wp"], tm)
    mk, mv = _memkv(mem.reshape(b * MEM_LEN, d), mem_norm.reshape(1, -1), w_mem_kv.astype(BF16),
                    k_norm_mem.reshape(1, -1))
    r3 = lambda a: a.reshape(b, l, a.shape[-1])
    o_swa, kwin = _swa_prompt(r3(qs), r3(ks), r3(vs), lw["qn_s"], lw["kn_s"], lw["sinks"])
    o_g, s_fin = _gdn_prompt(r3(qkvg), r3(z), r3(ab), lw["conv_w"], lw["alog"], lw["dtb"], lw["gnorm"])
    o_m = _memattn_prompt(r3(qm), mk.reshape(b, MEM_LEN, BRANCH_W), mv.reshape(b, MEM_LEN, BRANCH_W), lw["qn_m"])
    y = _finish(lw, o_swa.reshape(b * l, -1), o_g.reshape(b * l, -1), o_m.reshape(b * l, -1), gate, x2, tm)
    new_k = kwin.reshape(b, WINDOW, SWA_KV_HEADS, SWA_HEAD_DIM)
    new_v = r3(vs)[:, l - WINDOW:].reshape(b, WINDOW, SWA_KV_HEADS, SWA_HEAD_DIM)
    new_conv = r3(qkvg)[:, l - (CONV_WIDTH - 1):]
    mk4 = mk.reshape(b, MEM_LEN, MEM_HEADS, MEM_HEAD_DIM)
    mv4 = mv.reshape(b, MEM_LEN, MEM_HEADS, MEM_HEAD_DIM)
    return y.reshape(b, l, d), new_k, new_v, s_fin, new_conv, mk4, mv4


def _sample_layer(x, cache_k, cache_v, state, conv_state, mem_k, mem_v, lw):
    b, l, d = x.shape
    x2 = x.reshape(b, d)
    tm = TOK_TILE
    qs, ks, vs, qkvg, z, ab, qm, gate = _inproj(x2, lw["ln1"], lw["wp"], tm)
    o_swa, new_k, new_v = _swa_decode(qs, ks, vs, cache_k, cache_v, lw["qn_s"], lw["kn_s"], lw["sinks"])
    o_g, new_s = _gdn_decode(qkvg, z, ab, state, conv_state, lw["conv_w"], lw["alog"], lw["dtb"], lw["gnorm"])
    o_m = _memattn_decode(qm, mem_k.reshape(b, MEM_LEN, BRANCH_W), mem_v.reshape(b, MEM_LEN, BRANCH_W), lw["qn_m"])
    y = _finish(lw, o_swa, o_g, o_m, gate, x2, tm)
    new_conv = jnp.concatenate([conv_state[:, 1:], qkvg[:, None, :]], axis=1)
    shp = (b, WINDOW, SWA_KV_HEADS, SWA_HEAD_DIM)
    return y.reshape(b, l, d), new_k.reshape(shp), new_v.reshape(shp), new_s, new_conv


def kernel(x_prompt, x_sample, cache_swa_k, cache_swa_v, state_gdn, state_conv, cache_mem_k, cache_mem_v, mem_prompt, ln1_gain, w_in, q_norm_swa, k_norm_swa, swa_sinks, conv_w, a_log, dt_bias, gdn_norm, q_norm_mem, k_norm_mem, mem_norm, w_mem_kv, w_branch, w_out, ln2_gain, w_router, b_router, w_mlp1, b_mlp1, w_mlp2, b_mlp2):
    xp, xs = x_prompt, x_sample
    outs = [[] for _ in range(10)]
    for l in range(ln1_gain.shape[0]):
        lw = _layer_weights(ln1_gain[l], w_in[l], q_norm_swa[l], k_norm_swa[l], swa_sinks[l], conv_w[l], a_log[l],
                            dt_bias[l], gdn_norm[l], q_norm_mem[l], w_branch[l], w_out[l], ln2_gain[l], w_router[l],
                            b_router[l], w_mlp1[l], b_mlp1[l], w_mlp2[l], b_mlp2[l])
        xp, pk, pv, ps, pc, mk, mv = _prompt_layer(xp, mem_prompt, lw, mem_norm[l], w_mem_kv[l], k_norm_mem[l])
        xs, sk, sv, ss, sc = _sample_layer(xs, cache_swa_k[l], cache_swa_v[l], state_gdn[l], state_conv[l],
                                           cache_mem_k[l], cache_mem_v[l], lw)
        for acc, v in zip(outs, (pk, pv, sk, sv, ps, ss, pc, sc, mk, mv)):
            acc.append(v)
    return (xp, xs) + tuple(jnp.stack(o) for o in outs)
```

```python
import functools

import numpy as np
import jax
import jax.numpy as jnp
from jax import lax
from jax.experimental import pallas as pl
from jax.experimental.pallas import tpu as pltpu

F32 = jnp.float32
BF16 = jnp.bfloat16
I32 = jnp.int32

D_MODEL = 1024
BRANCH_W = 512
SWA_HEADS = 8
SWA_KV_HEADS = 2
SWA_HEAD_DIM = 64
SWA_GROUP = SWA_HEADS // SWA_KV_HEADS
SWA_KV_W = SWA_KV_HEADS * SWA_HEAD_DIM
WINDOW = 128
GDN_HEADS = 4
GDN_HEAD_DIM = 128
GDN_W = GDN_HEADS * GDN_HEAD_DIM
GDN_CONV_W = 3 * GDN_W
GDN_CHUNK = 64
CONV_WIDTH = 4
MEM_LEN = 256
MEM_HEADS = 4
MEM_HEAD_DIM = 128
N_BRANCH = 3
N_EXPERTS = 32
TOP_K = 4
D_FF = 1024
SWIGLU_LIMIT = 7.0
SWIGLU_ALPHA = 1.702
EPS = 1e-6
PAST_LEN = 16384

LANES = 128
MOE_BM = 256
TOK_TILE = 128
VMEM_LIMIT = 56 * 1024 * 1024

_SEG_QS = (0, 512)
_SEG_KS = (512, 128)
_SEG_VS = (640, 128)
_SEG_QKVG = (768, 1536)
_SEG_Z = (2304, 512)
_SEG_AB = (2816, 128)
_SEG_QM = (2944, 512)
_SEG_GATE = (3456, 3072)
_PACKED_W = 6528
_IN_MAIN = 2816
_IN_AB = 8

_NEG = -1e30
_HI = lax.Precision.HIGHEST


def _cparams(sem, vmem=VMEM_LIMIT):
    return pltpu.CompilerParams(dimension_semantics=sem, vmem_limit_bytes=vmem)


def _bdot(a, b):
    return jnp.dot(a.astype(BF16), b.astype(BF16), preferred_element_type=F32)


def _bdot_nt(a, b):
    return lax.dot_general(a.astype(BF16), b.astype(BF16), (((1,), (1,)), ((), ())),
                           preferred_element_type=F32)


def _bdot_tn(a, b):
    return lax.dot_general(a.astype(BF16), b.astype(BF16), (((0,), (0,)), ((), ())),
                           preferred_element_type=F32)


def _hdot(a, b):
    return jnp.dot(a, b, preferred_element_type=F32, precision=_HI)


def _hdot_nt(a, b):
    return lax.dot_general(a, b, (((1,), (1,)), ((), ())), preferred_element_type=F32, precision=_HI)


def _rms(x, gain):
    return x * lax.rsqrt(jnp.mean(x * x, axis=-1, keepdims=True) + EPS) * gain


def _l2(x):
    return x * lax.rsqrt(jnp.sum(x * x, axis=-1, keepdims=True) + EPS)


def _silu(x):
    return x * jax.nn.sigmoid(x)


def _softplus(x):
    return jnp.maximum(x, 0.0) + jnp.log1p(jnp.exp(-jnp.abs(x)))


def _inproj_kernel(x_ref, g_ref, w_ref, qs, ks, vs, qkvg, z, ab, qm, gate):
    x = x_ref[...]
    hb = _rms(x, g_ref[...]).astype(BF16)
    for ref, (off, width) in ((qs, _SEG_QS), (ks, _SEG_KS), (vs, _SEG_VS), (qkvg, _SEG_QKVG), (z, _SEG_Z),
                              (ab, _SEG_AB), (qm, _SEG_QM), (gate, _SEG_GATE)):
        step = min(width, 512)
        for c0 in range(0, width, step):
            ref[:, c0:c0 + step] = jnp.dot(hb, w_ref[:, off + c0:off + c0 + step], preferred_element_type=F32)


def _inproj(x2, gain, wp, tm):
    t = x2.shape[0]
    segs = (_SEG_QS, _SEG_KS, _SEG_VS, _SEG_QKVG, _SEG_Z, _SEG_AB, _SEG_QM, _SEG_GATE)
    return pl.pallas_call(
        _inproj_kernel,
        out_shape=[jax.ShapeDtypeStruct((t, w), F32) for _, w in segs],
        grid=(t // tm,),
        in_specs=[pl.BlockSpec((tm, D_MODEL), lambda i: (i, 0)),
                  pl.BlockSpec((1, D_MODEL), lambda i: (0, 0)),
                  pl.BlockSpec((D_MODEL, _PACKED_W), lambda i: (0, 0))],
        out_specs=[pl.BlockSpec((tm, w), lambda i: (i, 0)) for _, w in segs],
        compiler_params=_cparams(("parallel",)),
        name="inproj",
    )(x2, gain, wp)


def _alibi_slopes(n):
    return [float(2.0 ** (-8.0 * (i + 1) / n)) for i in range(n)]


def _swa_prompt_kernel(q_ref, kc_ref, kp_ref, vc_ref, vp_ref, qn_ref, kn_ref, sink_ref, o_ref, kwin_ref):
    n = pl.program_id(1)
    q = q_ref[0]
    kc, kp, vc, vp = kc_ref[0], kp_ref[0], vc_ref[0], vp_ref[0]
    row = lax.broadcasted_iota(I32, (WINDOW, 2 * WINDOW), 0)
    col = lax.broadcasted_iota(I32, (WINDOW, 2 * WINDOW), 1)
    dist = row + WINDOW - col
    valid = (dist >= 0) & (dist <= WINDOW) & ((col >= WINDOW) | (n > 0))
    distf = dist.astype(F32)
    slopes = _alibi_slopes(SWA_HEADS)
    scale = SWA_HEAD_DIM ** -0.5
    kk, vv = [], []
    for g in range(SWA_KV_HEADS):
        sl = slice(g * SWA_HEAD_DIM, (g + 1) * SWA_HEAD_DIM)
        kcn = _rms(kc[:, sl], kn_ref[...])
        kpn = _rms(kp[:, sl], kn_ref[...])
        kwin_ref[0, :, sl] = kcn
        kk.append(jnp.concatenate([kpn, kcn], axis=0).astype(BF16))
        vv.append(jnp.concatenate([vp[:, sl], vc[:, sl]], axis=0).astype(BF16))
    heads = range(SWA_HEADS)
    hsl = lambda h: slice(h * SWA_HEAD_DIM, (h + 1) * SWA_HEAD_DIM)
    qn = [_rms(q[:, hsl(h)], qn_ref[...]) for h in heads]
    s = [jnp.where(valid, _bdot_nt(qn[h], kk[h // SWA_GROUP]) * scale - slopes[h] * distf, _NEG) for h in heads]
    sink = [sink_ref[0:1, h:h + 1] for h in heads]
    m = [jnp.maximum(jnp.max(s[h], axis=-1, keepdims=True), sink[h]) for h in heads]
    p = [jnp.exp(s[h] - m[h]) for h in heads]
    denom = [jnp.sum(p[h], axis=-1, keepdims=True) + jnp.exp(sink[h] - m[h]) for h in heads]
    pv = [_bdot(p[h], vv[h // SWA_GROUP]) for h in heads]
    for h in heads:
        o_ref[0, :, hsl(h)] = pv[h] / denom[h]


def _swa_prompt(qs, ks, vs, qn, kn, sinks):
    b, l, _ = qs.shape
    nb = l // WINDOW
    cur = lambda i, j: (i, j, 0)
    prev = lambda i, j: (i, jnp.maximum(j - 1, 0), 0)
    const2 = lambda i, j: (0, 0)
    return pl.pallas_call(
        _swa_prompt_kernel,
        out_shape=[jax.ShapeDtypeStruct((b, l, BRANCH_W), F32),
                   jax.ShapeDtypeStruct((b, WINDOW, SWA_KV_W), F32)],
        grid=(b, nb),
        in_specs=[pl.BlockSpec((1, WINDOW, BRANCH_W), cur),
                  pl.BlockSpec((1, WINDOW, SWA_KV_W), cur),
                  pl.BlockSpec((1, WINDOW, SWA_KV_W), prev),
                  pl.BlockSpec((1, WINDOW, SWA_KV_W), cur),
                  pl.BlockSpec((1, WINDOW, SWA_KV_W), prev),
                  pl.BlockSpec((1, SWA_HEAD_DIM), const2),
                  pl.BlockSpec((1, SWA_HEAD_DIM), const2),
                  pl.BlockSpec((1, SWA_HEADS), const2)],
        out_specs=[pl.BlockSpec((1, WINDOW, BRANCH_W), cur),
                   pl.BlockSpec((1, WINDOW, SWA_KV_W), lambda i, j: (i, 0, 0))],
        compiler_params=_cparams(("parallel", "arbitrary")),
        name="swa_prompt",
    )(qs, ks, ks, vs, vs, qn, kn, sinks)


def _swa_decode_kernel(bs, q_ref, k3_ref, kf_ref, v3_ref, vf_ref, ck_ref, cv_ref, qn_ref, kn_ref, kn2_ref,
                       sink_ref, slope_ref, o_ref, ok_ref, ov_ref):
    scale = SWA_HEAD_DIM ** -0.5
    lane = lax.broadcasted_iota(I32, (1, SWA_KV_W), 1)
    rowi = lax.broadcasted_iota(I32, (WINDOW, SWA_KV_W), 0)
    keyd = (WINDOW - lax.broadcasted_iota(I32, (1, WINDOW), 1)).astype(F32)
    for b in range(bs):
        qn = _rms(q_ref[b], qn_ref[...])
        kn3 = _rms(k3_ref[b], kn_ref[...])
        v3 = v3_ref[b]
        kf = kf_ref[b]
        sq = kf * kf
        ms0 = jnp.sum(jnp.where(lane < SWA_HEAD_DIM, sq, 0.0), axis=-1, keepdims=True) / SWA_HEAD_DIM
        ms1 = jnp.sum(jnp.where(lane >= SWA_HEAD_DIM, sq, 0.0), axis=-1, keepdims=True) / SWA_HEAD_DIM
        knf = kf * lax.rsqrt(jnp.where(lane < SWA_HEAD_DIM, ms0, ms1) + EPS) * kn2_ref[...]
        ck = ck_ref[b]
        cv = cv_ref[b]
        for g in range(SWA_KV_HEADS):
            sl = slice(g * SWA_HEAD_DIM, (g + 1) * SWA_HEAD_DIM)
            hs = slice(g * SWA_GROUP, (g + 1) * SWA_GROUP)
            qg = qn[hs]
            slope = slope_ref[hs]
            sink = sink_ref[hs]
            s = _bdot_nt(qg, ck[:, sl]) * scale - slope * keyd
            s_new = jnp.sum(qg * kn3[g:g + 1], axis=-1, keepdims=True) * scale
            m = jnp.maximum(jnp.maximum(jnp.max(s, axis=-1, keepdims=True), s_new), sink)
            p = jnp.exp(s - m)
            p_new = jnp.exp(s_new - m)
            denom = jnp.sum(p, axis=-1, keepdims=True) + p_new + jnp.exp(sink - m)
            o_ref[b, hs, :] = (_bdot(p, cv[:, sl]) + p_new * v3[g:g + 1]) / denom
        last = rowi == WINDOW - 1
        ok_ref[b] = jnp.where(last, knf, pltpu.roll(ck, WINDOW - 1, 0))
        ov_ref[b] = jnp.where(last, vf_ref[b], pltpu.roll(cv, WINDOW - 1, 0))


def _swa_decode(qs, ks, vs, cache_k, cache_v, qn, kn, sinks, bs=8):
    b = qs.shape[0]
    q3 = qs.reshape(b, SWA_HEADS, SWA_HEAD_DIM)
    k3 = ks.reshape(b, SWA_KV_HEADS, SWA_HEAD_DIM)
    kf = ks.reshape(b, 1, SWA_KV_W)
    v3 = vs.reshape(b, SWA_KV_HEADS, SWA_HEAD_DIM)
    vf = vs.reshape(b, 1, SWA_KV_W)
    ck = cache_k.reshape(b, WINDOW, SWA_KV_W)
    cv = cache_v.reshape(b, WINDOW, SWA_KV_W)
    kn2 = jnp.concatenate([kn, kn], axis=-1)
    sink_col = sinks.reshape(SWA_HEADS, 1)
    slope_col = jnp.asarray(np.asarray(_alibi_slopes(SWA_HEADS), np.float32).reshape(SWA_HEADS, 1))
    blk = lambda *shape: pl.BlockSpec((bs,) + shape, lambda i: (i,) + (0,) * len(shape))
    full = lambda *shape: pl.BlockSpec(shape, lambda i: (0,) * len(shape))
    o, ok, ov = pl.pallas_call(
        functools.partial(_swa_decode_kernel, bs),
        out_shape=[jax.ShapeDtypeStruct((b, SWA_HEADS, SWA_HEAD_DIM), F32),
                   jax.ShapeDtypeStruct((b, WINDOW, SWA_KV_W), F32),
                   jax.ShapeDtypeStruct((b, WINDOW, SWA_KV_W), F32)],
        grid=(b // bs,),
        in_specs=[blk(SWA_HEADS, SWA_HEAD_DIM), blk(SWA_KV_HEADS, SWA_HEAD_DIM), blk(1, SWA_KV_W),
                  blk(SWA_KV_HEADS, SWA_HEAD_DIM), blk(1, SWA_KV_W), blk(WINDOW, SWA_KV_W), blk(WINDOW, SWA_KV_W),
                  full(1, SWA_HEAD_DIM), full(1, SWA_HEAD_DIM), full(1, SWA_KV_W),
                  full(SWA_HEADS, 1), full(SWA_HEADS, 1)],
        out_specs=[blk(SWA_HEADS, SWA_HEAD_DIM), blk(WINDOW, SWA_KV_W), blk(WINDOW, SWA_KV_W)],
        compiler_params=_cparams(("parallel",)),
        name="swa_decode",
    )(q3, k3, kf, v3, vf, ck, cv, qn, kn, kn2, sink_col, slope_col)
    return o.reshape(b, BRANCH_W), ok, ov


def _gate_rows(ab, alog_ref, dtb_ref):
    g = -jnp.exp(alog_ref[...]) * _softplus(ab + dtb_ref[...])
    return g, jax.nn.sigmoid(ab)


GDN_PREP_CHUNKS = 4
GDN_SCAN_BATCH = 2
_HALO = 8


def _gdn_prep_kernel(x_ref, xh_ref, ab_ref, cw_ref, alog_ref, dtb_ref,
                     u_ref, w_ref, qg_ref, kd_ref, in_ref, eg_ref, xp, ys):
    j = pl.program_id(1)
    C = GDN_CHUNK
    d = GDN_HEAD_DIM
    n = GDN_PREP_CHUNKS * C
    heads = range(GDN_HEADS)
    xp[0:_HALO, :] = jnp.where(j > 0, xh_ref[0], 0.0)
    xp[_HALO:, :] = x_ref[0]
    for ci in range(GDN_PREP_CHUNKS):
        r0 = ci * C
        y = xp[_HALO + r0:_HALO + r0 + C, :] * cw_ref[CONV_WIDTH - 1:CONV_WIDTH, :]
        for jj in range(CONV_WIDTH - 1):
            sh = CONV_WIDTH - 1 - jj
            y = y + xp[_HALO + r0 - sh:_HALO + r0 - sh + C, :] * cw_ref[jj:jj + 1, :]
        ys[r0:r0 + C, :] = _silu(y)
    g_all, beta_all = _gate_rows(ab_ref[0], alog_ref, dtb_ref)
    r = lax.broadcasted_iota(I32, (n, n), 0)
    cc = lax.broadcasted_iota(I32, (n, n), 1)
    same = (r // C) == (cc // C)
    incl = same & (r >= cc)
    strict = same & (r > cc)
    upto = same & (r <= cc)
    eye = r == cc
    q = [_l2(ys[:, h * d:(h + 1) * d]) * (d ** -0.5) for h in heads]
    k = [_l2(ys[:, GDN_W + h * d:GDN_W + (h + 1) * d]) for h in heads]
    beta = [beta_all[:, GDN_HEADS + h:GDN_HEADS + h + 1] for h in heads]
    gc_row = [jnp.sum(jnp.where(upto, g_all[:, h:h + 1], 0.0), axis=0, keepdims=True) for h in heads]
    gc_col = [jnp.sum(jnp.where(eye, gc_row[h], 0.0), axis=1, keepdims=True) for h in heads]
    decay = [jnp.where(incl, jnp.exp(jnp.where(incl, gc_col[h] - gc_row[h], 0.0)), 0.0) for h in heads]
    kb = [k[h] * beta[h] for h in heads]
    bp = [-jnp.where(strict, _bdot_nt(kb[h], k[h]) * decay[h], 0.0) for h in heads]
    intra = [jnp.where(incl, _bdot_nt(q[h], k[h]) * decay[h], 0.0) for h in heads]
    p = [eye.astype(F32) + bp[h] for h in heads]
    span = 2
    while span < C:
        bp = [_bdot(bp[h], bp[h]) for h in heads]
        p = [p[h] + _bdot(p[h], bp[h]) for h in heads]
        span *= 2
    uw = [_bdot(p[h], jnp.concatenate([ys[:, 2 * GDN_W + h * d:2 * GDN_W + (h + 1) * d] * beta[h],
                                       kb[h] * jnp.exp(gc_col[h])], axis=1)) for h in heads]
    for h in heads:
        hs = slice(h * d, (h + 1) * d)
        u_ref[0, :, hs] = uw[h][:, :d]
        w_ref[0, :, hs] = uw[h][:, d:].astype(BF16)
        qg_ref[0, :, hs] = (q[h] * jnp.exp(gc_col[h])).astype(BF16)
        for ci in range(GDN_PREP_CHUNKS):
            r0 = ci * C
            g_last = gc_row[h][:, r0 + C - 1:r0 + C]
            kd_ref[0, r0:r0 + C, hs] = (k[h][r0:r0 + C] * jnp.exp(g_last - gc_col[h][r0:r0 + C])).astype(BF16)
            in_ref[0, r0:r0 + C, h * C:(h + 1) * C] = intra[h][r0:r0 + C, r0:r0 + C].astype(BF16)
            eg_ref[0, ci, h:h + 1, :] = jnp.broadcast_to(jnp.exp(g_last), (1, LANES))


def _gdn_scan_kernel(u_ref, w_ref, qg_ref, kd_ref, in_ref, eg_ref, z_ref, gn_ref, o_ref, s_ref, st):
    c = pl.program_id(1)
    C = GDN_CHUNK
    d = GDN_HEAD_DIM

    @pl.when(c == 0)
    def _():
        st[...] = jnp.zeros_like(st)

    chains = [(bi, h) for bi in range(GDN_SCAN_BATCH) for h in range(GDN_HEADS)]
    hsl = lambda h: slice(h * d, (h + 1) * d)
    s_f = [st[bi, h] for bi, h in chains]
    s_b = [s.astype(BF16) for s in s_f]
    ws = [jnp.dot(w_ref[bi, :, hsl(h)], s_b[i], preferred_element_type=F32) for i, (bi, h) in enumerate(chains)]
    qs = [jnp.dot(qg_ref[bi, :, hsl(h)], s_b[i], preferred_element_type=F32) for i, (bi, h) in enumerate(chains)]
    vb = [(u_ref[bi, :, hsl(h)] - ws[i]).astype(BF16) for i, (bi, h) in enumerate(chains)]
    iv = [jnp.dot(in_ref[bi, :, h * C:(h + 1) * C], vb[i], preferred_element_type=F32)
          for i, (bi, h) in enumerate(chains)]
    kv = [lax.dot_general(kd_ref[bi, :, hsl(h)], vb[i], (((0,), (0,)), ((), ())), preferred_element_type=F32)
          for i, (bi, h) in enumerate(chains)]
    for i, (bi, h) in enumerate(chains):
        st[bi, h] = s_f[i] * eg_ref[bi, 0, h:h + 1, :] + kv[i]
        o_ref[bi, :, hsl(h)] = _rms(qs[i] + iv[i], gn_ref[...]) * _silu(z_ref[bi, :, hsl(h)])
    s_ref[...] = st[...]


def _gdn_prompt(qkvg, z, ab, conv_w, alog, dtb, gnorm):
    b, l, _ = qkvg.shape
    C = GDN_CHUNK
    nc = l // C
    rows = GDN_PREP_CHUNKS * C
    cur = lambda i, j: (i, j, 0)
    const2 = lambda i, j: (0, 0)
    u, w, qg, kd, intra, eg = pl.pallas_call(
        _gdn_prep_kernel,
        out_shape=[jax.ShapeDtypeStruct((b, l, GDN_W), F32),
                   jax.ShapeDtypeStruct((b, l, GDN_W), BF16),
                   jax.ShapeDtypeStruct((b, l, GDN_W), BF16),
                   jax.ShapeDtypeStruct((b, l, GDN_W), BF16),
                   jax.ShapeDtypeStruct((b, l, GDN_HEADS * C), BF16),
                   jax.ShapeDtypeStruct((b, nc, GDN_HEADS, LANES), F32)],
        grid=(b, l // rows),
        in_specs=[pl.BlockSpec((1, rows, GDN_CONV_W), cur),
                  pl.BlockSpec((1, _HALO, GDN_CONV_W),
                               lambda i, j: (i, jnp.maximum(j * (rows // _HALO) - 1, 0), 0)),
                  pl.BlockSpec((1, rows, LANES), cur),
                  pl.BlockSpec((CONV_WIDTH, GDN_CONV_W), const2),
                  pl.BlockSpec((1, LANES), const2),
                  pl.BlockSpec((1, LANES), const2)],
        out_specs=[pl.BlockSpec((1, rows, GDN_W), cur),
                   pl.BlockSpec((1, rows, GDN_W), cur),
                   pl.BlockSpec((1, rows, GDN_W), cur),
                   pl.BlockSpec((1, rows, GDN_W), cur),
                   pl.BlockSpec((1, rows, GDN_HEADS * C), cur),
                   pl.BlockSpec((1, GDN_PREP_CHUNKS, GDN_HEADS, LANES), lambda i, j: (i, j, 0, 0))],
        scratch_shapes=[pltpu.VMEM((_HALO + rows, GDN_CONV_W), F32), pltpu.VMEM((rows, GDN_CONV_W), F32)],
        compiler_params=_cparams(("parallel", "parallel")),
        name="gdn_prep",
    )(qkvg, qkvg, ab, conv_w, alog, dtb)
    bb = GDN_SCAN_BATCH
    seq = lambda wd: pl.BlockSpec((bb, C, wd), cur)
    return pl.pallas_call(
        _gdn_scan_kernel,
        out_shape=[jax.ShapeDtypeStruct((b, l, GDN_W), F32),
                   jax.ShapeDtypeStruct((b, GDN_HEADS, GDN_HEAD_DIM, GDN_HEAD_DIM), F32)],
        grid=(b // bb, nc),
        in_specs=[seq(GDN_W), seq(GDN_W), seq(GDN_W), seq(GDN_W), seq(GDN_HEADS * C),
                  pl.BlockSpec((bb, 1, GDN_HEADS, LANES), lambda i, j: (i, j, 0, 0)),
                  seq(GDN_W),
                  pl.BlockSpec((1, GDN_HEAD_DIM), const2)],
        out_specs=[seq(GDN_W),
                   pl.BlockSpec((bb, GDN_HEADS, GDN_HEAD_DIM, GDN_HEAD_DIM), lambda i, j: (i, 0, 0, 0))],
        scratch_shapes=[pltpu.VMEM((bb, GDN_HEADS, GDN_HEAD_DIM, GDN_HEAD_DIM), F32)],
        compiler_params=_cparams(("parallel", "arbitrary")),
        name="gdn_scan",
    )(u, w, qg, kd, intra, eg, z, gnorm)


def _gdn_decode_kernel(bs, x_ref, cs_ref, z_ref, ab_ref, s_ref, cw_ref, alog_ref, dtb_ref, gn_ref, o_ref, so_ref):
    x = x_ref[...]
    y = x * cw_ref[CONV_WIDTH - 1:CONV_WIDTH, :]
    for j in range(CONV_WIDTH - 1):
        y = y + cs_ref[:, j, :] * cw_ref[j:j + 1, :]
    y = _silu(y)
    g_all, beta_all = _gate_rows(ab_ref[...], alog_ref, dtb_ref)
    eg_all = jnp.exp(g_all)
    z = z_ref[...]
    d = GDN_HEAD_DIM
    eye = (lax.broadcasted_iota(I32, (d, d), 0) == lax.broadcasted_iota(I32, (d, d), 1)).astype(F32)
    for h in range(GDN_HEADS):
        hs = slice(h * d, (h + 1) * d)
        qh = _l2(y[:, h * d:(h + 1) * d]) * (d ** -0.5)
        kh = _l2(y[:, GDN_W + h * d:GDN_W + (h + 1) * d])
        vh = y[:, 2 * GDN_W + h * d:2 * GDN_W + (h + 1) * d]
        kt = _hdot_nt(eye, kh)
        qt = _hdot_nt(eye, qh)
        outs = []
        for b in range(bs):
            s_b = s_ref[b, h]
            kc = kt[:, b:b + 1]
            eg = eg_all[b:b + 1, h:h + 1]
            beta = beta_all[b:b + 1, GDN_HEADS + h:GDN_HEADS + h + 1]
            sk = jnp.sum(s_b * kc, axis=0, keepdims=True)
            v_new = beta * (vh[b:b + 1] - eg * sk)
            s_new = eg * s_b + kc * v_new
            so_ref[b, h] = s_new
            outs.append(jnp.sum(s_new * qt[:, b:b + 1], axis=0, keepdims=True))
        o = jnp.concatenate(outs, axis=0)
        o_ref[:, hs] = _rms(o, gn_ref[...]) * _silu(z[:, hs])


def _gdn_decode(qkvg, z, ab, state, conv_state, conv_w, alog, dtb, gnorm, bs=8):
    b = qkvg.shape[0]
    row = lambda w: pl.BlockSpec((bs, w), lambda i: (i, 0))
    full = lambda *shape: pl.BlockSpec(shape, lambda i: (0,) * len(shape))
    sspec = pl.BlockSpec((bs, GDN_HEADS, GDN_HEAD_DIM, GDN_HEAD_DIM), lambda i: (i, 0, 0, 0))
    return pl.pallas_call(
        functools.partial(_gdn_decode_kernel, bs),
        out_shape=[jax.ShapeDtypeStruct((b, GDN_W), F32),
                   jax.ShapeDtypeStruct(state.shape, F32)],
        grid=(b // bs,),
        in_specs=[row(GDN_CONV_W),
                  pl.BlockSpec((bs, CONV_WIDTH - 1, GDN_CONV_W), lambda i: (i, 0, 0)),
                  row(GDN_W), row(LANES), sspec,
                  full(CONV_WIDTH, GDN_CONV_W), full(1, LANES), full(1, LANES), full(1, GDN_HEAD_DIM)],
        out_specs=[row(GDN_W), sspec],
        compiler_params=_cparams(("parallel",)),
        name="gdn_decode",
    )(qkvg, conv_state, z, ab, state, conv_w, alog, dtb, gnorm)


def _memkv_kernel(x_ref, g_ref, w_ref, kn_ref, k_ref, v_ref):
    hb = _rms(x_ref[...], g_ref[...]).astype(BF16)
    for hd in range(MEM_HEADS):
        hs = slice(hd * MEM_HEAD_DIM, (hd + 1) * MEM_HEAD_DIM)
        k_ref[:, hs] = _rms(jnp.dot(hb, w_ref[:, hs], preferred_element_type=F32), kn_ref[...])
    v_ref[...] = jnp.dot(hb, w_ref[:, BRANCH_W:], preferred_element_type=F32)


def _memkv(mem2, gain, w_kv, kn, tm=512):
    t = mem2.shape[0]
    return pl.pallas_call(
        _memkv_kernel,
        out_shape=[jax.ShapeDtypeStruct((t, BRANCH_W), F32)] * 2,
        grid=(t // tm,),
        in_specs=[pl.BlockSpec((tm, D_MODEL), lambda i: (i, 0)),
                  pl.BlockSpec((1, D_MODEL), lambda i: (0, 0)),
                  pl.BlockSpec((D_MODEL, 2 * BRANCH_W), lambda i: (0, 0)),
                  pl.BlockSpec((1, MEM_HEAD_DIM), lambda i: (0, 0))],
        out_specs=[pl.BlockSpec((tm, BRANCH_W), lambda i: (i, 0))] * 2,
        compiler_params=_cparams(("parallel",)),
        name="memkv",
    )(mem2, gain, w_kv, kn)


def _memattn_prompt_kernel(q_ref, k_ref, v_ref, qn_ref, o_ref):
    q = q_ref[0]
    k = k_ref[0]
    v = v_ref[0]
    scale = MEM_HEAD_DIM ** -0.5
    for hd in range(MEM_HEADS):
        hs = slice(hd * MEM_HEAD_DIM, (hd + 1) * MEM_HEAD_DIM)
        s = _bdot_nt(_rms(q[:, hs], qn_ref[...]), k[:, hs]) * scale
        p = jnp.exp(s - jnp.max(s, axis=-1, keepdims=True))
        o_ref[0, :, hs] = _bdot(p, v[:, hs]) / jnp.sum(p, axis=-1, keepdims=True)


def _memattn_prompt(qm, mk, mv, qn):
    b, l, _ = qm.shape
    tq = 512 if l % 512 == 0 else WINDOW
    return pl.pallas_call(
        _memattn_prompt_kernel,
        out_shape=jax.ShapeDtypeStruct((b, l, BRANCH_W), F32),
        grid=(b, l // tq),
        in_specs=[pl.BlockSpec((1, tq, BRANCH_W), lambda i, j: (i, j, 0)),
                  pl.BlockSpec((1, MEM_LEN, BRANCH_W), lambda i, j: (i, 0, 0)),
                  pl.BlockSpec((1, MEM_LEN, BRANCH_W), lambda i, j: (i, 0, 0)),
                  pl.BlockSpec((1, MEM_HEAD_DIM), lambda i, j: (0, 0))],
        out_specs=pl.BlockSpec((1, tq, BRANCH_W), lambda i, j: (i, j, 0)),
        compiler_params=_cparams(("parallel", "parallel")),
        name="memattn_prompt",
    )(qm, mk, mv, qn)


def _memattn_decode_kernel(bs, q_ref, k_ref, v_ref, qn_ref, o_ref):
    scale = MEM_HEAD_DIM ** -0.5
    q = q_ref[...]
    for hd in range(MEM_HEADS):
        hs = slice(hd * MEM_HEAD_DIM, (hd + 1) * MEM_HEAD_DIM)
        qn = _rms(q[:, hs], qn_ref[...])
        outs = []
        for b in range(bs):
            s = jnp.sum(k_ref[b, :, hs] * qn[b:b + 1], axis=-1, keepdims=True) * scale
            p = jnp.exp(s - jnp.max(s, axis=0, keepdims=True))
            outs.append(jnp.sum(p * v_ref[b, :, hs], axis=0, keepdims=True) / jnp.sum(p, axis=0, keepdims=True))
        o_ref[:, hs] = jnp.concatenate(outs, axis=0)


def _memattn_decode(qm, ck, cv, qn, bs=8):
    b = qm.shape[0]
    kv = pl.BlockSpec((bs, MEM_LEN, BRANCH_W), lambda i: (i, 0, 0))
    return pl.pallas_call(
        functools.partial(_memattn_decode_kernel, bs),
        out_shape=jax.ShapeDtypeStruct((b, BRANCH_W), F32),
        grid=(b // bs,),
        in_specs=[pl.BlockSpec((bs, BRANCH_W), lambda i: (i, 0)), kv, kv,
                  pl.BlockSpec((1, MEM_HEAD_DIM), lambda i: (0, 0))],
        out_specs=pl.BlockSpec((bs, BRANCH_W), lambda i: (i, 0)),
        compiler_params=_cparams(("parallel",)),
        name="memattn_decode",
    )(qm, ck, cv, qn)


def _merge_kernel(oa_ref, ob_ref, oc_ref, gate_ref, x_ref, wb_ref, wo_ref, g2_ref, wr_ref, br_ref,
                  x1_ref, h2_ref, lg_ref):
    acc = None
    for i, o_ref in enumerate((oa_ref, ob_ref, oc_ref)):
        mixed = jnp.dot(o_ref[...].astype(BF16), wb_ref[i], preferred_element_type=F32)
        term = jax.nn.sigmoid(gate_ref[:, i * D_MODEL:(i + 1) * D_MODEL]) * mixed
        acc = term if acc is None else acc + term
    x1 = x_ref[...] + jnp.dot(acc.astype(BF16), wo_ref[...], preferred_element_type=F32)
    x1_ref[...] = x1
    h2 = _rms(x1, g2_ref[...])
    h2_ref[...] = h2
    lg_ref[...] = _hdot_nt(wr_ref[...], h2) + br_ref[...]


def _merge(oa, ob, oc, gate, x2, wb, wo, g2, wr_t, br_col, tm):
    t = x2.shape[0]
    row = lambda w: pl.BlockSpec((tm, w), lambda i: (i, 0))
    full = lambda *shape: pl.BlockSpec(shape, lambda i: (0,) * len(shape))
    return pl.pallas_call(
        _merge_kernel,
        out_shape=[jax.ShapeDtypeStruct((t, D_MODEL), F32),
                   jax.ShapeDtypeStruct((t, D_MODEL), F32),
                   jax.ShapeDtypeStruct((N_EXPERTS, t), F32)],
        grid=(t // tm,),
        in_specs=[row(BRANCH_W), row(BRANCH_W), row(BRANCH_W), row(N_BRANCH * D_MODEL), row(D_MODEL),
                  full(N_BRANCH, BRANCH_W, D_MODEL), full(D_MODEL, D_MODEL), full(1, D_MODEL),
                  full(N_EXPERTS, D_MODEL), full(N_EXPERTS, 1)],
        out_specs=[row(D_MODEL), row(D_MODEL), pl.BlockSpec((N_EXPERTS, tm), lambda i: (0, i))],
        compiler_params=_cparams(("parallel",)),
        name="merge",
    )(oa, ob, oc, gate, x2, wb, wo, g2, wr_t, br_col)


def _route_topk_kernel(lg_ref, eidx_ref, gate_ref, rank_ref, cnt_ref, carry):
    i = pl.program_id(0)
    tn = lg_ref.shape[1]

    @pl.when(i == 0)
    def _():
        carry[...] = jnp.zeros_like(carry)

    l = lg_ref[...]
    eio = lax.broadcasted_iota(I32, (N_EXPERTS, tn), 0)
    hot = jnp.zeros((N_EXPERTS, tn), F32)
    vals, idxs = [], []
    for _ in range(TOP_K):
        m = jnp.max(l, axis=0, keepdims=True)
        idx = jnp.min(jnp.where(l == m, eio, N_EXPERTS), axis=0, keepdims=True)
        sel = eio == idx
        vals.append(m)
        idxs.append(idx)
        hot = hot + sel.astype(F32)
        l = jnp.where(sel, -jnp.inf, l)
    ex = [jnp.exp(v - vals[0]) for v in vals]
    tot = ex[0] + ex[1] + ex[2] + ex[3]
    before = (lax.broadcasted_iota(I32, (tn, tn), 0) < lax.broadcasted_iota(I32, (tn, tn), 1)).astype(BF16)
    rank = jnp.dot(hot.astype(BF16), before, preferred_element_type=F32) + carry[:, 0:1]
    carry[...] = carry[...] + jnp.sum(hot, axis=1, keepdims=True)
    for k in range(TOP_K):
        eidx_ref[k:k + 1, :] = idxs[k]
        gate_ref[k:k + 1, :] = ex[k] / tot
        rank_ref[k:k + 1, :] = jnp.sum(jnp.where(eio == idxs[k], rank, 0.0), axis=0, keepdims=True).astype(I32)
    cnt_ref[...] = carry[...]


def _route_topk(logits_t, tn):
    t = logits_t.shape[1]
    kt = pl.BlockSpec((TOP_K, tn), lambda i: (0, i))
    return pl.pallas_call(
        _route_topk_kernel,
        out_shape=[jax.ShapeDtypeStruct((TOP_K, t), I32),
                   jax.ShapeDtypeStruct((TOP_K, t), F32),
                   jax.ShapeDtypeStruct((TOP_K, t), I32),
                   jax.ShapeDtypeStruct((N_EXPERTS, LANES), F32)],
        grid=(t // tn,),
        in_specs=[pl.BlockSpec((N_EXPERTS, tn), lambda i: (0, i))],
        out_specs=[kt, kt, kt, pl.BlockSpec((N_EXPERTS, LANES), lambda i: (0, 0))],
        scratch_shapes=[pltpu.VMEM((N_EXPERTS, LANES), F32)],
        compiler_params=_cparams(("arbitrary",)),
        name="route_topk",
    )(logits_t)


def _expert_offsets(cnt_col):
    padded = jnp.floor((cnt_col + (MOE_BM - 1)) * (1.0 / MOE_BM)) * MOE_BM
    r = lax.broadcasted_iota(I32, (N_EXPERTS, N_EXPERTS), 0)
    c = lax.broadcasted_iota(I32, (N_EXPERTS, N_EXPERTS), 1)
    ends_row = jnp.sum(jnp.where(r <= c, padded, 0.0), axis=0, keepdims=True)
    ends_col = jnp.sum(jnp.where(r == c, ends_row, 0.0), axis=1, keepdims=True)
    return ends_col, ends_col - padded


def _route_dest_kernel(eidx_ref, rank_ref, cnt_ref, dest_ref):
    tn = eidx_ref.shape[1]
    _, starts = _expert_offsets(cnt_ref[:, 0:1])
    eio = lax.broadcasted_iota(I32, (N_EXPERTS, tn), 0)
    for k in range(TOP_K):
        base = jnp.sum(jnp.where(eio == eidx_ref[k:k + 1, :], starts, 0.0), axis=0, keepdims=True)
        d = base.astype(I32) + rank_ref[k:k + 1, :]
        for j in range(tn // TOK_TILE):
            dest_ref[j, k:k + 1, :] = d[:, j * TOK_TILE:(j + 1) * TOK_TILE]


def _route_dest(eidx, rank, cnt, tn):
    t = eidx.shape[1]
    kt = pl.BlockSpec((TOP_K, tn), lambda i: (0, i))
    return pl.pallas_call(
        _route_dest_kernel,
        out_shape=jax.ShapeDtypeStruct((t // TOK_TILE, TOP_K, TOK_TILE), I32),
        grid=(t // tn,),
        in_specs=[kt, kt, pl.BlockSpec((N_EXPERTS, LANES), lambda i: (0, 0))],
        out_specs=pl.BlockSpec((tn // TOK_TILE, TOP_K, TOK_TILE), lambda i: (i, 0, 0)),
        compiler_params=_cparams(("parallel",)),
        name="route_dest",
    )(eidx, rank, cnt)


def _route_blocks_kernel(cnt_ref, be_ref, na_ref):
    nbp = be_ref.shape[1]
    ends, _ = _expert_offsets(cnt_ref[:, 0:1])
    nact = ends[N_EXPERTS - 1:N_EXPERTS, :] * (1.0 / MOE_BM)
    blk = jnp.minimum(lax.broadcasted_iota(I32, (1, nbp), 1).astype(F32), nact - 1.0)
    be = jnp.sum((ends <= blk * MOE_BM).astype(F32), axis=0, keepdims=True)
    be_ref[...] = jnp.minimum(be, N_EXPERTS - 1.0).astype(I32)
    na_ref[...] = jnp.broadcast_to(nact, na_ref.shape).astype(I32)


def _route_blocks(cnt, nbp):
    return pl.pallas_call(
        _route_blocks_kernel,
        out_shape=[jax.ShapeDtypeStruct((1, nbp), I32), jax.ShapeDtypeStruct((1, LANES), I32)],
        name="route_blocks",
    )(cnt)


_ISSUE_UNROLL = 8


def _dispatch_kernel(dest_ref, h_ref, xg_in, xg_ref, buf, sem):
    del xg_in
    i = pl.program_id(0)
    n = pl.num_programs(0)
    slot = i % 2

    def copy(s, t, k):
        return pltpu.make_async_copy(buf.at[s, pl.ds(t, 1)], xg_ref.at[pl.ds(dest_ref[0, k, t], 1)], sem.at[s])

    def drain(s):
        for _ in range(TOP_K):
            pltpu.make_async_copy(buf.at[s], xg_ref.at[pl.ds(0, TOK_TILE)], sem.at[s]).wait()

    @pl.when(i >= 2)
    def _():
        drain(slot)

    buf[slot] = h_ref[...]

    def issue(t, carry):
        for k in range(TOP_K):
            copy(slot, t, k).start(priority=k % 2)
        return carry
    lax.fori_loop(0, TOK_TILE, issue, 0, unroll=_ISSUE_UNROLL)

    @pl.when(i == n - 1)
    def _():
        drain(slot)

        @pl.when(n >= 2)
        def _():
            drain(1 - slot)


def _dispatch(dest3, h2, n_slots):
    t = h2.shape[0]
    xg0 = jnp.zeros((n_slots, D_MODEL), F32)
    return pl.pallas_call(
        _dispatch_kernel,
        out_shape=jax.ShapeDtypeStruct((n_slots, D_MODEL), F32),
        grid=(t // TOK_TILE,),
        in_specs=[pl.BlockSpec((1, TOP_K, TOK_TILE), lambda i: (i, 0, 0), memory_space=pltpu.SMEM),
                  pl.BlockSpec((TOK_TILE, D_MODEL), lambda i: (i, 0)),
                  pl.BlockSpec(memory_space=pl.ANY)],
        out_specs=pl.BlockSpec(memory_space=pl.ANY),
        scratch_shapes=[pltpu.VMEM((2, TOK_TILE, D_MODEL), F32), pltpu.SemaphoreType.DMA((2,))],
        input_output_aliases={2: 0},
        compiler_params=_cparams(("arbitrary",)),
        name="moe_dispatch",
    )(dest3, h2, xg0)


def _expert_kernel(be_ref, na_ref, x_ref, w1_ref, b1_ref, w2_ref, b2_ref, y_ref, w1s, w2s):
    i = pl.program_id(0)
    e = be_ref[i]
    prev = be_ref[jnp.maximum(i - 1, 0)]

    @pl.when((i == 0) | (e != prev))
    def _():
        rows = 128
        for r0 in range(0, D_MODEL, rows):
            w1s[r0:r0 + rows, :] = w1_ref[r0:r0 + rows, :].astype(BF16)
        for r0 in range(0, D_FF, rows):
            w2s[r0:r0 + rows, :] = w2_ref[r0:r0 + rows, :].astype(BF16)

    @pl.when(i < na_ref[0])
    def _():
        hmid = jnp.dot(x_ref[...].astype(BF16), w1s[...], preferred_element_type=F32) + b1_ref[...]
        glu = jnp.minimum(hmid[:, :D_FF], SWIGLU_LIMIT)
        lin = jnp.clip(hmid[:, D_FF:], -SWIGLU_LIMIT, SWIGLU_LIMIT)
        act = glu * jax.nn.sigmoid(SWIGLU_ALPHA * glu) * (lin + 1.0)
        y_ref[...] = jnp.dot(act.astype(BF16), w2s[...], preferred_element_type=F32) + b2_ref[...]

    @pl.when(i >= na_ref[0])
    def _():
        y_ref[...] = jnp.zeros_like(y_ref)


def _experts(be, nact, xg, layer, w1, b1, w2, b2):
    n_slots = xg.shape[0]
    nb = n_slots // MOE_BM
    blk = lambda i, be_r, na_r: (jnp.minimum(i, na_r[0] - 1), 0)
    wsel = lambda i, be_r, na_r: (layer, be_r[i], 0, 0)
    return pl.pallas_call(
        _expert_kernel,
        out_shape=jax.ShapeDtypeStruct((n_slots, D_MODEL), F32),
        grid_spec=pltpu.PrefetchScalarGridSpec(
            num_scalar_prefetch=2,
            grid=(nb,),
            in_specs=[pl.BlockSpec((MOE_BM, D_MODEL), blk),
                      pl.BlockSpec((None, None, D_MODEL, 2 * D_FF), wsel),
                      pl.BlockSpec((None, None, 1, 2 * D_FF), wsel),
                      pl.BlockSpec((None, None, D_FF, D_MODEL), wsel),
                      pl.BlockSpec((None, None, 1, D_MODEL), wsel)],
            out_specs=pl.BlockSpec((MOE_BM, D_MODEL), lambda i, be_r, na_r: (i, 0)),
            scratch_shapes=[pltpu.VMEM((D_MODEL, 2 * D_FF), BF16), pltpu.VMEM((D_FF, D_MODEL), BF16)]),
        compiler_params=_cparams(("arbitrary",)),
        name="moe_experts",
    )(be, nact, xg, w1, b1, w2, b2)


def _combine_kernel(dcur_ref, dnext_ref, gate_ref, x1_ref, yg_ref, o_ref, rows, sem):
    i = pl.program_id(0)
    n = pl.num_programs(0)
    slot = i % 2

    def copy(dref, s, t, k):
        return pltpu.make_async_copy(yg_ref.at[pl.ds(dref[0, k, t], 1)], rows.at[s, k, pl.ds(t, 1)], sem.at[s])

    def issue(dref, s):
        def body(t, carry):
            for k in range(TOP_K):
                copy(dref, s, t, k).start(priority=k % 2)
            return carry
        lax.fori_loop(0, TOK_TILE, body, 0, unroll=_ISSUE_UNROLL)

    @pl.when(i == 0)
    def _():
        issue(dcur_ref, 0)

    @pl.when(i + 1 < n)
    def _():
        issue(dnext_ref, 1 - slot)

    for k in range(TOP_K):
        pltpu.make_async_copy(yg_ref.at[pl.ds(0, TOK_TILE)], rows.at[slot, k], sem.at[slot]).wait()

    eye = (lax.broadcasted_iota(I32, (TOK_TILE, TOK_TILE), 0)
           == lax.broadcasted_iota(I32, (TOK_TILE, TOK_TILE), 1)).astype(F32)
    gcol = _hdot_nt(eye, gate_ref[...])
    acc = x1_ref[...]
    for k in range(TOP_K):
        acc = acc + gcol[:, k:k + 1] * rows[slot, k]
    o_ref[...] = acc


def _combine(dest3, gates, x1, yg):
    t = x1.shape[0]
    nt = t // TOK_TILE
    return pl.pallas_call(
        _combine_kernel,
        out_shape=jax.ShapeDtypeStruct((t, D_MODEL), F32),
        grid=(nt,),
        in_specs=[pl.BlockSpec((1, TOP_K, TOK_TILE), lambda i: (i, 0, 0), memory_space=pltpu.SMEM),
                  pl.BlockSpec((1, TOP_K, TOK_TILE), lambda i: (jnp.minimum(i + 1, nt - 1), 0, 0),
                               memory_space=pltpu.SMEM),
                  pl.BlockSpec((TOP_K, TOK_TILE), lambda i: (0, i)),
                  pl.BlockSpec((TOK_TILE, D_MODEL), lambda i: (i, 0)),
                  pl.BlockSpec(memory_space=pl.ANY)],
        out_specs=pl.BlockSpec((TOK_TILE, D_MODEL), lambda i: (i, 0)),
        scratch_shapes=[pltpu.VMEM((2, TOP_K, TOK_TILE, D_MODEL), F32), pltpu.SemaphoreType.DMA((2,))],
        compiler_params=_cparams(("arbitrary",)),
        name="moe_combine",
    )(dest3, dest3, gates, x1, yg)


def _moe(x1, h2, logits_t, layer, w1, b1, w2, b2):
    t = x1.shape[0]
    tn = 512 if t % 512 == 0 else TOK_TILE
    n_blocks = t * TOP_K // MOE_BM + N_EXPERTS
    nbp = -(-n_blocks // LANES) * LANES
    eidx, gates, rank, cnt = _route_topk(logits_t, tn)
    dest3 = _route_dest(eidx, rank, cnt, tn)
    be, nact = _route_blocks(cnt, nbp)
    xg = _dispatch(dest3, h2, n_blocks * MOE_BM)
    yg = _experts(be[0, :n_blocks], nact[0, :1], xg, layer, w1, b1, w2, b2)
    return _combine(dest3, gates, x1, yg)


def _pack_w_in(w):
    ab = jnp.pad(w[:, _IN_MAIN:_IN_MAIN + _IN_AB], ((0, 0), (0, LANES - _IN_AB)))
    return jnp.concatenate([w[:, :_IN_MAIN], ab, w[:, _IN_MAIN + _IN_AB:]], axis=1).astype(BF16)


def _lane_row(v):
    return jnp.pad(v.astype(F32), (0, LANES - v.shape[0])).reshape(1, LANES)


def _layer_weights(ln1_gain, w_in, q_norm_swa, k_norm_swa, swa_sinks, conv_w, a_log, dt_bias, gdn_norm,
                   q_norm_mem, w_branch, w_out, ln2_gain, w_router, b_router, w_mlp1, b_mlp1, w_mlp2, b_mlp2, layer):
    depth = w_mlp1.shape[0]
    return dict(
        layer=layer, b1r=b_mlp1.reshape(depth, N_EXPERTS, 1, -1), b2r=b_mlp2.reshape(depth, N_EXPERTS, 1, -1),
        ln1=ln1_gain.reshape(1, -1), wp=_pack_w_in(w_in),
        qn_s=q_norm_swa.reshape(1, -1), kn_s=k_norm_swa.reshape(1, -1), sinks=swa_sinks.reshape(1, -1),
        conv_w=conv_w, alog=_lane_row(a_log), dtb=_lane_row(dt_bias), gnorm=gdn_norm.reshape(1, -1),
        qn_m=q_norm_mem.reshape(1, -1), wb=w_branch.astype(BF16), wo=w_out.astype(BF16),
        ln2=ln2_gain.reshape(1, -1), wr_t=w_router.T, br=b_router.reshape(-1, 1),
        w1=w_mlp1, w2=w_mlp2)


def _finish(lw, oa, ob, oc, gate, x2, tm):
    x1, h2, logits_t = _merge(oa, ob, oc, gate, x2, lw["wb"], lw["wo"], lw["ln2"], lw["wr_t"], lw["br"], tm)
    return _moe(x1, h2, logits_t, lw["layer"], lw["w1"], lw["b1r"], lw["w2"], lw["b2r"])


def _prompt_layer(x, mem, lw, mem_norm, w_mem_kv, k_norm_mem):
    b, l, d = x.shape
    x2 = x.reshape(b * l, d)
    tm = 256
    qs, ks, vs, qkvg, z, ab, qm, gate = _inproj(x2, lw["ln1"], lw["wp"], tm)
    mk, mv = _memkv(mem.reshape(b * MEM_LEN, d), mem_norm.reshape(1, -1), w_mem_kv.astype(BF16),
                    k_norm_mem.reshape(1, -1))
    r3 = lambda a: a.reshape(b, l, a.shape[-1])
    o_swa, kwin = _swa_prompt(r3(qs), r3(ks), r3(vs), lw["qn_s"], lw["kn_s"], lw["sinks"])
    o_g, s_fin = _gdn_prompt(r3(qkvg), r3(z), r3(ab), lw["conv_w"], lw["alog"], lw["dtb"], lw["gnorm"])
    o_m = _memattn_prompt(r3(qm), mk.reshape(b, MEM_LEN, BRANCH_W), mv.reshape(b, MEM_LEN, BRANCH_W), lw["qn_m"])
    y = _finish(lw, o_swa.reshape(b * l, -1), o_g.reshape(b * l, -1), o_m.reshape(b * l, -1), gate, x2, tm)
    new_k = kwin.reshape(b, WINDOW, SWA_KV_HEADS, SWA_HEAD_DIM)
    new_v = r3(vs)[:, l - WINDOW:].reshape(b, WINDOW, SWA_KV_HEADS, SWA_HEAD_DIM)
    new_conv = r3(qkvg)[:, l - (CONV_WIDTH - 1):]
    mk4 = mk.reshape(b, MEM_LEN, MEM_HEADS, MEM_HEAD_DIM)
    mv4 = mv.reshape(b, MEM_LEN, MEM_HEADS, MEM_HEAD_DIM)
    return y.reshape(b, l, d), new_k, new_v, s_fin, new_conv, mk4, mv4


def _sample_layer(x, cache_k, cache_v, state, conv_state, mem_k, mem_v, lw):
    b, l, d = x.shape
    x2 = x.reshape(b, d)
    tm = TOK_TILE
    qs, ks, vs, qkvg, z, ab, qm, gate = _inproj(x2, lw["ln1"], lw["wp"], tm)
    o_swa, new_k, new_v = _swa_decode(qs, ks, vs, cache_k, cache_v, lw["qn_s"], lw["kn_s"], lw["sinks"])
    o_g, new_s = _gdn_decode(qkvg, z, ab, state, conv_state, lw["conv_w"], lw["alog"], lw["dtb"], lw["gnorm"])
    o_m = _memattn_decode(qm, mem_k.reshape(b, MEM_LEN, BRANCH_W), mem_v.reshape(b, MEM_LEN, BRANCH_W), lw["qn_m"])
    y = _finish(lw, o_swa, o_g, o_m, gate, x2, tm)
    new_conv = jnp.concatenate([conv_state[:, 1:], qkvg[:, None, :]], axis=1)
    shp = (b, WINDOW, SWA_KV_HEADS, SWA_HEAD_DIM)
    return y.reshape(b, l, d), new_k.reshape(shp), new_v.reshape(shp), new_s, new_conv


def kernel(x_prompt, x_sample, cache_swa_k, cache_swa_v, state_gdn, state_conv, cache_mem_k, cache_mem_v, mem_prompt, ln1_gain, w_in, q_norm_swa, k_norm_swa, swa_sinks, conv_w, a_log, dt_bias, gdn_norm, q_norm_mem, k_norm_mem, mem_norm, w_mem_kv, w_branch, w_out, ln2_gain, w_router, b_router, w_mlp1, b_mlp1, w_mlp2, b_mlp2):
    xp, xs = x_prompt, x_sample
    outs = [[] for _ in range(10)]
    for l in range(ln1_gain.shape[0]):
        lw = _layer_weights(ln1_gain[l], w_in[l], q_norm_swa[l], k_norm_swa[l], swa_sinks[l], conv_w[l], a_log[l],
                            dt_bias[l], gdn_norm[l], q_norm_mem[l], w_branch[l], w_out[l], ln2_gain[l], w_router[l],
                            b_router[l], w_mlp1, b_mlp1, w_mlp2, b_mlp2, l)
        xp, pk, pv, ps, pc, mk, mv = _prompt_layer(xp, mem_prompt, lw, mem_norm[l], w_mem_kv[l], k_norm_mem[l])
        xs, sk, sv, ss, sc = _sample_layer(xs, cache_swa_k[l], cache_swa_v[l], state_gdn[l], state_conv[l],
                                           cache_mem_k[l], cache_mem_v[l], lw)
        for acc, v in zip(outs, (pk, pv, sk, sv, ps, ss, pc, sc, mk, mv)):
            acc.append(v)
    return (xp, xs) + tuple(jnp.stack(o) for o in outs)
```

```python
import functools

import numpy as np
import jax
import jax.numpy as jnp
from jax import lax
from jax.experimental import pallas as pl
from jax.experimental.pallas import tpu as pltpu

F32 = jnp.float32
BF16 = jnp.bfloat16
I32 = jnp.int32

D_MODEL = 1024
BRANCH_W = 512
SWA_HEADS = 8
SWA_KV_HEADS = 2
SWA_HEAD_DIM = 64
SWA_GROUP = SWA_HEADS // SWA_KV_HEADS
SWA_KV_W = SWA_KV_HEADS * SWA_HEAD_DIM
WINDOW = 128
GDN_HEADS = 4
GDN_HEAD_DIM = 128
GDN_W = GDN_HEADS * GDN_HEAD_DIM
GDN_CONV_W = 3 * GDN_W
GDN_CHUNK = 64
CONV_WIDTH = 4
MEM_LEN = 256
MEM_HEADS = 4
MEM_HEAD_DIM = 128
N_BRANCH = 3
N_EXPERTS = 32
TOP_K = 4
D_FF = 1024
SWIGLU_LIMIT = 7.0
SWIGLU_ALPHA = 1.702
EPS = 1e-6
PAST_LEN = 16384

LANES = 128
MOE_BM = 512
MOE_TOK_TILE = 256
MOE_UNIT = 8
TOK_TILE = 128
VMEM_LIMIT = 56 * 1024 * 1024

_SEG_QS = (0, 512)
_SEG_KS = (512, 128)
_SEG_VS = (640, 128)
_SEG_QKVG = (768, 1536)
_SEG_Z = (2304, 512)
_SEG_AB = (2816, 128)
_SEG_QM = (2944, 512)
_SEG_GATE = (3456, 3072)
_PACKED_W = 6528
_IN_MAIN = 2816
_IN_AB = 8

_NEG = -1e30
_HI = lax.Precision.HIGHEST


def _cparams(sem, vmem=VMEM_LIMIT):
    return pltpu.CompilerParams(dimension_semantics=sem, vmem_limit_bytes=vmem)


def _bdot(a, b):
    return jnp.dot(a.astype(BF16), b.astype(BF16), preferred_element_type=F32)


def _bdot_nt(a, b):
    return lax.dot_general(a.astype(BF16), b.astype(BF16), (((1,), (1,)), ((), ())),
                           preferred_element_type=F32)


def _bdot_tn(a, b):
    return lax.dot_general(a.astype(BF16), b.astype(BF16), (((0,), (0,)), ((), ())),
                           preferred_element_type=F32)


def _hdot(a, b):
    return jnp.dot(a, b, preferred_element_type=F32, precision=_HI)


def _hdot_nt(a, b):
    return lax.dot_general(a, b, (((1,), (1,)), ((), ())), preferred_element_type=F32, precision=_HI)


def _rms(x, gain):
    return x * lax.rsqrt(jnp.mean(x * x, axis=-1, keepdims=True) + EPS) * gain


def _l2(x):
    return x * lax.rsqrt(jnp.sum(x * x, axis=-1, keepdims=True) + EPS)


def _silu(x):
    return x * jax.nn.sigmoid(x)


def _softplus(x):
    return jnp.maximum(x, 0.0) + jnp.log1p(jnp.exp(-jnp.abs(x)))


def _inproj_kernel(x_ref, g_ref, w_ref, qs, ks, vs, qkvg, z, ab, qm, gate):
    x = x_ref[...]
    hb = _rms(x, g_ref[...]).astype(BF16)
    for ref, (off, width) in ((qs, _SEG_QS), (ks, _SEG_KS), (vs, _SEG_VS), (qkvg, _SEG_QKVG), (z, _SEG_Z),
                              (ab, _SEG_AB), (qm, _SEG_QM), (gate, _SEG_GATE)):
        step = min(width, 512)
        for c0 in range(0, width, step):
            ref[:, c0:c0 + step] = jnp.dot(hb, w_ref[:, off + c0:off + c0 + step], preferred_element_type=F32)


def _inproj(x2, gain, wp, tm):
    t = x2.shape[0]
    segs = (_SEG_QS, _SEG_KS, _SEG_VS, _SEG_QKVG, _SEG_Z, _SEG_AB, _SEG_QM, _SEG_GATE)
    return pl.pallas_call(
        _inproj_kernel,
        out_shape=[jax.ShapeDtypeStruct((t, w), F32) for _, w in segs],
        grid=(t // tm,),
        in_specs=[pl.BlockSpec((tm, D_MODEL), lambda i: (i, 0)),
                  pl.BlockSpec((1, D_MODEL), lambda i: (0, 0)),
                  pl.BlockSpec((D_MODEL, _PACKED_W), lambda i: (0, 0))],
        out_specs=[pl.BlockSpec((tm, w), lambda i: (i, 0)) for _, w in segs],
        compiler_params=_cparams(("parallel",)),
        name="inproj",
    )(x2, gain, wp)


def _alibi_slopes(n):
    return [float(2.0 ** (-8.0 * (i + 1) / n)) for i in range(n)]


def _swa_prompt_kernel(q_ref, kc_ref, kp_ref, vc_ref, vp_ref, qg_ref, kg_ref, sink_ref, o_ref, kwin_ref, bias):
    n = pl.program_id(1)
    heads = range(SWA_HEADS)

    @pl.when(n <= 1)
    def _():
        row = lax.broadcasted_iota(I32, (WINDOW, 2 * WINDOW), 0)
        col = lax.broadcasted_iota(I32, (WINDOW, 2 * WINDOW), 1)
        dist = row + WINDOW - col
        valid = (dist >= 0) & (dist <= WINDOW) & ((col >= WINDOW) | (n > 0))
        distf = dist.astype(F32)
        for h, slope in enumerate(_alibi_slopes(SWA_HEADS)):
            bias[h] = jnp.where(valid, -slope * distf, _NEG)

    q = q_ref[0]
    kc, kp, vc, vp = kc_ref[0], kp_ref[0], vc_ref[0], vp_ref[0]
    kk, vv = [], []
    for g in range(SWA_KV_HEADS):
        sl = slice(g * SWA_HEAD_DIM, (g + 1) * SWA_HEAD_DIM)
        kcn = _rms(kc[:, sl], kg_ref[...])
        kpn = _rms(kp[:, sl], kg_ref[...])
        kwin_ref[0, :, sl] = kcn
        kk.append(jnp.concatenate([kpn, kcn], axis=0).astype(BF16))
        vv.append(jnp.concatenate([vp[:, sl], vc[:, sl]], axis=0).astype(BF16))
    hsl = lambda h: slice(h * SWA_HEAD_DIM, (h + 1) * SWA_HEAD_DIM)
    qn = [_rms(q[:, hsl(h)], qg_ref[...]) for h in heads]
    s = [_bdot_nt(qn[h], kk[h // SWA_GROUP]) + bias[h] for h in heads]
    sink = [sink_ref[0:1, h:h + 1] for h in heads]
    m = [jnp.maximum(jnp.max(s[h], axis=-1, keepdims=True), sink[h]) for h in heads]
    p = [jnp.exp(s[h] - m[h]) for h in heads]
    denom = [jnp.sum(p[h], axis=-1, keepdims=True) + jnp.exp(sink[h] - m[h]) for h in heads]
    pv = [_bdot(p[h], vv[h // SWA_GROUP]) for h in heads]
    for h in heads:
        o_ref[0, :, hsl(h)] = pv[h] / denom[h]


def _swa_prompt(qs, ks, vs, qn, kn, sinks):
    b, l, _ = qs.shape
    nb = l // WINDOW
    cur = lambda i, j: (i, j, 0)
    prev = lambda i, j: (i, jnp.maximum(j - 1, 0), 0)
    const2 = lambda i, j: (0, 0)
    q_gain = qn * (SWA_HEAD_DIM ** -0.5)
    k_gain = kn
    return pl.pallas_call(
        _swa_prompt_kernel,
        out_shape=[jax.ShapeDtypeStruct((b, l, BRANCH_W), F32),
                   jax.ShapeDtypeStruct((b, WINDOW, SWA_KV_W), F32)],
        grid=(b, nb),
        in_specs=[pl.BlockSpec((1, WINDOW, BRANCH_W), cur),
                  pl.BlockSpec((1, WINDOW, SWA_KV_W), cur),
                  pl.BlockSpec((1, WINDOW, SWA_KV_W), prev),
                  pl.BlockSpec((1, WINDOW, SWA_KV_W), cur),
                  pl.BlockSpec((1, WINDOW, SWA_KV_W), prev),
                  pl.BlockSpec((1, SWA_HEAD_DIM), const2),
                  pl.BlockSpec((1, SWA_HEAD_DIM), const2),
                  pl.BlockSpec((1, SWA_HEADS), const2)],
        out_specs=[pl.BlockSpec((1, WINDOW, BRANCH_W), cur),
                   pl.BlockSpec((1, WINDOW, SWA_KV_W), lambda i, j: (i, 0, 0))],
        scratch_shapes=[pltpu.VMEM((SWA_HEADS, WINDOW, 2 * WINDOW), F32)],
        compiler_params=_cparams(("arbitrary", "arbitrary")),
        name="swa_prompt",
    )(qs, ks, ks, vs, vs, q_gain, k_gain, sinks)


def _swa_decode_kernel(bs, q_ref, k3_ref, kf_ref, v3_ref, vf_ref, ck_ref, cv_ref, qn_ref, kn_ref, kn2_ref,
                       sink_ref, slope_ref, o_ref, ok_ref, ov_ref):
    scale = SWA_HEAD_DIM ** -0.5
    lane = lax.broadcasted_iota(I32, (1, SWA_KV_W), 1)
    rowi = lax.broadcasted_iota(I32, (WINDOW, SWA_KV_W), 0)
    keyd = (WINDOW - lax.broadcasted_iota(I32, (1, WINDOW), 1)).astype(F32)
    for b in range(bs):
        qn = _rms(q_ref[b], qn_ref[...])
        kn3 = _rms(k3_ref[b], kn_ref[...])
        v3 = v3_ref[b]
        kf = kf_ref[b]
        sq = kf * kf
        ms0 = jnp.sum(jnp.where(lane < SWA_HEAD_DIM, sq, 0.0), axis=-1, keepdims=True) / SWA_HEAD_DIM
        ms1 = jnp.sum(jnp.where(lane >= SWA_HEAD_DIM, sq, 0.0), axis=-1, keepdims=True) / SWA_HEAD_DIM
        knf = kf * lax.rsqrt(jnp.where(lane < SWA_HEAD_DIM, ms0, ms1) + EPS) * kn2_ref[...]
        ck = ck_ref[b]
        cv = cv_ref[b]
        for g in range(SWA_KV_HEADS):
            sl = slice(g * SWA_HEAD_DIM, (g + 1) * SWA_HEAD_DIM)
            hs = slice(g * SWA_GROUP, (g + 1) * SWA_GROUP)
            qg = qn[hs]
            slope = slope_ref[hs]
            sink = sink_ref[hs]
            s = _bdot_nt(qg, ck[:, sl]) * scale - slope * keyd
            s_new = jnp.sum(qg * kn3[g:g + 1], axis=-1, keepdims=True) * scale
            m = jnp.maximum(jnp.maximum(jnp.max(s, axis=-1, keepdims=True), s_new), sink)
            p = jnp.exp(s - m)
            p_new = jnp.exp(s_new - m)
            denom = jnp.sum(p, axis=-1, keepdims=True) + p_new + jnp.exp(sink - m)
            o_ref[b, hs, :] = (_bdot(p, cv[:, sl]) + p_new * v3[g:g + 1]) / denom
        last = rowi == WINDOW - 1
        ok_ref[b] = jnp.where(last, knf, pltpu.roll(ck, WINDOW - 1, 0))
        ov_ref[b] = jnp.where(last, vf_ref[b], pltpu.roll(cv, WINDOW - 1, 0))


def _swa_decode(qs, ks, vs, cache_k, cache_v, qn, kn, sinks, bs=8):
    b = qs.shape[0]
    q3 = qs.reshape(b, SWA_HEADS, SWA_HEAD_DIM)
    k3 = ks.reshape(b, SWA_KV_HEADS, SWA_HEAD_DIM)
    kf = ks.reshape(b, 1, SWA_KV_W)
    v3 = vs.reshape(b, SWA_KV_HEADS, SWA_HEAD_DIM)
    vf = vs.reshape(b, 1, SWA_KV_W)
    ck = cache_k.reshape(b, WINDOW, SWA_KV_W)
    cv = cache_v.reshape(b, WINDOW, SWA_KV_W)
    kn2 = jnp.concatenate([kn, kn], axis=-1)
    sink_col = sinks.reshape(SWA_HEADS, 1)
    slope_col = jnp.asarray(np.asarray(_alibi_slopes(SWA_HEADS), np.float32).reshape(SWA_HEADS, 1))
    blk = lambda *shape: pl.BlockSpec((bs,) + shape, lambda i: (i,) + (0,) * len(shape))
    full = lambda *shape: pl.BlockSpec(shape, lambda i: (0,) * len(shape))
    o, ok, ov = pl.pallas_call(
        functools.partial(_swa_decode_kernel, bs),
        out_shape=[jax.ShapeDtypeStruct((b, SWA_HEADS, SWA_HEAD_DIM), F32),
                   jax.ShapeDtypeStruct((b, WINDOW, SWA_KV_W), F32),
                   jax.ShapeDtypeStruct((b, WINDOW, SWA_KV_W), F32)],
        grid=(b // bs,),
        in_specs=[blk(SWA_HEADS, SWA_HEAD_DIM), blk(SWA_KV_HEADS, SWA_HEAD_DIM), blk(1, SWA_KV_W),
                  blk(SWA_KV_HEADS, SWA_HEAD_DIM), blk(1, SWA_KV_W), blk(WINDOW, SWA_KV_W), blk(WINDOW, SWA_KV_W),
                  full(1, SWA_HEAD_DIM), full(1, SWA_HEAD_DIM), full(1, SWA_KV_W),
                  full(SWA_HEADS, 1), full(SWA_HEADS, 1)],
        out_specs=[blk(SWA_HEADS, SWA_HEAD_DIM), blk(WINDOW, SWA_KV_W), blk(WINDOW, SWA_KV_W)],
        compiler_params=_cparams(("parallel",)),
        name="swa_decode",
    )(q3, k3, kf, v3, vf, ck, cv, qn, kn, kn2, sink_col, slope_col)
    return o.reshape(b, BRANCH_W), ok, ov


def _gate_rows(ab, alog_ref, dtb_ref):
    g = -jnp.exp(alog_ref[...]) * _softplus(ab + dtb_ref[...])
    return g, jax.nn.sigmoid(ab)


GDN_PREP_CHUNKS = 4
GDN_SCAN_BATCH = 2
_HALO = 8


def _gdn_prep_kernel(x_ref, xh_ref, ab_ref, cw_ref, alog_ref, dtb_ref,
                     u_ref, w_ref, qg_ref, kd_ref, in_ref, eg_ref, xp, ys):
    j = pl.program_id(1)
    C = GDN_CHUNK
    d = GDN_HEAD_DIM
    n = GDN_PREP_CHUNKS * C
    heads = range(GDN_HEADS)
    xp[0:_HALO, :] = jnp.where(j > 0, xh_ref[0], 0.0)
    xp[_HALO:, :] = x_ref[0]
    for ci in range(GDN_PREP_CHUNKS):
        r0 = ci * C
        y = xp[_HALO + r0:_HALO + r0 + C, :] * cw_ref[CONV_WIDTH - 1:CONV_WIDTH, :]
        for jj in range(CONV_WIDTH - 1):
            sh = CONV_WIDTH - 1 - jj
            y = y + xp[_HALO + r0 - sh:_HALO + r0 - sh + C, :] * cw_ref[jj:jj + 1, :]
        ys[r0:r0 + C, :] = _silu(y)
    g_all, beta_all = _gate_rows(ab_ref[0], alog_ref, dtb_ref)
    r = lax.broadcasted_iota(I32, (n, n), 0)
    cc = lax.broadcasted_iota(I32, (n, n), 1)
    same = (r // C) == (cc // C)
    incl = same & (r >= cc)
    strict = same & (r > cc)
    upto = same & (r <= cc)
    eye = r == cc
    q = [_l2(ys[:, h * d:(h + 1) * d]) * (d ** -0.5) for h in heads]
    k = [_l2(ys[:, GDN_W + h * d:GDN_W + (h + 1) * d]) for h in heads]
    beta = [beta_all[:, GDN_HEADS + h:GDN_HEADS + h + 1] for h in heads]
    gc_row = [jnp.sum(jnp.where(upto, g_all[:, h:h + 1], 0.0), axis=0, keepdims=True) for h in heads]
    gc_col = [jnp.sum(jnp.where(eye, gc_row[h], 0.0), axis=1, keepdims=True) for h in heads]
    decay = [jnp.where(incl, jnp.exp(jnp.where(incl, gc_col[h] - gc_row[h], 0.0)), 0.0) for h in heads]
    kb = [k[h] * beta[h] for h in heads]
    bp = [-jnp.where(strict, _bdot_nt(kb[h], k[h]) * decay[h], 0.0) for h in heads]
    intra = [jnp.where(incl, _bdot_nt(q[h], k[h]) * decay[h], 0.0) for h in heads]
    p = [eye.astype(F32) + bp[h] for h in heads]
    span = 2
    while span < C:
        bp = [_bdot(bp[h], bp[h]) for h in heads]
        p = [p[h] + _bdot(p[h], bp[h]) for h in heads]
        span *= 2
    uw = [_bdot(p[h], jnp.concatenate([ys[:, 2 * GDN_W + h * d:2 * GDN_W + (h + 1) * d] * beta[h],
                                       kb[h] * jnp.exp(gc_col[h])], axis=1)) for h in heads]
    for h in heads:
        hs = slice(h * d, (h + 1) * d)
        u_ref[0, :, hs] = uw[h][:, :d]
        w_ref[0, :, hs] = uw[h][:, d:].astype(BF16)
        qg_ref[0, :, hs] = (q[h] * jnp.exp(gc_col[h])).astype(BF16)
        for ci in range(GDN_PREP_CHUNKS):
            r0 = ci * C
            g_last = gc_row[h][:, r0 + C - 1:r0 + C]
            kd_ref[0, r0:r0 + C, hs] = (k[h][r0:r0 + C] * jnp.exp(g_last - gc_col[h][r0:r0 + C])).astype(BF16)
            in_ref[0, r0:r0 + C, h * C:(h + 1) * C] = intra[h][r0:r0 + C, r0:r0 + C].astype(BF16)
            eg_ref[0, ci, h:h + 1, :] = jnp.broadcast_to(jnp.exp(g_last), (1, LANES))


def _gdn_scan_kernel(u_ref, w_ref, qg_ref, kd_ref, in_ref, eg_ref, z_ref, gn_ref, o_ref, s_ref, st):
    c = pl.program_id(1)
    C = GDN_CHUNK
    d = GDN_HEAD_DIM

    @pl.when(c == 0)
    def _():
        st[...] = jnp.zeros_like(st)

    chains = [(bi, h) for bi in range(GDN_SCAN_BATCH) for h in range(GDN_HEADS)]
    hsl = lambda h: slice(h * d, (h + 1) * d)
    s_f = [st[bi, h] for bi, h in chains]
    s_b = [s.astype(BF16) for s in s_f]
    ws = [jnp.dot(w_ref[bi, :, hsl(h)], s_b[i], preferred_element_type=F32) for i, (bi, h) in enumerate(chains)]
    qs = [jnp.dot(qg_ref[bi, :, hsl(h)], s_b[i], preferred_element_type=F32) for i, (bi, h) in enumerate(chains)]
    vb = [(u_ref[bi, :, hsl(h)] - ws[i]).astype(BF16) for i, (bi, h) in enumerate(chains)]
    iv = [jnp.dot(in_ref[bi, :, h * C:(h + 1) * C], vb[i], preferred_element_type=F32)
          for i, (bi, h) in enumerate(chains)]
    kv = [lax.dot_general(kd_ref[bi, :, hsl(h)], vb[i], (((0,), (0,)), ((), ())), preferred_element_type=F32)
          for i, (bi, h) in enumerate(chains)]
    for i, (bi, h) in enumerate(chains):
        st[bi, h] = s_f[i] * eg_ref[bi, 0, h:h + 1, :] + kv[i]
        o_ref[bi, :, hsl(h)] = _rms(qs[i] + iv[i], gn_ref[...]) * _silu(z_ref[bi, :, hsl(h)])
    s_ref[...] = st[...]


def _gdn_prompt(qkvg, z, ab, conv_w, alog, dtb, gnorm):
    b, l, _ = qkvg.shape
    C = GDN_CHUNK
    nc = l // C
    rows = GDN_PREP_CHUNKS * C
    cur = lambda i, j: (i, j, 0)
    const2 = lambda i, j: (0, 0)
    u, w, qg, kd, intra, eg = pl.pallas_call(
        _gdn_prep_kernel,
        out_shape=[jax.ShapeDtypeStruct((b, l, GDN_W), F32),
                   jax.ShapeDtypeStruct((b, l, GDN_W), BF16),
                   jax.ShapeDtypeStruct((b, l, GDN_W), BF16),
                   jax.ShapeDtypeStruct((b, l, GDN_W), BF16),
                   jax.ShapeDtypeStruct((b, l, GDN_HEADS * C), BF16),
                   jax.ShapeDtypeStruct((b, nc, GDN_HEADS, LANES), F32)],
        grid=(b, l // rows),
        in_specs=[pl.BlockSpec((1, rows, GDN_CONV_W), cur),
                  pl.BlockSpec((1, _HALO, GDN_CONV_W),
                               lambda i, j: (i, jnp.maximum(j * (rows // _HALO) - 1, 0), 0)),
                  pl.BlockSpec((1, rows, LANES), cur),
                  pl.BlockSpec((CONV_WIDTH, GDN_CONV_W), const2),
                  pl.BlockSpec((1, LANES), const2),
                  pl.BlockSpec((1, LANES), const2)],
        out_specs=[pl.BlockSpec((1, rows, GDN_W), cur),
                   pl.BlockSpec((1, rows, GDN_W), cur),
                   pl.BlockSpec((1, rows, GDN_W), cur),
                   pl.BlockSpec((1, rows, GDN_W), cur),
                   pl.BlockSpec((1, rows, GDN_HEADS * C), cur),
                   pl.BlockSpec((1, GDN_PREP_CHUNKS, GDN_HEADS, LANES), lambda i, j: (i, j, 0, 0))],
        scratch_shapes=[pltpu.VMEM((_HALO + rows, GDN_CONV_W), F32), pltpu.VMEM((rows, GDN_CONV_W), F32)],
        compiler_params=_cparams(("parallel", "parallel")),
        name="gdn_prep",
    )(qkvg, qkvg, ab, conv_w, alog, dtb)
    bb = GDN_SCAN_BATCH
    seq = lambda wd: pl.BlockSpec((bb, C, wd), cur)
    return pl.pallas_call(
        _gdn_scan_kernel,
        out_shape=[jax.ShapeDtypeStruct((b, l, GDN_W), F32),
                   jax.ShapeDtypeStruct((b, GDN_HEADS, GDN_HEAD_DIM, GDN_HEAD_DIM), F32)],
        grid=(b // bb, nc),
        in_specs=[seq(GDN_W), seq(GDN_W), seq(GDN_W), seq(GDN_W), seq(GDN_HEADS * C),
                  pl.BlockSpec((bb, 1, GDN_HEADS, LANES), lambda i, j: (i, j, 0, 0)),
                  seq(GDN_W),
                  pl.BlockSpec((1, GDN_HEAD_DIM), const2)],
        out_specs=[seq(GDN_W),
                   pl.BlockSpec((bb, GDN_HEADS, GDN_HEAD_DIM, GDN_HEAD_DIM), lambda i, j: (i, 0, 0, 0))],
        scratch_shapes=[pltpu.VMEM((bb, GDN_HEADS, GDN_HEAD_DIM, GDN_HEAD_DIM), F32)],
        compiler_params=_cparams(("parallel", "arbitrary")),
        name="gdn_scan",
    )(u, w, qg, kd, intra, eg, z, gnorm)


def _gdn_decode_kernel(bs, x_ref, cs_ref, z_ref, ab_ref, s_ref, cw_ref, alog_ref, dtb_ref, gn_ref, o_ref, so_ref):
    x = x_ref[...]
    y = x * cw_ref[CONV_WIDTH - 1:CONV_WIDTH, :]
    for j in range(CONV_WIDTH - 1):
        y = y + cs_ref[:, j, :] * cw_ref[j:j + 1, :]
    y = _silu(y)
    g_all, beta_all = _gate_rows(ab_ref[...], alog_ref, dtb_ref)
    eg_all = jnp.exp(g_all)
    z = z_ref[...]
    d = GDN_HEAD_DIM
    eye = (lax.broadcasted_iota(I32, (d, d), 0) == lax.broadcasted_iota(I32, (d, d), 1)).astype(F32)
    for h in range(GDN_HEADS):
        hs = slice(h * d, (h + 1) * d)
        qh = _l2(y[:, h * d:(h + 1) * d]) * (d ** -0.5)
        kh = _l2(y[:, GDN_W + h * d:GDN_W + (h + 1) * d])
        vh = y[:, 2 * GDN_W + h * d:2 * GDN_W + (h + 1) * d]
        kt = _hdot_nt(eye, kh)
        qt = _hdot_nt(eye, qh)
        outs = []
        for b in range(bs):
            s_b = s_ref[b, h]
            kc = kt[:, b:b + 1]
            eg = eg_all[b:b + 1, h:h + 1]
            beta = beta_all[b:b + 1, GDN_HEADS + h:GDN_HEADS + h + 1]
            sk = jnp.sum(s_b * kc, axis=0, keepdims=True)
            v_new = beta * (vh[b:b + 1] - eg * sk)
            s_new = eg * s_b + kc * v_new
            so_ref[b, h] = s_new
            outs.append(jnp.sum(s_new * qt[:, b:b + 1], axis=0, keepdims=True))
        o = jnp.concatenate(outs, axis=0)
        o_ref[:, hs] = _rms(o, gn_ref[...]) * _silu(z[:, hs])


def _gdn_decode(qkvg, z, ab, state, conv_state, conv_w, alog, dtb, gnorm, bs=8):
    b = qkvg.shape[0]
    row = lambda w: pl.BlockSpec((bs, w), lambda i: (i, 0))
    full = lambda *shape: pl.BlockSpec(shape, lambda i: (0,) * len(shape))
    sspec = pl.BlockSpec((bs, GDN_HEADS, GDN_HEAD_DIM, GDN_HEAD_DIM), lambda i: (i, 0, 0, 0))
    return pl.pallas_call(
        functools.partial(_gdn_decode_kernel, bs),
        out_shape=[jax.ShapeDtypeStruct((b, GDN_W), F32),
                   jax.ShapeDtypeStruct(state.shape, F32)],
        grid=(b // bs,),
        in_specs=[row(GDN_CONV_W),
                  pl.BlockSpec((bs, CONV_WIDTH - 1, GDN_CONV_W), lambda i: (i, 0, 0)),
                  row(GDN_W), row(LANES), sspec,
                  full(CONV_WIDTH, GDN_CONV_W), full(1, LANES), full(1, LANES), full(1, GDN_HEAD_DIM)],
        out_specs=[row(GDN_W), sspec],
        compiler_params=_cparams(("parallel",)),
        name="gdn_decode",
    )(qkvg, conv_state, z, ab, state, conv_w, alog, dtb, gnorm)


def _memkv_kernel(x_ref, g_ref, w_ref, kn_ref, k_ref, v_ref):
    hb = _rms(x_ref[...], g_ref[...]).astype(BF16)
    for hd in range(MEM_HEADS):
        hs = slice(hd * MEM_HEAD_DIM, (hd + 1) * MEM_HEAD_DIM)
        k_ref[:, hs] = _rms(jnp.dot(hb, w_ref[:, hs], preferred_element_type=F32), kn_ref[...])
    v_ref[...] = jnp.dot(hb, w_ref[:, BRANCH_W:], preferred_element_type=F32)


def _memkv(mem2, gain, w_kv, kn, tm=512):
    t = mem2.shape[0]
    return pl.pallas_call(
        _memkv_kernel,
        out_shape=[jax.ShapeDtypeStruct((t, BRANCH_W), F32)] * 2,
        grid=(t // tm,),
        in_specs=[pl.BlockSpec((tm, D_MODEL), lambda i: (i, 0)),
                  pl.BlockSpec((1, D_MODEL), lambda i: (0, 0)),
                  pl.BlockSpec((D_MODEL, 2 * BRANCH_W), lambda i: (0, 0)),
                  pl.BlockSpec((1, MEM_HEAD_DIM), lambda i: (0, 0))],
        out_specs=[pl.BlockSpec((tm, BRANCH_W), lambda i: (i, 0))] * 2,
        compiler_params=_cparams(("parallel",)),
        name="memkv",
    )(mem2, gain, w_kv, kn)


def _memattn_prompt_kernel(q_ref, k_ref, v_ref, qn_ref, o_ref):
    q = q_ref[0]
    k = k_ref[0]
    v = v_ref[0]
    scale = MEM_HEAD_DIM ** -0.5
    for hd in range(MEM_HEADS):
        hs = slice(hd * MEM_HEAD_DIM, (hd + 1) * MEM_HEAD_DIM)
        s = _bdot_nt(_rms(q[:, hs], qn_ref[...]), k[:, hs]) * scale
        p = jnp.exp(s - jnp.max(s, axis=-1, keepdims=True))
        o_ref[0, :, hs] = _bdot(p, v[:, hs]) / jnp.sum(p, axis=-1, keepdims=True)


def _memattn_prompt(qm, mk, mv, qn):
    b, l, _ = qm.shape
    tq = 512 if l % 512 == 0 else WINDOW
    return pl.pallas_call(
        _memattn_prompt_kernel,
        out_shape=jax.ShapeDtypeStruct((b, l, BRANCH_W), F32),
        grid=(b, l // tq),
        in_specs=[pl.BlockSpec((1, tq, BRANCH_W), lambda i, j: (i, j, 0)),
                  pl.BlockSpec((1, MEM_LEN, BRANCH_W), lambda i, j: (i, 0, 0)),
                  pl.BlockSpec((1, MEM_LEN, BRANCH_W), lambda i, j: (i, 0, 0)),
                  pl.BlockSpec((1, MEM_HEAD_DIM), lambda i, j: (0, 0))],
        out_specs=pl.BlockSpec((1, tq, BRANCH_W), lambda i, j: (i, j, 0)),
        compiler_params=_cparams(("parallel", "parallel")),
        name="memattn_prompt",
    )(qm, mk, mv, qn)


def _memattn_decode_kernel(bs, q_ref, k_ref, v_ref, qn_ref, o_ref):
    scale = MEM_HEAD_DIM ** -0.5
    q = q_ref[...]
    for hd in range(MEM_HEADS):
        hs = slice(hd * MEM_HEAD_DIM, (hd + 1) * MEM_HEAD_DIM)
        qn = _rms(q[:, hs], qn_ref[...])
        outs = []
        for b in range(bs):
            s = jnp.sum(k_ref[b, :, hs] * qn[b:b + 1], axis=-1, keepdims=True) * scale
            p = jnp.exp(s - jnp.max(s, axis=0, keepdims=True))
            outs.append(jnp.sum(p * v_ref[b, :, hs], axis=0, keepdims=True) / jnp.sum(p, axis=0, keepdims=True))
        o_ref[:, hs] = jnp.concatenate(outs, axis=0)


def _memattn_decode(qm, ck, cv, qn, bs=8):
    b = qm.shape[0]
    kv = pl.BlockSpec((bs, MEM_LEN, BRANCH_W), lambda i: (i, 0, 0))
    return pl.pallas_call(
        functools.partial(_memattn_decode_kernel, bs),
        out_shape=jax.ShapeDtypeStruct((b, BRANCH_W), F32),
        grid=(b // bs,),
        in_specs=[pl.BlockSpec((bs, BRANCH_W), lambda i: (i, 0)), kv, kv,
                  pl.BlockSpec((1, MEM_HEAD_DIM), lambda i: (0, 0))],
        out_specs=pl.BlockSpec((bs, BRANCH_W), lambda i: (i, 0)),
        compiler_params=_cparams(("parallel",)),
        name="memattn_decode",
    )(qm, ck, cv, qn)


def _merge_kernel(oa_ref, ob_ref, oc_ref, gate_ref, x_ref, wb_ref, wo_ref, g2_ref, wr_ref, br_ref,
                  x1_ref, h2_ref, lg_ref):
    acc = None
    for i, o_ref in enumerate((oa_ref, ob_ref, oc_ref)):
        mixed = jnp.dot(o_ref[...].astype(BF16), wb_ref[i], preferred_element_type=F32)
        term = jax.nn.sigmoid(gate_ref[:, i * D_MODEL:(i + 1) * D_MODEL]) * mixed
        acc = term if acc is None else acc + term
    x1 = x_ref[...] + jnp.dot(acc.astype(BF16), wo_ref[...], preferred_element_type=F32)
    x1_ref[...] = x1
    h2 = _rms(x1, g2_ref[...])
    h2_ref[...] = h2.astype(BF16)
    h_hi = h2.astype(BF16)
    h_lo = (h2 - h_hi.astype(F32)).astype(BF16)
    nt = (((1,), (1,)), ((), ()))
    a = lax.dot_general(wr_ref[...], h_hi, nt, preferred_element_type=F32)
    b = lax.dot_general(wr_ref[0:N_EXPERTS, :], h_lo, nt, preferred_element_type=F32)
    lg_ref[...] = a[0:N_EXPERTS] + a[N_EXPERTS:] + b + br_ref[...]


def _merge(oa, ob, oc, gate, x2, wb, wo, g2, wr_t, br_col, tm):
    t = x2.shape[0]
    row = lambda w: pl.BlockSpec((tm, w), lambda i: (i, 0))
    full = lambda *shape: pl.BlockSpec(shape, lambda i: (0,) * len(shape))
    return pl.pallas_call(
        _merge_kernel,
        out_shape=[jax.ShapeDtypeStruct((t, D_MODEL), F32),
                   jax.ShapeDtypeStruct((t, D_MODEL), BF16),
                   jax.ShapeDtypeStruct((N_EXPERTS, t), F32)],
        grid=(t // tm,),
        in_specs=[row(BRANCH_W), row(BRANCH_W), row(BRANCH_W), row(N_BRANCH * D_MODEL), row(D_MODEL),
                  full(N_BRANCH, BRANCH_W, D_MODEL), full(D_MODEL, D_MODEL), full(1, D_MODEL),
                  full(2 * N_EXPERTS, D_MODEL), full(N_EXPERTS, 1)],
        out_specs=[row(D_MODEL), row(D_MODEL), pl.BlockSpec((N_EXPERTS, tm), lambda i: (0, i))],
        compiler_params=_cparams(("parallel",)),
        name="merge",
    )(oa, ob, oc, gate, x2, wb, wo, g2, wr_t, br_col)


def _col_to_row(col):
    n = col.shape[0]
    r = lax.broadcasted_iota(I32, (n, n), 0)
    c = lax.broadcasted_iota(I32, (n, n), 1)
    return jnp.sum(jnp.where(r == c, col, 0.0), axis=0, keepdims=True)


def _row_to_col(row):
    n = row.shape[1]
    r = lax.broadcasted_iota(I32, (n, n), 0)
    c = lax.broadcasted_iota(I32, (n, n), 1)
    return jnp.sum(jnp.where(r == c, row, 0.0), axis=1, keepdims=True)


def _lane_pad(row):
    return jnp.concatenate([row, jnp.zeros((1, LANES - row.shape[1]), row.dtype)], axis=1)


def _route_kernel(lg_ref, pos_ref, gate_ref, meta_ref, cnt_ref, carry):
    i = pl.program_id(0)
    tn = lg_ref.shape[1]

    @pl.when(i == 0)
    def _():
        carry[...] = jnp.zeros_like(carry)

    l = lg_ref[...]
    eio = lax.broadcasted_iota(I32, (N_EXPERTS, tn), 0)
    hot = jnp.zeros((N_EXPERTS, tn), F32)
    vals, idxs = [], []
    for _ in range(TOP_K):
        m = jnp.max(l, axis=0, keepdims=True)
        idx = jnp.min(jnp.where(l == m, eio, N_EXPERTS), axis=0, keepdims=True)
        sel = eio == idx
        vals.append(m)
        idxs.append(idx)
        hot = hot + sel.astype(F32)
        l = jnp.where(sel, -jnp.inf, l)
    ex = [jnp.exp(v - vals[0]) for v in vals]
    tot = ex[0] + ex[1] + ex[2] + ex[3]
    before = (lax.broadcasted_iota(I32, (tn, tn), 0) < lax.broadcasted_iota(I32, (tn, tn), 1)).astype(BF16)
    within = jnp.dot(hot.astype(BF16), before, preferred_element_type=F32)
    cnt_col = jnp.sum(hot, axis=1, keepdims=True)
    room = jnp.floor((cnt_col + (MOE_UNIT - 1)) * (1.0 / MOE_UNIT)) * MOE_UNIT
    r = lax.broadcasted_iota(I32, (N_EXPERTS, N_EXPERTS), 0)
    c = lax.broadcasted_iota(I32, (N_EXPERTS, N_EXPERTS), 1)
    start_col = _row_to_col(jnp.sum(jnp.where(r < c, room, 0.0), axis=0, keepdims=True))
    for k in range(TOP_K):
        gate_ref[k:k + 1, :] = ex[k] / tot
        pos_ref[k:k + 1, :] = jnp.sum(jnp.where(eio == idxs[k], start_col + within, 0.0), axis=0,
                                      keepdims=True).astype(I32)
    meta_ref[0] = _lane_pad(jnp.concatenate([_col_to_row(room), _col_to_row(carry[:, 0:1])], axis=1)).astype(I32)
    carry[...] = carry[...] + room
    cnt_ref[...] = carry[...]


def _route(logits_t, tn):
    t = logits_t.shape[1]
    kt = pl.BlockSpec((TOP_K, tn), lambda i: (0, i))
    return pl.pallas_call(
        _route_kernel,
        out_shape=[jax.ShapeDtypeStruct((TOP_K, t), I32),
                   jax.ShapeDtypeStruct((TOP_K, t), F32),
                   jax.ShapeDtypeStruct((t // tn, 1, LANES), I32),
                   jax.ShapeDtypeStruct((N_EXPERTS, LANES), F32)],
        grid=(t // tn,),
        in_specs=[pl.BlockSpec((N_EXPERTS, tn), lambda i: (0, i))],
        out_specs=[kt, kt, pl.BlockSpec((1, 1, LANES), lambda i: (i, 0, 0)),
                   pl.BlockSpec((N_EXPERTS, LANES), lambda i: (0, 0))],
        scratch_shapes=[pltpu.VMEM((N_EXPERTS, LANES), F32)],
        compiler_params=_cparams(("arbitrary",)),
        name="moe_route",
    )(logits_t)


def _layout_kernel(bm, cnt_ref, be_ref, na_ref, ps_ref, pe_ref, ct_ref):
    nbp = be_ref.shape[1]
    cnt_col = cnt_ref[:, 0:1]
    size = jnp.floor((cnt_col + (bm - 1)) * (1.0 / bm)) * bm
    r = lax.broadcasted_iota(I32, (N_EXPERTS, N_EXPERTS), 0)
    c = lax.broadcasted_iota(I32, (N_EXPERTS, N_EXPERTS), 1)
    ends_row = jnp.sum(jnp.where(r <= c, size, 0.0), axis=0, keepdims=True)
    ends_col = _row_to_col(ends_row)
    nact = ends_row[:, N_EXPERTS - 1:N_EXPERTS] * (1.0 / bm)
    blk = jnp.minimum(lax.broadcasted_iota(I32, (1, nbp), 1).astype(F32), nact - 1.0)
    be = jnp.sum((ends_col <= blk * bm).astype(F32), axis=0, keepdims=True)
    be_ref[...] = jnp.minimum(be, N_EXPERTS - 1.0).astype(I32)
    na_ref[...] = jnp.broadcast_to(nact, na_ref.shape).astype(I32)
    ps_ref[...] = _lane_pad(ends_row - _col_to_row(size)).astype(I32)
    pe_ref[...] = _lane_pad(ends_row).astype(I32)
    ct_ref[...] = _lane_pad(_col_to_row(cnt_col)).astype(I32)


def _layout(cnt, nbp, bm):
    row = jax.ShapeDtypeStruct((1, LANES), I32)
    return pl.pallas_call(
        functools.partial(_layout_kernel, bm),
        out_shape=[jax.ShapeDtypeStruct((1, nbp), I32), row, row, row, row],
        name="moe_layout",
    )(cnt)


def _stage_rows(tn):
    return tn * TOP_K + N_EXPERTS * MOE_UNIT


def _run_copies(meta_ref, ps_ref, make_copy):
    src = jnp.int32(0)
    total = jnp.int32(0)
    for e in range(N_EXPERTS):
        units = lax.shift_right_logical(meta_ref[0, 0, e], 3)
        dst = ps_ref[e] + meta_ref[0, 0, N_EXPERTS + e]

        def body(u, carry, src=src, dst=dst):
            make_copy(pl.multiple_of(src + u * MOE_UNIT, MOE_UNIT),
                      pl.multiple_of(dst + u * MOE_UNIT, MOE_UNIT)).start()
            return carry
        lax.fori_loop(0, units, body, 0)
        src = src + units * MOE_UNIT
        total = total + units
    return total


def _wait_copies(count, make_copy):
    def body(u, carry):
        make_copy(0, 0).wait()
        return carry
    lax.fori_loop(0, count, body, 0)


def _dispatch_kernel(ps_ref, pe_ref, ct_ref, meta_ref, pos_ref, h_ref, xg_ref, stage, zrows, zblock, started, sem,
                     zsem):
    i = pl.program_id(0)
    n = pl.num_programs(0)
    slot = i % 2
    rows, tn = stage.shape[1], pos_ref.shape[1]

    def copy(s):
        return lambda src, dst: pltpu.make_async_copy(stage.at[s, pl.ds(src, MOE_UNIT)],
                                                      xg_ref.at[pl.ds(dst, MOE_UNIT)], sem.at[s])

    @pl.when(i >= 2)
    def _():
        _wait_copies(started[slot], copy(slot))

    srow = lax.broadcasted_iota(I32, (rows, tn), 0)
    hit = srow == pos_ref[0:1, :]
    for k in range(1, TOP_K):
        hit = hit | (srow == pos_ref[k:k + 1, :])
    onehot = jnp.where(hit, 1.0, 0.0).astype(BF16)
    stage[slot] = jnp.dot(onehot, h_ref[...], preferred_element_type=F32)
    started[slot] = _run_copies(meta_ref, ps_ref, copy(slot))

    @pl.when(i == n - 1)
    def _():
        _wait_copies(started[slot], copy(slot))

        @pl.when(n >= 2)
        def _():
            _wait_copies(started[1 - slot], copy(1 - slot))

        zrows[...] = jnp.zeros_like(zrows)
        zero = lambda src, dst: pltpu.make_async_copy(zrows, xg_ref.at[pl.ds(dst, MOE_UNIT)], zsem)
        total = jnp.int32(0)
        for e in range(N_EXPERTS):
            lo = ps_ref[e] + ct_ref[e]
            units = lax.shift_right_logical(pe_ref[e] - lo, 3)

            def body(u, carry, lo=lo):
                zero(0, pl.multiple_of(lo + u * MOE_UNIT, MOE_UNIT)).start()
                return carry
            lax.fori_loop(0, units, body, 0)
            total = total + units
        _wait_copies(total, zero)

        zblock[...] = jnp.zeros_like(zblock)
        bm = zblock.shape[0]
        zero_block = lambda blk: pltpu.make_async_copy(zblock, xg_ref.at[pl.ds(pl.multiple_of(blk * bm, bm), bm)],
                                                       zsem)
        first = lax.shift_right_logical(pe_ref[N_EXPERTS - 1], bm.bit_length() - 1)
        last = xg_ref.shape[0] // bm

        def start_block(blk, carry):
            zero_block(blk).start()
            return carry
        lax.fori_loop(first, last, start_block, 0)

        def wait_block(blk, carry):
            zero_block(0).wait()
            return carry
        lax.fori_loop(first, last, wait_block, 0)


def _dispatch(ps, pe, ct, meta, pos, h2, n_slots, tn, bm):
    t = h2.shape[0]
    return pl.pallas_call(
        _dispatch_kernel,
        out_shape=jax.ShapeDtypeStruct((n_slots, D_MODEL), F32),
        grid_spec=pltpu.PrefetchScalarGridSpec(
            num_scalar_prefetch=3,
            grid=(t // tn,),
            in_specs=[pl.BlockSpec((1, 1, LANES), lambda i, *_: (i, 0, 0), memory_space=pltpu.SMEM),
                      pl.BlockSpec((TOP_K, tn), lambda i, *_: (0, i)),
                      pl.BlockSpec((tn, D_MODEL), lambda i, *_: (i, 0))],
            out_specs=pl.BlockSpec(memory_space=pl.ANY),
            scratch_shapes=[pltpu.VMEM((2, _stage_rows(tn), D_MODEL), F32),
                            pltpu.VMEM((MOE_UNIT, D_MODEL), F32),
                            pltpu.VMEM((bm, D_MODEL), F32),
                            pltpu.SMEM((2,), I32),
                            pltpu.SemaphoreType.DMA((2,)),
                            pltpu.SemaphoreType.DMA(())]),
        compiler_params=_cparams(("arbitrary",)),
        name="moe_dispatch",
    )(ps, pe, ct, meta, pos, h2)


def _expert_kernel(be_ref, na_ref, x_ref, w1_ref, b1_ref, w2_ref, b2_ref, y_ref, w1s, w2s):
    i = pl.program_id(0)
    e = be_ref[i]
    prev = be_ref[jnp.maximum(i - 1, 0)]

    @pl.when((i == 0) | (e != prev))
    def _():
        rows = 128
        for r0 in range(0, D_MODEL, rows):
            w1s[r0:r0 + rows, :] = w1_ref[r0:r0 + rows, :].astype(BF16)
        for r0 in range(0, D_FF, rows):
            w2s[r0:r0 + rows, :] = w2_ref[r0:r0 + rows, :].astype(BF16)

    @pl.when(i < na_ref[0])
    def _():
        hmid = jnp.dot(x_ref[...].astype(BF16), w1s[...], preferred_element_type=F32) + b1_ref[...]
        glu = jnp.minimum(hmid[:, :D_FF], SWIGLU_LIMIT)
        lin = jnp.clip(hmid[:, D_FF:], -SWIGLU_LIMIT, SWIGLU_LIMIT)
        act = glu * jax.nn.sigmoid(SWIGLU_ALPHA * glu) * (lin + 1.0)
        y_ref[...] = jnp.dot(act.astype(BF16), w2s[...], preferred_element_type=F32) + b2_ref[...]

    @pl.when(i >= na_ref[0])
    def _():
        y_ref[...] = jnp.zeros_like(y_ref)


def _experts(be, nact, xg, layer, w1, b1, w2, b2, bm):
    n_slots = xg.shape[0]
    nb = n_slots // bm
    blk = lambda i, be_r, na_r: (jnp.minimum(i, na_r[0] - 1), 0)
    wsel = lambda i, be_r, na_r: (layer, be_r[i], 0, 0)
    return pl.pallas_call(
        _expert_kernel,
        out_shape=jax.ShapeDtypeStruct((n_slots, D_MODEL), F32),
        grid_spec=pltpu.PrefetchScalarGridSpec(
            num_scalar_prefetch=2,
            grid=(nb,),
            in_specs=[pl.BlockSpec((bm, D_MODEL), blk),
                      pl.BlockSpec((None, None, D_MODEL, 2 * D_FF), wsel),
                      pl.BlockSpec((None, None, 1, 2 * D_FF), wsel),
                      pl.BlockSpec((None, None, D_FF, D_MODEL), wsel),
                      pl.BlockSpec((None, None, 1, D_MODEL), wsel)],
            out_specs=pl.BlockSpec((bm, D_MODEL), lambda i, be_r, na_r: (i, 0)),
            scratch_shapes=[pltpu.VMEM((D_MODEL, 2 * D_FF), BF16), pltpu.VMEM((D_FF, D_MODEL), BF16)]),
        compiler_params=_cparams(("arbitrary",)),
        name="moe_experts",
    )(be, nact, xg, w1, b1, w2, b2)


def _combine_kernel(ps_ref, mcur_ref, mnext_ref, pos_ref, gate_ref, x1_ref, yg_ref, o_ref, stage, started, sem):
    i = pl.program_id(0)
    n = pl.num_programs(0)
    slot = i % 2
    rows, tn = stage.shape[1], pos_ref.shape[1]

    @pl.when(i == 0)
    def _():
        stage[...] = jnp.zeros_like(stage)

    def copy(s):
        return lambda src, dst: pltpu.make_async_copy(yg_ref.at[pl.ds(dst, MOE_UNIT)],
                                                      stage.at[s, pl.ds(src, MOE_UNIT)], sem.at[s])

    @pl.when(i == 0)
    def _():
        started[0] = _run_copies(mcur_ref, ps_ref, copy(0))

    @pl.when(i + 1 < n)
    def _():
        started[1 - slot] = _run_copies(mnext_ref, ps_ref, copy(1 - slot))

    _wait_copies(started[slot], copy(slot))

    eye = (lax.broadcasted_iota(I32, (tn, tn), 0) == lax.broadcasted_iota(I32, (tn, tn), 1)).astype(F32)
    cols = _hdot_nt(eye, jnp.concatenate([pos_ref[...].astype(F32), gate_ref[...]], axis=0))
    lane = lax.broadcasted_iota(I32, (tn, rows), 1)
    weight = jnp.zeros((tn, rows), F32)
    for k in range(TOP_K):
        weight = weight + jnp.where(lane == cols[:, k:k + 1].astype(I32), cols[:, TOP_K + k:TOP_K + k + 1], 0.0)
    w_hi = weight.astype(BF16)
    w_lo = (weight - w_hi.astype(F32)).astype(BF16)
    y = stage[slot].astype(BF16)
    o_ref[...] = (x1_ref[...] + jnp.dot(w_hi, y, preferred_element_type=F32)
                  + jnp.dot(w_lo, y, preferred_element_type=F32))


def _combine(ps, meta, pos, gates, x1, yg, tn):
    t = x1.shape[0]
    nt = t // tn
    return pl.pallas_call(
        _combine_kernel,
        out_shape=jax.ShapeDtypeStruct((t, D_MODEL), F32),
        grid_spec=pltpu.PrefetchScalarGridSpec(
            num_scalar_prefetch=1,
            grid=(nt,),
            in_specs=[pl.BlockSpec((1, 1, LANES), lambda i, *_: (i, 0, 0), memory_space=pltpu.SMEM),
                      pl.BlockSpec((1, 1, LANES), lambda i, *_: (jnp.minimum(i + 1, nt - 1), 0, 0),
                                   memory_space=pltpu.SMEM),
                      pl.BlockSpec((TOP_K, tn), lambda i, *_: (0, i)),
                      pl.BlockSpec((TOP_K, tn), lambda i, *_: (0, i)),
                      pl.BlockSpec((tn, D_MODEL), lambda i, *_: (i, 0)),
                      pl.BlockSpec(memory_space=pl.ANY)],
            out_specs=pl.BlockSpec((tn, D_MODEL), lambda i, *_: (i, 0)),
            scratch_shapes=[pltpu.VMEM((2, _stage_rows(tn), D_MODEL), F32),
                            pltpu.SMEM((2,), I32),
                            pltpu.SemaphoreType.DMA((2,))]),
        compiler_params=_cparams(("arbitrary",)),
        name="moe_combine",
    )(ps, meta, meta, pos, gates, x1, yg)


def _moe(x1, h2, logits_t, layer, w1, b1, w2, b2):
    t = x1.shape[0]
    tn = min(MOE_TOK_TILE, t)
    bm = MOE_BM if t * TOP_K >= N_EXPERTS * MOE_BM else LANES
    n_blocks = (t * TOP_K + (t // tn) * N_EXPERTS * (MOE_UNIT - 1)) // bm + N_EXPERTS + 1
    nbp = -(-n_blocks // LANES) * LANES
    pos, gates, meta, cnt = _route(logits_t, tn)
    be, nact, ps, pe, ct = _layout(cnt, nbp, bm)
    ps, pe, ct = ps[0, :N_EXPERTS], pe[0, :N_EXPERTS], ct[0, :N_EXPERTS]
    xg = _dispatch(ps, pe, ct, meta, pos, h2, n_blocks * bm, tn, bm)
    yg = _experts(be[0, :n_blocks], nact[0, :1], xg, layer, w1, b1, w2, b2, bm)
    return _combine(ps, meta, pos, gates, x1, yg, tn)


def _pack_w_in(w):
    ab = jnp.pad(w[:, _IN_MAIN:_IN_MAIN + _IN_AB], ((0, 0), (0, LANES - _IN_AB)))
    return jnp.concatenate([w[:, :_IN_MAIN], ab, w[:, _IN_MAIN + _IN_AB:]], axis=1).astype(BF16)


def _lane_row(v):
    return jnp.pad(v.astype(F32), (0, LANES - v.shape[0])).reshape(1, LANES)


def _layer_weights(ln1_gain, w_in, q_norm_swa, k_norm_swa, swa_sinks, conv_w, a_log, dt_bias, gdn_norm,
                   q_norm_mem, w_branch, w_out, ln2_gain, w_router, b_router, w_mlp1, b_mlp1, w_mlp2, b_mlp2, layer):
    depth = w_mlp1.shape[0]
    wr_hi = w_router.T.astype(BF16)
    wr_lo = (w_router.T - wr_hi.astype(F32)).astype(BF16)
    return dict(
        layer=layer, wr_t=jnp.concatenate([wr_hi, wr_lo], axis=0), b1r=b_mlp1.reshape(depth, N_EXPERTS, 1, -1), b2r=b_mlp2.reshape(depth, N_EXPERTS, 1, -1),
        ln1=ln1_gain.reshape(1, -1), wp=_pack_w_in(w_in),
        qn_s=q_norm_swa.reshape(1, -1), kn_s=k_norm_swa.reshape(1, -1), sinks=swa_sinks.reshape(1, -1),
        conv_w=conv_w, alog=_lane_row(a_log), dtb=_lane_row(dt_bias), gnorm=gdn_norm.reshape(1, -1),
        qn_m=q_norm_mem.reshape(1, -1), wb=w_branch.astype(BF16), wo=w_out.astype(BF16),
        ln2=ln2_gain.reshape(1, -1), br=b_router.reshape(-1, 1),
        w1=w_mlp1, w2=w_mlp2)


def _finish(lw, oa, ob, oc, gate, x2, tm):
    x1, h2, logits_t = _merge(oa, ob, oc, gate, x2, lw["wb"], lw["wo"], lw["ln2"], lw["wr_t"], lw["br"], tm)
    return _moe(x1, h2, logits_t, lw["layer"], lw["w1"], lw["b1r"], lw["w2"], lw["b2r"])


def _prompt_layer(x, mem, lw, mem_norm, w_mem_kv, k_norm_mem):
    b, l, d = x.shape
    x2 = x.reshape(b * l, d)
    tm = 256
    qs, ks, vs, qkvg, z, ab, qm, gate = _inproj(x2, lw["ln1"], lw["wp"], tm)
    mk, mv = _memkv(mem.reshape(b * MEM_LEN, d), mem_norm.reshape(1, -1), w_mem_kv.astype(BF16),
                    k_norm_mem.reshape(1, -1))
    r3 = lambda a: a.reshape(b, l, a.shape[-1])
    o_swa, kwin = _swa_prompt(r3(qs), r3(ks), r3(vs), lw["qn_s"], lw["kn_s"], lw["sinks"])
    o_g, s_fin = _gdn_prompt(r3(qkvg), r3(z), r3(ab), lw["conv_w"], lw["alog"], lw["dtb"], lw["gnorm"])
    o_m = _memattn_prompt(r3(qm), mk.reshape(b, MEM_LEN, BRANCH_W), mv.reshape(b, MEM_LEN, BRANCH_W), lw["qn_m"])
    y = _finish(lw, o_swa.reshape(b * l, -1), o_g.reshape(b * l, -1), o_m.reshape(b * l, -1), gate, x2, tm)
    new_k = kwin.reshape(b, WINDOW, SWA_KV_HEADS, SWA_HEAD_DIM)
    new_v = r3(vs)[:, l - WINDOW:].reshape(b, WINDOW, SWA_KV_HEADS, SWA_HEAD_DIM)
    new_conv = r3(qkvg)[:, l - (CONV_WIDTH - 1):]
    mk4 = mk.reshape(b, MEM_LEN, MEM_HEADS, MEM_HEAD_DIM)
    mv4 = mv.reshape(b, MEM_LEN, MEM_HEADS, MEM_HEAD_DIM)
    return y.reshape(b, l, d), new_k, new_v, s_fin, new_conv, mk4, mv4


def _sample_layer(x, cache_k, cache_v, state, conv_state, mem_k, mem_v, lw):
    b, l, d = x.shape
    x2 = x.reshape(b, d)
    tm = TOK_TILE
    qs, ks, vs, qkvg, z, ab, qm, gate = _inproj(x2, lw["ln1"], lw["wp"], tm)
    o_swa, new_k, new_v = _swa_decode(qs, ks, vs, cache_k, cache_v, lw["qn_s"], lw["kn_s"], lw["sinks"])
    o_g, new_s = _gdn_decode(qkvg, z, ab, state, conv_state, lw["conv_w"], lw["alog"], lw["dtb"], lw["gnorm"])
    o_m = _memattn_decode(qm, mem_k.reshape(b, MEM_LEN, BRANCH_W), mem_v.reshape(b, MEM_LEN, BRANCH_W), lw["qn_m"])
    y = _finish(lw, o_swa, o_g, o_m, gate, x2, tm)
    new_conv = jnp.concatenate([conv_state[:, 1:], qkvg[:, None, :]], axis=1)
    shp = (b, WINDOW, SWA_KV_HEADS, SWA_HEAD_DIM)
    return y.reshape(b, l, d), new_k.reshape(shp), new_v.reshape(shp), new_s, new_conv


def kernel(x_prompt, x_sample, cache_swa_k, cache_swa_v, state_gdn, state_conv, cache_mem_k, cache_mem_v, mem_prompt, ln1_gain, w_in, q_norm_swa, k_norm_swa, swa_sinks, conv_w, a_log, dt_bias, gdn_norm, q_norm_mem, k_norm_mem, mem_norm, w_mem_kv, w_branch, w_out, ln2_gain, w_router, b_router, w_mlp1, b_mlp1, w_mlp2, b_mlp2):
    xp, xs = x_prompt, x_sample
    outs = [[] for _ in range(10)]
    for l in range(ln1_gain.shape[0]):
        lw = _layer_weights(ln1_gain[l], w_in[l], q_norm_swa[l], k_norm_swa[l], swa_sinks[l], conv_w[l], a_log[l],
                            dt_bias[l], gdn_norm[l], q_norm_mem[l], w_branch[l], w_out[l], ln2_gain[l], w_router[l],
                            b_router[l], w_mlp1, b_mlp1, w_mlp2, b_mlp2, l)
        xp, pk, pv, ps, pc, mk, mv = _prompt_layer(xp, mem_prompt, lw, mem_norm[l], w_mem_kv[l], k_norm_mem[l])
        xs, sk, sv, ss, sc = _sample_layer(xs, cache_swa_k[l], cache_swa_v[l], state_gdn[l], state_conv[l],
                                           cache_mem_k[l], cache_mem_v[l], lw)
        for acc, v in zip(outs, (pk, pv, sk, sv, ps, ss, pc, sc, mk, mv)):
            acc.append(v)
    return (xp, xs) + tuple(jnp.stack(o) for o in outs)
```

```python
import functools

import numpy as np
import jax
import jax.numpy as jnp
from jax import lax
from jax.experimental import pallas as pl
from jax.experimental.pallas import tpu as pltpu

F32 = jnp.float32
BF16 = jnp.bfloat16
I32 = jnp.int32

D_MODEL = 1024
BRANCH_W = 512
SWA_HEADS = 8
SWA_KV_HEADS = 2
SWA_HEAD_DIM = 64
SWA_GROUP = SWA_HEADS // SWA_KV_HEADS
SWA_KV_W = SWA_KV_HEADS * SWA_HEAD_DIM
WINDOW = 128
GDN_HEADS = 4
GDN_HEAD_DIM = 128
GDN_W = GDN_HEADS * GDN_HEAD_DIM
GDN_CONV_W = 3 * GDN_W
GDN_CHUNK = 64
CONV_WIDTH = 4
MEM_LEN = 256
MEM_HEADS = 4
MEM_HEAD_DIM = 128
N_BRANCH = 3
N_EXPERTS = 32
TOP_K = 4
D_FF = 1024
SWIGLU_LIMIT = 7.0
SWIGLU_ALPHA = 1.702
EPS = 1e-6
PAST_LEN = 16384

LANES = 128
MOE_BM = 512
MOE_TOK_TILE = 256
MOE_UNIT = 8
TOK_TILE = 128
VMEM_LIMIT = 56 * 1024 * 1024

_SEG_QS = (0, 512)
_SEG_KS = (512, 128)
_SEG_VS = (640, 128)
_SEG_QKVG = (768, 1536)
_SEG_Z = (2304, 512)
_SEG_AB = (2816, 128)
_SEG_QM = (2944, 512)
_SEG_GATE = (3456, 3072)
_PACKED_W = 6528
_IN_MAIN = 2816
_IN_AB = 8

_NEG = -1e30
_HI = lax.Precision.HIGHEST


def _cparams(sem, vmem=VMEM_LIMIT):
    return pltpu.CompilerParams(dimension_semantics=sem, vmem_limit_bytes=vmem)


def _bdot(a, b):
    return jnp.dot(a.astype(BF16), b.astype(BF16), preferred_element_type=F32)


def _bdot_nt(a, b):
    return lax.dot_general(a.astype(BF16), b.astype(BF16), (((1,), (1,)), ((), ())),
                           preferred_element_type=F32)


def _bdot_tn(a, b):
    return lax.dot_general(a.astype(BF16), b.astype(BF16), (((0,), (0,)), ((), ())),
                           preferred_element_type=F32)


def _hdot(a, b):
    return jnp.dot(a, b, preferred_element_type=F32, precision=_HI)


def _hdot_nt(a, b):
    return lax.dot_general(a, b, (((1,), (1,)), ((), ())), preferred_element_type=F32, precision=_HI)


def _rms(x, gain):
    return x * lax.rsqrt(jnp.mean(x * x, axis=-1, keepdims=True) + EPS) * gain


def _l2(x):
    return x * lax.rsqrt(jnp.sum(x * x, axis=-1, keepdims=True) + EPS)


def _silu(x):
    return x * jax.nn.sigmoid(x)


def _softplus(x):
    return jnp.maximum(x, 0.0) + jnp.log1p(jnp.exp(-jnp.abs(x)))


def _inproj_kernel(x_ref, g_ref, w_ref, qs, ks, vs, qkvg, z, ab, qm, gate):
    x = x_ref[...]
    hb = _rms(x, g_ref[...]).astype(BF16)
    for ref, (off, width) in ((qs, _SEG_QS), (ks, _SEG_KS), (vs, _SEG_VS), (qkvg, _SEG_QKVG), (z, _SEG_Z),
                              (ab, _SEG_AB), (qm, _SEG_QM), (gate, _SEG_GATE)):
        step = min(width, 512)
        for c0 in range(0, width, step):
            ref[:, c0:c0 + step] = jnp.dot(hb, w_ref[:, off + c0:off + c0 + step], preferred_element_type=F32)


def _inproj(x2, gain, wp, tm):
    t = x2.shape[0]
    segs = (_SEG_QS, _SEG_KS, _SEG_VS, _SEG_QKVG, _SEG_Z, _SEG_AB, _SEG_QM, _SEG_GATE)
    return pl.pallas_call(
        _inproj_kernel,
        out_shape=[jax.ShapeDtypeStruct((t, w), F32) for _, w in segs],
        grid=(t // tm,),
        in_specs=[pl.BlockSpec((tm, D_MODEL), lambda i: (i, 0)),
                  pl.BlockSpec((1, D_MODEL), lambda i: (0, 0)),
                  pl.BlockSpec((D_MODEL, _PACKED_W), lambda i: (0, 0))],
        out_specs=[pl.BlockSpec((tm, w), lambda i: (i, 0)) for _, w in segs],
        compiler_params=_cparams(("parallel",)),
        name="inproj",
    )(x2, gain, wp)


SWA_STEP_BLOCKS = 4


def _alibi_slopes(n):
    return [float(2.0 ** (-8.0 * (i + 1) / n)) for i in range(n)]


def _swa_prompt_kernel(q_ref, kc_ref, vc_ref, qg_ref, kg_ref, sink_ref, o_ref, kwin_ref, bias, kprev, vprev):
    n = pl.program_id(1)
    half = SWA_HEAD_DIM
    nblk = q_ref.shape[1] // WINDOW
    work = [(j, h) for j in range(nblk) for h in range(SWA_HEADS)]

    @pl.when(n == 0)
    def _():
        kprev[...] = jnp.zeros_like(kprev)
        vprev[...] = jnp.zeros_like(vprev)

    def fill(tbl, first):
        row = lax.broadcasted_iota(I32, (WINDOW, 2 * WINDOW), 0)
        col = lax.broadcasted_iota(I32, (WINDOW, 2 * WINDOW), 1)
        dist = row + WINDOW - col
        valid = (dist >= 0) & (dist <= WINDOW) & ((col >= WINDOW) | jnp.logical_not(first))
        distf = dist.astype(F32)
        for h, slope in enumerate(_alibi_slopes(SWA_HEADS)):
            bias[tbl, h] = jnp.where(valid, -slope * distf, _NEG)

    @pl.when(n <= 1)
    def _():
        fill(0, n == 0)

    if nblk > 1:
        @pl.when(n == 0)
        def _():
            fill(1, False)

    low = lax.broadcasted_iota(I32, (1, 2 * half), 1) < half

    def pair_rms(x, gain):
        sq = x * x
        s_lo = jnp.sum(jnp.where(low, sq, 0.0), axis=-1, keepdims=True)
        s_hi = jnp.sum(jnp.where(low, 0.0, sq), axis=-1, keepdims=True)
        return x * lax.rsqrt(jnp.where(low, s_lo, s_hi) * (1.0 / half) + EPS) * gain

    kcn = pair_rms(kc_ref[0], kg_ref[...])
    kwin_ref[0] = kcn[(nblk - 1) * WINDOW:]
    kall = jnp.concatenate([kprev[...], kcn], axis=0)
    vall = jnp.concatenate([vprev[...], vc_ref[0]], axis=0)
    kprev[...] = kcn[(nblk - 1) * WINDOW:]
    vprev[...] = vc_ref[0, (nblk - 1) * WINDOW:, :]
    ones = jnp.ones(((nblk + 1) * WINDOW, 2 * half), BF16)

    def placed(x, aug):
        sw = pltpu.roll(x, half, 1)
        out = {(0, 0): jnp.where(low, x, 0.0), (1, 1): jnp.where(low, 0.0, x),
               (0, 1): jnp.where(low, 0.0, sw), (1, 0): jnp.where(low, sw, 0.0)}
        out = {key: val.astype(BF16) for key, val in out.items()}
        return {key: jnp.concatenate([val, ones], axis=1) for key, val in out.items()} if aug else out

    kvar = placed(kall, False)
    vvar = placed(vall, True)
    qn = [pair_rms(q_ref[0, :, t * 2 * half:(t + 1) * 2 * half], qg_ref[...]).astype(BF16)
          for t in range(SWA_HEADS // 2)]
    where_of = lambda h: (h // SWA_GROUP, h % 2)
    qrows = lambda j: slice(j * WINDOW, (j + 1) * WINDOW)
    krows = lambda j: slice(j * WINDOW, (j + 2) * WINDOW)
    nt = (((1,), (1,)), ((), ()))
    s = {(j, h): lax.dot_general(qn[h // 2][qrows(j)], kvar[where_of(h)][krows(j)], nt, preferred_element_type=F32)
         + bias[min(j, 1), h] for j, h in work}
    sink = [sink_ref[0:1, h:h + 1] for h in range(SWA_HEADS)]
    m = {(j, h): jnp.maximum(jnp.max(s[j, h], axis=-1, keepdims=True), sink[h]) for j, h in work}
    p = {(j, h): jnp.exp(s[j, h] - m[j, h]).astype(BF16) for j, h in work}
    res = {(j, h): jnp.dot(p[j, h], vvar[where_of(h)][krows(j)], preferred_element_type=F32)
           for j, h in work}
    inv = {(j, h): 1.0 / (res[j, h][:, 2 * half:2 * half + 1] + jnp.exp(sink[h] - m[j, h])) for j, h in work}
    for j in range(nblk):
        for t in range(SWA_HEADS // 2):
            h0, h1 = 2 * t, 2 * t + 1
            num = res[j, h0][:, :2 * half] + res[j, h1][:, :2 * half]
            o_ref[0, qrows(j), t * 2 * half:(t + 1) * 2 * half] = num * jnp.where(low, inv[j, h0], inv[j, h1])


def _swa_prompt(qs, ks, vs, qn, kn, sinks):
    b, l, _ = qs.shape
    rows = SWA_STEP_BLOCKS * WINDOW if l % (SWA_STEP_BLOCKS * WINDOW) == 0 else WINDOW
    cur = lambda i, j: (i, j, 0)
    const2 = lambda i, j: (0, 0)
    q_gain = jnp.tile(qn, (1, 2)) * (SWA_HEAD_DIM ** -0.5)
    k_gain = jnp.tile(kn, (1, 2))
    return pl.pallas_call(
        _swa_prompt_kernel,
        out_shape=[jax.ShapeDtypeStruct((b, l, BRANCH_W), F32),
                   jax.ShapeDtypeStruct((b, WINDOW, SWA_KV_W), F32)],
        grid=(b, l // rows),
        in_specs=[pl.BlockSpec((1, rows, BRANCH_W), cur),
                  pl.BlockSpec((1, rows, SWA_KV_W), cur),
                  pl.BlockSpec((1, rows, SWA_KV_W), cur),
                  pl.BlockSpec((1, 2 * SWA_HEAD_DIM), const2),
                  pl.BlockSpec((1, 2 * SWA_HEAD_DIM), const2),
                  pl.BlockSpec((1, SWA_HEADS), const2)],
        out_specs=[pl.BlockSpec((1, rows, BRANCH_W), cur),
                   pl.BlockSpec((1, WINDOW, SWA_KV_W), lambda i, j: (i, 0, 0))],
        scratch_shapes=[pltpu.VMEM((2, SWA_HEADS, WINDOW, 2 * WINDOW), F32),
                        pltpu.VMEM((WINDOW, SWA_KV_W), F32),
                        pltpu.VMEM((WINDOW, SWA_KV_W), F32)],
        compiler_params=_cparams(("arbitrary", "arbitrary")),
        name="swa_prompt",
    )(qs, ks, vs, q_gain, k_gain, sinks)


def _swa_decode_kernel(bs, q_ref, k3_ref, kf_ref, v3_ref, vf_ref, ck_ref, cv_ref, qn_ref, kn_ref, kn2_ref,
                       sink_ref, slope_ref, o_ref, ok_ref, ov_ref):
    scale = SWA_HEAD_DIM ** -0.5
    lane = lax.broadcasted_iota(I32, (1, SWA_KV_W), 1)
    rowi = lax.broadcasted_iota(I32, (WINDOW, SWA_KV_W), 0)
    keyd = (WINDOW - lax.broadcasted_iota(I32, (1, WINDOW), 1)).astype(F32)
    for b in range(bs):
        qn = _rms(q_ref[b], qn_ref[...])
        kn3 = _rms(k3_ref[b], kn_ref[...])
        v3 = v3_ref[b]
        kf = kf_ref[b]
        sq = kf * kf
        ms0 = jnp.sum(jnp.where(lane < SWA_HEAD_DIM, sq, 0.0), axis=-1, keepdims=True) / SWA_HEAD_DIM
        ms1 = jnp.sum(jnp.where(lane >= SWA_HEAD_DIM, sq, 0.0), axis=-1, keepdims=True) / SWA_HEAD_DIM
        knf = kf * lax.rsqrt(jnp.where(lane < SWA_HEAD_DIM, ms0, ms1) + EPS) * kn2_ref[...]
        ck = ck_ref[b]
        cv = cv_ref[b]
        for g in range(SWA_KV_HEADS):
            sl = slice(g * SWA_HEAD_DIM, (g + 1) * SWA_HEAD_DIM)
            hs = slice(g * SWA_GROUP, (g + 1) * SWA_GROUP)
            qg = qn[hs]
            slope = slope_ref[hs]
            sink = sink_ref[hs]
            s = _bdot_nt(qg, ck[:, sl]) * scale - slope * keyd
            s_new = jnp.sum(qg * kn3[g:g + 1], axis=-1, keepdims=True) * scale
            m = jnp.maximum(jnp.maximum(jnp.max(s, axis=-1, keepdims=True), s_new), sink)
            p = jnp.exp(s - m)
            p_new = jnp.exp(s_new - m)
            denom = jnp.sum(p, axis=-1, keepdims=True) + p_new + jnp.exp(sink - m)
            o_ref[b, hs, :] = (_bdot(p, cv[:, sl]) + p_new * v3[g:g + 1]) / denom
        last = rowi == WINDOW - 1
        ok_ref[b] = jnp.where(last, knf, pltpu.roll(ck, WINDOW - 1, 0))
        ov_ref[b] = jnp.where(last, vf_ref[b], pltpu.roll(cv, WINDOW - 1, 0))


def _swa_decode(qs, ks, vs, cache_k, cache_v, qn, kn, sinks, bs=8):
    b = qs.shape[0]
    q3 = qs.reshape(b, SWA_HEADS, SWA_HEAD_DIM)
    k3 = ks.reshape(b, SWA_KV_HEADS, SWA_HEAD_DIM)
    kf = ks.reshape(b, 1, SWA_KV_W)
    v3 = vs.reshape(b, SWA_KV_HEADS, SWA_HEAD_DIM)
    vf = vs.reshape(b, 1, SWA_KV_W)
    ck = cache_k.reshape(b, WINDOW, SWA_KV_W)
    cv = cache_v.reshape(b, WINDOW, SWA_KV_W)
    kn2 = jnp.concatenate([kn, kn], axis=-1)
    sink_col = sinks.reshape(SWA_HEADS, 1)
    slope_col = jnp.asarray(np.asarray(_alibi_slopes(SWA_HEADS), np.float32).reshape(SWA_HEADS, 1))
    blk = lambda *shape: pl.BlockSpec((bs,) + shape, lambda i: (i,) + (0,) * len(shape))
    full = lambda *shape: pl.BlockSpec(shape, lambda i: (0,) * len(shape))
    o, ok, ov = pl.pallas_call(
        functools.partial(_swa_decode_kernel, bs),
        out_shape=[jax.ShapeDtypeStruct((b, SWA_HEADS, SWA_HEAD_DIM), F32),
                   jax.ShapeDtypeStruct((b, WINDOW, SWA_KV_W), F32),
                   jax.ShapeDtypeStruct((b, WINDOW, SWA_KV_W), F32)],
        grid=(b // bs,),
        in_specs=[blk(SWA_HEADS, SWA_HEAD_DIM), blk(SWA_KV_HEADS, SWA_HEAD_DIM), blk(1, SWA_KV_W),
                  blk(SWA_KV_HEADS, SWA_HEAD_DIM), blk(1, SWA_KV_W), blk(WINDOW, SWA_KV_W), blk(WINDOW, SWA_KV_W),
                  full(1, SWA_HEAD_DIM), full(1, SWA_HEAD_DIM), full(1, SWA_KV_W),
                  full(SWA_HEADS, 1), full(SWA_HEADS, 1)],
        out_specs=[blk(SWA_HEADS, SWA_HEAD_DIM), blk(WINDOW, SWA_KV_W), blk(WINDOW, SWA_KV_W)],
        compiler_params=_cparams(("parallel",)),
        name="swa_decode",
    )(q3, k3, kf, v3, vf, ck, cv, qn, kn, kn2, sink_col, slope_col)
    return o.reshape(b, BRANCH_W), ok, ov


def _gate_rows(ab, alog_ref, dtb_ref):
    g = -jnp.exp(alog_ref[...]) * _softplus(ab + dtb_ref[...])
    return g, jax.nn.sigmoid(ab)


GDN_PREP_CHUNKS = 4
GDN_SCAN_BATCH = 2
_HALO = 8


def _gdn_prep_kernel(x_ref, xh_ref, ab_ref, cw_ref, alog_ref, dtb_ref,
                     u_ref, w_ref, qg_ref, kd_ref, in_ref, eg_ref, xp, ys):
    j = pl.program_id(1)
    C = GDN_CHUNK
    d = GDN_HEAD_DIM
    n = GDN_PREP_CHUNKS * C
    heads = range(GDN_HEADS)
    xp[0:_HALO, :] = jnp.where(j > 0, xh_ref[0], 0.0)
    xp[_HALO:, :] = x_ref[0]
    for ci in range(GDN_PREP_CHUNKS):
        r0 = ci * C
        y = xp[_HALO + r0:_HALO + r0 + C, :] * cw_ref[CONV_WIDTH - 1:CONV_WIDTH, :]
        for jj in range(CONV_WIDTH - 1):
            sh = CONV_WIDTH - 1 - jj
            y = y + xp[_HALO + r0 - sh:_HALO + r0 - sh + C, :] * cw_ref[jj:jj + 1, :]
        ys[r0:r0 + C, :] = _silu(y)
    g_all, beta_all = _gate_rows(ab_ref[0], alog_ref, dtb_ref)
    r = lax.broadcasted_iota(I32, (n, n), 0)
    cc = lax.broadcasted_iota(I32, (n, n), 1)
    same = (r // C) == (cc // C)
    incl = same & (r >= cc)
    strict = same & (r > cc)
    upto = same & (r <= cc)
    eye = r == cc
    q = [_l2(ys[:, h * d:(h + 1) * d]) * (d ** -0.5) for h in heads]
    k = [_l2(ys[:, GDN_W + h * d:GDN_W + (h + 1) * d]) for h in heads]
    beta = [beta_all[:, GDN_HEADS + h:GDN_HEADS + h + 1] for h in heads]
    gc_row = [jnp.sum(jnp.where(upto, g_all[:, h:h + 1], 0.0), axis=0, keepdims=True) for h in heads]
    gc_col = [jnp.sum(jnp.where(eye, gc_row[h], 0.0), axis=1, keepdims=True) for h in heads]
    decay = [jnp.where(incl, jnp.exp(jnp.where(incl, gc_col[h] - gc_row[h], 0.0)), 0.0) for h in heads]
    kb = [k[h] * beta[h] for h in heads]
    bp = [-jnp.where(strict, _bdot_nt(kb[h], k[h]) * decay[h], 0.0) for h in heads]
    intra = [jnp.where(incl, _bdot_nt(q[h], k[h]) * decay[h], 0.0) for h in heads]
    p = [eye.astype(F32) + bp[h] for h in heads]
    span = 2
    while span < C:
        bp = [_bdot(bp[h], bp[h]) for h in heads]
        p = [p[h] + _bdot(p[h], bp[h]) for h in heads]
        span *= 2
    uw = [_bdot(p[h], jnp.concatenate([ys[:, 2 * GDN_W + h * d:2 * GDN_W + (h + 1) * d] * beta[h],
                                       kb[h] * jnp.exp(gc_col[h])], axis=1)) for h in heads]
    for h in heads:
        hs = slice(h * d, (h + 1) * d)
        u_ref[0, :, hs] = uw[h][:, :d]
        w_ref[0, :, hs] = uw[h][:, d:].astype(BF16)
        qg_ref[0, :, hs] = (q[h] * jnp.exp(gc_col[h])).astype(BF16)
        for ci in range(GDN_PREP_CHUNKS):
            r0 = ci * C
            g_last = gc_row[h][:, r0 + C - 1:r0 + C]
            kd_ref[0, r0:r0 + C, hs] = (k[h][r0:r0 + C] * jnp.exp(g_last - gc_col[h][r0:r0 + C])).astype(BF16)
            in_ref[0, r0:r0 + C, h * C:(h + 1) * C] = intra[h][r0:r0 + C, r0:r0 + C].astype(BF16)
            eg_ref[0, ci, h:h + 1, :] = jnp.broadcast_to(jnp.exp(g_last), (1, LANES))


def _gdn_scan_kernel(u_ref, w_ref, qg_ref, kd_ref, in_ref, eg_ref, z_ref, gn_ref, o_ref, s_ref, st):
    c = pl.program_id(1)
    C = GDN_CHUNK
    d = GDN_HEAD_DIM

    @pl.when(c == 0)
    def _():
        st[...] = jnp.zeros_like(st)

    chains = [(bi, h) for bi in range(GDN_SCAN_BATCH) for h in range(GDN_HEADS)]
    hsl = lambda h: slice(h * d, (h + 1) * d)
    s_f = [st[bi, h] for bi, h in chains]
    s_b = [s.astype(BF16) for s in s_f]
    ws = [jnp.dot(w_ref[bi, :, hsl(h)], s_b[i], preferred_element_type=F32) for i, (bi, h) in enumerate(chains)]
    qs = [jnp.dot(qg_ref[bi, :, hsl(h)], s_b[i], preferred_element_type=F32) for i, (bi, h) in enumerate(chains)]
    vb = [(u_ref[bi, :, hsl(h)] - ws[i]).astype(BF16) for i, (bi, h) in enumerate(chains)]
    iv = [jnp.dot(in_ref[bi, :, h * C:(h + 1) * C], vb[i], preferred_element_type=F32)
          for i, (bi, h) in enumerate(chains)]
    kv = [lax.dot_general(kd_ref[bi, :, hsl(h)], vb[i], (((0,), (0,)), ((), ())), preferred_element_type=F32)
          for i, (bi, h) in enumerate(chains)]
    for i, (bi, h) in enumerate(chains):
        st[bi, h] = s_f[i] * eg_ref[bi, 0, h:h + 1, :] + kv[i]
        o_ref[bi, :, hsl(h)] = _rms(qs[i] + iv[i], gn_ref[...]) * _silu(z_ref[bi, :, hsl(h)])
    s_ref[...] = st[...]


def _gdn_prompt(qkvg, z, ab, conv_w, alog, dtb, gnorm):
    b, l, _ = qkvg.shape
    C = GDN_CHUNK
    nc = l // C
    rows = GDN_PREP_CHUNKS * C
    cur = lambda i, j: (i, j, 0)
    const2 = lambda i, j: (0, 0)
    u, w, qg, kd, intra, eg = pl.pallas_call(
        _gdn_prep_kernel,
        out_shape=[jax.ShapeDtypeStruct((b, l, GDN_W), F32),
                   jax.ShapeDtypeStruct((b, l, GDN_W), BF16),
                   jax.ShapeDtypeStruct((b, l, GDN_W), BF16),
                   jax.ShapeDtypeStruct((b, l, GDN_W), BF16),
                   jax.ShapeDtypeStruct((b, l, GDN_HEADS * C), BF16),
                   jax.ShapeDtypeStruct((b, nc, GDN_HEADS, LANES), F32)],
        grid=(b, l // rows),
        in_specs=[pl.BlockSpec((1, rows, GDN_CONV_W), cur),
                  pl.BlockSpec((1, _HALO, GDN_CONV_W),
                               lambda i, j: (i, jnp.maximum(j * (rows // _HALO) - 1, 0), 0)),
                  pl.BlockSpec((1, rows, LANES), cur),
                  pl.BlockSpec((CONV_WIDTH, GDN_CONV_W), const2),
                  pl.BlockSpec((1, LANES), const2),
                  pl.BlockSpec((1, LANES), const2)],
        out_specs=[pl.BlockSpec((1, rows, GDN_W), cur),
                   pl.BlockSpec((1, rows, GDN_W), cur),
                   pl.BlockSpec((1, rows, GDN_W), cur),
                   pl.BlockSpec((1, rows, GDN_W), cur),
                   pl.BlockSpec((1, rows, GDN_HEADS * C), cur),
                   pl.BlockSpec((1, GDN_PREP_CHUNKS, GDN_HEADS, LANES), lambda i, j: (i, j, 0, 0))],
        scratch_shapes=[pltpu.VMEM((_HALO + rows, GDN_CONV_W), F32), pltpu.VMEM((rows, GDN_CONV_W), F32)],
        compiler_params=_cparams(("parallel", "parallel")),
        name="gdn_prep",
    )(qkvg, qkvg, ab, conv_w, alog, dtb)
    bb = GDN_SCAN_BATCH
    seq = lambda wd: pl.BlockSpec((bb, C, wd), cur)
    return pl.pallas_call(
        _gdn_scan_kernel,
        out_shape=[jax.ShapeDtypeStruct((b, l, GDN_W), F32),
                   jax.ShapeDtypeStruct((b, GDN_HEADS, GDN_HEAD_DIM, GDN_HEAD_DIM), F32)],
        grid=(b // bb, nc),
        in_specs=[seq(GDN_W), seq(GDN_W), seq(GDN_W), seq(GDN_W), seq(GDN_HEADS * C),
                  pl.BlockSpec((bb, 1, GDN_HEADS, LANES), lambda i, j: (i, j, 0, 0)),
                  seq(GDN_W),
                  pl.BlockSpec((1, GDN_HEAD_DIM), const2)],
        out_specs=[seq(GDN_W),
                   pl.BlockSpec((bb, GDN_HEADS, GDN_HEAD_DIM, GDN_HEAD_DIM), lambda i, j: (i, 0, 0, 0))],
        scratch_shapes=[pltpu.VMEM((bb, GDN_HEADS, GDN_HEAD_DIM, GDN_HEAD_DIM), F32)],
        compiler_params=_cparams(("parallel", "arbitrary")),
        name="gdn_scan",
    )(u, w, qg, kd, intra, eg, z, gnorm)


def _gdn_decode_kernel(bs, x_ref, cs_ref, z_ref, ab_ref, s_ref, cw_ref, alog_ref, dtb_ref, gn_ref, o_ref, so_ref):
    x = x_ref[...]
    y = x * cw_ref[CONV_WIDTH - 1:CONV_WIDTH, :]
    for j in range(CONV_WIDTH - 1):
        y = y + cs_ref[:, j, :] * cw_ref[j:j + 1, :]
    y = _silu(y)
    g_all, beta_all = _gate_rows(ab_ref[...], alog_ref, dtb_ref)
    eg_all = jnp.exp(g_all)
    z = z_ref[...]
    d = GDN_HEAD_DIM
    eye = (lax.broadcasted_iota(I32, (d, d), 0) == lax.broadcasted_iota(I32, (d, d), 1)).astype(F32)
    for h in range(GDN_HEADS):
        hs = slice(h * d, (h + 1) * d)
        qh = _l2(y[:, h * d:(h + 1) * d]) * (d ** -0.5)
        kh = _l2(y[:, GDN_W + h * d:GDN_W + (h + 1) * d])
        vh = y[:, 2 * GDN_W + h * d:2 * GDN_W + (h + 1) * d]
        kt = _hdot_nt(eye, kh)
        qt = _hdot_nt(eye, qh)
        outs = []
        for b in range(bs):
            s_b = s_ref[b, h]
            kc = kt[:, b:b + 1]
            eg = eg_all[b:b + 1, h:h + 1]
            beta = beta_all[b:b + 1, GDN_HEADS + h:GDN_HEADS + h + 1]
            sk = jnp.sum(s_b * kc, axis=0, keepdims=True)
            v_new = beta * (vh[b:b + 1] - eg * sk)
            s_new = eg * s_b + kc * v_new
            so_ref[b, h] = s_new
            outs.append(jnp.sum(s_new * qt[:, b:b + 1], axis=0, keepdims=True))
        o = jnp.concatenate(outs, axis=0)
        o_ref[:, hs] = _rms(o, gn_ref[...]) * _silu(z[:, hs])


def _gdn_decode(qkvg, z, ab, state, conv_state, conv_w, alog, dtb, gnorm, bs=8):
    b = qkvg.shape[0]
    row = lambda w: pl.BlockSpec((bs, w), lambda i: (i, 0))
    full = lambda *shape: pl.BlockSpec(shape, lambda i: (0,) * len(shape))
    sspec = pl.BlockSpec((bs, GDN_HEADS, GDN_HEAD_DIM, GDN_HEAD_DIM), lambda i: (i, 0, 0, 0))
    return pl.pallas_call(
        functools.partial(_gdn_decode_kernel, bs),
        out_shape=[jax.ShapeDtypeStruct((b, GDN_W), F32),
                   jax.ShapeDtypeStruct(state.shape, F32)],
        grid=(b // bs,),
        in_specs=[row(GDN_CONV_W),
                  pl.BlockSpec((bs, CONV_WIDTH - 1, GDN_CONV_W), lambda i: (i, 0, 0)),
                  row(GDN_W), row(LANES), sspec,
                  full(CONV_WIDTH, GDN_CONV_W), full(1, LANES), full(1, LANES), full(1, GDN_HEAD_DIM)],
        out_specs=[row(GDN_W), sspec],
        compiler_params=_cparams(("parallel",)),
        name="gdn_decode",
    )(qkvg, conv_state, z, ab, state, conv_w, alog, dtb, gnorm)


def _memkv_kernel(x_ref, g_ref, w_ref, kn_ref, k_ref, v_ref):
    hb = _rms(x_ref[...], g_ref[...]).astype(BF16)
    for hd in range(MEM_HEADS):
        hs = slice(hd * MEM_HEAD_DIM, (hd + 1) * MEM_HEAD_DIM)
        k_ref[:, hs] = _rms(jnp.dot(hb, w_ref[:, hs], preferred_element_type=F32), kn_ref[...])
    v_ref[...] = jnp.dot(hb, w_ref[:, BRANCH_W:], preferred_element_type=F32)


def _memkv(mem2, gain, w_kv, kn, tm=512):
    t = mem2.shape[0]
    return pl.pallas_call(
        _memkv_kernel,
        out_shape=[jax.ShapeDtypeStruct((t, BRANCH_W), F32)] * 2,
        grid=(t // tm,),
        in_specs=[pl.BlockSpec((tm, D_MODEL), lambda i: (i, 0)),
                  pl.BlockSpec((1, D_MODEL), lambda i: (0, 0)),
                  pl.BlockSpec((D_MODEL, 2 * BRANCH_W), lambda i: (0, 0)),
                  pl.BlockSpec((1, MEM_HEAD_DIM), lambda i: (0, 0))],
        out_specs=[pl.BlockSpec((tm, BRANCH_W), lambda i: (i, 0))] * 2,
        compiler_params=_cparams(("parallel",)),
        name="memkv",
    )(mem2, gain, w_kv, kn)


def _memattn_prompt_kernel(q_ref, k_ref, v_ref, qn_ref, o_ref):
    q = q_ref[0]
    k = k_ref[0]
    v = v_ref[0]
    scale = MEM_HEAD_DIM ** -0.5
    for hd in range(MEM_HEADS):
        hs = slice(hd * MEM_HEAD_DIM, (hd + 1) * MEM_HEAD_DIM)
        s = _bdot_nt(_rms(q[:, hs], qn_ref[...]), k[:, hs]) * scale
        p = jnp.exp(s - jnp.max(s, axis=-1, keepdims=True))
        o_ref[0, :, hs] = _bdot(p, v[:, hs]) / jnp.sum(p, axis=-1, keepdims=True)


def _memattn_prompt(qm, mk, mv, qn):
    b, l, _ = qm.shape
    tq = 512 if l % 512 == 0 else WINDOW
    return pl.pallas_call(
        _memattn_prompt_kernel,
        out_shape=jax.ShapeDtypeStruct((b, l, BRANCH_W), F32),
        grid=(b, l // tq),
        in_specs=[pl.BlockSpec((1, tq, BRANCH_W), lambda i, j: (i, j, 0)),
                  pl.BlockSpec((1, MEM_LEN, BRANCH_W), lambda i, j: (i, 0, 0)),
                  pl.BlockSpec((1, MEM_LEN, BRANCH_W), lambda i, j: (i, 0, 0)),
                  pl.BlockSpec((1, MEM_HEAD_DIM), lambda i, j: (0, 0))],
        out_specs=pl.BlockSpec((1, tq, BRANCH_W), lambda i, j: (i, j, 0)),
        compiler_params=_cparams(("parallel", "parallel")),
        name="memattn_prompt",
    )(qm, mk, mv, qn)


def _memattn_decode_kernel(bs, q_ref, k_ref, v_ref, qn_ref, o_ref):
    scale = MEM_HEAD_DIM ** -0.5
    for b in range(bs):
        qn = _rms(q_ref[b], qn_ref[...])
        s = jnp.sum(k_ref[b] * qn, axis=-1, keepdims=True) * scale
        p = jnp.exp(s - jnp.max(s, axis=0, keepdims=True))
        den = jnp.sum(p, axis=0)
        o_ref[b] = jnp.sum(p * v_ref[b], axis=0) / den


def _memattn_decode(qm, ck, cv, qn, bs=8):
    b = qm.shape[0]
    q3 = pl.BlockSpec((bs, MEM_HEADS, MEM_HEAD_DIM), lambda i: (i, 0, 0))
    kv = pl.BlockSpec((bs, MEM_LEN, MEM_HEADS, MEM_HEAD_DIM), lambda i: (i, 0, 0, 0))
    o = pl.pallas_call(
        functools.partial(_memattn_decode_kernel, bs),
        out_shape=jax.ShapeDtypeStruct((b, MEM_HEADS, MEM_HEAD_DIM), F32),
        grid=(b // bs,),
        in_specs=[q3, kv, kv, pl.BlockSpec((1, MEM_HEAD_DIM), lambda i: (0, 0))],
        out_specs=q3,
        compiler_params=_cparams(("parallel",)),
        name="memattn_decode",
    )(qm.reshape(b, MEM_HEADS, MEM_HEAD_DIM), ck, cv, qn)
    return o.reshape(b, BRANCH_W)


def _merge_kernel(oa_ref, ob_ref, oc_ref, gate_ref, x_ref, wb_ref, wo_ref, g2_ref, wr_ref, br_ref,
                  x1_ref, h2_ref, lg_ref):
    acc = None
    for i, o_ref in enumerate((oa_ref, ob_ref, oc_ref)):
        mixed = jnp.dot(o_ref[...].astype(BF16), wb_ref[i], preferred_element_type=F32)
        term = jax.nn.sigmoid(gate_ref[:, i * D_MODEL:(i + 1) * D_MODEL]) * mixed
        acc = term if acc is None else acc + term
    x1 = x_ref[...] + jnp.dot(acc.astype(BF16), wo_ref[...], preferred_element_type=F32)
    x1_ref[...] = x1
    h2 = _rms(x1, g2_ref[...])
    h2_ref[...] = h2.astype(BF16)
    h_hi = h2.astype(BF16)
    h_lo = (h2 - h_hi.astype(F32)).astype(BF16)
    nt = (((1,), (1,)), ((), ()))
    a = lax.dot_general(wr_ref[...], h_hi, nt, preferred_element_type=F32)
    b = lax.dot_general(wr_ref[0:N_EXPERTS, :], h_lo, nt, preferred_element_type=F32)
    lg_ref[...] = a[0:N_EXPERTS] + a[N_EXPERTS:] + b + br_ref[...]


def _merge(oa, ob, oc, gate, x2, wb, wo, g2, wr_t, br_col, tm):
    t = x2.shape[0]
    row = lambda w: pl.BlockSpec((tm, w), lambda i: (i, 0))
    full = lambda *shape: pl.BlockSpec(shape, lambda i: (0,) * len(shape))
    return pl.pallas_call(
        _merge_kernel,
        out_shape=[jax.ShapeDtypeStruct((t, D_MODEL), F32),
                   jax.ShapeDtypeStruct((t, D_MODEL), BF16),
                   jax.ShapeDtypeStruct((N_EXPERTS, t), F32)],
        grid=(t // tm,),
        in_specs=[row(BRANCH_W), row(BRANCH_W), row(BRANCH_W), row(N_BRANCH * D_MODEL), row(D_MODEL),
                  full(N_BRANCH, BRANCH_W, D_MODEL), full(D_MODEL, D_MODEL), full(1, D_MODEL),
                  full(2 * N_EXPERTS, D_MODEL), full(N_EXPERTS, 1)],
        out_specs=[row(D_MODEL), row(D_MODEL), pl.BlockSpec((N_EXPERTS, tm), lambda i: (0, i))],
        compiler_params=_cparams(("parallel",)),
        name="merge",
    )(oa, ob, oc, gate, x2, wb, wo, g2, wr_t, br_col)


def _col_to_row(col):
    n = col.shape[0]
    r = lax.broadcasted_iota(I32, (n, n), 0)
    c = lax.broadcasted_iota(I32, (n, n), 1)
    return jnp.sum(jnp.where(r == c, col, 0.0), axis=0, keepdims=True)


def _row_to_col(row):
    n = row.shape[1]
    r = lax.broadcasted_iota(I32, (n, n), 0)
    c = lax.broadcasted_iota(I32, (n, n), 1)
    return jnp.sum(jnp.where(r == c, row, 0.0), axis=1, keepdims=True)


def _lane_pad(row):
    return jnp.concatenate([row, jnp.zeros((1, LANES - row.shape[1]), row.dtype)], axis=1)


def _route_kernel(lg_ref, pos_ref, gate_ref, meta_ref, cnt_ref, carry):
    i = pl.program_id(0)
    tn = lg_ref.shape[1]

    @pl.when(i == 0)
    def _():
        carry[...] = jnp.zeros_like(carry)

    l = lg_ref[...]
    eio = lax.broadcasted_iota(I32, (N_EXPERTS, tn), 0)
    hot = jnp.zeros((N_EXPERTS, tn), F32)
    vals, idxs = [], []
    for _ in range(TOP_K):
        m = jnp.max(l, axis=0, keepdims=True)
        idx = jnp.min(jnp.where(l == m, eio, N_EXPERTS), axis=0, keepdims=True)
        sel = eio == idx
        vals.append(m)
        idxs.append(idx)
        hot = hot + sel.astype(F32)
        l = jnp.where(sel, -jnp.inf, l)
    ex = [jnp.exp(v - vals[0]) for v in vals]
    tot = ex[0] + ex[1] + ex[2] + ex[3]
    before = (lax.broadcasted_iota(I32, (tn, tn), 0) < lax.broadcasted_iota(I32, (tn, tn), 1)).astype(BF16)
    within = jnp.dot(hot.astype(BF16), before, preferred_element_type=F32)
    cnt_col = jnp.sum(hot, axis=1, keepdims=True)
    room = jnp.floor((cnt_col + (MOE_UNIT - 1)) * (1.0 / MOE_UNIT)) * MOE_UNIT
    r = lax.broadcasted_iota(I32, (N_EXPERTS, N_EXPERTS), 0)
    c = lax.broadcasted_iota(I32, (N_EXPERTS, N_EXPERTS), 1)
    start_col = _row_to_col(jnp.sum(jnp.where(r < c, room, 0.0), axis=0, keepdims=True))
    for k in range(TOP_K):
        gate_ref[k:k + 1, :] = ex[k] / tot
        pos_ref[k:k + 1, :] = jnp.sum(jnp.where(eio == idxs[k], start_col + within, 0.0), axis=0,
                                      keepdims=True).astype(I32)
    meta_ref[0] = _lane_pad(jnp.concatenate([_col_to_row(room), _col_to_row(carry[:, 0:1])], axis=1)).astype(I32)
    carry[...] = carry[...] + room
    cnt_ref[...] = carry[...]


def _route(logits_t, tn):
    t = logits_t.shape[1]
    kt = pl.BlockSpec((TOP_K, tn), lambda i: (0, i))
    return pl.pallas_call(
        _route_kernel,
        out_shape=[jax.ShapeDtypeStruct((TOP_K, t), I32),
                   jax.ShapeDtypeStruct((TOP_K, t), F32),
                   jax.ShapeDtypeStruct((t // tn, 1, LANES), I32),
                   jax.ShapeDtypeStruct((N_EXPERTS, LANES), F32)],
        grid=(t // tn,),
        in_specs=[pl.BlockSpec((N_EXPERTS, tn), lambda i: (0, i))],
        out_specs=[kt, kt, pl.BlockSpec((1, 1, LANES), lambda i: (i, 0, 0)),
                   pl.BlockSpec((N_EXPERTS, LANES), lambda i: (0, 0))],
        scratch_shapes=[pltpu.VMEM((N_EXPERTS, LANES), F32)],
        compiler_params=_cparams(("arbitrary",)),
        name="moe_route",
    )(logits_t)


def _layout_kernel(bm, cnt_ref, be_ref, na_ref, ps_ref, pe_ref, ct_ref):
    nbp = be_ref.shape[1]
    cnt_col = cnt_ref[:, 0:1]
    size = jnp.floor((cnt_col + (bm - 1)) * (1.0 / bm)) * bm
    r = lax.broadcasted_iota(I32, (N_EXPERTS, N_EXPERTS), 0)
    c = lax.broadcasted_iota(I32, (N_EXPERTS, N_EXPERTS), 1)
    ends_row = jnp.sum(jnp.where(r <= c, size, 0.0), axis=0, keepdims=True)
    ends_col = _row_to_col(ends_row)
    nact = ends_row[:, N_EXPERTS - 1:N_EXPERTS] * (1.0 / bm)
    blk = jnp.minimum(lax.broadcasted_iota(I32, (1, nbp), 1).astype(F32), nact - 1.0)
    be = jnp.sum((ends_col <= blk * bm).astype(F32), axis=0, keepdims=True)
    be_ref[...] = jnp.minimum(be, N_EXPERTS - 1.0).astype(I32)
    na_ref[...] = jnp.broadcast_to(nact, na_ref.shape).astype(I32)
    ps_ref[...] = _lane_pad(ends_row - _col_to_row(size)).astype(I32)
    pe_ref[...] = _lane_pad(ends_row).astype(I32)
    ct_ref[...] = _lane_pad(_col_to_row(cnt_col)).astype(I32)


def _layout(cnt, nbp, bm):
    row = jax.ShapeDtypeStruct((1, LANES), I32)
    return pl.pallas_call(
        functools.partial(_layout_kernel, bm),
        out_shape=[jax.ShapeDtypeStruct((1, nbp), I32), row, row, row, row],
        name="moe_layout",
    )(cnt)


def _stage_rows(tn):
    return tn * TOP_K + N_EXPERTS * MOE_UNIT


def _run_copies(meta_ref, ps_ref, make_copy):
    src = jnp.int32(0)
    total = jnp.int32(0)
    for e in range(N_EXPERTS):
        units = lax.shift_right_logical(meta_ref[0, 0, e], 3)
        dst = ps_ref[e] + meta_ref[0, 0, N_EXPERTS + e]

        def body(u, carry, src=src, dst=dst):
            make_copy(pl.multiple_of(src + u * MOE_UNIT, MOE_UNIT),
                      pl.multiple_of(dst + u * MOE_UNIT, MOE_UNIT)).start()
            return carry
        lax.fori_loop(0, units, body, 0)
        src = src + units * MOE_UNIT
        total = total + units
    return total


def _wait_copies(count, make_copy):
    def body(u, carry):
        make_copy(0, 0).wait()
        return carry
    lax.fori_loop(0, count, body, 0)


def _dispatch_kernel(ps_ref, pe_ref, ct_ref, meta_ref, pos_ref, h_ref, xg_ref, stage, zrows, zblock, started, sem,
                     zsem):
    i = pl.program_id(0)
    n = pl.num_programs(0)
    slot = i % 2
    rows, tn = stage.shape[1], pos_ref.shape[1]

    def copy(s):
        return lambda src, dst: pltpu.make_async_copy(stage.at[s, pl.ds(src, MOE_UNIT)],
                                                      xg_ref.at[pl.ds(dst, MOE_UNIT)], sem.at[s])

    @pl.when(i >= 2)
    def _():
        _wait_copies(started[slot], copy(slot))

    srow = lax.broadcasted_iota(I32, (rows, tn), 0)
    hit = srow == pos_ref[0:1, :]
    for k in range(1, TOP_K):
        hit = hit | (srow == pos_ref[k:k + 1, :])
    onehot = jnp.where(hit, 1.0, 0.0).astype(BF16)
    stage[slot] = jnp.dot(onehot, h_ref[...], preferred_element_type=F32)
    started[slot] = _run_copies(meta_ref, ps_ref, copy(slot))

    @pl.when(i == n - 1)
    def _():
        _wait_copies(started[slot], copy(slot))

        @pl.when(n >= 2)
        def _():
            _wait_copies(started[1 - slot], copy(1 - slot))

        zrows[...] = jnp.zeros_like(zrows)
        zero = lambda src, dst: pltpu.make_async_copy(zrows, xg_ref.at[pl.ds(dst, MOE_UNIT)], zsem)
        total = jnp.int32(0)
        for e in range(N_EXPERTS):
            lo = ps_ref[e] + ct_ref[e]
            units = lax.shift_right_logical(pe_ref[e] - lo, 3)

            def body(u, carry, lo=lo):
                zero(0, pl.multiple_of(lo + u * MOE_UNIT, MOE_UNIT)).start()
                return carry
            lax.fori_loop(0, units, body, 0)
            total = total + units
        _wait_copies(total, zero)

        zblock[...] = jnp.zeros_like(zblock)
        bm = zblock.shape[0]
        zero_block = lambda blk: pltpu.make_async_copy(zblock, xg_ref.at[pl.ds(pl.multiple_of(blk * bm, bm), bm)],
                                                       zsem)
        first = lax.shift_right_logical(pe_ref[N_EXPERTS - 1], bm.bit_length() - 1)
        last = xg_ref.shape[0] // bm

        def start_block(blk, carry):
            zero_block(blk).start()
            return carry
        lax.fori_loop(first, last, start_block, 0)

        def wait_block(blk, carry):
            zero_block(0).wait()
            return carry
        lax.fori_loop(first, last, wait_block, 0)


def _dispatch(ps, pe, ct, meta, pos, h2, n_slots, tn, bm):
    t = h2.shape[0]
    return pl.pallas_call(
        _dispatch_kernel,
        out_shape=jax.ShapeDtypeStruct((n_slots, D_MODEL), F32),
        grid_spec=pltpu.PrefetchScalarGridSpec(
            num_scalar_prefetch=3,
            grid=(t // tn,),
            in_specs=[pl.BlockSpec((1, 1, LANES), lambda i, *_: (i, 0, 0), memory_space=pltpu.SMEM),
                      pl.BlockSpec((TOP_K, tn), lambda i, *_: (0, i)),
                      pl.BlockSpec((tn, D_MODEL), lambda i, *_: (i, 0))],
            out_specs=pl.BlockSpec(memory_space=pl.ANY),
            scratch_shapes=[pltpu.VMEM((2, _stage_rows(tn), D_MODEL), F32),
                            pltpu.VMEM((MOE_UNIT, D_MODEL), F32),
                            pltpu.VMEM((bm, D_MODEL), F32),
                            pltpu.SMEM((2,), I32),
                            pltpu.SemaphoreType.DMA((2,)),
                            pltpu.SemaphoreType.DMA(())]),
        compiler_params=_cparams(("arbitrary",)),
        name="moe_dispatch",
    )(ps, pe, ct, meta, pos, h2)


def _expert_kernel(be_ref, na_ref, x_ref, w1_ref, b1_ref, w2_ref, b2_ref, y_ref, w1s, w2s):
    i = pl.program_id(0)
    e = be_ref[i]
    prev = be_ref[jnp.maximum(i - 1, 0)]

    @pl.when((i == 0) | (e != prev))
    def _():
        rows = 128
        for r0 in range(0, D_MODEL, rows):
            w1s[r0:r0 + rows, :] = w1_ref[r0:r0 + rows, :].astype(BF16)
        for r0 in range(0, D_FF, rows):
            w2s[r0:r0 + rows, :] = w2_ref[r0:r0 + rows, :].astype(BF16)

    @pl.when(i < na_ref[0])
    def _():
        hmid = jnp.dot(x_ref[...].astype(BF16), w1s[...], preferred_element_type=F32) + b1_ref[...]
        glu = jnp.minimum(hmid[:, :D_FF], SWIGLU_LIMIT)
        lin = jnp.clip(hmid[:, D_FF:], -SWIGLU_LIMIT, SWIGLU_LIMIT)
        act = glu * jax.nn.sigmoid(SWIGLU_ALPHA * glu) * (lin + 1.0)
        y_ref[...] = jnp.dot(act.astype(BF16), w2s[...], preferred_element_type=F32) + b2_ref[...]

    @pl.when(i >= na_ref[0])
    def _():
        y_ref[...] = jnp.zeros_like(y_ref)


def _experts(be, nact, xg, layer, w1, b1, w2, b2, bm):
    n_slots = xg.shape[0]
    nb = n_slots // bm
    blk = lambda i, be_r, na_r: (jnp.minimum(i, na_r[0] - 1), 0)
    wsel = lambda i, be_r, na_r: (layer, be_r[i], 0, 0)
    return pl.pallas_call(
        _expert_kernel,
        out_shape=jax.ShapeDtypeStruct((n_slots, D_MODEL), F32),
        grid_spec=pltpu.PrefetchScalarGridSpec(
            num_scalar_prefetch=2,
            grid=(nb,),
            in_specs=[pl.BlockSpec((bm, D_MODEL), blk),
                      pl.BlockSpec((None, None, D_MODEL, 2 * D_FF), wsel),
                      pl.BlockSpec((None, None, 1, 2 * D_FF), wsel),
                      pl.BlockSpec((None, None, D_FF, D_MODEL), wsel),
                      pl.BlockSpec((None, None, 1, D_MODEL), wsel)],
            out_specs=pl.BlockSpec((bm, D_MODEL), lambda i, be_r, na_r: (i, 0)),
            scratch_shapes=[pltpu.VMEM((D_MODEL, 2 * D_FF), BF16), pltpu.VMEM((D_FF, D_MODEL), BF16)]),
        compiler_params=_cparams(("arbitrary",)),
        name="moe_experts",
    )(be, nact, xg, w1, b1, w2, b2)


def _combine_kernel(ps_ref, mcur_ref, mnext_ref, pos_ref, gate_ref, x1_ref, yg_ref, o_ref, stage, started, sem):
    i = pl.program_id(0)
    n = pl.num_programs(0)
    slot = i % 2
    rows, tn = stage.shape[1], pos_ref.shape[1]

    @pl.when(i == 0)
    def _():
        stage[...] = jnp.zeros_like(stage)

    def copy(s):
        return lambda src, dst: pltpu.make_async_copy(yg_ref.at[pl.ds(dst, MOE_UNIT)],
                                                      stage.at[s, pl.ds(src, MOE_UNIT)], sem.at[s])

    @pl.when(i == 0)
    def _():
        started[0] = _run_copies(mcur_ref, ps_ref, copy(0))

    @pl.when(i + 1 < n)
    def _():
        started[1 - slot] = _run_copies(mnext_ref, ps_ref, copy(1 - slot))

    _wait_copies(started[slot], copy(slot))

    eye = (lax.broadcasted_iota(I32, (tn, tn), 0) == lax.broadcasted_iota(I32, (tn, tn), 1)).astype(F32)
    cols = _hdot_nt(eye, jnp.concatenate([pos_ref[...].astype(F32), gate_ref[...]], axis=0))
    lane = lax.broadcasted_iota(I32, (tn, rows), 1)
    weight = jnp.zeros((tn, rows), F32)
    for k in range(TOP_K):
        weight = weight + jnp.where(lane == cols[:, k:k + 1].astype(I32), cols[:, TOP_K + k:TOP_K + k + 1], 0.0)
    w_hi = weight.astype(BF16)
    w_lo = (weight - w_hi.astype(F32)).astype(BF16)
    y = stage[slot].astype(BF16)
    o_ref[...] = (x1_ref[...] + jnp.dot(w_hi, y, preferred_element_type=F32)
                  + jnp.dot(w_lo, y, preferred_element_type=F32))


def _combine(ps, meta, pos, gates, x1, yg, tn):
    t = x1.shape[0]
    nt = t // tn
    return pl.pallas_call(
        _combine_kernel,
        out_shape=jax.ShapeDtypeStruct((t, D_MODEL), F32),
        grid_spec=pltpu.PrefetchScalarGridSpec(
            num_scalar_prefetch=1,
            grid=(nt,),
            in_specs=[pl.BlockSpec((1, 1, LANES), lambda i, *_: (i, 0, 0), memory_space=pltpu.SMEM),
                      pl.BlockSpec((1, 1, LANES), lambda i, *_: (jnp.minimum(i + 1, nt - 1), 0, 0),
                                   memory_space=pltpu.SMEM),
                      pl.BlockSpec((TOP_K, tn), lambda i, *_: (0, i)),
                      pl.BlockSpec((TOP_K, tn), lambda i, *_: (0, i)),
                      pl.BlockSpec((tn, D_MODEL), lambda i, *_: (i, 0)),
                      pl.BlockSpec(memory_space=pl.ANY)],
            out_specs=pl.BlockSpec((tn, D_MODEL), lambda i, *_: (i, 0)),
            scratch_shapes=[pltpu.VMEM((2, _stage_rows(tn), D_MODEL), F32),
                            pltpu.SMEM((2,), I32),
                            pltpu.SemaphoreType.DMA((2,))]),
        compiler_params=_cparams(("arbitrary",)),
        name="moe_combine",
    )(ps, meta, meta, pos, gates, x1, yg)


def _moe(x1, h2, logits_t, layer, w1, b1, w2, b2):
    t = x1.shape[0]
    tn = min(MOE_TOK_TILE, t)
    bm = MOE_BM if t * TOP_K >= N_EXPERTS * MOE_BM else LANES
    n_blocks = (t * TOP_K + (t // tn) * N_EXPERTS * (MOE_UNIT - 1)) // bm + N_EXPERTS + 1
    nbp = -(-n_blocks // LANES) * LANES
    pos, gates, meta, cnt = _route(logits_t, tn)
    be, nact, ps, pe, ct = _layout(cnt, nbp, bm)
    ps, pe, ct = ps[0, :N_EXPERTS], pe[0, :N_EXPERTS], ct[0, :N_EXPERTS]
    xg = _dispatch(ps, pe, ct, meta, pos, h2, n_blocks * bm, tn, bm)
    yg = _experts(be[0, :n_blocks], nact[0, :1], xg, layer, w1, b1, w2, b2, bm)
    return _combine(ps, meta, pos, gates, x1, yg, tn)


def _pack_w_in(w):
    ab = jnp.pad(w[:, _IN_MAIN:_IN_MAIN + _IN_AB], ((0, 0), (0, LANES - _IN_AB)))
    return jnp.concatenate([w[:, :_IN_MAIN], ab, w[:, _IN_MAIN + _IN_AB:]], axis=1).astype(BF16)


def _lane_row(v):
    return jnp.pad(v.astype(F32), (0, LANES - v.shape[0])).reshape(1, LANES)


def _layer_weights(ln1_gain, w_in, q_norm_swa, k_norm_swa, swa_sinks, conv_w, a_log, dt_bias, gdn_norm,
                   q_norm_mem, w_branch, w_out, ln2_gain, w_router, b_router, w_mlp1, b_mlp1, w_mlp2, b_mlp2, layer):
    depth = w_mlp1.shape[0]
    wr_hi = w_router.T.astype(BF16)
    wr_lo = (w_router.T - wr_hi.astype(F32)).astype(BF16)
    return dict(
        layer=layer, wr_t=jnp.concatenate([wr_hi, wr_lo], axis=0), b1r=b_mlp1.reshape(depth, N_EXPERTS, 1, -1), b2r=b_mlp2.reshape(depth, N_EXPERTS, 1, -1),
        ln1=ln1_gain.reshape(1, -1), wp=_pack_w_in(w_in),
        qn_s=q_norm_swa.reshape(1, -1), kn_s=k_norm_swa.reshape(1, -1), sinks=swa_sinks.reshape(1, -1),
        conv_w=conv_w, alog=_lane_row(a_log), dtb=_lane_row(dt_bias), gnorm=gdn_norm.reshape(1, -1),
        qn_m=q_norm_mem.reshape(1, -1), wb=w_branch.astype(BF16), wo=w_out.astype(BF16),
        ln2=ln2_gain.reshape(1, -1), br=b_router.reshape(-1, 1),
        w1=w_mlp1, w2=w_mlp2)


def _finish(lw, oa, ob, oc, gate, x2, tm):
    x1, h2, logits_t = _merge(oa, ob, oc, gate, x2, lw["wb"], lw["wo"], lw["ln2"], lw["wr_t"], lw["br"], tm)
    return _moe(x1, h2, logits_t, lw["layer"], lw["w1"], lw["b1r"], lw["w2"], lw["b2r"])


def _prompt_layer(x, mem, lw, mem_norm, w_mem_kv, k_norm_mem):
    b, l, d = x.shape
    x2 = x.reshape(b * l, d)
    tm = 256
    qs, ks, vs, qkvg, z, ab, qm, gate = _inproj(x2, lw["ln1"], lw["wp"], tm)
    mk, mv = _memkv(mem.reshape(b * MEM_LEN, d), mem_norm.reshape(1, -1), w_mem_kv.astype(BF16),
                    k_norm_mem.reshape(1, -1))
    r3 = lambda a: a.reshape(b, l, a.shape[-1])
    o_swa, kwin = _swa_prompt(r3(qs), r3(ks), r3(vs), lw["qn_s"], lw["kn_s"], lw["sinks"])
    o_g, s_fin = _gdn_prompt(r3(qkvg), r3(z), r3(ab), lw["conv_w"], lw["alog"], lw["dtb"], lw["gnorm"])
    o_m = _memattn_prompt(r3(qm), mk.reshape(b, MEM_LEN, BRANCH_W), mv.reshape(b, MEM_LEN, BRANCH_W), lw["qn_m"])
    y = _finish(lw, o_swa.reshape(b * l, -1), o_g.reshape(b * l, -1), o_m.reshape(b * l, -1), gate, x2, tm)
    new_k = kwin.reshape(b, WINDOW, SWA_KV_HEADS, SWA_HEAD_DIM)
    new_v = r3(vs)[:, l - WINDOW:].reshape(b, WINDOW, SWA_KV_HEADS, SWA_HEAD_DIM)
    new_conv = r3(qkvg)[:, l - (CONV_WIDTH - 1):]
    mk4 = mk.reshape(b, MEM_LEN, MEM_HEADS, MEM_HEAD_DIM)
    mv4 = mv.reshape(b, MEM_LEN, MEM_HEADS, MEM_HEAD_DIM)
    return y.reshape(b, l, d), new_k, new_v, s_fin, new_conv, mk4, mv4


def _sample_layer(x, cache_k, cache_v, state, conv_state, mem_k, mem_v, lw):
    b, l, d = x.shape
    x2 = x.reshape(b, d)
    tm = TOK_TILE
    qs, ks, vs, qkvg, z, ab, qm, gate = _inproj(x2, lw["ln1"], lw["wp"], tm)
    o_swa, new_k, new_v = _swa_decode(qs, ks, vs, cache_k, cache_v, lw["qn_s"], lw["kn_s"], lw["sinks"])
    o_g, new_s = _gdn_decode(qkvg, z, ab, state, conv_state, lw["conv_w"], lw["alog"], lw["dtb"], lw["gnorm"])
    o_m = _memattn_decode(qm, mem_k, mem_v, lw["qn_m"])
    y = _finish(lw, o_swa, o_g, o_m, gate, x2, tm)
    new_conv = jnp.concatenate([conv_state[:, 1:], qkvg[:, None, :]], axis=1)
    shp = (b, WINDOW, SWA_KV_HEADS, SWA_HEAD_DIM)
    return y.reshape(b, l, d), new_k.reshape(shp), new_v.reshape(shp), new_s, new_conv


def kernel(x_prompt, x_sample, cache_swa_k, cache_swa_v, state_gdn, state_conv, cache_mem_k, cache_mem_v, mem_prompt, ln1_gain, w_in, q_norm_swa, k_norm_swa, swa_sinks, conv_w, a_log, dt_bias, gdn_norm, q_norm_mem, k_norm_mem, mem_norm, w_mem_kv, w_branch, w_out, ln2_gain, w_router, b_router, w_mlp1, b_mlp1, w_mlp2, b_mlp2):
    xp, xs = x_prompt, x_sample
    outs = [[] for _ in range(10)]
    for l in range(ln1_gain.shape[0]):
        lw = _layer_weights(ln1_gain[l], w_in[l], q_norm_swa[l], k_norm_swa[l], swa_sinks[l], conv_w[l], a_log[l],
                            dt_bias[l], gdn_norm[l], q_norm_mem[l], w_branch[l], w_out[l], ln2_gain[l], w_router[l],
                            b_router[l], w_mlp1, b_mlp1, w_mlp2, b_mlp2, l)
        xp, pk, pv, ps, pc, mk, mv = _prompt_layer(xp, mem_prompt, lw, mem_norm[l], w_mem_kv[l], k_norm_mem[l])
        xs, sk, sv, ss, sc = _sample_layer(xs, cache_swa_k[l], cache_swa_v[l], state_gdn[l], state_conv[l],
                                           cache_mem_k[l], cache_mem_v[l], lw)
        for acc, v in zip(outs, (pk, pv, sk, sv, ps, ss, pc, sc, mk, mv)):
            acc.append(v)
    return (xp, xs) + tuple(jnp.stack(o) for o in outs)
```

```python
import functools

import numpy as np
import jax
import jax.numpy as jnp
from jax import lax
from jax.experimental import pallas as pl
from jax.experimental.pallas import tpu as pltpu

F32 = jnp.float32
BF16 = jnp.bfloat16
I32 = jnp.int32

D_MODEL = 1024
BRANCH_W = 512
SWA_HEADS = 8
SWA_KV_HEADS = 2
SWA_HEAD_DIM = 64
SWA_GROUP = SWA_HEADS // SWA_KV_HEADS
SWA_KV_W = SWA_KV_HEADS * SWA_HEAD_DIM
WINDOW = 128
GDN_HEADS = 4
GDN_HEAD_DIM = 128
GDN_W = GDN_HEADS * GDN_HEAD_DIM
GDN_CONV_W = 3 * GDN_W
GDN_CHUNK = 64
CONV_WIDTH = 4
MEM_LEN = 256
MEM_HEADS = 4
MEM_HEAD_DIM = 128
N_BRANCH = 3
N_EXPERTS = 32
TOP_K = 4
D_FF = 1024
SWIGLU_LIMIT = 7.0
SWIGLU_ALPHA = 1.702
EPS = 1e-6
PAST_LEN = 16384

LANES = 128
MOE_BM = 512
MOE_TOK_TILE = 256
MOE_UNIT = 8
TOK_TILE = 128
VMEM_LIMIT = 56 * 1024 * 1024

_SEG_QS = (0, 512)
_SEG_KS = (512, 128)
_SEG_VS = (640, 128)
_SEG_QKVG = (768, 1536)
_SEG_Z = (2304, 512)
_SEG_AB = (2816, 128)
_SEG_QM = (2944, 512)
_SEG_GATE = (3456, 3072)
_PACKED_W = 6528
_IN_MAIN = 2816
_IN_AB = 8

_NEG = -1e30
_HI = lax.Precision.HIGHEST


def _cparams(sem, vmem=VMEM_LIMIT):
    return pltpu.CompilerParams(dimension_semantics=sem, vmem_limit_bytes=vmem)


def _bdot(a, b):
    return jnp.dot(a.astype(BF16), b.astype(BF16), preferred_element_type=F32)


def _bdot_nt(a, b):
    return lax.dot_general(a.astype(BF16), b.astype(BF16), (((1,), (1,)), ((), ())),
                           preferred_element_type=F32)


def _bdot_tn(a, b):
    return lax.dot_general(a.astype(BF16), b.astype(BF16), (((0,), (0,)), ((), ())),
                           preferred_element_type=F32)


def _hdot(a, b):
    return jnp.dot(a, b, preferred_element_type=F32, precision=_HI)


def _hdot_nt(a, b):
    return lax.dot_general(a, b, (((1,), (1,)), ((), ())), preferred_element_type=F32, precision=_HI)


def _rms(x, gain):
    return x * lax.rsqrt(jnp.mean(x * x, axis=-1, keepdims=True) + EPS) * gain


def _l2(x):
    return x * lax.rsqrt(jnp.sum(x * x, axis=-1, keepdims=True) + EPS)


def _silu(x):
    return x * jax.nn.sigmoid(x)


def _softplus(x):
    return jnp.maximum(x, 0.0) + jnp.log1p(jnp.exp(-jnp.abs(x)))


def _inproj_kernel(x_ref, g_ref, w_ref, qs, ks, vs, qkvg, z, ab, qm, gate):
    x = x_ref[...]
    hb = _rms(x, g_ref[...]).astype(BF16)
    for ref, (off, width) in ((qs, _SEG_QS), (ks, _SEG_KS), (vs, _SEG_VS), (qkvg, _SEG_QKVG), (z, _SEG_Z),
                              (ab, _SEG_AB), (qm, _SEG_QM), (gate, _SEG_GATE)):
        step = min(width, 512)
        for c0 in range(0, width, step):
            ref[:, c0:c0 + step] = jnp.dot(hb, w_ref[:, off + c0:off + c0 + step], preferred_element_type=F32)


def _inproj(x2, gain, wp, tm):
    t = x2.shape[0]
    segs = (_SEG_QS, _SEG_KS, _SEG_VS, _SEG_QKVG, _SEG_Z, _SEG_AB, _SEG_QM, _SEG_GATE)
    return pl.pallas_call(
        _inproj_kernel,
        out_shape=[jax.ShapeDtypeStruct((t, w), F32) for _, w in segs],
        grid=(t // tm,),
        in_specs=[pl.BlockSpec((tm, D_MODEL), lambda i: (i, 0)),
                  pl.BlockSpec((1, D_MODEL), lambda i: (0, 0)),
                  pl.BlockSpec((D_MODEL, _PACKED_W), lambda i: (0, 0))],
        out_specs=[pl.BlockSpec((tm, w), lambda i: (i, 0)) for _, w in segs],
        compiler_params=_cparams(("parallel",)),
        name="inproj",
    )(x2, gain, wp)


SWA_STEP_BLOCKS = 4


def _alibi_slopes(n):
    return [float(2.0 ** (-8.0 * (i + 1) / n)) for i in range(n)]


def _swa_prompt_kernel(q_ref, kc_ref, vc_ref, qg_ref, kg_ref, sink_ref, o_ref, kwin_ref, bias, kprev, vprev):
    n = pl.program_id(1)
    half = SWA_HEAD_DIM
    nblk = q_ref.shape[1] // WINDOW
    work = [(j, h) for j in range(nblk) for h in range(SWA_HEADS)]

    @pl.when(n == 0)
    def _():
        kprev[...] = jnp.zeros_like(kprev)
        vprev[...] = jnp.zeros_like(vprev)

    def fill(tbl, first):
        row = lax.broadcasted_iota(I32, (WINDOW, 2 * WINDOW), 0)
        col = lax.broadcasted_iota(I32, (WINDOW, 2 * WINDOW), 1)
        dist = row + WINDOW - col
        valid = (dist >= 0) & (dist <= WINDOW) & ((col >= WINDOW) | jnp.logical_not(first))
        distf = dist.astype(F32)
        for h, slope in enumerate(_alibi_slopes(SWA_HEADS)):
            bias[tbl, h] = jnp.where(valid, -slope * distf, _NEG)

    @pl.when(n <= 1)
    def _():
        fill(0, n == 0)

    if nblk > 1:
        @pl.when(n == 0)
        def _():
            fill(1, False)

    low = lax.broadcasted_iota(I32, (1, 2 * half), 1) < half

    def pair_rms(x, gain):
        sq = x * x
        s_lo = jnp.sum(jnp.where(low, sq, 0.0), axis=-1, keepdims=True)
        s_hi = jnp.sum(jnp.where(low, 0.0, sq), axis=-1, keepdims=True)
        return x * lax.rsqrt(jnp.where(low, s_lo, s_hi) * (1.0 / half) + EPS) * gain

    kcn = pair_rms(kc_ref[0], kg_ref[...])
    kwin_ref[0] = kcn[(nblk - 1) * WINDOW:]
    kall = jnp.concatenate([kprev[...], kcn], axis=0)
    vall = jnp.concatenate([vprev[...], vc_ref[0]], axis=0)
    kprev[...] = kcn[(nblk - 1) * WINDOW:]
    vprev[...] = vc_ref[0, (nblk - 1) * WINDOW:, :]
    ones = jnp.ones(((nblk + 1) * WINDOW, 2 * half), BF16)

    def placed(x, aug):
        sw = pltpu.roll(x, half, 1)
        out = {(0, 0): jnp.where(low, x, 0.0), (1, 1): jnp.where(low, 0.0, x),
               (0, 1): jnp.where(low, 0.0, sw), (1, 0): jnp.where(low, sw, 0.0)}
        out = {key: val.astype(BF16) for key, val in out.items()}
        return {key: jnp.concatenate([val, ones], axis=1) for key, val in out.items()} if aug else out

    kvar = placed(kall, False)
    vvar = placed(vall, True)
    qn = [pair_rms(q_ref[0, :, t * 2 * half:(t + 1) * 2 * half], qg_ref[...]).astype(BF16)
          for t in range(SWA_HEADS // 2)]
    where_of = lambda h: (h // SWA_GROUP, h % 2)
    qrows = lambda j: slice(j * WINDOW, (j + 1) * WINDOW)
    krows = lambda j: slice(j * WINDOW, (j + 2) * WINDOW)
    nt = (((1,), (1,)), ((), ()))
    s = {(j, h): lax.dot_general(qn[h // 2][qrows(j)], kvar[where_of(h)][krows(j)], nt, preferred_element_type=F32)
         + bias[min(j, 1), h] for j, h in work}
    sink = [sink_ref[0:1, h:h + 1] for h in range(SWA_HEADS)]
    m = {(j, h): jnp.maximum(jnp.max(s[j, h], axis=-1, keepdims=True), sink[h]) for j, h in work}
    p = {(j, h): jnp.exp(s[j, h] - m[j, h]).astype(BF16) for j, h in work}
    res = {(j, h): jnp.dot(p[j, h], vvar[where_of(h)][krows(j)], preferred_element_type=F32)
           for j, h in work}
    inv = {(j, h): 1.0 / (res[j, h][:, 2 * half:2 * half + 1] + jnp.exp(sink[h] - m[j, h])) for j, h in work}
    for j in range(nblk):
        for t in range(SWA_HEADS // 2):
            h0, h1 = 2 * t, 2 * t + 1
            num = res[j, h0][:, :2 * half] + res[j, h1][:, :2 * half]
            o_ref[0, qrows(j), t * 2 * half:(t + 1) * 2 * half] = num * jnp.where(low, inv[j, h0], inv[j, h1])


def _swa_prompt(qs, ks, vs, qn, kn, sinks):
    b, l, _ = qs.shape
    rows = SWA_STEP_BLOCKS * WINDOW if l % (SWA_STEP_BLOCKS * WINDOW) == 0 else WINDOW
    cur = lambda i, j: (i, j, 0)
    const2 = lambda i, j: (0, 0)
    q_gain = jnp.tile(qn, (1, 2)) * (SWA_HEAD_DIM ** -0.5)
    k_gain = jnp.tile(kn, (1, 2))
    return pl.pallas_call(
        _swa_prompt_kernel,
        out_shape=[jax.ShapeDtypeStruct((b, l, BRANCH_W), F32),
                   jax.ShapeDtypeStruct((b, WINDOW, SWA_KV_W), F32)],
        grid=(b, l // rows),
        in_specs=[pl.BlockSpec((1, rows, BRANCH_W), cur),
                  pl.BlockSpec((1, rows, SWA_KV_W), cur),
                  pl.BlockSpec((1, rows, SWA_KV_W), cur),
                  pl.BlockSpec((1, 2 * SWA_HEAD_DIM), const2),
                  pl.BlockSpec((1, 2 * SWA_HEAD_DIM), const2),
                  pl.BlockSpec((1, SWA_HEADS), const2)],
        out_specs=[pl.BlockSpec((1, rows, BRANCH_W), cur),
                   pl.BlockSpec((1, WINDOW, SWA_KV_W), lambda i, j: (i, 0, 0))],
        scratch_shapes=[pltpu.VMEM((2, SWA_HEADS, WINDOW, 2 * WINDOW), F32),
                        pltpu.VMEM((WINDOW, SWA_KV_W), F32),
                        pltpu.VMEM((WINDOW, SWA_KV_W), F32)],
        compiler_params=_cparams(("arbitrary", "arbitrary")),
        name="swa_prompt",
    )(qs, ks, vs, q_gain, k_gain, sinks)


def _swa_decode_kernel(bs, q_ref, k3_ref, kf_ref, v3_ref, vf_ref, ck_ref, cv_ref, qn_ref, kn_ref, kn2_ref,
                       sink_ref, slope_ref, o_ref, ok_ref, ov_ref):
    scale = SWA_HEAD_DIM ** -0.5
    lane = lax.broadcasted_iota(I32, (1, SWA_KV_W), 1)
    rowi = lax.broadcasted_iota(I32, (WINDOW, SWA_KV_W), 0)
    keyd = (WINDOW - lax.broadcasted_iota(I32, (1, WINDOW), 1)).astype(F32)
    for b in range(bs):
        qn = _rms(q_ref[b], qn_ref[...])
        kn3 = _rms(k3_ref[b], kn_ref[...])
        v3 = v3_ref[b]
        kf = kf_ref[b]
        sq = kf * kf
        ms0 = jnp.sum(jnp.where(lane < SWA_HEAD_DIM, sq, 0.0), axis=-1, keepdims=True) / SWA_HEAD_DIM
        ms1 = jnp.sum(jnp.where(lane >= SWA_HEAD_DIM, sq, 0.0), axis=-1, keepdims=True) / SWA_HEAD_DIM
        knf = kf * lax.rsqrt(jnp.where(lane < SWA_HEAD_DIM, ms0, ms1) + EPS) * kn2_ref[...]
        ck = ck_ref[b]
        cv = cv_ref[b]
        for g in range(SWA_KV_HEADS):
            sl = slice(g * SWA_HEAD_DIM, (g + 1) * SWA_HEAD_DIM)
            hs = slice(g * SWA_GROUP, (g + 1) * SWA_GROUP)
            qg = qn[hs]
            slope = slope_ref[hs]
            sink = sink_ref[hs]
            s = _bdot_nt(qg, ck[:, sl]) * scale - slope * keyd
            s_new = jnp.sum(qg * kn3[g:g + 1], axis=-1, keepdims=True) * scale
            m = jnp.maximum(jnp.maximum(jnp.max(s, axis=-1, keepdims=True), s_new), sink)
            p = jnp.exp(s - m)
            p_new = jnp.exp(s_new - m)
            denom = jnp.sum(p, axis=-1, keepdims=True) + p_new + jnp.exp(sink - m)
            o_ref[b, hs, :] = (_bdot(p, cv[:, sl]) + p_new * v3[g:g + 1]) / denom
        last = rowi == WINDOW - 1
        ok_ref[b] = jnp.where(last, knf, pltpu.roll(ck, WINDOW - 1, 0))
        ov_ref[b] = jnp.where(last, vf_ref[b], pltpu.roll(cv, WINDOW - 1, 0))


def _swa_decode(qs, ks, vs, cache_k, cache_v, qn, kn, sinks, bs=8):
    b = qs.shape[0]
    q3 = qs.reshape(b, SWA_HEADS, SWA_HEAD_DIM)
    k3 = ks.reshape(b, SWA_KV_HEADS, SWA_HEAD_DIM)
    kf = ks.reshape(b, 1, SWA_KV_W)
    v3 = vs.reshape(b, SWA_KV_HEADS, SWA_HEAD_DIM)
    vf = vs.reshape(b, 1, SWA_KV_W)
    ck = cache_k.reshape(b, WINDOW, SWA_KV_W)
    cv = cache_v.reshape(b, WINDOW, SWA_KV_W)
    kn2 = jnp.concatenate([kn, kn], axis=-1)
    sink_col = sinks.reshape(SWA_HEADS, 1)
    slope_col = jnp.asarray(np.asarray(_alibi_slopes(SWA_HEADS), np.float32).reshape(SWA_HEADS, 1))
    blk = lambda *shape: pl.BlockSpec((bs,) + shape, lambda i: (i,) + (0,) * len(shape))
    full = lambda *shape: pl.BlockSpec(shape, lambda i: (0,) * len(shape))
    o, ok, ov = pl.pallas_call(
        functools.partial(_swa_decode_kernel, bs),
        out_shape=[jax.ShapeDtypeStruct((b, SWA_HEADS, SWA_HEAD_DIM), F32),
                   jax.ShapeDtypeStruct((b, WINDOW, SWA_KV_W), F32),
                   jax.ShapeDtypeStruct((b, WINDOW, SWA_KV_W), F32)],
        grid=(b // bs,),
        in_specs=[blk(SWA_HEADS, SWA_HEAD_DIM), blk(SWA_KV_HEADS, SWA_HEAD_DIM), blk(1, SWA_KV_W),
                  blk(SWA_KV_HEADS, SWA_HEAD_DIM), blk(1, SWA_KV_W), blk(WINDOW, SWA_KV_W), blk(WINDOW, SWA_KV_W),
                  full(1, SWA_HEAD_DIM), full(1, SWA_HEAD_DIM), full(1, SWA_KV_W),
                  full(SWA_HEADS, 1), full(SWA_HEADS, 1)],
        out_specs=[blk(SWA_HEADS, SWA_HEAD_DIM), blk(WINDOW, SWA_KV_W), blk(WINDOW, SWA_KV_W)],
        compiler_params=_cparams(("parallel",)),
        name="swa_decode",
    )(q3, k3, kf, v3, vf, ck, cv, qn, kn, kn2, sink_col, slope_col)
    return o.reshape(b, BRANCH_W), ok, ov


def _gate_rows(ab, alog_ref, dtb_ref):
    g = -jnp.exp(alog_ref[...]) * _softplus(ab + dtb_ref[...])
    return g, jax.nn.sigmoid(ab)


GDN_PREP_CHUNKS = 4
GDN_SCAN_BATCH = 8
_HALO = 8


def _gdn_prep_kernel(x_ref, xh_ref, ab_ref, cw_ref, alog_ref, dtb_ref,
                     u_ref, w_ref, qg_ref, kd_ref, in_ref, eg_ref, ys):
    j = pl.program_id(1)
    C = GDN_CHUNK
    d = GDN_HEAD_DIM
    n = GDN_PREP_CHUNKS * C
    heads = range(GDN_HEADS)
    halo = jnp.where(j > 0, xh_ref[0], 0.0)
    sub = lax.broadcasted_iota(I32, (_HALO, 1), 0)
    for ci in range(GDN_PREP_CHUNKS):
        r0 = ci * C
        x = x_ref[0, r0:r0 + C, :]
        before = halo if ci == 0 else x_ref[0, r0 - _HALO:r0, :]
        y = x * cw_ref[CONV_WIDTH - 1:CONV_WIDTH, :]
        top = x[0:_HALO] * cw_ref[CONV_WIDTH - 1:CONV_WIDTH, :]
        for jj in range(CONV_WIDTH - 1):
            sh = CONV_WIDTH - 1 - jj
            y = y + pltpu.roll(x, sh, 0) * cw_ref[jj:jj + 1, :]
            first = jnp.where(sub < sh, pltpu.roll(before, sh, 0), pltpu.roll(x[0:_HALO], sh, 0))
            top = top + first * cw_ref[jj:jj + 1, :]
        ys[r0:r0 + C, :] = _silu(y)
        ys[r0:r0 + _HALO, :] = _silu(top)
    g_all, beta_all = _gate_rows(ab_ref[0], alog_ref, dtb_ref)
    r = lax.broadcasted_iota(I32, (n, n), 0)
    cc = lax.broadcasted_iota(I32, (n, n), 1)
    same = (r // C) == (cc // C)
    incl = same & (r >= cc)
    strict = same & (r > cc)
    upto = same & (r <= cc)
    eye = r == cc
    q = [_l2(ys[:, h * d:(h + 1) * d]) * (d ** -0.5) for h in heads]
    k = [_l2(ys[:, GDN_W + h * d:GDN_W + (h + 1) * d]) for h in heads]
    beta = [beta_all[:, GDN_HEADS + h:GDN_HEADS + h + 1] for h in heads]
    gc_row = [jnp.sum(jnp.where(upto, g_all[:, h:h + 1], 0.0), axis=0, keepdims=True) for h in heads]
    gc_col = [jnp.sum(jnp.where(eye, gc_row[h], 0.0), axis=1, keepdims=True) for h in heads]
    decay = [jnp.where(incl, jnp.exp(jnp.where(incl, gc_col[h] - gc_row[h], 0.0)), 0.0) for h in heads]
    kb = [k[h] * beta[h] for h in heads]
    bp = [-jnp.where(strict, _bdot_nt(kb[h], k[h]) * decay[h], 0.0) for h in heads]
    intra = [jnp.where(incl, _bdot_nt(q[h], k[h]) * decay[h], 0.0) for h in heads]
    p = [eye.astype(F32) + bp[h] for h in heads]
    span = 2
    while span < C:
        bp = [_bdot(bp[h], bp[h]) for h in heads]
        p = [p[h] + _bdot(p[h], bp[h]) for h in heads]
        span *= 2
    uw = [_bdot(p[h], jnp.concatenate([ys[:, 2 * GDN_W + h * d:2 * GDN_W + (h + 1) * d] * beta[h],
                                       kb[h] * jnp.exp(gc_col[h])], axis=1)) for h in heads]
    for h in heads:
        hs = slice(h * d, (h + 1) * d)
        u_ref[0, :, hs] = uw[h][:, :d]
        w_ref[0, :, hs] = uw[h][:, d:].astype(BF16)
        qg_ref[0, :, hs] = (q[h] * jnp.exp(gc_col[h])).astype(BF16)
        for ci in range(GDN_PREP_CHUNKS):
            r0 = ci * C
            g_last = gc_row[h][:, r0 + C - 1:r0 + C]
            kd_ref[0, r0:r0 + C, hs] = (k[h][r0:r0 + C] * jnp.exp(g_last - gc_col[h][r0:r0 + C])).astype(BF16)
            in_ref[0, r0:r0 + C, h * C:(h + 1) * C] = intra[h][r0:r0 + C, r0:r0 + C].astype(BF16)
            eg_ref[0, ci, h:h + 1, :] = jnp.broadcast_to(jnp.exp(g_last), (1, LANES))


def _gdn_scan_kernel(u_ref, w_ref, qg_ref, kd_ref, in_ref, eg_ref, z_ref, gn_ref, o_ref, s_ref, st):
    c = pl.program_id(1)
    C = GDN_CHUNK
    d = GDN_HEAD_DIM

    @pl.when(c == 0)
    def _():
        st[...] = jnp.zeros_like(st)

    chains = [(bi, h) for bi in range(u_ref.shape[0]) for h in range(GDN_HEADS)]
    hsl = lambda h: slice(h * d, (h + 1) * d)
    s_f = [st[bi, h] for bi, h in chains]
    s_b = [s.astype(BF16) for s in s_f]
    ws = [jnp.dot(w_ref[bi, :, hsl(h)], s_b[i], preferred_element_type=F32) for i, (bi, h) in enumerate(chains)]
    qs = [jnp.dot(qg_ref[bi, :, hsl(h)], s_b[i], preferred_element_type=F32) for i, (bi, h) in enumerate(chains)]
    vb = [(u_ref[bi, :, hsl(h)] - ws[i]).astype(BF16) for i, (bi, h) in enumerate(chains)]
    iv = [jnp.dot(in_ref[bi, :, h * C:(h + 1) * C], vb[i], preferred_element_type=F32)
          for i, (bi, h) in enumerate(chains)]
    kv = [lax.dot_general(kd_ref[bi, :, hsl(h)], vb[i], (((0,), (0,)), ((), ())), preferred_element_type=F32)
          for i, (bi, h) in enumerate(chains)]
    for i, (bi, h) in enumerate(chains):
        st[bi, h] = s_f[i] * eg_ref[bi, 0, h:h + 1, :] + kv[i]
        o_ref[bi, :, hsl(h)] = _rms(qs[i] + iv[i], gn_ref[...]) * _silu(z_ref[bi, :, hsl(h)])
    s_ref[...] = st[...]


def _gdn_prompt(qkvg, z, ab, conv_w, alog, dtb, gnorm):
    b, l, _ = qkvg.shape
    C = GDN_CHUNK
    nc = l // C
    rows = GDN_PREP_CHUNKS * C
    cur = lambda i, j: (i, j, 0)
    const2 = lambda i, j: (0, 0)
    u, w, qg, kd, intra, eg = pl.pallas_call(
        _gdn_prep_kernel,
        out_shape=[jax.ShapeDtypeStruct((b, l, GDN_W), F32),
                   jax.ShapeDtypeStruct((b, l, GDN_W), BF16),
                   jax.ShapeDtypeStruct((b, l, GDN_W), BF16),
                   jax.ShapeDtypeStruct((b, l, GDN_W), BF16),
                   jax.ShapeDtypeStruct((b, l, GDN_HEADS * C), BF16),
                   jax.ShapeDtypeStruct((b, nc, GDN_HEADS, LANES), F32)],
        grid=(b, l // rows),
        in_specs=[pl.BlockSpec((1, rows, GDN_CONV_W), cur),
                  pl.BlockSpec((1, _HALO, GDN_CONV_W),
                               lambda i, j: (i, jnp.maximum(j * (rows // _HALO) - 1, 0), 0)),
                  pl.BlockSpec((1, rows, LANES), cur),
                  pl.BlockSpec((CONV_WIDTH, GDN_CONV_W), const2),
                  pl.BlockSpec((1, LANES), const2),
                  pl.BlockSpec((1, LANES), const2)],
        out_specs=[pl.BlockSpec((1, rows, GDN_W), cur),
                   pl.BlockSpec((1, rows, GDN_W), cur),
                   pl.BlockSpec((1, rows, GDN_W), cur),
                   pl.BlockSpec((1, rows, GDN_W), cur),
                   pl.BlockSpec((1, rows, GDN_HEADS * C), cur),
                   pl.BlockSpec((1, GDN_PREP_CHUNKS, GDN_HEADS, LANES), lambda i, j: (i, j, 0, 0))],
        scratch_shapes=[pltpu.VMEM((rows, GDN_CONV_W), F32)],
        compiler_params=_cparams(("parallel", "parallel")),
        name="gdn_prep",
    )(qkvg, qkvg, ab, conv_w, alog, dtb)
    bb = GDN_SCAN_BATCH if b % GDN_SCAN_BATCH == 0 else 1
    seq = lambda wd: pl.BlockSpec((bb, C, wd), cur)
    return pl.pallas_call(
        _gdn_scan_kernel,
        out_shape=[jax.ShapeDtypeStruct((b, l, GDN_W), F32),
                   jax.ShapeDtypeStruct((b, GDN_HEADS, GDN_HEAD_DIM, GDN_HEAD_DIM), F32)],
        grid=(b // bb, nc),
        in_specs=[seq(GDN_W), seq(GDN_W), seq(GDN_W), seq(GDN_W), seq(GDN_HEADS * C),
                  pl.BlockSpec((bb, 1, GDN_HEADS, LANES), lambda i, j: (i, j, 0, 0)),
                  seq(GDN_W),
                  pl.BlockSpec((1, GDN_HEAD_DIM), const2)],
        out_specs=[seq(GDN_W),
                   pl.BlockSpec((bb, GDN_HEADS, GDN_HEAD_DIM, GDN_HEAD_DIM), lambda i, j: (i, 0, 0, 0))],
        scratch_shapes=[pltpu.VMEM((bb, GDN_HEADS, GDN_HEAD_DIM, GDN_HEAD_DIM), F32)],
        compiler_params=_cparams(("parallel", "arbitrary")),
        name="gdn_scan",
    )(u, w, qg, kd, intra, eg, z, gnorm)


def _gdn_decode_kernel(bs, x_ref, cs_ref, z_ref, ab_ref, s_ref, cw_ref, alog_ref, dtb_ref, gn_ref, o_ref, so_ref):
    x = x_ref[...]
    y = x * cw_ref[CONV_WIDTH - 1:CONV_WIDTH, :]
    for j in range(CONV_WIDTH - 1):
        y = y + cs_ref[:, j, :] * cw_ref[j:j + 1, :]
    y = _silu(y)
    g_all, beta_all = _gate_rows(ab_ref[...], alog_ref, dtb_ref)
    eg_all = jnp.exp(g_all)
    z = z_ref[...]
    d = GDN_HEAD_DIM
    eye = (lax.broadcasted_iota(I32, (d, d), 0) == lax.broadcasted_iota(I32, (d, d), 1)).astype(F32)
    for h in range(GDN_HEADS):
        hs = slice(h * d, (h + 1) * d)
        qh = _l2(y[:, h * d:(h + 1) * d]) * (d ** -0.5)
        kh = _l2(y[:, GDN_W + h * d:GDN_W + (h + 1) * d])
        vh = y[:, 2 * GDN_W + h * d:2 * GDN_W + (h + 1) * d]
        kt = _hdot_nt(eye, kh)
        qt = _hdot_nt(eye, qh)
        outs = []
        for b in range(bs):
            s_b = s_ref[b, h]
            kc = kt[:, b:b + 1]
            eg = eg_all[b:b + 1, h:h + 1]
            beta = beta_all[b:b + 1, GDN_HEADS + h:GDN_HEADS + h + 1]
            sk = jnp.sum(s_b * kc, axis=0, keepdims=True)
            v_new = beta * (vh[b:b + 1] - eg * sk)
            s_new = eg * s_b + kc * v_new
            so_ref[b, h] = s_new
            outs.append(jnp.sum(s_new * qt[:, b:b + 1], axis=0, keepdims=True))
        o = jnp.concatenate(outs, axis=0)
        o_ref[:, hs] = _rms(o, gn_ref[...]) * _silu(z[:, hs])


def _gdn_decode(qkvg, z, ab, state, conv_state, conv_w, alog, dtb, gnorm, bs=8):
    b = qkvg.shape[0]
    row = lambda w: pl.BlockSpec((bs, w), lambda i: (i, 0))
    full = lambda *shape: pl.BlockSpec(shape, lambda i: (0,) * len(shape))
    sspec = pl.BlockSpec((bs, GDN_HEADS, GDN_HEAD_DIM, GDN_HEAD_DIM), lambda i: (i, 0, 0, 0))
    return pl.pallas_call(
        functools.partial(_gdn_decode_kernel, bs),
        out_shape=[jax.ShapeDtypeStruct((b, GDN_W), F32),
                   jax.ShapeDtypeStruct(state.shape, F32)],
        grid=(b // bs,),
        in_specs=[row(GDN_CONV_W),
                  pl.BlockSpec((bs, CONV_WIDTH - 1, GDN_CONV_W), lambda i: (i, 0, 0)),
                  row(GDN_W), row(LANES), sspec,
                  full(CONV_WIDTH, GDN_CONV_W), full(1, LANES), full(1, LANES), full(1, GDN_HEAD_DIM)],
        out_specs=[row(GDN_W), sspec],
        compiler_params=_cparams(("parallel",)),
        name="gdn_decode",
    )(qkvg, conv_state, z, ab, state, conv_w, alog, dtb, gnorm)


def _memkv_kernel(x_ref, g_ref, w_ref, kn_ref, k_ref, v_ref):
    hb = _rms(x_ref[...], g_ref[...]).astype(BF16)
    for hd in range(MEM_HEADS):
        hs = slice(hd * MEM_HEAD_DIM, (hd + 1) * MEM_HEAD_DIM)
        k_ref[:, hs] = _rms(jnp.dot(hb, w_ref[:, hs], preferred_element_type=F32), kn_ref[...])
    v_ref[...] = jnp.dot(hb, w_ref[:, BRANCH_W:], preferred_element_type=F32)


def _memkv(mem2, gain, w_kv, kn, tm=512):
    t = mem2.shape[0]
    return pl.pallas_call(
        _memkv_kernel,
        out_shape=[jax.ShapeDtypeStruct((t, BRANCH_W), F32)] * 2,
        grid=(t // tm,),
        in_specs=[pl.BlockSpec((tm, D_MODEL), lambda i: (i, 0)),
                  pl.BlockSpec((1, D_MODEL), lambda i: (0, 0)),
                  pl.BlockSpec((D_MODEL, 2 * BRANCH_W), lambda i: (0, 0)),
                  pl.BlockSpec((1, MEM_HEAD_DIM), lambda i: (0, 0))],
        out_specs=[pl.BlockSpec((tm, BRANCH_W), lambda i: (i, 0))] * 2,
        compiler_params=_cparams(("parallel",)),
        name="memkv",
    )(mem2, gain, w_kv, kn)


def _memattn_prompt_kernel(q_ref, k_ref, v_ref, qn_ref, o_ref):
    q = q_ref[0]
    k = k_ref[0]
    v = v_ref[0]
    scale = MEM_HEAD_DIM ** -0.5
    for hd in range(MEM_HEADS):
        hs = slice(hd * MEM_HEAD_DIM, (hd + 1) * MEM_HEAD_DIM)
        s = _bdot_nt(_rms(q[:, hs], qn_ref[...]), k[:, hs]) * scale
        p = jnp.exp(s - jnp.max(s, axis=-1, keepdims=True))
        o_ref[0, :, hs] = _bdot(p, v[:, hs]) / jnp.sum(p, axis=-1, keepdims=True)


def _memattn_prompt(qm, mk, mv, qn):
    b, l, _ = qm.shape
    tq = 512 if l % 512 == 0 else WINDOW
    return pl.pallas_call(
        _memattn_prompt_kernel,
        out_shape=jax.ShapeDtypeStruct((b, l, BRANCH_W), F32),
        grid=(b, l // tq),
        in_specs=[pl.BlockSpec((1, tq, BRANCH_W), lambda i, j: (i, j, 0)),
                  pl.BlockSpec((1, MEM_LEN, BRANCH_W), lambda i, j: (i, 0, 0)),
                  pl.BlockSpec((1, MEM_LEN, BRANCH_W), lambda i, j: (i, 0, 0)),
                  pl.BlockSpec((1, MEM_HEAD_DIM), lambda i, j: (0, 0))],
        out_specs=pl.BlockSpec((1, tq, BRANCH_W), lambda i, j: (i, j, 0)),
        compiler_params=_cparams(("parallel", "parallel")),
        name="memattn_prompt",
    )(qm, mk, mv, qn)


def _memattn_decode_kernel(bs, q_ref, k_ref, v_ref, qn_ref, o_ref):
    scale = MEM_HEAD_DIM ** -0.5
    for b in range(bs):
        qn = _rms(q_ref[b], qn_ref[...])
        s = jnp.sum(k_ref[b] * qn, axis=-1, keepdims=True) * scale
        p = jnp.exp(s - jnp.max(s, axis=0, keepdims=True))
        den = jnp.sum(p, axis=0)
        o_ref[b] = jnp.sum(p * v_ref[b], axis=0) / den


def _memattn_decode(qm, ck, cv, qn, bs=8):
    b = qm.shape[0]
    q3 = pl.BlockSpec((bs, MEM_HEADS, MEM_HEAD_DIM), lambda i: (i, 0, 0))
    kv = pl.BlockSpec((bs, MEM_LEN, MEM_HEADS, MEM_HEAD_DIM), lambda i: (i, 0, 0, 0))
    o = pl.pallas_call(
        functools.partial(_memattn_decode_kernel, bs),
        out_shape=jax.ShapeDtypeStruct((b, MEM_HEADS, MEM_HEAD_DIM), F32),
        grid=(b // bs,),
        in_specs=[q3, kv, kv, pl.BlockSpec((1, MEM_HEAD_DIM), lambda i: (0, 0))],
        out_specs=q3,
        compiler_params=_cparams(("parallel",)),
        name="memattn_decode",
    )(qm.reshape(b, MEM_HEADS, MEM_HEAD_DIM), ck, cv, qn)
    return o.reshape(b, BRANCH_W)


def _merge_kernel(oa_ref, ob_ref, oc_ref, gate_ref, x_ref, wb_ref, wo_ref, g2_ref, wr_ref, br_ref,
                  x1_ref, h2_ref, lg_ref):
    acc = None
    for i, o_ref in enumerate((oa_ref, ob_ref, oc_ref)):
        mixed = jnp.dot(o_ref[...].astype(BF16), wb_ref[i], preferred_element_type=F32)
        term = jax.nn.sigmoid(gate_ref[:, i * D_MODEL:(i + 1) * D_MODEL]) * mixed
        acc = term if acc is None else acc + term
    x1 = x_ref[...] + jnp.dot(acc.astype(BF16), wo_ref[...], preferred_element_type=F32)
    x1_ref[...] = x1
    h2 = _rms(x1, g2_ref[...])
    h2_ref[...] = h2.astype(BF16)
    h_hi = h2.astype(BF16)
    h_lo = (h2 - h_hi.astype(F32)).astype(BF16)
    nt = (((1,), (1,)), ((), ()))
    a = lax.dot_general(wr_ref[...], h_hi, nt, preferred_element_type=F32)
    b = lax.dot_general(wr_ref[0:N_EXPERTS, :], h_lo, nt, preferred_element_type=F32)
    lg_ref[...] = a[0:N_EXPERTS] + a[N_EXPERTS:] + b + br_ref[...]


def _merge(oa, ob, oc, gate, x2, wb, wo, g2, wr_t, br_col, tm):
    t = x2.shape[0]
    row = lambda w: pl.BlockSpec((tm, w), lambda i: (i, 0))
    full = lambda *shape: pl.BlockSpec(shape, lambda i: (0,) * len(shape))
    return pl.pallas_call(
        _merge_kernel,
        out_shape=[jax.ShapeDtypeStruct((t, D_MODEL), F32),
                   jax.ShapeDtypeStruct((t, D_MODEL), BF16),
                   jax.ShapeDtypeStruct((N_EXPERTS, t), F32)],
        grid=(t // tm,),
        in_specs=[row(BRANCH_W), row(BRANCH_W), row(BRANCH_W), row(N_BRANCH * D_MODEL), row(D_MODEL),
                  full(N_BRANCH, BRANCH_W, D_MODEL), full(D_MODEL, D_MODEL), full(1, D_MODEL),
                  full(2 * N_EXPERTS, D_MODEL), full(N_EXPERTS, 1)],
        out_specs=[row(D_MODEL), row(D_MODEL), pl.BlockSpec((N_EXPERTS, tm), lambda i: (0, i))],
        compiler_params=_cparams(("parallel",)),
        name="merge",
    )(oa, ob, oc, gate, x2, wb, wo, g2, wr_t, br_col)


def _col_to_row(col):
    n = col.shape[0]
    r = lax.broadcasted_iota(I32, (n, n), 0)
    c = lax.broadcasted_iota(I32, (n, n), 1)
    return jnp.sum(jnp.where(r == c, col, 0.0), axis=0, keepdims=True)


def _row_to_col(row):
    n = row.shape[1]
    r = lax.broadcasted_iota(I32, (n, n), 0)
    c = lax.broadcasted_iota(I32, (n, n), 1)
    return jnp.sum(jnp.where(r == c, row, 0.0), axis=1, keepdims=True)


def _lane_pad(row):
    return jnp.concatenate([row, jnp.zeros((1, LANES - row.shape[1]), row.dtype)], axis=1)


def _route_kernel(lg_ref, pos_ref, gate_ref, meta_ref, cnt_ref, carry):
    i = pl.program_id(0)
    tn = lg_ref.shape[1]

    @pl.when(i == 0)
    def _():
        carry[...] = jnp.zeros_like(carry)

    l = lg_ref[...]
    eio = lax.broadcasted_iota(I32, (N_EXPERTS, tn), 0)
    hot = jnp.zeros((N_EXPERTS, tn), F32)
    vals, idxs = [], []
    for _ in range(TOP_K):
        m = jnp.max(l, axis=0, keepdims=True)
        idx = jnp.min(jnp.where(l == m, eio, N_EXPERTS), axis=0, keepdims=True)
        sel = eio == idx
        vals.append(m)
        idxs.append(idx)
        hot = hot + sel.astype(F32)
        l = jnp.where(sel, -jnp.inf, l)
    ex = [jnp.exp(v - vals[0]) for v in vals]
    tot = ex[0] + ex[1] + ex[2] + ex[3]
    before = (lax.broadcasted_iota(I32, (tn, tn), 0) < lax.broadcasted_iota(I32, (tn, tn), 1)).astype(BF16)
    within = jnp.dot(hot.astype(BF16), before, preferred_element_type=F32)
    cnt_col = jnp.sum(hot, axis=1, keepdims=True)
    room = jnp.floor((cnt_col + (MOE_UNIT - 1)) * (1.0 / MOE_UNIT)) * MOE_UNIT
    r = lax.broadcasted_iota(I32, (N_EXPERTS, N_EXPERTS), 0)
    c = lax.broadcasted_iota(I32, (N_EXPERTS, N_EXPERTS), 1)
    start_col = _row_to_col(jnp.sum(jnp.where(r < c, room, 0.0), axis=0, keepdims=True))
    for k in range(TOP_K):
        gate_ref[k:k + 1, :] = ex[k] / tot
        pos_ref[k:k + 1, :] = jnp.sum(jnp.where(eio == idxs[k], start_col + within, 0.0), axis=0,
                                      keepdims=True).astype(I32)
    meta_ref[0] = _lane_pad(jnp.concatenate([_col_to_row(room), _col_to_row(carry[:, 0:1])], axis=1)).astype(I32)
    carry[...] = carry[...] + room
    cnt_ref[...] = carry[...]


def _route(logits_t, tn):
    t = logits_t.shape[1]
    kt = pl.BlockSpec((TOP_K, tn), lambda i: (0, i))
    return pl.pallas_call(
        _route_kernel,
        out_shape=[jax.ShapeDtypeStruct((TOP_K, t), I32),
                   jax.ShapeDtypeStruct((TOP_K, t), F32),
                   jax.ShapeDtypeStruct((t // tn, 1, LANES), I32),
                   jax.ShapeDtypeStruct((N_EXPERTS, LANES), F32)],
        grid=(t // tn,),
        in_specs=[pl.BlockSpec((N_EXPERTS, tn), lambda i: (0, i))],
        out_specs=[kt, kt, pl.BlockSpec((1, 1, LANES), lambda i: (i, 0, 0)),
                   pl.BlockSpec((N_EXPERTS, LANES), lambda i: (0, 0))],
        scratch_shapes=[pltpu.VMEM((N_EXPERTS, LANES), F32)],
        compiler_params=_cparams(("arbitrary",)),
        name="moe_route",
    )(logits_t)


def _layout_kernel(bm, cnt_ref, be_ref, na_ref, ps_ref, pe_ref, ct_ref):
    nbp = be_ref.shape[1]
    cnt_col = cnt_ref[:, 0:1]
    size = jnp.floor((cnt_col + (bm - 1)) * (1.0 / bm)) * bm
    r = lax.broadcasted_iota(I32, (N_EXPERTS, N_EXPERTS), 0)
    c = lax.broadcasted_iota(I32, (N_EXPERTS, N_EXPERTS), 1)
    ends_row = jnp.sum(jnp.where(r <= c, size, 0.0), axis=0, keepdims=True)
    ends_col = _row_to_col(ends_row)
    nact = ends_row[:, N_EXPERTS - 1:N_EXPERTS] * (1.0 / bm)
    blk = jnp.minimum(lax.broadcasted_iota(I32, (1, nbp), 1).astype(F32), nact - 1.0)
    be = jnp.sum((ends_col <= blk * bm).astype(F32), axis=0, keepdims=True)
    be_ref[...] = jnp.minimum(be, N_EXPERTS - 1.0).astype(I32)
    na_ref[...] = jnp.broadcast_to(nact, na_ref.shape).astype(I32)
    ps_ref[...] = _lane_pad(ends_row - _col_to_row(size)).astype(I32)
    pe_ref[...] = _lane_pad(ends_row).astype(I32)
    ct_ref[...] = _lane_pad(_col_to_row(cnt_col)).astype(I32)


def _layout(cnt, nbp, bm):
    row = jax.ShapeDtypeStruct((1, LANES), I32)
    return pl.pallas_call(
        functools.partial(_layout_kernel, bm),
        out_shape=[jax.ShapeDtypeStruct((1, nbp), I32), row, row, row, row],
        name="moe_layout",
    )(cnt)


def _stage_rows(tn):
    return tn * TOP_K + N_EXPERTS * MOE_UNIT


def _run_copies(meta_ref, ps_ref, make_copy, tn):
    top = (tn // MOE_UNIT).bit_length() - 1
    src = jnp.int32(0)
    total = jnp.int32(0)
    for e in range(N_EXPERTS):
        units = lax.shift_right_logical(meta_ref[0, 0, e], 3)
        dst = ps_ref[e] + meta_ref[0, 0, N_EXPERTS + e]
        off = jnp.int32(0)
        for k in range(top, -1, -1):
            rows = MOE_UNIT << k
            take = lax.shift_right_logical(units, k) & 1

            @pl.when(take == 1)
            def _(src=src, dst=dst, off=off, rows=rows):
                make_copy(pl.multiple_of(src + off, MOE_UNIT), pl.multiple_of(dst + off, MOE_UNIT), rows).start()
            off = off + take * rows
        src = src + units * MOE_UNIT
        total = total + units
    return total


def _wait_copies(units, make_copy, max_rows):
    for k in range((max_rows // MOE_UNIT).bit_length()):
        if (MOE_UNIT << k) > max_rows:
            break

        @pl.when(lax.shift_right_logical(units, k) & 1 == 1)
        def _(k=k):
            make_copy(0, 0, MOE_UNIT << k).wait()


def _wait_units(count, make_copy):
    def body(u, carry):
        make_copy(0, 0).wait()
        return carry
    lax.fori_loop(0, count, body, 0)


def _dispatch_kernel(ps_ref, pe_ref, ct_ref, meta_ref, pos_ref, h_ref, xg_ref, stage, zrows, zblock, started, sem,
                     zsem):
    i = pl.program_id(0)
    n = pl.num_programs(0)
    slot = i % 2
    rows, tn = stage.shape[1], pos_ref.shape[1]

    def copy(s):
        return lambda src, dst, nrows: pltpu.make_async_copy(stage.at[s, pl.ds(src, nrows)],
                                                             xg_ref.at[pl.ds(dst, nrows)], sem.at[s])

    @pl.when(i >= 2)
    def _():
        _wait_copies(started[slot], copy(slot), rows)

    srow = lax.broadcasted_iota(I32, (rows, tn), 0)
    hit = srow == pos_ref[0:1, :]
    for k in range(1, TOP_K):
        hit = hit | (srow == pos_ref[k:k + 1, :])
    onehot = jnp.where(hit, 1.0, 0.0).astype(BF16)
    stage[slot] = jnp.dot(onehot, h_ref[...], preferred_element_type=F32)
    started[slot] = _run_copies(meta_ref, ps_ref, copy(slot), tn)

    @pl.when(i == n - 1)
    def _():
        _wait_copies(started[slot], copy(slot), rows)

        @pl.when(n >= 2)
        def _():
            _wait_copies(started[1 - slot], copy(1 - slot), rows)

        zrows[...] = jnp.zeros_like(zrows)
        zero = lambda src, dst: pltpu.make_async_copy(zrows, xg_ref.at[pl.ds(dst, MOE_UNIT)], zsem)
        total = jnp.int32(0)
        for e in range(N_EXPERTS):
            lo = ps_ref[e] + ct_ref[e]
            units = lax.shift_right_logical(pe_ref[e] - lo, 3)

            def body(u, carry, lo=lo):
                zero(0, pl.multiple_of(lo + u * MOE_UNIT, MOE_UNIT)).start()
                return carry
            lax.fori_loop(0, units, body, 0)
            total = total + units
        _wait_units(total, zero)

        zblock[...] = jnp.zeros_like(zblock)
        bm = zblock.shape[0]
        zero_block = lambda blk: pltpu.make_async_copy(zblock, xg_ref.at[pl.ds(pl.multiple_of(blk * bm, bm), bm)],
                                                       zsem)
        first = lax.shift_right_logical(pe_ref[N_EXPERTS - 1], bm.bit_length() - 1)
        last = xg_ref.shape[0] // bm

        def start_block(blk, carry):
            zero_block(blk).start()
            return carry
        lax.fori_loop(first, last, start_block, 0)

        def wait_block(blk, carry):
            zero_block(0).wait()
            return carry
        lax.fori_loop(first, last, wait_block, 0)


def _dispatch(ps, pe, ct, meta, pos, h2, n_slots, tn, bm):
    t = h2.shape[0]
    return pl.pallas_call(
        _dispatch_kernel,
        out_shape=jax.ShapeDtypeStruct((n_slots, D_MODEL), F32),
        grid_spec=pltpu.PrefetchScalarGridSpec(
            num_scalar_prefetch=3,
            grid=(t // tn,),
            in_specs=[pl.BlockSpec((1, 1, LANES), lambda i, *_: (i, 0, 0), memory_space=pltpu.SMEM),
                      pl.BlockSpec((TOP_K, tn), lambda i, *_: (0, i)),
                      pl.BlockSpec((tn, D_MODEL), lambda i, *_: (i, 0))],
            out_specs=pl.BlockSpec(memory_space=pl.ANY),
            scratch_shapes=[pltpu.VMEM((2, _stage_rows(tn), D_MODEL), F32),
                            pltpu.VMEM((MOE_UNIT, D_MODEL), F32),
                            pltpu.VMEM((bm, D_MODEL), F32),
                            pltpu.SMEM((2,), I32),
                            pltpu.SemaphoreType.DMA((2,)),
                            pltpu.SemaphoreType.DMA(())]),
        compiler_params=_cparams(("arbitrary",)),
        name="moe_dispatch",
    )(ps, pe, ct, meta, pos, h2)


def _expert_kernel(be_ref, na_ref, x_ref, w1_ref, b1_ref, w2_ref, b2_ref, y_ref, w1s, w2s):
    i = pl.program_id(0)
    e = be_ref[i]
    prev = be_ref[jnp.maximum(i - 1, 0)]

    @pl.when((i == 0) | (e != prev))
    def _():
        rows = 128
        for r0 in range(0, D_MODEL, rows):
            w1s[r0:r0 + rows, :] = w1_ref[r0:r0 + rows, :].astype(BF16)
        for r0 in range(0, D_FF, rows):
            w2s[r0:r0 + rows, :] = w2_ref[r0:r0 + rows, :].astype(BF16)

    @pl.when(i < na_ref[0])
    def _():
        hmid = jnp.dot(x_ref[...].astype(BF16), w1s[...], preferred_element_type=F32) + b1_ref[...]
        glu = jnp.minimum(hmid[:, :D_FF], SWIGLU_LIMIT)
        lin = jnp.clip(hmid[:, D_FF:], -SWIGLU_LIMIT, SWIGLU_LIMIT)
        act = glu * jax.nn.sigmoid(SWIGLU_ALPHA * glu) * (lin + 1.0)
        y_ref[...] = jnp.dot(act.astype(BF16), w2s[...], preferred_element_type=F32) + b2_ref[...]

    @pl.when(i >= na_ref[0])
    def _():
        y_ref[...] = jnp.zeros_like(y_ref)


def _experts(be, nact, xg, layer, w1, b1, w2, b2, bm):
    n_slots = xg.shape[0]
    nb = n_slots // bm
    blk = lambda i, be_r, na_r: (jnp.minimum(i, na_r[0] - 1), 0)
    wsel = lambda i, be_r, na_r: (layer, be_r[i], 0, 0)
    return pl.pallas_call(
        _expert_kernel,
        out_shape=jax.ShapeDtypeStruct((n_slots, D_MODEL), F32),
        grid_spec=pltpu.PrefetchScalarGridSpec(
            num_scalar_prefetch=2,
            grid=(nb,),
            in_specs=[pl.BlockSpec((bm, D_MODEL), blk),
                      pl.BlockSpec((None, None, D_MODEL, 2 * D_FF), wsel),
                      pl.BlockSpec((None, None, 1, 2 * D_FF), wsel),
                      pl.BlockSpec((None, None, D_FF, D_MODEL), wsel),
                      pl.BlockSpec((None, None, 1, D_MODEL), wsel)],
            out_specs=pl.BlockSpec((bm, D_MODEL), lambda i, be_r, na_r: (i, 0)),
            scratch_shapes=[pltpu.VMEM((D_MODEL, 2 * D_FF), BF16), pltpu.VMEM((D_FF, D_MODEL), BF16)]),
        compiler_params=_cparams(("arbitrary",)),
        name="moe_experts",
    )(be, nact, xg, w1, b1, w2, b2)


def _combine_kernel(ps_ref, mcur_ref, mnext_ref, pos_ref, gate_ref, x1_ref, yg_ref, o_ref, stage, started, sem):
    i = pl.program_id(0)
    n = pl.num_programs(0)
    slot = i % 2
    rows, tn = stage.shape[1], pos_ref.shape[1]

    @pl.when(i == 0)
    def _():
        stage[...] = jnp.zeros_like(stage)

    def copy(s):
        return lambda src, dst, nrows: pltpu.make_async_copy(yg_ref.at[pl.ds(dst, nrows)],
                                                             stage.at[s, pl.ds(src, nrows)], sem.at[s])

    @pl.when(i == 0)
    def _():
        started[0] = _run_copies(mcur_ref, ps_ref, copy(0), tn)

    @pl.when(i + 1 < n)
    def _():
        started[1 - slot] = _run_copies(mnext_ref, ps_ref, copy(1 - slot), tn)

    _wait_copies(started[slot], copy(slot), rows)

    eye = (lax.broadcasted_iota(I32, (tn, tn), 0) == lax.broadcasted_iota(I32, (tn, tn), 1)).astype(F32)
    cols = _hdot_nt(eye, jnp.concatenate([pos_ref[...].astype(F32), gate_ref[...]], axis=0))
    lane = lax.broadcasted_iota(I32, (tn, rows), 1)
    weight = jnp.zeros((tn, rows), F32)
    for k in range(TOP_K):
        weight = weight + jnp.where(lane == cols[:, k:k + 1].astype(I32), cols[:, TOP_K + k:TOP_K + k + 1], 0.0)
    w_hi = weight.astype(BF16)
    w_lo = (weight - w_hi.astype(F32)).astype(BF16)
    y = stage[slot].astype(BF16)
    o_ref[...] = (x1_ref[...] + jnp.dot(w_hi, y, preferred_element_type=F32)
                  + jnp.dot(w_lo, y, preferred_element_type=F32))


def _combine(ps, meta, pos, gates, x1, yg, tn):
    t = x1.shape[0]
    nt = t // tn
    return pl.pallas_call(
        _combine_kernel,
        out_shape=jax.ShapeDtypeStruct((t, D_MODEL), F32),
        grid_spec=pltpu.PrefetchScalarGridSpec(
            num_scalar_prefetch=1,
            grid=(nt,),
            in_specs=[pl.BlockSpec((1, 1, LANES), lambda i, *_: (i, 0, 0), memory_space=pltpu.SMEM),
                      pl.BlockSpec((1, 1, LANES), lambda i, *_: (jnp.minimum(i + 1, nt - 1), 0, 0),
                                   memory_space=pltpu.SMEM),
                      pl.BlockSpec((TOP_K, tn), lambda i, *_: (0, i)),
                      pl.BlockSpec((TOP_K, tn), lambda i, *_: (0, i)),
                      pl.BlockSpec((tn, D_MODEL), lambda i, *_: (i, 0)),
                      pl.BlockSpec(memory_space=pl.ANY)],
            out_specs=pl.BlockSpec((tn, D_MODEL), lambda i, *_: (i, 0)),
            scratch_shapes=[pltpu.VMEM((2, _stage_rows(tn), D_MODEL), F32),
                            pltpu.SMEM((2,), I32),
                            pltpu.SemaphoreType.DMA((2,))]),
        compiler_params=_cparams(("arbitrary",)),
        name="moe_combine",
    )(ps, meta, meta, pos, gates, x1, yg)


def _moe(x1, h2, logits_t, layer, w1, b1, w2, b2):
    t = x1.shape[0]
    tn = min(MOE_TOK_TILE, t)
    bm = MOE_BM if t * TOP_K >= N_EXPERTS * MOE_BM else LANES
    n_blocks = (t * TOP_K + (t // tn) * N_EXPERTS * (MOE_UNIT - 1)) // bm + N_EXPERTS + 1
    nbp = -(-n_blocks // LANES) * LANES
    pos, gates, meta, cnt = _route(logits_t, tn)
    be, nact, ps, pe, ct = _layout(cnt, nbp, bm)
    ps, pe, ct = ps[0, :N_EXPERTS], pe[0, :N_EXPERTS], ct[0, :N_EXPERTS]
    xg = _dispatch(ps, pe, ct, meta, pos, h2, n_blocks * bm, tn, bm)
    yg = _experts(be[0, :n_blocks], nact[0, :1], xg, layer, w1, b1, w2, b2, bm)
    return _combine(ps, meta, pos, gates, x1, yg, tn)


def _pack_w_in(w):
    ab = jnp.pad(w[:, _IN_MAIN:_IN_MAIN + _IN_AB], ((0, 0), (0, LANES - _IN_AB)))
    return jnp.concatenate([w[:, :_IN_MAIN], ab, w[:, _IN_MAIN + _IN_AB:]], axis=1).astype(BF16)


def _lane_row(v):
    return jnp.pad(v.astype(F32), (0, LANES - v.shape[0])).reshape(1, LANES)


def _layer_weights(ln1_gain, w_in, q_norm_swa, k_norm_swa, swa_sinks, conv_w, a_log, dt_bias, gdn_norm,
                   q_norm_mem, w_branch, w_out, ln2_gain, w_router, b_router, w_mlp1, b_mlp1, w_mlp2, b_mlp2, layer):
    depth = w_mlp1.shape[0]
    wr_hi = w_router.T.astype(BF16)
    wr_lo = (w_router.T - wr_hi.astype(F32)).astype(BF16)
    return dict(
        layer=layer, wr_t=jnp.concatenate([wr_hi, wr_lo], axis=0), b1r=b_mlp1.reshape(depth, N_EXPERTS, 1, -1), b2r=b_mlp2.reshape(depth, N_EXPERTS, 1, -1),
        ln1=ln1_gain.reshape(1, -1), wp=_pack_w_in(w_in),
        qn_s=q_norm_swa.reshape(1, -1), kn_s=k_norm_swa.reshape(1, -1), sinks=swa_sinks.reshape(1, -1),
        conv_w=conv_w, alog=_lane_row(a_log), dtb=_lane_row(dt_bias), gnorm=gdn_norm.reshape(1, -1),
        qn_m=q_norm_mem.reshape(1, -1), wb=w_branch.astype(BF16), wo=w_out.astype(BF16),
        ln2=ln2_gain.reshape(1, -1), br=b_router.reshape(-1, 1),
        w1=w_mlp1, w2=w_mlp2)


def _finish(lw, oa, ob, oc, gate, x2, tm):
    x1, h2, logits_t = _merge(oa, ob, oc, gate, x2, lw["wb"], lw["wo"], lw["ln2"], lw["wr_t"], lw["br"], tm)
    return _moe(x1, h2, logits_t, lw["layer"], lw["w1"], lw["b1r"], lw["w2"], lw["b2r"])


def _prompt_layer(x, mem, lw, mem_norm, w_mem_kv, k_norm_mem):
    b, l, d = x.shape
    x2 = x.reshape(b * l, d)
    tm = 256
    qs, ks, vs, qkvg, z, ab, qm, gate = _inproj(x2, lw["ln1"], lw["wp"], tm)
    mk, mv = _memkv(mem.reshape(b * MEM_LEN, d), mem_norm.reshape(1, -1), w_mem_kv.astype(BF16),
                    k_norm_mem.reshape(1, -1))
    r3 = lambda a: a.reshape(b, l, a.shape[-1])
    o_swa, kwin = _swa_prompt(r3(qs), r3(ks), r3(vs), lw["qn_s"], lw["kn_s"], lw["sinks"])
    o_g, s_fin = _gdn_prompt(r3(qkvg), r3(z), r3(ab), lw["conv_w"], lw["alog"], lw["dtb"], lw["gnorm"])
    o_m = _memattn_prompt(r3(qm), mk.reshape(b, MEM_LEN, BRANCH_W), mv.reshape(b, MEM_LEN, BRANCH_W), lw["qn_m"])
    y = _finish(lw, o_swa.reshape(b * l, -1), o_g.reshape(b * l, -1), o_m.reshape(b * l, -1), gate, x2, tm)
    new_k = kwin.reshape(b, WINDOW, SWA_KV_HEADS, SWA_HEAD_DIM)
    new_v = r3(vs)[:, l - WINDOW:].reshape(b, WINDOW, SWA_KV_HEADS, SWA_HEAD_DIM)
    new_conv = r3(qkvg)[:, l - (CONV_WIDTH - 1):]
    mk4 = mk.reshape(b, MEM_LEN, MEM_HEADS, MEM_HEAD_DIM)
    mv4 = mv.reshape(b, MEM_LEN, MEM_HEADS, MEM_HEAD_DIM)
    return y.reshape(b, l, d), new_k, new_v, s_fin, new_conv, mk4, mv4


def _sample_layer(x, cache_k, cache_v, state, conv_state, mem_k, mem_v, lw):
    b, l, d = x.shape
    x2 = x.reshape(b, d)
    tm = TOK_TILE
    qs, ks, vs, qkvg, z, ab, qm, gate = _inproj(x2, lw["ln1"], lw["wp"], tm)
    o_swa, new_k, new_v = _swa_decode(qs, ks, vs, cache_k, cache_v, lw["qn_s"], lw["kn_s"], lw["sinks"])
    o_g, new_s = _gdn_decode(qkvg, z, ab, state, conv_state, lw["conv_w"], lw["alog"], lw["dtb"], lw["gnorm"])
    o_m = _memattn_decode(qm, mem_k, mem_v, lw["qn_m"])
    y = _finish(lw, o_swa, o_g, o_m, gate, x2, tm)
    new_conv = jnp.concatenate([conv_state[:, 1:], qkvg[:, None, :]], axis=1)
    shp = (b, WINDOW, SWA_KV_HEADS, SWA_HEAD_DIM)
    return y.reshape(b, l, d), new_k.reshape(shp), new_v.reshape(shp), new_s, new_conv


def kernel(x_prompt, x_sample, cache_swa_k, cache_swa_v, state_gdn, state_conv, cache_mem_k, cache_mem_v, mem_prompt, ln1_gain, w_in, q_norm_swa, k_norm_swa, swa_sinks, conv_w, a_log, dt_bias, gdn_norm, q_norm_mem, k_norm_mem, mem_norm, w_mem_kv, w_branch, w_out, ln2_gain, w_router, b_router, w_mlp1, b_mlp1, w_mlp2, b_mlp2):
    xp, xs = x_prompt, x_sample
    outs = [[] for _ in range(10)]
    for l in range(ln1_gain.shape[0]):
        lw = _layer_weights(ln1_gain[l], w_in[l], q_norm_swa[l], k_norm_swa[l], swa_sinks[l], conv_w[l], a_log[l],
                            dt_bias[l], gdn_norm[l], q_norm_mem[l], w_branch[l], w_out[l], ln2_gain[l], w_router[l],
                            b_router[l], w_mlp1, b_mlp1, w_mlp2, b_mlp2, l)
        xp, pk, pv, ps, pc, mk, mv = _prompt_layer(xp, mem_prompt, lw, mem_norm[l], w_mem_kv[l], k_norm_mem[l])
        xs, sk, sv, ss, sc = _sample_layer(xs, cache_swa_k[l], cache_swa_v[l], state_gdn[l], state_conv[l],
                                           cache_mem_k[l], cache_mem_v[l], lw)
        for acc, v in zip(outs, (pk, pv, sk, sv, ps, ss, pc, sc, mk, mv)):
            acc.append(v)
    return (xp, xs) + tuple(jnp.stack(o) for o in outs)
```

```python
import functools

import numpy as np
import jax
import jax.numpy as jnp
from jax import lax
from jax.experimental import pallas as pl
from jax.experimental.pallas import tpu as pltpu

F32 = jnp.float32
BF16 = jnp.bfloat16
I32 = jnp.int32

D_MODEL = 1024
BRANCH_W = 512
SWA_HEADS = 8
SWA_KV_HEADS = 2
SWA_HEAD_DIM = 64
SWA_GROUP = SWA_HEADS // SWA_KV_HEADS
SWA_KV_W = SWA_KV_HEADS * SWA_HEAD_DIM
WINDOW = 128
GDN_HEADS = 4
GDN_HEAD_DIM = 128
GDN_W = GDN_HEADS * GDN_HEAD_DIM
GDN_CONV_W = 3 * GDN_W
GDN_CHUNK = 64
CONV_WIDTH = 4
MEM_LEN = 256
MEM_HEADS = 4
MEM_HEAD_DIM = 128
N_BRANCH = 3
N_EXPERTS = 32
TOP_K = 4
D_FF = 1024
SWIGLU_LIMIT = 7.0
SWIGLU_ALPHA = 1.702
EPS = 1e-6
PAST_LEN = 16384

LANES = 128
MOE_BM = 512
MOE_TOK_TILE = 256
MOE_UNIT = 8
TOK_TILE = 128
MERGE_ROWS = 512
VMEM_LIMIT = 56 * 1024 * 1024

_SEG_QS = (0, 512)
_SEG_KS = (512, 128)
_SEG_VS = (640, 128)
_SEG_QKVG = (768, 1536)
_SEG_Z = (2304, 512)
_SEG_AB = (2816, 128)
_SEG_QM = (2944, 512)
_SEG_GATE = (3456, 3072)
_PACKED_W = 6528
_IN_MAIN = 2816
_IN_AB = 8

_NEG = -1e30
_HI = lax.Precision.HIGHEST


def _cparams(sem, vmem=VMEM_LIMIT):
    return pltpu.CompilerParams(dimension_semantics=sem, vmem_limit_bytes=vmem)


def _bdot(a, b):
    return jnp.dot(a.astype(BF16), b.astype(BF16), preferred_element_type=F32)


def _bdot_nt(a, b):
    return lax.dot_general(a.astype(BF16), b.astype(BF16), (((1,), (1,)), ((), ())),
                           preferred_element_type=F32)


def _bdot_tn(a, b):
    return lax.dot_general(a.astype(BF16), b.astype(BF16), (((0,), (0,)), ((), ())),
                           preferred_element_type=F32)


def _hdot(a, b):
    return jnp.dot(a, b, preferred_element_type=F32, precision=_HI)


def _hdot_nt(a, b):
    return lax.dot_general(a, b, (((1,), (1,)), ((), ())), preferred_element_type=F32, precision=_HI)


def _rms(x, gain):
    return x * lax.rsqrt(jnp.mean(x * x, axis=-1, keepdims=True) + EPS) * gain


def _l2(x):
    return x * lax.rsqrt(jnp.sum(x * x, axis=-1, keepdims=True) + EPS)


def _sigmoid(x):
    return 0.5 * jnp.tanh(0.5 * x) + 0.5


def _silu(x):
    return x * _sigmoid(x)


def _softplus(x):
    return jnp.maximum(x, 0.0) + jnp.log1p(jnp.exp(-jnp.abs(x)))


_HALO = 8


def _inproj_kernel(seq_steps, x_ref, g_ref, w_ref, cw_ref, qs, ks, vs, qkvg, z, ab, qm, gate, *conv_refs):
    i = pl.program_id(0)
    x = x_ref[...]
    hb = _rms(x, g_ref[...]).astype(BF16)

    def proj(off, width):
        return jnp.dot(hb, w_ref[:, off:off + width], preferred_element_type=F32)

    step = 512

    def plain(ref, c0=0, width=None):
        def finish(r):
            ref[:, c0:c0 + (width or r.shape[1])] = r
        return finish

    def gate_chunk(c0):
        def finish(r):
            gate[:, c0:c0 + step] = _sigmoid(r).astype(BF16)
        return finish

    def conv_part(part):
        tail_ref, carry = conv_refs
        c0 = part * GDN_W
        cols = slice(c0, c0 + GDN_W)

        def finish(r):
            first = (i % seq_steps) == 0
            sub = lax.broadcasted_iota(I32, (_HALO, 1), 0)
            rows = r.shape[0]
            before = jnp.where(first, 0.0, carry[:, cols])
            y = r * cw_ref[CONV_WIDTH - 1:CONV_WIDTH, cols]
            top = r[0:_HALO] * cw_ref[CONV_WIDTH - 1:CONV_WIDTH, cols]
            for jj in range(CONV_WIDTH - 1):
                sh = CONV_WIDTH - 1 - jj
                y = y + pltpu.roll(r, sh, 0) * cw_ref[jj:jj + 1, cols]
                head_rows = jnp.where(sub < sh, pltpu.roll(before, sh, 0), pltpu.roll(r[0:_HALO], sh, 0))
                top = top + head_rows * cw_ref[jj:jj + 1, cols]
            carry[:, cols] = r[rows - _HALO:]
            tail_ref[0, :, cols] = r[rows - _HALO:]
            act = _silu(jnp.concatenate([top, y[_HALO:]], axis=0))
            if part < 2:
                scale = GDN_HEAD_DIM ** -0.5 if part == 0 else 1.0
                for h in range(GDN_HEADS):
                    hs = slice(h * GDN_HEAD_DIM, (h + 1) * GDN_HEAD_DIM)
                    qkvg[:, c0 + h * GDN_HEAD_DIM:c0 + (h + 1) * GDN_HEAD_DIM] = _l2(act[:, hs]) * scale
            else:
                qkvg[:, cols] = act
        return finish

    gates = [(_SEG_GATE[0] + c0, step, gate_chunk(c0)) for c0 in range(0, _SEG_GATE[1], step)]
    light = [(_SEG_QS[0], _SEG_QS[1], plain(qs)), (_SEG_KS[0], _SEG_KS[1], plain(ks)),
             (_SEG_VS[0], _SEG_VS[1], plain(vs)), (_SEG_AB[0], _SEG_AB[1], plain(ab)),
             (_SEG_QM[0], _SEG_QM[1], plain(qm))]
    if seq_steps is None:
        heavy = gates
        light += [(_SEG_Z[0], _SEG_Z[1], plain(z))]
        light += [(_SEG_QKVG[0] + c0, step, plain(qkvg, c0, step)) for c0 in range(0, GDN_CONV_W, step)]
    else:
        def silu_z(r):
            z[...] = _silu(r)
        heavy = gates + [(_SEG_QKVG[0] + p * GDN_W, GDN_W, conv_part(p)) for p in range(3)]
        heavy.append((_SEG_Z[0], _SEG_Z[1], silu_z))
    tasks = []
    while heavy or light:
        if heavy:
            tasks.append(heavy.pop(0))
        if light:
            tasks.append(light.pop(0))
    pending = None
    for off, width, finish in tasks:
        r = proj(off, width)
        if pending is not None:
            pending[1](pending[0])
        pending = (r, finish)
    pending[1](pending[0])


def _inproj(x2, gain, wp, conv_w, tm, seq_len=None):
    t = x2.shape[0]
    segs = (_SEG_QS, _SEG_KS, _SEG_VS, _SEG_QKVG, _SEG_Z, _SEG_AB, _SEG_QM, _SEG_GATE)
    dtypes = (F32,) * 7 + (BF16,)
    out_shape = [jax.ShapeDtypeStruct((t, w), dt) for (_, w), dt in zip(segs, dtypes)]
    out_specs = [pl.BlockSpec((tm, w), lambda i: (i, 0)) for _, w in segs]
    scratch = []
    seq_steps = None
    if seq_len is not None:
        seq_steps = seq_len // tm
        out_shape.append(jax.ShapeDtypeStruct((t // seq_len, _HALO, GDN_CONV_W), F32))
        out_specs.append(pl.BlockSpec((1, _HALO, GDN_CONV_W), lambda i: (i // seq_steps, 0, 0)))
        scratch.append(pltpu.VMEM((_HALO, GDN_CONV_W), F32))
    return pl.pallas_call(
        functools.partial(_inproj_kernel, seq_steps),
        out_shape=out_shape,
        grid=(t // tm,),
        in_specs=[pl.BlockSpec((tm, D_MODEL), lambda i: (i, 0)),
                  pl.BlockSpec((1, D_MODEL), lambda i: (0, 0)),
                  pl.BlockSpec((D_MODEL, _PACKED_W), lambda i: (0, 0), pipeline_mode=pl.Buffered(1)),
                  pl.BlockSpec((CONV_WIDTH, GDN_CONV_W), lambda i: (0, 0))],
        out_specs=out_specs,
        scratch_shapes=scratch,
        compiler_params=_cparams(("arbitrary",)),
        name="inproj",
    )(x2, gain, wp, conv_w)


SWA_STEP_BLOCKS = 4


def _alibi_slopes(n):
    return [float(2.0 ** (-8.0 * (i + 1) / n)) for i in range(n)]


def _swa_prompt_kernel(q_ref, kc_ref, vc_ref, qg_ref, kg_ref, sink_ref, o_ref, kwin_ref, bias, kprev, vprev):
    n = pl.program_id(1)
    half = SWA_HEAD_DIM
    nblk = q_ref.shape[1] // WINDOW
    work = [(j, h) for j in range(nblk) for h in range(SWA_HEADS)]

    @pl.when(n == 0)
    def _():
        kprev[...] = jnp.zeros_like(kprev)
        vprev[...] = jnp.zeros_like(vprev)

    def fill(tbl, first):
        row = lax.broadcasted_iota(I32, (WINDOW, 2 * WINDOW), 0)
        col = lax.broadcasted_iota(I32, (WINDOW, 2 * WINDOW), 1)
        dist = row + WINDOW - col
        valid = (dist >= 0) & (dist <= WINDOW) & ((col >= WINDOW) | jnp.logical_not(first))
        distf = dist.astype(F32)
        for h, slope in enumerate(_alibi_slopes(SWA_HEADS)):
            bias[tbl, h] = jnp.where(valid, -slope * distf, _NEG)

    @pl.when(n <= 1)
    def _():
        fill(0, n == 0)

    if nblk > 1:
        @pl.when(n == 0)
        def _():
            fill(1, False)

    low = lax.broadcasted_iota(I32, (1, 2 * half), 1) < half

    def pair_rms(x, gain):
        sq = x * x
        s_lo = jnp.sum(jnp.where(low, sq, 0.0), axis=-1, keepdims=True)
        s_hi = jnp.sum(jnp.where(low, 0.0, sq), axis=-1, keepdims=True)
        return x * lax.rsqrt(jnp.where(low, s_lo, s_hi) * (1.0 / half) + EPS) * gain

    kcn = pair_rms(kc_ref[0], kg_ref[...])
    kwin_ref[0] = kcn[(nblk - 1) * WINDOW:]
    kall = jnp.concatenate([kprev[...], kcn], axis=0)
    vall = jnp.concatenate([vprev[...], vc_ref[0]], axis=0)
    kprev[...] = kcn[(nblk - 1) * WINDOW:]
    vprev[...] = vc_ref[0, (nblk - 1) * WINDOW:, :]
    ones = jnp.ones(((nblk + 1) * WINDOW, 2 * half), BF16)

    def placed(x, aug):
        sw = pltpu.roll(x, half, 1)
        out = {(0, 0): jnp.where(low, x, 0.0), (1, 1): jnp.where(low, 0.0, x),
               (0, 1): jnp.where(low, 0.0, sw), (1, 0): jnp.where(low, sw, 0.0)}
        out = {key: val.astype(BF16) for key, val in out.items()}
        return {key: jnp.concatenate([val, ones], axis=1) for key, val in out.items()} if aug else out

    kvar = placed(kall, False)
    vvar = placed(vall, True)
    qn = [pair_rms(q_ref[0, :, t * 2 * half:(t + 1) * 2 * half], qg_ref[...]).astype(BF16)
          for t in range(SWA_HEADS // 2)]
    where_of = lambda h: (h // SWA_GROUP, h % 2)
    qrows = lambda j: slice(j * WINDOW, (j + 1) * WINDOW)
    krows = lambda j: slice(j * WINDOW, (j + 2) * WINDOW)
    nt = (((1,), (1,)), ((), ()))
    s = {(j, h): lax.dot_general(qn[h // 2][qrows(j)], kvar[where_of(h)][krows(j)], nt, preferred_element_type=F32)
         + bias[min(j, 1), h] for j, h in work}
    sink = [sink_ref[0:1, h:h + 1] for h in range(SWA_HEADS)]
    m = {(j, h): jnp.maximum(jnp.max(s[j, h], axis=-1, keepdims=True), sink[h]) for j, h in work}
    p = {(j, h): jnp.exp(s[j, h] - m[j, h]).astype(BF16) for j, h in work}
    res = {(j, h): jnp.dot(p[j, h], vvar[where_of(h)][krows(j)], preferred_element_type=F32)
           for j, h in work}
    inv = {(j, h): 1.0 / (res[j, h][:, 2 * half:2 * half + 1] + jnp.exp(sink[h] - m[j, h])) for j, h in work}
    for j in range(nblk):
        for t in range(SWA_HEADS // 2):
            h0, h1 = 2 * t, 2 * t + 1
            num = res[j, h0][:, :2 * half] + res[j, h1][:, :2 * half]
            o_ref[0, qrows(j), t * 2 * half:(t + 1) * 2 * half] = num * jnp.where(low, inv[j, h0], inv[j, h1])


def _swa_prompt(qs, ks, vs, qn, kn, sinks):
    b, l, _ = qs.shape
    rows = SWA_STEP_BLOCKS * WINDOW if l % (SWA_STEP_BLOCKS * WINDOW) == 0 else WINDOW
    cur = lambda i, j: (i, j, 0)
    const2 = lambda i, j: (0, 0)
    q_gain = jnp.tile(qn, (1, 2)) * (SWA_HEAD_DIM ** -0.5)
    k_gain = jnp.tile(kn, (1, 2))
    return pl.pallas_call(
        _swa_prompt_kernel,
        out_shape=[jax.ShapeDtypeStruct((b, l, BRANCH_W), F32),
                   jax.ShapeDtypeStruct((b, WINDOW, SWA_KV_W), F32)],
        grid=(b, l // rows),
        in_specs=[pl.BlockSpec((1, rows, BRANCH_W), cur),
                  pl.BlockSpec((1, rows, SWA_KV_W), cur),
                  pl.BlockSpec((1, rows, SWA_KV_W), cur),
                  pl.BlockSpec((1, 2 * SWA_HEAD_DIM), const2),
                  pl.BlockSpec((1, 2 * SWA_HEAD_DIM), const2),
                  pl.BlockSpec((1, SWA_HEADS), const2)],
        out_specs=[pl.BlockSpec((1, rows, BRANCH_W), cur),
                   pl.BlockSpec((1, WINDOW, SWA_KV_W), lambda i, j: (i, 0, 0))],
        scratch_shapes=[pltpu.VMEM((2, SWA_HEADS, WINDOW, 2 * WINDOW), F32),
                        pltpu.VMEM((WINDOW, SWA_KV_W), F32),
                        pltpu.VMEM((WINDOW, SWA_KV_W), F32)],
        compiler_params=_cparams(("arbitrary", "arbitrary")),
        name="swa_prompt",
    )(qs, ks, vs, q_gain, k_gain, sinks)


def _swa_decode_kernel(bs, q_ref, k3_ref, kf_ref, v3_ref, vf_ref, ck_ref, cv_ref, qn_ref, kn_ref, kn2_ref,
                       sink_ref, slope_ref, o_ref, ok_ref, ov_ref):
    scale = SWA_HEAD_DIM ** -0.5
    lane = lax.broadcasted_iota(I32, (1, SWA_KV_W), 1)
    rowi = lax.broadcasted_iota(I32, (WINDOW, SWA_KV_W), 0)
    keyd = (WINDOW - lax.broadcasted_iota(I32, (1, WINDOW), 1)).astype(F32)
    for b in range(bs):
        qn = _rms(q_ref[b], qn_ref[...])
        kn3 = _rms(k3_ref[b], kn_ref[...])
        v3 = v3_ref[b]
        kf = kf_ref[b]
        sq = kf * kf
        ms0 = jnp.sum(jnp.where(lane < SWA_HEAD_DIM, sq, 0.0), axis=-1, keepdims=True) / SWA_HEAD_DIM
        ms1 = jnp.sum(jnp.where(lane >= SWA_HEAD_DIM, sq, 0.0), axis=-1, keepdims=True) / SWA_HEAD_DIM
        knf = kf * lax.rsqrt(jnp.where(lane < SWA_HEAD_DIM, ms0, ms1) + EPS) * kn2_ref[...]
        ck = ck_ref[b]
        cv = cv_ref[b]
        for g in range(SWA_KV_HEADS):
            sl = slice(g * SWA_HEAD_DIM, (g + 1) * SWA_HEAD_DIM)
            hs = slice(g * SWA_GROUP, (g + 1) * SWA_GROUP)
            qg = qn[hs]
            slope = slope_ref[hs]
            sink = sink_ref[hs]
            s = _bdot_nt(qg, ck[:, sl]) * scale - slope * keyd
            s_new = jnp.sum(qg * kn3[g:g + 1], axis=-1, keepdims=True) * scale
            m = jnp.maximum(jnp.maximum(jnp.max(s, axis=-1, keepdims=True), s_new), sink)
            p = jnp.exp(s - m)
            p_new = jnp.exp(s_new - m)
            denom = jnp.sum(p, axis=-1, keepdims=True) + p_new + jnp.exp(sink - m)
            o_ref[b, hs, :] = (_bdot(p, cv[:, sl]) + p_new * v3[g:g + 1]) / denom
        last = rowi == WINDOW - 1
        ok_ref[b] = jnp.where(last, knf, pltpu.roll(ck, WINDOW - 1, 0))
        ov_ref[b] = jnp.where(last, vf_ref[b], pltpu.roll(cv, WINDOW - 1, 0))


def _swa_decode(qs, ks, vs, cache_k, cache_v, qn, kn, sinks, bs=8):
    b = qs.shape[0]
    q3 = qs.reshape(b, SWA_HEADS, SWA_HEAD_DIM)
    k3 = ks.reshape(b, SWA_KV_HEADS, SWA_HEAD_DIM)
    kf = ks.reshape(b, 1, SWA_KV_W)
    v3 = vs.reshape(b, SWA_KV_HEADS, SWA_HEAD_DIM)
    vf = vs.reshape(b, 1, SWA_KV_W)
    ck = cache_k.reshape(b, WINDOW, SWA_KV_W)
    cv = cache_v.reshape(b, WINDOW, SWA_KV_W)
    kn2 = jnp.concatenate([kn, kn], axis=-1)
    sink_col = sinks.reshape(SWA_HEADS, 1)
    slope_col = jnp.asarray(np.asarray(_alibi_slopes(SWA_HEADS), np.float32).reshape(SWA_HEADS, 1))
    blk = lambda *shape: pl.BlockSpec((bs,) + shape, lambda i: (i,) + (0,) * len(shape))
    full = lambda *shape: pl.BlockSpec(shape, lambda i: (0,) * len(shape))
    o, ok, ov = pl.pallas_call(
        functools.partial(_swa_decode_kernel, bs),
        out_shape=[jax.ShapeDtypeStruct((b, SWA_HEADS, SWA_HEAD_DIM), F32),
                   jax.ShapeDtypeStruct((b, WINDOW, SWA_KV_W), F32),
                   jax.ShapeDtypeStruct((b, WINDOW, SWA_KV_W), F32)],
        grid=(b // bs,),
        in_specs=[blk(SWA_HEADS, SWA_HEAD_DIM), blk(SWA_KV_HEADS, SWA_HEAD_DIM), blk(1, SWA_KV_W),
                  blk(SWA_KV_HEADS, SWA_HEAD_DIM), blk(1, SWA_KV_W), blk(WINDOW, SWA_KV_W), blk(WINDOW, SWA_KV_W),
                  full(1, SWA_HEAD_DIM), full(1, SWA_HEAD_DIM), full(1, SWA_KV_W),
                  full(SWA_HEADS, 1), full(SWA_HEADS, 1)],
        out_specs=[blk(SWA_HEADS, SWA_HEAD_DIM), blk(WINDOW, SWA_KV_W), blk(WINDOW, SWA_KV_W)],
        compiler_params=_cparams(("parallel",)),
        name="swa_decode",
    )(q3, k3, kf, v3, vf, ck, cv, qn, kn, kn2, sink_col, slope_col)
    return o.reshape(b, BRANCH_W), ok, ov


def _gate_rows(ab, alog_ref, dtb_ref):
    g = -jnp.exp(alog_ref[...]) * _softplus(ab + dtb_ref[...])
    return g, jax.nn.sigmoid(ab)


GDN_PREP_CHUNKS = 4
GDN_SCAN_BATCH = 8


def _gdn_prep_kernel(ys, ab_ref, alog_ref, dtb_ref, u_ref, w_ref, qg_ref, kd_ref, in_ref, eg_ref):
    C = GDN_CHUNK
    d = GDN_HEAD_DIM
    n = GDN_PREP_CHUNKS * C
    heads = range(GDN_HEADS)
    g_all, beta_all = _gate_rows(ab_ref[0], alog_ref, dtb_ref)
    r = lax.broadcasted_iota(I32, (n, n), 0)
    cc = lax.broadcasted_iota(I32, (n, n), 1)
    same = (r // C) == (cc // C)
    incl = same & (r >= cc)
    strict = same & (r > cc)
    upto = same & (r <= cc)
    eye = r == cc
    q = [ys[0, :, h * d:(h + 1) * d] for h in heads]
    k = [ys[0, :, GDN_W + h * d:GDN_W + (h + 1) * d] for h in heads]
    beta = [beta_all[:, GDN_HEADS + h:GDN_HEADS + h + 1] for h in heads]
    gc_row = [jnp.sum(jnp.where(upto, g_all[:, h:h + 1], 0.0), axis=0, keepdims=True) for h in heads]
    gc_col = [jnp.sum(jnp.where(eye, gc_row[h], 0.0), axis=1, keepdims=True) for h in heads]
    decay = [jnp.where(incl, jnp.exp(jnp.where(incl, gc_col[h] - gc_row[h], 0.0)), 0.0) for h in heads]
    kb = [k[h] * beta[h] for h in heads]
    bp = [-jnp.where(strict, _bdot_nt(kb[h], k[h]) * decay[h], 0.0) for h in heads]
    intra = [jnp.where(incl, _bdot_nt(q[h], k[h]) * decay[h], 0.0) for h in heads]
    p = [eye.astype(F32) + bp[h] for h in heads]
    span = 2
    while span < C:
        bp = [_bdot(bp[h], bp[h]) for h in heads]
        p = [p[h] + _bdot(p[h], bp[h]) for h in heads]
        span *= 2
    uw = [_bdot(p[h], jnp.concatenate([ys[0, :, 2 * GDN_W + h * d:2 * GDN_W + (h + 1) * d] * beta[h],
                                       kb[h] * jnp.exp(gc_col[h])], axis=1)) for h in heads]
    for h in heads:
        hs = slice(h * d, (h + 1) * d)
        u_ref[0, :, hs] = uw[h][:, :d]
        w_ref[0, :, hs] = uw[h][:, d:].astype(BF16)
        qg_ref[0, :, hs] = (q[h] * jnp.exp(gc_col[h])).astype(BF16)
        for ci in range(GDN_PREP_CHUNKS):
            r0 = ci * C
            g_last = gc_row[h][:, r0 + C - 1:r0 + C]
            kd_ref[0, r0:r0 + C, hs] = (k[h][r0:r0 + C] * jnp.exp(g_last - gc_col[h][r0:r0 + C])).astype(BF16)
            in_ref[0, r0:r0 + C, h * C:(h + 1) * C] = intra[h][r0:r0 + C, r0:r0 + C].astype(BF16)
            eg_ref[0, ci, h:h + 1, :] = jnp.broadcast_to(jnp.exp(g_last), (1, LANES))


def _gdn_scan_kernel(u_ref, w_ref, qg_ref, kd_ref, in_ref, eg_ref, z_ref, gn_ref, o_ref, s_ref, st):
    c = pl.program_id(1)
    C = GDN_CHUNK
    d = GDN_HEAD_DIM

    @pl.when(c == 0)
    def _():
        st[...] = jnp.zeros_like(st)

    chains = [(bi, h) for bi in range(u_ref.shape[0]) for h in range(GDN_HEADS)]
    hsl = lambda h: slice(h * d, (h + 1) * d)
    s_f = [st[bi, h] for bi, h in chains]
    s_b = [s.astype(BF16) for s in s_f]
    ws = [jnp.dot(w_ref[bi, :, hsl(h)], s_b[i], preferred_element_type=F32) for i, (bi, h) in enumerate(chains)]
    qs = [jnp.dot(qg_ref[bi, :, hsl(h)], s_b[i], preferred_element_type=F32) for i, (bi, h) in enumerate(chains)]
    vb = [(u_ref[bi, :, hsl(h)] - ws[i]).astype(BF16) for i, (bi, h) in enumerate(chains)]
    iv = [jnp.dot(in_ref[bi, :, h * C:(h + 1) * C], vb[i], preferred_element_type=F32)
          for i, (bi, h) in enumerate(chains)]
    kv = [lax.dot_general(kd_ref[bi, :, hsl(h)], vb[i], (((0,), (0,)), ((), ())), preferred_element_type=F32)
          for i, (bi, h) in enumerate(chains)]
    for i, (bi, h) in enumerate(chains):
        st[bi, h] = s_f[i] * eg_ref[bi, 0, h:h + 1, :] + kv[i]
        o_ref[bi, :, hsl(h)] = _rms(qs[i] + iv[i], gn_ref[...]) * z_ref[bi, :, hsl(h)]
    s_ref[...] = st[...]


def _gdn_prompt(qkvg, z, ab, alog, dtb, gnorm):
    b, l, _ = qkvg.shape
    C = GDN_CHUNK
    nc = l // C
    rows = GDN_PREP_CHUNKS * C
    cur = lambda i, j: (i, j, 0)
    const2 = lambda i, j: (0, 0)
    u, w, qg, kd, intra, eg = pl.pallas_call(
        _gdn_prep_kernel,
        out_shape=[jax.ShapeDtypeStruct((b, l, GDN_W), F32),
                   jax.ShapeDtypeStruct((b, l, GDN_W), BF16),
                   jax.ShapeDtypeStruct((b, l, GDN_W), BF16),
                   jax.ShapeDtypeStruct((b, l, GDN_W), BF16),
                   jax.ShapeDtypeStruct((b, l, GDN_HEADS * C), BF16),
                   jax.ShapeDtypeStruct((b, nc, GDN_HEADS, LANES), F32)],
        grid=(b, l // rows),
        in_specs=[pl.BlockSpec((1, rows, GDN_CONV_W), cur),
                  pl.BlockSpec((1, rows, LANES), cur),
                  pl.BlockSpec((1, LANES), const2),
                  pl.BlockSpec((1, LANES), const2)],
        out_specs=[pl.BlockSpec((1, rows, GDN_W), cur),
                   pl.BlockSpec((1, rows, GDN_W), cur),
                   pl.BlockSpec((1, rows, GDN_W), cur),
                   pl.BlockSpec((1, rows, GDN_W), cur),
                   pl.BlockSpec((1, rows, GDN_HEADS * C), cur),
                   pl.BlockSpec((1, GDN_PREP_CHUNKS, GDN_HEADS, LANES), lambda i, j: (i, j, 0, 0))],
        compiler_params=_cparams(("parallel", "parallel")),
        name="gdn_prep",
    )(qkvg, ab, alog, dtb)
    bb = GDN_SCAN_BATCH if b % GDN_SCAN_BATCH == 0 else 1
    seq = lambda wd: pl.BlockSpec((bb, C, wd), cur)
    return pl.pallas_call(
        _gdn_scan_kernel,
        out_shape=[jax.ShapeDtypeStruct((b, l, GDN_W), F32),
                   jax.ShapeDtypeStruct((b, GDN_HEADS, GDN_HEAD_DIM, GDN_HEAD_DIM), F32)],
        grid=(b // bb, nc),
        in_specs=[seq(GDN_W), seq(GDN_W), seq(GDN_W), seq(GDN_W), seq(GDN_HEADS * C),
                  pl.BlockSpec((bb, 1, GDN_HEADS, LANES), lambda i, j: (i, j, 0, 0)),
                  seq(GDN_W),
                  pl.BlockSpec((1, GDN_HEAD_DIM), const2)],
        out_specs=[seq(GDN_W),
                   pl.BlockSpec((bb, GDN_HEADS, GDN_HEAD_DIM, GDN_HEAD_DIM), lambda i, j: (i, 0, 0, 0))],
        scratch_shapes=[pltpu.VMEM((bb, GDN_HEADS, GDN_HEAD_DIM, GDN_HEAD_DIM), F32)],
        compiler_params=_cparams(("parallel", "arbitrary")),
        name="gdn_scan",
    )(u, w, qg, kd, intra, eg, z, gnorm)


def _gdn_decode_kernel(bs, x_ref, cs_ref, z_ref, ab_ref, s_ref, cw_ref, alog_ref, dtb_ref, gn_ref, o_ref, so_ref):
    x = x_ref[...]
    y = x * cw_ref[CONV_WIDTH - 1:CONV_WIDTH, :]
    for j in range(CONV_WIDTH - 1):
        y = y + cs_ref[:, j, :] * cw_ref[j:j + 1, :]
    y = _silu(y)
    g_all, beta_all = _gate_rows(ab_ref[...], alog_ref, dtb_ref)
    eg_all = jnp.exp(g_all)
    z = z_ref[...]
    d = GDN_HEAD_DIM
    eye = (lax.broadcasted_iota(I32, (d, d), 0) == lax.broadcasted_iota(I32, (d, d), 1)).astype(F32)
    for h in range(GDN_HEADS):
        hs = slice(h * d, (h + 1) * d)
        qh = _l2(y[:, h * d:(h + 1) * d]) * (d ** -0.5)
        kh = _l2(y[:, GDN_W + h * d:GDN_W + (h + 1) * d])
        vh = y[:, 2 * GDN_W + h * d:2 * GDN_W + (h + 1) * d]
        kt = _hdot_nt(eye, kh)
        qt = _hdot_nt(eye, qh)
        outs = []
        for b in range(bs):
            s_b = s_ref[b, h]
            kc = kt[:, b:b + 1]
            eg = eg_all[b:b + 1, h:h + 1]
            beta = beta_all[b:b + 1, GDN_HEADS + h:GDN_HEADS + h + 1]
            sk = jnp.sum(s_b * kc, axis=0, keepdims=True)
            v_new = beta * (vh[b:b + 1] - eg * sk)
            s_new = eg * s_b + kc * v_new
            so_ref[b, h] = s_new
            outs.append(jnp.sum(s_new * qt[:, b:b + 1], axis=0, keepdims=True))
        o = jnp.concatenate(outs, axis=0)
        o_ref[:, hs] = _rms(o, gn_ref[...]) * _silu(z[:, hs])


def _gdn_decode(qkvg, z, ab, state, conv_state, conv_w, alog, dtb, gnorm, bs=8):
    b = qkvg.shape[0]
    row = lambda w: pl.BlockSpec((bs, w), lambda i: (i, 0))
    full = lambda *shape: pl.BlockSpec(shape, lambda i: (0,) * len(shape))
    sspec = pl.BlockSpec((bs, GDN_HEADS, GDN_HEAD_DIM, GDN_HEAD_DIM), lambda i: (i, 0, 0, 0))
    return pl.pallas_call(
        functools.partial(_gdn_decode_kernel, bs),
        out_shape=[jax.ShapeDtypeStruct((b, GDN_W), F32),
                   jax.ShapeDtypeStruct(state.shape, F32)],
        grid=(b // bs,),
        in_specs=[row(GDN_CONV_W),
                  pl.BlockSpec((bs, CONV_WIDTH - 1, GDN_CONV_W), lambda i: (i, 0, 0)),
                  row(GDN_W), row(LANES), sspec,
                  full(CONV_WIDTH, GDN_CONV_W), full(1, LANES), full(1, LANES), full(1, GDN_HEAD_DIM)],
        out_specs=[row(GDN_W), sspec],
        compiler_params=_cparams(("parallel",)),
        name="gdn_decode",
    )(qkvg, conv_state, z, ab, state, conv_w, alog, dtb, gnorm)


def _memkv_kernel(x_ref, g_ref, w_ref, kn_ref, k_ref, v_ref):
    hb = _rms(x_ref[...], g_ref[...]).astype(BF16)
    for hd in range(MEM_HEADS):
        hs = slice(hd * MEM_HEAD_DIM, (hd + 1) * MEM_HEAD_DIM)
        k_ref[:, hs] = _rms(jnp.dot(hb, w_ref[:, hs], preferred_element_type=F32), kn_ref[...])
    v_ref[...] = jnp.dot(hb, w_ref[:, BRANCH_W:], preferred_element_type=F32)


def _memkv(mem2, gain, w_kv, kn, tm=512):
    t = mem2.shape[0]
    return pl.pallas_call(
        _memkv_kernel,
        out_shape=[jax.ShapeDtypeStruct((t, BRANCH_W), F32)] * 2,
        grid=(t // tm,),
        in_specs=[pl.BlockSpec((tm, D_MODEL), lambda i: (i, 0)),
                  pl.BlockSpec((1, D_MODEL), lambda i: (0, 0)),
                  pl.BlockSpec((D_MODEL, 2 * BRANCH_W), lambda i: (0, 0)),
                  pl.BlockSpec((1, MEM_HEAD_DIM), lambda i: (0, 0))],
        out_specs=[pl.BlockSpec((tm, BRANCH_W), lambda i: (i, 0))] * 2,
        compiler_params=_cparams(("parallel",)),
        name="memkv",
    )(mem2, gain, w_kv, kn)


def _memattn_prompt_kernel(q_ref, k_ref, v_ref, qn_ref, o_ref):
    q = q_ref[0]
    k = k_ref[0]
    v = v_ref[0]
    scale = MEM_HEAD_DIM ** -0.5
    for hd in range(MEM_HEADS):
        hs = slice(hd * MEM_HEAD_DIM, (hd + 1) * MEM_HEAD_DIM)
        s = _bdot_nt(_rms(q[:, hs], qn_ref[...]), k[:, hs]) * scale
        p = jnp.exp(s - jnp.max(s, axis=-1, keepdims=True))
        o_ref[0, :, hs] = _bdot(p, v[:, hs]) / jnp.sum(p, axis=-1, keepdims=True)


def _memattn_prompt(qm, mk, mv, qn):
    b, l, _ = qm.shape
    tq = 512 if l % 512 == 0 else WINDOW
    return pl.pallas_call(
        _memattn_prompt_kernel,
        out_shape=jax.ShapeDtypeStruct((b, l, BRANCH_W), F32),
        grid=(b, l // tq),
        in_specs=[pl.BlockSpec((1, tq, BRANCH_W), lambda i, j: (i, j, 0)),
                  pl.BlockSpec((1, MEM_LEN, BRANCH_W), lambda i, j: (i, 0, 0)),
                  pl.BlockSpec((1, MEM_LEN, BRANCH_W), lambda i, j: (i, 0, 0)),
                  pl.BlockSpec((1, MEM_HEAD_DIM), lambda i, j: (0, 0))],
        out_specs=pl.BlockSpec((1, tq, BRANCH_W), lambda i, j: (i, j, 0)),
        compiler_params=_cparams(("parallel", "parallel")),
        name="memattn_prompt",
    )(qm, mk, mv, qn)


def _memattn_decode_kernel(bs, q_ref, k_ref, v_ref, qn_ref, o_ref):
    scale = MEM_HEAD_DIM ** -0.5
    for b in range(bs):
        qn = _rms(q_ref[b], qn_ref[...])
        s = jnp.sum(k_ref[b] * qn, axis=-1, keepdims=True) * scale
        p = jnp.exp(s - jnp.max(s, axis=0, keepdims=True))
        den = jnp.sum(p, axis=0)
        o_ref[b] = jnp.sum(p * v_ref[b], axis=0) / den


def _memattn_decode(qm, ck, cv, qn, bs=8):
    b = qm.shape[0]
    q3 = pl.BlockSpec((bs, MEM_HEADS, MEM_HEAD_DIM), lambda i: (i, 0, 0))
    kv = pl.BlockSpec((bs, MEM_LEN, MEM_HEADS, MEM_HEAD_DIM), lambda i: (i, 0, 0, 0))
    o = pl.pallas_call(
        functools.partial(_memattn_decode_kernel, bs),
        out_shape=jax.ShapeDtypeStruct((b, MEM_HEADS, MEM_HEAD_DIM), F32),
        grid=(b // bs,),
        in_specs=[q3, kv, kv, pl.BlockSpec((1, MEM_HEAD_DIM), lambda i: (0, 0))],
        out_specs=q3,
        compiler_params=_cparams(("parallel",)),
        name="memattn_decode",
    )(qm.reshape(b, MEM_HEADS, MEM_HEAD_DIM), ck, cv, qn)
    return o.reshape(b, BRANCH_W)


def _merge_kernel(oa_ref, ob_ref, oc_ref, gate_ref, x_ref, wb_ref, wo_ref, g2_ref, wr_ref, br_ref,
                  x1_ref, h2_ref, lg_ref):
    acc = None
    for i, o_ref in enumerate((oa_ref, ob_ref, oc_ref)):
        mixed = jnp.dot(o_ref[...].astype(BF16), wb_ref[i], preferred_element_type=F32)
        term = gate_ref[:, i * D_MODEL:(i + 1) * D_MODEL].astype(F32) * mixed
        acc = term if acc is None else acc + term
    x1 = x_ref[...] + jnp.dot(acc.astype(BF16), wo_ref[...], preferred_element_type=F32)
    x1_ref[...] = x1
    h2 = _rms(x1, g2_ref[...])
    h2_ref[...] = h2.astype(BF16)
    h_hi = h2.astype(BF16)
    h_lo = (h2 - h_hi.astype(F32)).astype(BF16)
    nt = (((1,), (1,)), ((), ()))
    a = lax.dot_general(wr_ref[...], h_hi, nt, preferred_element_type=F32)
    b = lax.dot_general(wr_ref[0:N_EXPERTS, :], h_lo, nt, preferred_element_type=F32)
    lg_ref[...] = a[0:N_EXPERTS] + a[N_EXPERTS:] + b + br_ref[...]


def _merge(oa, ob, oc, gate, x2, wb, wo, g2, wr_t, br_col, tm):
    t = x2.shape[0]
    row = lambda w: pl.BlockSpec((tm, w), lambda i: (i, 0))
    full = lambda *shape: pl.BlockSpec(shape, lambda i: (0,) * len(shape))
    return pl.pallas_call(
        _merge_kernel,
        out_shape=[jax.ShapeDtypeStruct((t, D_MODEL), F32),
                   jax.ShapeDtypeStruct((t, D_MODEL), BF16),
                   jax.ShapeDtypeStruct((N_EXPERTS, t), F32)],
        grid=(t // tm,),
        in_specs=[row(BRANCH_W), row(BRANCH_W), row(BRANCH_W), row(N_BRANCH * D_MODEL), row(D_MODEL),
                  full(N_BRANCH, BRANCH_W, D_MODEL), full(D_MODEL, D_MODEL), full(1, D_MODEL),
                  full(2 * N_EXPERTS, D_MODEL), full(N_EXPERTS, 1)],
        out_specs=[row(D_MODEL), row(D_MODEL), pl.BlockSpec((N_EXPERTS, tm), lambda i: (0, i))],
        compiler_params=_cparams(("parallel",)),
        name="merge",
    )(oa, ob, oc, gate, x2, wb, wo, g2, wr_t, br_col)


def _col_to_row(col):
    n = col.shape[0]
    r = lax.broadcasted_iota(I32, (n, n), 0)
    c = lax.broadcasted_iota(I32, (n, n), 1)
    return jnp.sum(jnp.where(r == c, col, 0.0), axis=0, keepdims=True)


def _row_to_col(row):
    n = row.shape[1]
    r = lax.broadcasted_iota(I32, (n, n), 0)
    c = lax.broadcasted_iota(I32, (n, n), 1)
    return jnp.sum(jnp.where(r == c, row, 0.0), axis=1, keepdims=True)


def _lane_pad(row):
    return jnp.concatenate([row, jnp.zeros((1, LANES - row.shape[1]), row.dtype)], axis=1)


def _route_kernel(lg_ref, pos_ref, gate_ref, meta_ref, cnt_ref, carry):
    i = pl.program_id(0)
    tn = lg_ref.shape[1]

    @pl.when(i == 0)
    def _():
        carry[...] = jnp.zeros_like(carry)

    l = lg_ref[...]
    eio = lax.broadcasted_iota(I32, (N_EXPERTS, tn), 0)
    hot = jnp.zeros((N_EXPERTS, tn), F32)
    vals, idxs = [], []
    for _ in range(TOP_K):
        m = jnp.max(l, axis=0, keepdims=True)
        idx = jnp.min(jnp.where(l == m, eio, N_EXPERTS), axis=0, keepdims=True)
        sel = eio == idx
        vals.append(m)
        idxs.append(idx)
        hot = hot + sel.astype(F32)
        l = jnp.where(sel, -jnp.inf, l)
    ex = [jnp.exp(v - vals[0]) for v in vals]
    tot = ex[0] + ex[1] + ex[2] + ex[3]
    before = (lax.broadcasted_iota(I32, (tn, tn), 0) < lax.broadcasted_iota(I32, (tn, tn), 1)).astype(BF16)
    within = jnp.dot(hot.astype(BF16), before, preferred_element_type=F32)
    cnt_col = jnp.sum(hot, axis=1, keepdims=True)
    room = jnp.floor((cnt_col + (MOE_UNIT - 1)) * (1.0 / MOE_UNIT)) * MOE_UNIT
    r = lax.broadcasted_iota(I32, (N_EXPERTS, N_EXPERTS), 0)
    c = lax.broadcasted_iota(I32, (N_EXPERTS, N_EXPERTS), 1)
    start_col = _row_to_col(jnp.sum(jnp.where(r < c, room, 0.0), axis=0, keepdims=True))
    for k in range(TOP_K):
        gate_ref[k:k + 1, :] = ex[k] / tot
        pos_ref[k:k + 1, :] = jnp.sum(jnp.where(eio == idxs[k], start_col + within, 0.0), axis=0,
                                      keepdims=True).astype(I32)
    meta_ref[0] = _lane_pad(jnp.concatenate([_col_to_row(room), _col_to_row(carry[:, 0:1])], axis=1)).astype(I32)
    carry[...] = carry[...] + room
    cnt_ref[...] = carry[...]


def _route(logits_t, tn):
    t = logits_t.shape[1]
    kt = pl.BlockSpec((TOP_K, tn), lambda i: (0, i))
    return pl.pallas_call(
        _route_kernel,
        out_shape=[jax.ShapeDtypeStruct((TOP_K, t), I32),
                   jax.ShapeDtypeStruct((TOP_K, t), F32),
                   jax.ShapeDtypeStruct((t // tn, 1, LANES), I32),
                   jax.ShapeDtypeStruct((N_EXPERTS, LANES), F32)],
        grid=(t // tn,),
        in_specs=[pl.BlockSpec((N_EXPERTS, tn), lambda i: (0, i))],
        out_specs=[kt, kt, pl.BlockSpec((1, 1, LANES), lambda i: (i, 0, 0)),
                   pl.BlockSpec((N_EXPERTS, LANES), lambda i: (0, 0))],
        scratch_shapes=[pltpu.VMEM((N_EXPERTS, LANES), F32)],
        compiler_params=_cparams(("arbitrary",)),
        name="moe_route",
    )(logits_t)


def _layout_kernel(bm, cnt_ref, be_ref, na_ref, ps_ref, pe_ref, ct_ref):
    nbp = be_ref.shape[1]
    cnt_col = cnt_ref[:, 0:1]
    size = jnp.floor((cnt_col + (bm - 1)) * (1.0 / bm)) * bm
    r = lax.broadcasted_iota(I32, (N_EXPERTS, N_EXPERTS), 0)
    c = lax.broadcasted_iota(I32, (N_EXPERTS, N_EXPERTS), 1)
    ends_row = jnp.sum(jnp.where(r <= c, size, 0.0), axis=0, keepdims=True)
    ends_col = _row_to_col(ends_row)
    nact = ends_row[:, N_EXPERTS - 1:N_EXPERTS] * (1.0 / bm)
    blk = jnp.minimum(lax.broadcasted_iota(I32, (1, nbp), 1).astype(F32), nact - 1.0)
    be = jnp.sum((ends_col <= blk * bm).astype(F32), axis=0, keepdims=True)
    be_ref[...] = jnp.minimum(be, N_EXPERTS - 1.0).astype(I32)
    na_ref[...] = jnp.broadcast_to(nact, na_ref.shape).astype(I32)
    ps_ref[...] = _lane_pad(ends_row - _col_to_row(size)).astype(I32)
    pe_ref[...] = _lane_pad(ends_row).astype(I32)
    ct_ref[...] = _lane_pad(_col_to_row(cnt_col)).astype(I32)


def _layout(cnt, nbp, bm):
    row = jax.ShapeDtypeStruct((1, LANES), I32)
    return pl.pallas_call(
        functools.partial(_layout_kernel, bm),
        out_shape=[jax.ShapeDtypeStruct((1, nbp), I32), row, row, row, row],
        name="moe_layout",
    )(cnt)


def _stage_rows(tn):
    return tn * TOP_K + N_EXPERTS * MOE_UNIT


def _run_copies(meta_ref, ps_ref, make_copy, tn):
    top = (tn // MOE_UNIT).bit_length() - 1
    src = jnp.int32(0)
    total = jnp.int32(0)
    for e in range(N_EXPERTS):
        units = lax.shift_right_logical(meta_ref[0, 0, e], 3)
        dst = ps_ref[e] + meta_ref[0, 0, N_EXPERTS + e]
        off = jnp.int32(0)
        for k in range(top, -1, -1):
            rows = MOE_UNIT << k
            take = lax.shift_right_logical(units, k) & 1

            @pl.when(take == 1)
            def _(src=src, dst=dst, off=off, rows=rows):
                make_copy(pl.multiple_of(src + off, MOE_UNIT), pl.multiple_of(dst + off, MOE_UNIT), rows).start()
            off = off + take * rows
        src = src + units * MOE_UNIT
        total = total + units
    return total


def _wait_copies(units, make_copy, max_rows):
    for k in range((max_rows // MOE_UNIT).bit_length()):
        if (MOE_UNIT << k) > max_rows:
            break

        @pl.when(lax.shift_right_logical(units, k) & 1 == 1)
        def _(k=k):
            make_copy(0, 0, MOE_UNIT << k).wait()


def _wait_units(count, make_copy):
    def body(u, carry):
        make_copy(0, 0).wait()
        return carry
    lax.fori_loop(0, count, body, 0)


def _dispatch_kernel(ps_ref, pe_ref, ct_ref, meta_ref, pos_ref, h_ref, xg_ref, stage, zrows, zblock, started, sem,
                     zsem):
    i = pl.program_id(0)
    n = pl.num_programs(0)
    slot = i % 2
    rows, tn = stage.shape[1], pos_ref.shape[1]

    def copy(s):
        return lambda src, dst, nrows: pltpu.make_async_copy(stage.at[s, pl.ds(src, nrows)],
                                                             xg_ref.at[pl.ds(dst, nrows)], sem.at[s])

    @pl.when(i >= 2)
    def _():
        _wait_copies(started[slot], copy(slot), rows)

    srow = lax.broadcasted_iota(I32, (rows, tn), 0)
    hit = srow == pos_ref[0:1, :]
    for k in range(1, TOP_K):
        hit = hit | (srow == pos_ref[k:k + 1, :])
    onehot = jnp.where(hit, 1.0, 0.0).astype(BF16)
    stage[slot] = jnp.dot(onehot, h_ref[...], preferred_element_type=F32)
    started[slot] = _run_copies(meta_ref, ps_ref, copy(slot), tn)

    @pl.when(i == n - 1)
    def _():
        _wait_copies(started[slot], copy(slot), rows)

        @pl.when(n >= 2)
        def _():
            _wait_copies(started[1 - slot], copy(1 - slot), rows)

        zrows[...] = jnp.zeros_like(zrows)
        zero = lambda src, dst: pltpu.make_async_copy(zrows, xg_ref.at[pl.ds(dst, MOE_UNIT)], zsem)
        total = jnp.int32(0)
        for e in range(N_EXPERTS):
            lo = ps_ref[e] + ct_ref[e]
            units = lax.shift_right_logical(pe_ref[e] - lo, 3)

            def body(u, carry, lo=lo):
                zero(0, pl.multiple_of(lo + u * MOE_UNIT, MOE_UNIT)).start()
                return carry
            lax.fori_loop(0, units, body, 0)
            total = total + units
        _wait_units(total, zero)

        zblock[...] = jnp.zeros_like(zblock)
        bm = zblock.shape[0]
        zero_block = lambda blk: pltpu.make_async_copy(zblock, xg_ref.at[pl.ds(pl.multiple_of(blk * bm, bm), bm)],
                                                       zsem)
        first = lax.shift_right_logical(pe_ref[N_EXPERTS - 1], bm.bit_length() - 1)
        last = xg_ref.shape[0] // bm

        def start_block(blk, carry):
            zero_block(blk).start()
            return carry
        lax.fori_loop(first, last, start_block, 0)

        def wait_block(blk, carry):
            zero_block(0).wait()
            return carry
        lax.fori_loop(first, last, wait_block, 0)


def _dispatch(ps, pe, ct, meta, pos, h2, n_slots, tn, bm):
    t = h2.shape[0]
    return pl.pallas_call(
        _dispatch_kernel,
        out_shape=jax.ShapeDtypeStruct((n_slots, D_MODEL), F32),
        grid_spec=pltpu.PrefetchScalarGridSpec(
            num_scalar_prefetch=3,
            grid=(t // tn,),
            in_specs=[pl.BlockSpec((1, 1, LANES), lambda i, *_: (i, 0, 0), memory_space=pltpu.SMEM),
                      pl.BlockSpec((TOP_K, tn), lambda i, *_: (0, i)),
                      pl.BlockSpec((tn, D_MODEL), lambda i, *_: (i, 0))],
            out_specs=pl.BlockSpec(memory_space=pl.ANY),
            scratch_shapes=[pltpu.VMEM((2, _stage_rows(tn), D_MODEL), F32),
                            pltpu.VMEM((MOE_UNIT, D_MODEL), F32),
                            pltpu.VMEM((bm, D_MODEL), F32),
                            pltpu.SMEM((2,), I32),
                            pltpu.SemaphoreType.DMA((2,)),
                            pltpu.SemaphoreType.DMA(())]),
        compiler_params=_cparams(("arbitrary",)),
        name="moe_dispatch",
    )(ps, pe, ct, meta, pos, h2)


def _expert_kernel(be_ref, na_ref, x_ref, w1_ref, b1_ref, w2_ref, b2_ref, y_ref, w1s, w2s):
    i = pl.program_id(0)
    e = be_ref[i]
    prev = be_ref[jnp.maximum(i - 1, 0)]

    @pl.when((i == 0) | (e != prev))
    def _():
        rows = 128
        for r0 in range(0, D_MODEL, rows):
            w1s[r0:r0 + rows, :] = w1_ref[r0:r0 + rows, :].astype(BF16)
        for r0 in range(0, D_FF, rows):
            w2s[r0:r0 + rows, :] = w2_ref[r0:r0 + rows, :].astype(BF16)

    @pl.when(i < na_ref[0])
    def _():
        hmid = jnp.dot(x_ref[...].astype(BF16), w1s[...], preferred_element_type=F32) + b1_ref[...]
        glu = jnp.minimum(hmid[:, :D_FF], SWIGLU_LIMIT)
        lin = jnp.clip(hmid[:, D_FF:], -SWIGLU_LIMIT, SWIGLU_LIMIT)
        act = glu * jax.nn.sigmoid(SWIGLU_ALPHA * glu) * (lin + 1.0)
        y_ref[...] = jnp.dot(act.astype(BF16), w2s[...], preferred_element_type=F32) + b2_ref[...]

    @pl.when(i >= na_ref[0])
    def _():
        y_ref[...] = jnp.zeros_like(y_ref)


def _experts(be, nact, xg, layer, w1, b1, w2, b2, bm):
    n_slots = xg.shape[0]
    nb = n_slots // bm
    blk = lambda i, be_r, na_r: (jnp.minimum(i, na_r[0] - 1), 0)
    wsel = lambda i, be_r, na_r: (layer, be_r[i], 0, 0)
    return pl.pallas_call(
        _expert_kernel,
        out_shape=jax.ShapeDtypeStruct((n_slots, D_MODEL), F32),
        grid_spec=pltpu.PrefetchScalarGridSpec(
            num_scalar_prefetch=2,
            grid=(nb,),
            in_specs=[pl.BlockSpec((bm, D_MODEL), blk),
                      pl.BlockSpec((None, None, D_MODEL, 2 * D_FF), wsel),
                      pl.BlockSpec((None, None, 1, 2 * D_FF), wsel),
                      pl.BlockSpec((None, None, D_FF, D_MODEL), wsel),
                      pl.BlockSpec((None, None, 1, D_MODEL), wsel)],
            out_specs=pl.BlockSpec((bm, D_MODEL), lambda i, be_r, na_r: (i, 0)),
            scratch_shapes=[pltpu.VMEM((D_MODEL, 2 * D_FF), BF16), pltpu.VMEM((D_FF, D_MODEL), BF16)]),
        compiler_params=_cparams(("arbitrary",)),
        name="moe_experts",
    )(be, nact, xg, w1, b1, w2, b2)


def _combine_kernel(ps_ref, mcur_ref, mnext_ref, pos_ref, gate_ref, x1_ref, yg_ref, o_ref, stage, started, sem):
    i = pl.program_id(0)
    n = pl.num_programs(0)
    slot = i % 2
    rows, tn = stage.shape[1], pos_ref.shape[1]

    @pl.when(i == 0)
    def _():
        stage[...] = jnp.zeros_like(stage)

    def copy(s):
        return lambda src, dst, nrows: pltpu.make_async_copy(yg_ref.at[pl.ds(dst, nrows)],
                                                             stage.at[s, pl.ds(src, nrows)], sem.at[s])

    @pl.when(i == 0)
    def _():
        started[0] = _run_copies(mcur_ref, ps_ref, copy(0), tn)

    @pl.when(i + 1 < n)
    def _():
        started[1 - slot] = _run_copies(mnext_ref, ps_ref, copy(1 - slot), tn)

    _wait_copies(started[slot], copy(slot), rows)

    eye = (lax.broadcasted_iota(I32, (tn, tn), 0) == lax.broadcasted_iota(I32, (tn, tn), 1)).astype(F32)
    cols = _hdot_nt(eye, jnp.concatenate([pos_ref[...].astype(F32), gate_ref[...]], axis=0))
    lane = lax.broadcasted_iota(I32, (tn, rows), 1)
    weight = jnp.zeros((tn, rows), F32)
    for k in range(TOP_K):
        weight = weight + jnp.where(lane == cols[:, k:k + 1].astype(I32), cols[:, TOP_K + k:TOP_K + k + 1], 0.0)
    w_hi = weight.astype(BF16)
    w_lo = (weight - w_hi.astype(F32)).astype(BF16)
    y = stage[slot].astype(BF16)
    o_ref[...] = (x1_ref[...] + jnp.dot(w_hi, y, preferred_element_type=F32)
                  + jnp.dot(w_lo, y, preferred_element_type=F32))


def _combine(ps, meta, pos, gates, x1, yg, tn):
    t = x1.shape[0]
    nt = t // tn
    return pl.pallas_call(
        _combine_kernel,
        out_shape=jax.ShapeDtypeStruct((t, D_MODEL), F32),
        grid_spec=pltpu.PrefetchScalarGridSpec(
            num_scalar_prefetch=1,
            grid=(nt,),
            in_specs=[pl.BlockSpec((1, 1, LANES), lambda i, *_: (i, 0, 0), memory_space=pltpu.SMEM),
                      pl.BlockSpec((1, 1, LANES), lambda i, *_: (jnp.minimum(i + 1, nt - 1), 0, 0),
                                   memory_space=pltpu.SMEM),
                      pl.BlockSpec((TOP_K, tn), lambda i, *_: (0, i)),
                      pl.BlockSpec((TOP_K, tn), lambda i, *_: (0, i)),
                      pl.BlockSpec((tn, D_MODEL), lambda i, *_: (i, 0)),
                      pl.BlockSpec(memory_space=pl.ANY)],
            out_specs=pl.BlockSpec((tn, D_MODEL), lambda i, *_: (i, 0)),
            scratch_shapes=[pltpu.VMEM((2, _stage_rows(tn), D_MODEL), F32),
                            pltpu.SMEM((2,), I32),
                            pltpu.SemaphoreType.DMA((2,))]),
        compiler_params=_cparams(("arbitrary",)),
        name="moe_combine",
    )(ps, meta, meta, pos, gates, x1, yg)


def _moe(x1, h2, logits_t, layer, w1, b1, w2, b2):
    t = x1.shape[0]
    tn = min(MOE_TOK_TILE, t)
    bm = MOE_BM if t * TOP_K >= N_EXPERTS * MOE_BM else LANES
    n_blocks = (t * TOP_K + (t // tn) * N_EXPERTS * (MOE_UNIT - 1)) // bm + N_EXPERTS + 1
    nbp = -(-n_blocks // LANES) * LANES
    pos, gates, meta, cnt = _route(logits_t, tn)
    be, nact, ps, pe, ct = _layout(cnt, nbp, bm)
    ps, pe, ct = ps[0, :N_EXPERTS], pe[0, :N_EXPERTS], ct[0, :N_EXPERTS]
    xg = _dispatch(ps, pe, ct, meta, pos, h2, n_blocks * bm, tn, bm)
    yg = _experts(be[0, :n_blocks], nact[0, :1], xg, layer, w1, b1, w2, b2, bm)
    return _combine(ps, meta, pos, gates, x1, yg, tn)


def _pack_w_in(w):
    ab = jnp.pad(w[:, _IN_MAIN:_IN_MAIN + _IN_AB], ((0, 0), (0, LANES - _IN_AB)))
    return jnp.concatenate([w[:, :_IN_MAIN], ab, w[:, _IN_MAIN + _IN_AB:]], axis=1).astype(BF16)


def _lane_row(v):
    return jnp.pad(v.astype(F32), (0, LANES - v.shape[0])).reshape(1, LANES)


def _layer_weights(ln1_gain, w_in, q_norm_swa, k_norm_swa, swa_sinks, conv_w, a_log, dt_bias, gdn_norm,
                   q_norm_mem, w_branch, w_out, ln2_gain, w_router, b_router, w_mlp1, b_mlp1, w_mlp2, b_mlp2, layer):
    depth = w_mlp1.shape[0]
    wr_hi = w_router.T.astype(BF16)
    wr_lo = (w_router.T - wr_hi.astype(F32)).astype(BF16)
    return dict(
        layer=layer, wr_t=jnp.concatenate([wr_hi, wr_lo], axis=0), b1r=b_mlp1.reshape(depth, N_EXPERTS, 1, -1), b2r=b_mlp2.reshape(depth, N_EXPERTS, 1, -1),
        ln1=ln1_gain.reshape(1, -1), wp=_pack_w_in(w_in),
        qn_s=q_norm_swa.reshape(1, -1), kn_s=k_norm_swa.reshape(1, -1), sinks=swa_sinks.reshape(1, -1),
        conv_w=conv_w, alog=_lane_row(a_log), dtb=_lane_row(dt_bias), gnorm=gdn_norm.reshape(1, -1),
        qn_m=q_norm_mem.reshape(1, -1), wb=w_branch.astype(BF16), wo=w_out.astype(BF16),
        ln2=ln2_gain.reshape(1, -1), br=b_router.reshape(-1, 1),
        w1=w_mlp1, w2=w_mlp2)


def _finish(lw, oa, ob, oc, gate, x2, tm):
    tm_merge = MERGE_ROWS if x2.shape[0] % MERGE_ROWS == 0 else tm
    x1, h2, logits_t = _merge(oa, ob, oc, gate, x2, lw["wb"], lw["wo"], lw["ln2"], lw["wr_t"], lw["br"], tm_merge)
    return _moe(x1, h2, logits_t, lw["layer"], lw["w1"], lw["b1r"], lw["w2"], lw["b2r"])


def _prompt_layer(x, mem, lw, mem_norm, w_mem_kv, k_norm_mem):
    b, l, d = x.shape
    x2 = x.reshape(b * l, d)
    tm = 256
    tm_in = 2 * tm if l % (2 * tm) == 0 else tm
    qs, ks, vs, qkvg, z, ab, qm, gate, conv_tail = _inproj(x2, lw["ln1"], lw["wp"], lw["conv_w"], tm_in, seq_len=l)
    mk, mv = _memkv(mem.reshape(b * MEM_LEN, d), mem_norm.reshape(1, -1), w_mem_kv.astype(BF16),
                    k_norm_mem.reshape(1, -1))
    r3 = lambda a: a.reshape(b, l, a.shape[-1])
    o_swa, kwin = _swa_prompt(r3(qs), r3(ks), r3(vs), lw["qn_s"], lw["kn_s"], lw["sinks"])
    o_g, s_fin = _gdn_prompt(r3(qkvg), r3(z), r3(ab), lw["alog"], lw["dtb"], lw["gnorm"])
    o_m = _memattn_prompt(r3(qm), mk.reshape(b, MEM_LEN, BRANCH_W), mv.reshape(b, MEM_LEN, BRANCH_W), lw["qn_m"])
    y = _finish(lw, o_swa.reshape(b * l, -1), o_g.reshape(b * l, -1), o_m.reshape(b * l, -1), gate, x2, tm)
    new_k = kwin.reshape(b, WINDOW, SWA_KV_HEADS, SWA_HEAD_DIM)
    new_v = r3(vs)[:, l - WINDOW:].reshape(b, WINDOW, SWA_KV_HEADS, SWA_HEAD_DIM)
    new_conv = conv_tail[:, _HALO - (CONV_WIDTH - 1):]
    mk4 = mk.reshape(b, MEM_LEN, MEM_HEADS, MEM_HEAD_DIM)
    mv4 = mv.reshape(b, MEM_LEN, MEM_HEADS, MEM_HEAD_DIM)
    return y.reshape(b, l, d), new_k, new_v, s_fin, new_conv, mk4, mv4


def _sample_layer(x, cache_k, cache_v, state, conv_state, mem_k, mem_v, lw):
    b, l, d = x.shape
    x2 = x.reshape(b, d)
    tm = TOK_TILE
    qs, ks, vs, qkvg, z, ab, qm, gate = _inproj(x2, lw["ln1"], lw["wp"], lw["conv_w"], tm)
    o_swa, new_k, new_v = _swa_decode(qs, ks, vs, cache_k, cache_v, lw["qn_s"], lw["kn_s"], lw["sinks"])
    o_g, new_s = _gdn_decode(qkvg, z, ab, state, conv_state, lw["conv_w"], lw["alog"], lw["dtb"], lw["gnorm"])
    o_m = _memattn_decode(qm, mem_k, mem_v, lw["qn_m"])
    y = _finish(lw, o_swa, o_g, o_m, gate, x2, tm)
    new_conv = jnp.concatenate([conv_state[:, 1:], qkvg[:, None, :]], axis=1)
    shp = (b, WINDOW, SWA_KV_HEADS, SWA_HEAD_DIM)
    return y.reshape(b, l, d), new_k.reshape(shp), new_v.reshape(shp), new_s, new_conv


def kernel(x_prompt, x_sample, cache_swa_k, cache_swa_v, state_gdn, state_conv, cache_mem_k, cache_mem_v, mem_prompt, ln1_gain, w_in, q_norm_swa, k_norm_swa, swa_sinks, conv_w, a_log, dt_bias, gdn_norm, q_norm_mem, k_norm_mem, mem_norm, w_mem_kv, w_branch, w_out, ln2_gain, w_router, b_router, w_mlp1, b_mlp1, w_mlp2, b_mlp2):
    xp, xs = x_prompt, x_sample
    outs = [[] for _ in range(10)]
    for l in range(ln1_gain.shape[0]):
        lw = _layer_weights(ln1_gain[l], w_in[l], q_norm_swa[l], k_norm_swa[l], swa_sinks[l], conv_w[l], a_log[l],
                            dt_bias[l], gdn_norm[l], q_norm_mem[l], w_branch[l], w_out[l], ln2_gain[l], w_router[l],
                            b_router[l], w_mlp1, b_mlp1, w_mlp2, b_mlp2, l)
        xp, pk, pv, ps, pc, mk, mv = _prompt_layer(xp, mem_prompt, lw, mem_norm[l], w_mem_kv[l], k_norm_mem[l])
        xs, sk, sv, ss, sc = _sample_layer(xs, cache_swa_k[l], cache_swa_v[l], state_gdn[l], state_conv[l],
                                           cache_mem_k[l], cache_mem_v[l], lw)
        for acc, v in zip(outs, (pk, pv, sk, sv, ps, ss, pc, sc, mk, mv)):
            acc.append(v)
    return (xp, xs) + tuple(jnp.stack(o) for o in outs)
```

```python
import functools

import numpy as np
import jax
import jax.numpy as jnp
from jax import lax
from jax.experimental import pallas as pl
from jax.experimental.pallas import tpu as pltpu

F32 = jnp.float32
BF16 = jnp.bfloat16
I32 = jnp.int32

D_MODEL = 1024
BRANCH_W = 512
SWA_HEADS = 8
SWA_KV_HEADS = 2
SWA_HEAD_DIM = 64
SWA_GROUP = SWA_HEADS // SWA_KV_HEADS
SWA_KV_W = SWA_KV_HEADS * SWA_HEAD_DIM
WINDOW = 128
GDN_HEADS = 4
GDN_HEAD_DIM = 128
GDN_W = GDN_HEADS * GDN_HEAD_DIM
GDN_CONV_W = 3 * GDN_W
GDN_CHUNK = 64
CONV_WIDTH = 4
MEM_LEN = 256
MEM_HEADS = 4
MEM_HEAD_DIM = 128
N_BRANCH = 3
N_EXPERTS = 32
TOP_K = 4
D_FF = 1024
SWIGLU_LIMIT = 7.0
SWIGLU_ALPHA = 1.702
EPS = 1e-6
PAST_LEN = 16384

LANES = 128
MOE_BM = 512
MOE_TOK_TILE = 256
MOE_UNIT = 8
TOK_TILE = 128
MERGE_ROWS = 512
VMEM_LIMIT = 56 * 1024 * 1024

_SEG_QS = (0, 512)
_SEG_KS = (512, 128)
_SEG_VS = (640, 128)
_SEG_QKVG = (768, 1536)
_SEG_Z = (2304, 512)
_SEG_AB = (2816, 128)
_SEG_QM = (2944, 512)
_SEG_GATE = (3456, 3072)
_PACKED_W = 6528
_IN_MAIN = 2816
_IN_AB = 8

_NEG = -1e30
_HI = lax.Precision.HIGHEST


def _cparams(sem, vmem=VMEM_LIMIT):
    return pltpu.CompilerParams(dimension_semantics=sem, vmem_limit_bytes=vmem)


def _bdot(a, b):
    return jnp.dot(a.astype(BF16), b.astype(BF16), preferred_element_type=F32)


def _bdot_nt(a, b):
    return lax.dot_general(a.astype(BF16), b.astype(BF16), (((1,), (1,)), ((), ())),
                           preferred_element_type=F32)


def _bdot_tn(a, b):
    return lax.dot_general(a.astype(BF16), b.astype(BF16), (((0,), (0,)), ((), ())),
                           preferred_element_type=F32)


def _hdot(a, b):
    return jnp.dot(a, b, preferred_element_type=F32, precision=_HI)


def _hdot_nt(a, b):
    return lax.dot_general(a, b, (((1,), (1,)), ((), ())), preferred_element_type=F32, precision=_HI)


def _rms(x, gain):
    return x * lax.rsqrt(jnp.mean(x * x, axis=-1, keepdims=True) + EPS) * gain


def _l2(x):
    return x * lax.rsqrt(jnp.sum(x * x, axis=-1, keepdims=True) + EPS)


def _sigmoid(x):
    return 0.5 * jnp.tanh(0.5 * x) + 0.5


def _silu(x):
    return x * _sigmoid(x)


def _softplus(x):
    return jnp.maximum(x, 0.0) + jnp.log1p(jnp.exp(-jnp.abs(x)))


_HALO = 8


def _inproj_kernel(seq_steps, x_ref, g_ref, w_ref, cw_ref, qs, ks, vs, qkvg, z, ab, qm, gate, *conv_refs):
    i = pl.program_id(0)
    x = x_ref[...]
    hb = _rms(x, g_ref[...]).astype(BF16)

    def proj(off, width):
        return jnp.dot(hb, w_ref[:, off:off + width], preferred_element_type=F32)

    step = 512

    def plain(ref, c0=0, width=None):
        def finish(r):
            ref[:, c0:c0 + (width or r.shape[1])] = r
        return finish

    def gate_chunk(c0):
        def finish(r):
            gate[:, c0:c0 + step] = _sigmoid(r).astype(BF16)
        return finish

    def conv_part(part):
        tail_ref, carry = conv_refs
        c0 = part * GDN_W
        cols = slice(c0, c0 + GDN_W)

        def finish(r):
            first = (i % seq_steps) == 0
            sub = lax.broadcasted_iota(I32, (_HALO, 1), 0)
            rows = r.shape[0]
            before = jnp.where(first, 0.0, carry[:, cols])
            y = r * cw_ref[CONV_WIDTH - 1:CONV_WIDTH, cols]
            top = r[0:_HALO] * cw_ref[CONV_WIDTH - 1:CONV_WIDTH, cols]
            for jj in range(CONV_WIDTH - 1):
                sh = CONV_WIDTH - 1 - jj
                y = y + pltpu.roll(r, sh, 0) * cw_ref[jj:jj + 1, cols]
                head_rows = jnp.where(sub < sh, pltpu.roll(before, sh, 0), pltpu.roll(r[0:_HALO], sh, 0))
                top = top + head_rows * cw_ref[jj:jj + 1, cols]
            carry[:, cols] = r[rows - _HALO:]
            tail_ref[0, :, cols] = r[rows - _HALO:]
            act = _silu(jnp.concatenate([top, y[_HALO:]], axis=0))
            if part < 2:
                scale = GDN_HEAD_DIM ** -0.5 if part == 0 else 1.0
                for h in range(GDN_HEADS):
                    hs = slice(h * GDN_HEAD_DIM, (h + 1) * GDN_HEAD_DIM)
                    qkvg[:, c0 + h * GDN_HEAD_DIM:c0 + (h + 1) * GDN_HEAD_DIM] = _l2(act[:, hs]) * scale
            else:
                qkvg[:, cols] = act
        return finish

    gates = [(_SEG_GATE[0] + c0, step, gate_chunk(c0)) for c0 in range(0, _SEG_GATE[1], step)]
    light = [(_SEG_QS[0], _SEG_QS[1], plain(qs)), (_SEG_KS[0], _SEG_KS[1], plain(ks)),
             (_SEG_VS[0], _SEG_VS[1], plain(vs)), (_SEG_AB[0], _SEG_AB[1], plain(ab)),
             (_SEG_QM[0], _SEG_QM[1], plain(qm))]
    if seq_steps is None:
        heavy = gates
        light += [(_SEG_Z[0], _SEG_Z[1], plain(z))]
        light += [(_SEG_QKVG[0] + c0, step, plain(qkvg, c0, step)) for c0 in range(0, GDN_CONV_W, step)]
    else:
        def silu_z(r):
            z[...] = _silu(r)
        heavy = gates + [(_SEG_QKVG[0] + p * GDN_W, GDN_W, conv_part(p)) for p in range(3)]
        heavy.append((_SEG_Z[0], _SEG_Z[1], silu_z))
    tasks = []
    while heavy or light:
        if heavy:
            tasks.append(heavy.pop(0))
        if light:
            tasks.append(light.pop(0))
    pending = None
    for off, width, finish in tasks:
        r = proj(off, width)
        if pending is not None:
            pending[1](pending[0])
        pending = (r, finish)
    pending[1](pending[0])


def _inproj(x2, gain, wp, conv_w, tm, seq_len=None):
    t = x2.shape[0]
    segs = (_SEG_QS, _SEG_KS, _SEG_VS, _SEG_QKVG, _SEG_Z, _SEG_AB, _SEG_QM, _SEG_GATE)
    dtypes = (F32,) * 7 + (BF16,)
    out_shape = [jax.ShapeDtypeStruct((t, w), dt) for (_, w), dt in zip(segs, dtypes)]
    out_specs = [pl.BlockSpec((tm, w), lambda i: (i, 0)) for _, w in segs]
    scratch = []
    seq_steps = None
    if seq_len is not None:
        seq_steps = seq_len // tm
        out_shape.append(jax.ShapeDtypeStruct((t // seq_len, _HALO, GDN_CONV_W), F32))
        out_specs.append(pl.BlockSpec((1, _HALO, GDN_CONV_W), lambda i: (i // seq_steps, 0, 0)))
        scratch.append(pltpu.VMEM((_HALO, GDN_CONV_W), F32))
    return pl.pallas_call(
        functools.partial(_inproj_kernel, seq_steps),
        out_shape=out_shape,
        grid=(t // tm,),
        in_specs=[pl.BlockSpec((tm, D_MODEL), lambda i: (i, 0)),
                  pl.BlockSpec((1, D_MODEL), lambda i: (0, 0)),
                  pl.BlockSpec((D_MODEL, _PACKED_W), lambda i: (0, 0), pipeline_mode=pl.Buffered(1)),
                  pl.BlockSpec((CONV_WIDTH, GDN_CONV_W), lambda i: (0, 0))],
        out_specs=out_specs,
        scratch_shapes=scratch,
        compiler_params=_cparams(("arbitrary",)),
        name="inproj",
    )(x2, gain, wp, conv_w)


SWA_STEP_BLOCKS = 4


def _alibi_slopes(n):
    return [float(2.0 ** (-8.0 * (i + 1) / n)) for i in range(n)]


def _swa_prompt_kernel(q_ref, kc_ref, vc_ref, qg_ref, kg_ref, sink_ref, o_ref, kwin_ref, bias, kprev, vprev):
    n = pl.program_id(1)
    half = SWA_HEAD_DIM
    nblk = q_ref.shape[1] // WINDOW
    work = [(j, h) for j in range(nblk) for h in range(SWA_HEADS)]

    @pl.when(n == 0)
    def _():
        kprev[...] = jnp.zeros_like(kprev)
        vprev[...] = jnp.zeros_like(vprev)

    def fill(tbl, first):
        row = lax.broadcasted_iota(I32, (WINDOW, 2 * WINDOW), 0)
        col = lax.broadcasted_iota(I32, (WINDOW, 2 * WINDOW), 1)
        dist = row + WINDOW - col
        valid = (dist >= 0) & (dist <= WINDOW) & ((col >= WINDOW) | jnp.logical_not(first))
        distf = dist.astype(F32)
        for h, slope in enumerate(_alibi_slopes(SWA_HEADS)):
            bias[tbl, h] = jnp.where(valid, -slope * distf, _NEG)

    @pl.when(n <= 1)
    def _():
        fill(0, n == 0)

    if nblk > 1:
        @pl.when(n == 0)
        def _():
            fill(1, False)

    low = lax.broadcasted_iota(I32, (1, 2 * half), 1) < half

    def pair_rms(x, gain):
        sq = x * x
        s_lo = jnp.sum(jnp.where(low, sq, 0.0), axis=-1, keepdims=True)
        s_hi = jnp.sum(jnp.where(low, 0.0, sq), axis=-1, keepdims=True)
        return x * lax.rsqrt(jnp.where(low, s_lo, s_hi) * (1.0 / half) + EPS) * gain

    kcn = pair_rms(kc_ref[0], kg_ref[...])
    kwin_ref[0] = kcn[(nblk - 1) * WINDOW:]
    kall = jnp.concatenate([kprev[...], kcn], axis=0)
    vall = jnp.concatenate([vprev[...], vc_ref[0]], axis=0)
    kprev[...] = kcn[(nblk - 1) * WINDOW:]
    vprev[...] = vc_ref[0, (nblk - 1) * WINDOW:, :]
    ones = jnp.ones(((nblk + 1) * WINDOW, 2 * half), BF16)

    def placed(x, aug):
        sw = pltpu.roll(x, half, 1)
        out = {(0, 0): jnp.where(low, x, 0.0), (1, 1): jnp.where(low, 0.0, x),
               (0, 1): jnp.where(low, 0.0, sw), (1, 0): jnp.where(low, sw, 0.0)}
        out = {key: val.astype(BF16) for key, val in out.items()}
        return {key: jnp.concatenate([val, ones], axis=1) for key, val in out.items()} if aug else out

    kvar = placed(kall, False)
    vvar = placed(vall, True)
    qn = [pair_rms(q_ref[0, :, t * 2 * half:(t + 1) * 2 * half], qg_ref[...]).astype(BF16)
          for t in range(SWA_HEADS // 2)]
    where_of = lambda h: (h // SWA_GROUP, h % 2)
    qrows = lambda j: slice(j * WINDOW, (j + 1) * WINDOW)
    krows = lambda j: slice(j * WINDOW, (j + 2) * WINDOW)
    nt = (((1,), (1,)), ((), ()))
    s = {(j, h): lax.dot_general(qn[h // 2][qrows(j)], kvar[where_of(h)][krows(j)], nt, preferred_element_type=F32)
         + bias[min(j, 1), h] for j, h in work}
    sink = [sink_ref[0:1, h:h + 1] for h in range(SWA_HEADS)]
    m = {(j, h): jnp.maximum(jnp.max(s[j, h], axis=-1, keepdims=True), sink[h]) for j, h in work}
    p = {(j, h): jnp.exp(s[j, h] - m[j, h]).astype(BF16) for j, h in work}
    res = {(j, h): jnp.dot(p[j, h], vvar[where_of(h)][krows(j)], preferred_element_type=F32)
           for j, h in work}
    inv = {(j, h): 1.0 / (res[j, h][:, 2 * half:2 * half + 1] + jnp.exp(sink[h] - m[j, h])) for j, h in work}
    for j in range(nblk):
        for t in range(SWA_HEADS // 2):
            h0, h1 = 2 * t, 2 * t + 1
            num = res[j, h0][:, :2 * half] + res[j, h1][:, :2 * half]
            o_ref[0, qrows(j), t * 2 * half:(t + 1) * 2 * half] = num * jnp.where(low, inv[j, h0], inv[j, h1])


def _swa_prompt(qs, ks, vs, qn, kn, sinks):
    b, l, _ = qs.shape
    rows = SWA_STEP_BLOCKS * WINDOW if l % (SWA_STEP_BLOCKS * WINDOW) == 0 else WINDOW
    cur = lambda i, j: (i, j, 0)
    const2 = lambda i, j: (0, 0)
    q_gain = jnp.tile(qn, (1, 2)) * (SWA_HEAD_DIM ** -0.5)
    k_gain = jnp.tile(kn, (1, 2))
    return pl.pallas_call(
        _swa_prompt_kernel,
        out_shape=[jax.ShapeDtypeStruct((b, l, BRANCH_W), F32),
                   jax.ShapeDtypeStruct((b, WINDOW, SWA_KV_W), F32)],
        grid=(b, l // rows),
        in_specs=[pl.BlockSpec((1, rows, BRANCH_W), cur),
                  pl.BlockSpec((1, rows, SWA_KV_W), cur),
                  pl.BlockSpec((1, rows, SWA_KV_W), cur),
                  pl.BlockSpec((1, 2 * SWA_HEAD_DIM), const2),
                  pl.BlockSpec((1, 2 * SWA_HEAD_DIM), const2),
                  pl.BlockSpec((1, SWA_HEADS), const2)],
        out_specs=[pl.BlockSpec((1, rows, BRANCH_W), cur),
                   pl.BlockSpec((1, WINDOW, SWA_KV_W), lambda i, j: (i, 0, 0))],
        scratch_shapes=[pltpu.VMEM((2, SWA_HEADS, WINDOW, 2 * WINDOW), F32),
                        pltpu.VMEM((WINDOW, SWA_KV_W), F32),
                        pltpu.VMEM((WINDOW, SWA_KV_W), F32)],
        compiler_params=_cparams(("arbitrary", "arbitrary")),
        name="swa_prompt",
    )(qs, ks, vs, q_gain, k_gain, sinks)


def _swa_decode_kernel(bs, q_ref, k3_ref, kf_ref, v3_ref, vf_ref, ck_ref, cv_ref, qn_ref, kn_ref, kn2_ref,
                       sink_ref, slope_ref, o_ref, ok_ref, ov_ref):
    scale = SWA_HEAD_DIM ** -0.5
    lane = lax.broadcasted_iota(I32, (1, SWA_KV_W), 1)
    rowi = lax.broadcasted_iota(I32, (WINDOW, SWA_KV_W), 0)
    keyd = (WINDOW - lax.broadcasted_iota(I32, (1, WINDOW), 1)).astype(F32)
    for b in range(bs):
        qn = _rms(q_ref[b], qn_ref[...])
        kn3 = _rms(k3_ref[b], kn_ref[...])
        v3 = v3_ref[b]
        kf = kf_ref[b]
        sq = kf * kf
        ms0 = jnp.sum(jnp.where(lane < SWA_HEAD_DIM, sq, 0.0), axis=-1, keepdims=True) / SWA_HEAD_DIM
        ms1 = jnp.sum(jnp.where(lane >= SWA_HEAD_DIM, sq, 0.0), axis=-1, keepdims=True) / SWA_HEAD_DIM
        knf = kf * lax.rsqrt(jnp.where(lane < SWA_HEAD_DIM, ms0, ms1) + EPS) * kn2_ref[...]
        ck = ck_ref[b]
        cv = cv_ref[b]
        for g in range(SWA_KV_HEADS):
            sl = slice(g * SWA_HEAD_DIM, (g + 1) * SWA_HEAD_DIM)
            hs = slice(g * SWA_GROUP, (g + 1) * SWA_GROUP)
            qg = qn[hs]
            slope = slope_ref[hs]
            sink = sink_ref[hs]
            s = _bdot_nt(qg, ck[:, sl]) * scale - slope * keyd
            s_new = jnp.sum(qg * kn3[g:g + 1], axis=-1, keepdims=True) * scale
            m = jnp.maximum(jnp.maximum(jnp.max(s, axis=-1, keepdims=True), s_new), sink)
            p = jnp.exp(s - m)
            p_new = jnp.exp(s_new - m)
            denom = jnp.sum(p, axis=-1, keepdims=True) + p_new + jnp.exp(sink - m)
            o_ref[b, hs, :] = (_bdot(p, cv[:, sl]) + p_new * v3[g:g + 1]) / denom
        last = rowi == WINDOW - 1
        ok_ref[b] = jnp.where(last, knf, pltpu.roll(ck, WINDOW - 1, 0))
        ov_ref[b] = jnp.where(last, vf_ref[b], pltpu.roll(cv, WINDOW - 1, 0))


def _swa_decode(qs, ks, vs, cache_k, cache_v, qn, kn, sinks, bs=8):
    b = qs.shape[0]
    q3 = qs.reshape(b, SWA_HEADS, SWA_HEAD_DIM)
    k3 = ks.reshape(b, SWA_KV_HEADS, SWA_HEAD_DIM)
    kf = ks.reshape(b, 1, SWA_KV_W)
    v3 = vs.reshape(b, SWA_KV_HEADS, SWA_HEAD_DIM)
    vf = vs.reshape(b, 1, SWA_KV_W)
    ck = cache_k.reshape(b, WINDOW, SWA_KV_W)
    cv = cache_v.reshape(b, WINDOW, SWA_KV_W)
    kn2 = jnp.concatenate([kn, kn], axis=-1)
    sink_col = sinks.reshape(SWA_HEADS, 1)
    slope_col = jnp.asarray(np.asarray(_alibi_slopes(SWA_HEADS), np.float32).reshape(SWA_HEADS, 1))
    blk = lambda *shape: pl.BlockSpec((bs,) + shape, lambda i: (i,) + (0,) * len(shape))
    full = lambda *shape: pl.BlockSpec(shape, lambda i: (0,) * len(shape))
    o, ok, ov = pl.pallas_call(
        functools.partial(_swa_decode_kernel, bs),
        out_shape=[jax.ShapeDtypeStruct((b, SWA_HEADS, SWA_HEAD_DIM), F32),
                   jax.ShapeDtypeStruct((b, WINDOW, SWA_KV_W), F32),
                   jax.ShapeDtypeStruct((b, WINDOW, SWA_KV_W), F32)],
        grid=(b // bs,),
        in_specs=[blk(SWA_HEADS, SWA_HEAD_DIM), blk(SWA_KV_HEADS, SWA_HEAD_DIM), blk(1, SWA_KV_W),
                  blk(SWA_KV_HEADS, SWA_HEAD_DIM), blk(1, SWA_KV_W), blk(WINDOW, SWA_KV_W), blk(WINDOW, SWA_KV_W),
                  full(1, SWA_HEAD_DIM), full(1, SWA_HEAD_DIM), full(1, SWA_KV_W),
                  full(SWA_HEADS, 1), full(SWA_HEADS, 1)],
        out_specs=[blk(SWA_HEADS, SWA_HEAD_DIM), blk(WINDOW, SWA_KV_W), blk(WINDOW, SWA_KV_W)],
        compiler_params=_cparams(("parallel",)),
        name="swa_decode",
    )(q3, k3, kf, v3, vf, ck, cv, qn, kn, kn2, sink_col, slope_col)
    return o.reshape(b, BRANCH_W), ok, ov


def _gate_rows(ab, alog_ref, dtb_ref):
    g = -jnp.exp(alog_ref[...]) * _softplus(ab + dtb_ref[...])
    return g, jax.nn.sigmoid(ab)


GDN_PREP_CHUNKS = 4
GDN_SCAN_BATCH = 8


def _gdn_prep_kernel(ys, ab_ref, alog_ref, dtb_ref, u_ref, w_ref, qg_ref, kd_ref, in_ref, eg_ref):
    C = GDN_CHUNK
    d = GDN_HEAD_DIM
    n = GDN_PREP_CHUNKS * C
    heads = range(GDN_HEADS)
    g_all, beta_all = _gate_rows(ab_ref[0], alog_ref, dtb_ref)
    r = lax.broadcasted_iota(I32, (n, n), 0)
    cc = lax.broadcasted_iota(I32, (n, n), 1)
    same = (r // C) == (cc // C)
    incl = same & (r >= cc)
    strict = same & (r > cc)
    upto = same & (r <= cc)
    eye = r == cc
    q = [ys[0, :, h * d:(h + 1) * d] for h in heads]
    k = [ys[0, :, GDN_W + h * d:GDN_W + (h + 1) * d] for h in heads]
    beta = [beta_all[:, GDN_HEADS + h:GDN_HEADS + h + 1] for h in heads]
    gc_row = [jnp.sum(jnp.where(upto, g_all[:, h:h + 1], 0.0), axis=0, keepdims=True) for h in heads]
    gc_col = [jnp.sum(jnp.where(eye, gc_row[h], 0.0), axis=1, keepdims=True) for h in heads]
    decay = [jnp.where(incl, jnp.exp(jnp.where(incl, gc_col[h] - gc_row[h], 0.0)), 0.0) for h in heads]
    kb = [k[h] * beta[h] for h in heads]
    bp = [-jnp.where(strict, _bdot_nt(kb[h], k[h]) * decay[h], 0.0) for h in heads]
    intra = [jnp.where(incl, _bdot_nt(q[h], k[h]) * decay[h], 0.0) for h in heads]
    p = [eye.astype(F32) + bp[h] for h in heads]
    span = 2
    while span < C:
        bp = [_bdot(bp[h], bp[h]) for h in heads]
        p = [p[h] + _bdot(p[h], bp[h]) for h in heads]
        span *= 2
    uw = [_bdot(p[h], jnp.concatenate([ys[0, :, 2 * GDN_W + h * d:2 * GDN_W + (h + 1) * d] * beta[h],
                                       kb[h] * jnp.exp(gc_col[h])], axis=1)) for h in heads]
    for h in heads:
        hs = slice(h * d, (h + 1) * d)
        u_ref[0, :, hs] = uw[h][:, :d]
        w_ref[0, :, hs] = uw[h][:, d:].astype(BF16)
        qg_ref[0, :, hs] = (q[h] * jnp.exp(gc_col[h])).astype(BF16)
        for ci in range(GDN_PREP_CHUNKS):
            r0 = ci * C
            g_last = gc_row[h][:, r0 + C - 1:r0 + C]
            kd_ref[0, r0:r0 + C, hs] = (k[h][r0:r0 + C] * jnp.exp(g_last - gc_col[h][r0:r0 + C])).astype(BF16)
            in_ref[0, r0:r0 + C, h * C:(h + 1) * C] = intra[h][r0:r0 + C, r0:r0 + C].astype(BF16)
            eg_ref[0, ci, h:h + 1, :] = jnp.broadcast_to(jnp.exp(g_last), (1, LANES))


def _gdn_scan_kernel(u_ref, w_ref, qg_ref, kd_ref, in_ref, eg_ref, z_ref, gn_ref, o_ref, s_ref, st):
    c = pl.program_id(1)
    C = GDN_CHUNK
    d = GDN_HEAD_DIM

    @pl.when(c == 0)
    def _():
        st[...] = jnp.zeros_like(st)

    chains = [(bi, h) for bi in range(u_ref.shape[0]) for h in range(GDN_HEADS)]
    hsl = lambda h: slice(h * d, (h + 1) * d)
    s_f = [st[bi, h] for bi, h in chains]
    s_b = [s.astype(BF16) for s in s_f]
    ws = [jnp.dot(w_ref[bi, :, hsl(h)], s_b[i], preferred_element_type=F32) for i, (bi, h) in enumerate(chains)]
    qs = [jnp.dot(qg_ref[bi, :, hsl(h)], s_b[i], preferred_element_type=F32) for i, (bi, h) in enumerate(chains)]
    vb = [(u_ref[bi, :, hsl(h)] - ws[i]).astype(BF16) for i, (bi, h) in enumerate(chains)]
    iv = [jnp.dot(in_ref[bi, :, h * C:(h + 1) * C], vb[i], preferred_element_type=F32)
          for i, (bi, h) in enumerate(chains)]
    kv = [lax.dot_general(kd_ref[bi, :, hsl(h)], vb[i], (((0,), (0,)), ((), ())), preferred_element_type=F32)
          for i, (bi, h) in enumerate(chains)]
    for i, (bi, h) in enumerate(chains):
        st[bi, h] = s_f[i] * eg_ref[bi, 0, h:h + 1, :] + kv[i]
        o_ref[bi, :, hsl(h)] = _rms(qs[i] + iv[i], gn_ref[...]) * z_ref[bi, :, hsl(h)]
    s_ref[...] = st[...]


def _gdn_prompt(qkvg, z, ab, alog, dtb, gnorm):
    b, l, _ = qkvg.shape
    C = GDN_CHUNK
    nc = l // C
    rows = GDN_PREP_CHUNKS * C
    cur = lambda i, j: (i, j, 0)
    const2 = lambda i, j: (0, 0)
    u, w, qg, kd, intra, eg = pl.pallas_call(
        _gdn_prep_kernel,
        out_shape=[jax.ShapeDtypeStruct((b, l, GDN_W), F32),
                   jax.ShapeDtypeStruct((b, l, GDN_W), BF16),
                   jax.ShapeDtypeStruct((b, l, GDN_W), BF16),
                   jax.ShapeDtypeStruct((b, l, GDN_W), BF16),
                   jax.ShapeDtypeStruct((b, l, GDN_HEADS * C), BF16),
                   jax.ShapeDtypeStruct((b, nc, GDN_HEADS, LANES), F32)],
        grid=(b, l // rows),
        in_specs=[pl.BlockSpec((1, rows, GDN_CONV_W), cur),
                  pl.BlockSpec((1, rows, LANES), cur),
                  pl.BlockSpec((1, LANES), const2),
                  pl.BlockSpec((1, LANES), const2)],
        out_specs=[pl.BlockSpec((1, rows, GDN_W), cur),
                   pl.BlockSpec((1, rows, GDN_W), cur),
                   pl.BlockSpec((1, rows, GDN_W), cur),
                   pl.BlockSpec((1, rows, GDN_W), cur),
                   pl.BlockSpec((1, rows, GDN_HEADS * C), cur),
                   pl.BlockSpec((1, GDN_PREP_CHUNKS, GDN_HEADS, LANES), lambda i, j: (i, j, 0, 0))],
        compiler_params=_cparams(("parallel", "parallel")),
        name="gdn_prep",
    )(qkvg, ab, alog, dtb)
    bb = GDN_SCAN_BATCH if b % GDN_SCAN_BATCH == 0 else 1
    seq = lambda wd: pl.BlockSpec((bb, C, wd), cur)
    return pl.pallas_call(
        _gdn_scan_kernel,
        out_shape=[jax.ShapeDtypeStruct((b, l, GDN_W), F32),
                   jax.ShapeDtypeStruct((b, GDN_HEADS, GDN_HEAD_DIM, GDN_HEAD_DIM), F32)],
        grid=(b // bb, nc),
        in_specs=[seq(GDN_W), seq(GDN_W), seq(GDN_W), seq(GDN_W), seq(GDN_HEADS * C),
                  pl.BlockSpec((bb, 1, GDN_HEADS, LANES), lambda i, j: (i, j, 0, 0)),
                  seq(GDN_W),
                  pl.BlockSpec((1, GDN_HEAD_DIM), const2)],
        out_specs=[seq(GDN_W),
                   pl.BlockSpec((bb, GDN_HEADS, GDN_HEAD_DIM, GDN_HEAD_DIM), lambda i, j: (i, 0, 0, 0))],
        scratch_shapes=[pltpu.VMEM((bb, GDN_HEADS, GDN_HEAD_DIM, GDN_HEAD_DIM), F32)],
        compiler_params=_cparams(("parallel", "arbitrary")),
        name="gdn_scan",
    )(u, w, qg, kd, intra, eg, z, gnorm)


def _gdn_decode_kernel(bs, x_ref, cs_ref, z_ref, ab_ref, s_ref, cw_ref, alog_ref, dtb_ref, gn_ref, o_ref, so_ref):
    x = x_ref[...]
    y = x * cw_ref[CONV_WIDTH - 1:CONV_WIDTH, :]
    for j in range(CONV_WIDTH - 1):
        y = y + cs_ref[:, j, :] * cw_ref[j:j + 1, :]
    y = _silu(y)
    g_all, beta_all = _gate_rows(ab_ref[...], alog_ref, dtb_ref)
    eg_all = jnp.exp(g_all)
    z = z_ref[...]
    d = GDN_HEAD_DIM
    eye = (lax.broadcasted_iota(I32, (d, d), 0) == lax.broadcasted_iota(I32, (d, d), 1)).astype(F32)
    for h in range(GDN_HEADS):
        hs = slice(h * d, (h + 1) * d)
        qh = _l2(y[:, h * d:(h + 1) * d]) * (d ** -0.5)
        kh = _l2(y[:, GDN_W + h * d:GDN_W + (h + 1) * d])
        vh = y[:, 2 * GDN_W + h * d:2 * GDN_W + (h + 1) * d]
        kt = _hdot_nt(eye, kh)
        qt = _hdot_nt(eye, qh)
        outs = []
        for b in range(bs):
            s_b = s_ref[b, h]
            kc = kt[:, b:b + 1]
            eg = eg_all[b:b + 1, h:h + 1]
            beta = beta_all[b:b + 1, GDN_HEADS + h:GDN_HEADS + h + 1]
            sk = jnp.sum(s_b * kc, axis=0, keepdims=True)
            v_new = beta * (vh[b:b + 1] - eg * sk)
            s_new = eg * s_b + kc * v_new
            so_ref[b, h] = s_new
            outs.append(jnp.sum(s_new * qt[:, b:b + 1], axis=0, keepdims=True))
        o = jnp.concatenate(outs, axis=0)
        o_ref[:, hs] = _rms(o, gn_ref[...]) * _silu(z[:, hs])


def _gdn_decode(qkvg, z, ab, state, conv_state, conv_w, alog, dtb, gnorm, bs=8):
    b = qkvg.shape[0]
    row = lambda w: pl.BlockSpec((bs, w), lambda i: (i, 0))
    full = lambda *shape: pl.BlockSpec(shape, lambda i: (0,) * len(shape))
    sspec = pl.BlockSpec((bs, GDN_HEADS, GDN_HEAD_DIM, GDN_HEAD_DIM), lambda i: (i, 0, 0, 0))
    return pl.pallas_call(
        functools.partial(_gdn_decode_kernel, bs),
        out_shape=[jax.ShapeDtypeStruct((b, GDN_W), F32),
                   jax.ShapeDtypeStruct(state.shape, F32)],
        grid=(b // bs,),
        in_specs=[row(GDN_CONV_W),
                  pl.BlockSpec((bs, CONV_WIDTH - 1, GDN_CONV_W), lambda i: (i, 0, 0)),
                  row(GDN_W), row(LANES), sspec,
                  full(CONV_WIDTH, GDN_CONV_W), full(1, LANES), full(1, LANES), full(1, GDN_HEAD_DIM)],
        out_specs=[row(GDN_W), sspec],
        compiler_params=_cparams(("parallel",)),
        name="gdn_decode",
    )(qkvg, conv_state, z, ab, state, conv_w, alog, dtb, gnorm)


def _memkv_kernel(x_ref, g_ref, w_ref, kn_ref, k_ref, v_ref):
    hb = _rms(x_ref[...], g_ref[...]).astype(BF16)
    for hd in range(MEM_HEADS):
        hs = slice(hd * MEM_HEAD_DIM, (hd + 1) * MEM_HEAD_DIM)
        k_ref[:, hs] = _rms(jnp.dot(hb, w_ref[:, hs], preferred_element_type=F32), kn_ref[...])
    v_ref[...] = jnp.dot(hb, w_ref[:, BRANCH_W:], preferred_element_type=F32)


def _memkv(mem2, gain, w_kv, kn, tm=512):
    t = mem2.shape[0]
    return pl.pallas_call(
        _memkv_kernel,
        out_shape=[jax.ShapeDtypeStruct((t, BRANCH_W), F32)] * 2,
        grid=(t // tm,),
        in_specs=[pl.BlockSpec((tm, D_MODEL), lambda i: (i, 0)),
                  pl.BlockSpec((1, D_MODEL), lambda i: (0, 0)),
                  pl.BlockSpec((D_MODEL, 2 * BRANCH_W), lambda i: (0, 0)),
                  pl.BlockSpec((1, MEM_HEAD_DIM), lambda i: (0, 0))],
        out_specs=[pl.BlockSpec((tm, BRANCH_W), lambda i: (i, 0))] * 2,
        compiler_params=_cparams(("parallel",)),
        name="memkv",
    )(mem2, gain, w_kv, kn)


def _memattn_prompt_kernel(q_ref, k_ref, v_ref, qn_ref, o_ref):
    q = q_ref[0]
    k = k_ref[0]
    v = v_ref[0]
    scale = MEM_HEAD_DIM ** -0.5
    for hd in range(MEM_HEADS):
        hs = slice(hd * MEM_HEAD_DIM, (hd + 1) * MEM_HEAD_DIM)
        s = _bdot_nt(_rms(q[:, hs], qn_ref[...]), k[:, hs]) * scale
        p = jnp.exp(s - jnp.max(s, axis=-1, keepdims=True))
        o_ref[0, :, hs] = _bdot(p, v[:, hs]) / jnp.sum(p, axis=-1, keepdims=True)


def _memattn_prompt(qm, mk, mv, qn):
    b, l, _ = qm.shape
    tq = 512 if l % 512 == 0 else WINDOW
    return pl.pallas_call(
        _memattn_prompt_kernel,
        out_shape=jax.ShapeDtypeStruct((b, l, BRANCH_W), F32),
        grid=(b, l // tq),
        in_specs=[pl.BlockSpec((1, tq, BRANCH_W), lambda i, j: (i, j, 0)),
                  pl.BlockSpec((1, MEM_LEN, BRANCH_W), lambda i, j: (i, 0, 0)),
                  pl.BlockSpec((1, MEM_LEN, BRANCH_W), lambda i, j: (i, 0, 0)),
                  pl.BlockSpec((1, MEM_HEAD_DIM), lambda i, j: (0, 0))],
        out_specs=pl.BlockSpec((1, tq, BRANCH_W), lambda i, j: (i, j, 0)),
        compiler_params=_cparams(("parallel", "parallel")),
        name="memattn_prompt",
    )(qm, mk, mv, qn)


def _memattn_decode_kernel(bs, q_ref, k_ref, v_ref, qn_ref, o_ref):
    scale = MEM_HEAD_DIM ** -0.5
    for b in range(bs):
        qn = _rms(q_ref[b], qn_ref[...])
        s = jnp.sum(k_ref[b] * qn, axis=-1, keepdims=True) * scale
        p = jnp.exp(s - jnp.max(s, axis=0, keepdims=True))
        den = jnp.sum(p, axis=0)
        o_ref[b] = jnp.sum(p * v_ref[b], axis=0) / den


def _memattn_decode(qm, ck, cv, qn, bs=8):
    b = qm.shape[0]
    q3 = pl.BlockSpec((bs, MEM_HEADS, MEM_HEAD_DIM), lambda i: (i, 0, 0))
    kv = pl.BlockSpec((bs, MEM_LEN, MEM_HEADS, MEM_HEAD_DIM), lambda i: (i, 0, 0, 0))
    o = pl.pallas_call(
        functools.partial(_memattn_decode_kernel, bs),
        out_shape=jax.ShapeDtypeStruct((b, MEM_HEADS, MEM_HEAD_DIM), F32),
        grid=(b // bs,),
        in_specs=[q3, kv, kv, pl.BlockSpec((1, MEM_HEAD_DIM), lambda i: (0, 0))],
        out_specs=q3,
        compiler_params=_cparams(("parallel",)),
        name="memattn_decode",
    )(qm.reshape(b, MEM_HEADS, MEM_HEAD_DIM), ck, cv, qn)
    return o.reshape(b, BRANCH_W)


def _merge_kernel(oa_ref, ob_ref, oc_ref, gate_ref, x_ref, wb_ref, wo_ref, g2_ref, wr_ref, br_ref,
                  x1_ref, h2_ref, lg_ref):
    acc = None
    for i, o_ref in enumerate((oa_ref, ob_ref, oc_ref)):
        mixed = jnp.dot(o_ref[...].astype(BF16), wb_ref[i], preferred_element_type=F32)
        term = gate_ref[:, i * D_MODEL:(i + 1) * D_MODEL].astype(F32) * mixed
        acc = term if acc is None else acc + term
    x1 = x_ref[...] + jnp.dot(acc.astype(BF16), wo_ref[...], preferred_element_type=F32)
    x1_ref[...] = x1
    h2 = _rms(x1, g2_ref[...])
    h2_ref[...] = h2.astype(BF16)
    h_hi = h2.astype(BF16)
    h_lo = (h2 - h_hi.astype(F32)).astype(BF16)
    nt = (((1,), (1,)), ((), ()))
    a = lax.dot_general(wr_ref[...], h_hi, nt, preferred_element_type=F32)
    b = lax.dot_general(wr_ref[0:N_EXPERTS, :], h_lo, nt, preferred_element_type=F32)
    lg_ref[...] = a[0:N_EXPERTS] + a[N_EXPERTS:] + b + br_ref[...]


def _merge_into_kernel(oa_ref, ob_ref, oc_ref, gate_ref, x_ref, wb_ref, wo_ref, g2_ref, wr_ref, br_ref,
                       x1_in, h2_in, lg_in, x1_ref, h2_ref, lg_ref):
    del x1_in, h2_in, lg_in
    _merge_kernel(oa_ref, ob_ref, oc_ref, gate_ref, x_ref, wb_ref, wo_ref, g2_ref, wr_ref, br_ref,
                  x1_ref, h2_ref, lg_ref)


def _merge(oa, ob, oc, gate, x2, wb, wo, g2, wr_t, br_col, tm, t_out=None, into=None):
    t = x2.shape[0]
    t_out = t_out or t
    blk0 = 0 if into is None else (into[0].shape[0] - t) // tm
    row = lambda w: pl.BlockSpec((tm, w), lambda i: (i, 0))
    orow = lambda w: pl.BlockSpec((tm, w), lambda i: (i + blk0, 0))
    full = lambda *shape: pl.BlockSpec(shape, lambda i: (0,) * len(shape))
    in_specs = [row(BRANCH_W), row(BRANCH_W), row(BRANCH_W), row(N_BRANCH * D_MODEL), row(D_MODEL),
                full(N_BRANCH, BRANCH_W, D_MODEL), full(D_MODEL, D_MODEL), full(1, D_MODEL),
                full(2 * N_EXPERTS, D_MODEL), full(N_EXPERTS, 1)]
    args = [oa, ob, oc, gate, x2, wb, wo, g2, wr_t, br_col]
    aliases = {}
    if into is not None:
        t_out = into[0].shape[0]
        in_specs += [pl.BlockSpec(memory_space=pl.ANY)] * 3
        aliases = {len(args) + k: k for k in range(3)}
        args += list(into)
    return pl.pallas_call(
        _merge_kernel if into is None else _merge_into_kernel,
        out_shape=[jax.ShapeDtypeStruct((t_out, D_MODEL), F32),
                   jax.ShapeDtypeStruct((t_out, D_MODEL), BF16),
                   jax.ShapeDtypeStruct((N_EXPERTS, t_out), F32)],
        grid=(t // tm,),
        in_specs=in_specs,
        out_specs=[orow(D_MODEL), orow(D_MODEL), pl.BlockSpec((N_EXPERTS, tm), lambda i: (0, i + blk0))],
        input_output_aliases=aliases,
        compiler_params=_cparams(("parallel",)),
        name="merge",
    )(*args)


def _col_to_row(col):
    n = col.shape[0]
    r = lax.broadcasted_iota(I32, (n, n), 0)
    c = lax.broadcasted_iota(I32, (n, n), 1)
    return jnp.sum(jnp.where(r == c, col, 0.0), axis=0, keepdims=True)


def _row_to_col(row):
    n = row.shape[1]
    r = lax.broadcasted_iota(I32, (n, n), 0)
    c = lax.broadcasted_iota(I32, (n, n), 1)
    return jnp.sum(jnp.where(r == c, row, 0.0), axis=1, keepdims=True)


def _lane_pad(row):
    return jnp.concatenate([row, jnp.zeros((1, LANES - row.shape[1]), row.dtype)], axis=1)


def _route_kernel(lg_ref, pos_ref, gate_ref, meta_ref, cnt_ref, carry):
    i = pl.program_id(0)
    tn = lg_ref.shape[1]

    @pl.when(i == 0)
    def _():
        carry[...] = jnp.zeros_like(carry)

    l = lg_ref[...]
    eio = lax.broadcasted_iota(I32, (N_EXPERTS, tn), 0)
    hot = jnp.zeros((N_EXPERTS, tn), F32)
    vals, idxs = [], []
    for _ in range(TOP_K):
        m = jnp.max(l, axis=0, keepdims=True)
        idx = jnp.min(jnp.where(l == m, eio, N_EXPERTS), axis=0, keepdims=True)
        sel = eio == idx
        vals.append(m)
        idxs.append(idx)
        hot = hot + sel.astype(F32)
        l = jnp.where(sel, -jnp.inf, l)
    ex = [jnp.exp(v - vals[0]) for v in vals]
    tot = ex[0] + ex[1] + ex[2] + ex[3]
    before = (lax.broadcasted_iota(I32, (tn, tn), 0) < lax.broadcasted_iota(I32, (tn, tn), 1)).astype(BF16)
    within = jnp.dot(hot.astype(BF16), before, preferred_element_type=F32)
    cnt_col = jnp.sum(hot, axis=1, keepdims=True)
    room = jnp.floor((cnt_col + (MOE_UNIT - 1)) * (1.0 / MOE_UNIT)) * MOE_UNIT
    r = lax.broadcasted_iota(I32, (N_EXPERTS, N_EXPERTS), 0)
    c = lax.broadcasted_iota(I32, (N_EXPERTS, N_EXPERTS), 1)
    start_col = _row_to_col(jnp.sum(jnp.where(r < c, room, 0.0), axis=0, keepdims=True))
    for k in range(TOP_K):
        gate_ref[k:k + 1, :] = ex[k] / tot
        pos_ref[k:k + 1, :] = jnp.sum(jnp.where(eio == idxs[k], start_col + within, 0.0), axis=0,
                                      keepdims=True).astype(I32)
    meta_ref[0] = _lane_pad(jnp.concatenate([_col_to_row(room), _col_to_row(carry[:, 0:1])], axis=1)).astype(I32)
    carry[...] = carry[...] + room
    cnt_ref[...] = carry[...]


def _route(logits_t, tn):
    t = logits_t.shape[1]
    kt = pl.BlockSpec((TOP_K, tn), lambda i: (0, i))
    return pl.pallas_call(
        _route_kernel,
        out_shape=[jax.ShapeDtypeStruct((TOP_K, t), I32),
                   jax.ShapeDtypeStruct((TOP_K, t), F32),
                   jax.ShapeDtypeStruct((t // tn, 1, LANES), I32),
                   jax.ShapeDtypeStruct((N_EXPERTS, LANES), F32)],
        grid=(t // tn,),
        in_specs=[pl.BlockSpec((N_EXPERTS, tn), lambda i: (0, i))],
        out_specs=[kt, kt, pl.BlockSpec((1, 1, LANES), lambda i: (i, 0, 0)),
                   pl.BlockSpec((N_EXPERTS, LANES), lambda i: (0, 0))],
        scratch_shapes=[pltpu.VMEM((N_EXPERTS, LANES), F32)],
        compiler_params=_cparams(("arbitrary",)),
        name="moe_route",
    )(logits_t)


def _layout_kernel(bm, cnt_ref, be_ref, na_ref, ps_ref, pe_ref, ct_ref):
    nbp = be_ref.shape[1]
    cnt_col = cnt_ref[:, 0:1]
    size = jnp.floor((cnt_col + (bm - 1)) * (1.0 / bm)) * bm
    r = lax.broadcasted_iota(I32, (N_EXPERTS, N_EXPERTS), 0)
    c = lax.broadcasted_iota(I32, (N_EXPERTS, N_EXPERTS), 1)
    ends_row = jnp.sum(jnp.where(r <= c, size, 0.0), axis=0, keepdims=True)
    ends_col = _row_to_col(ends_row)
    nact = ends_row[:, N_EXPERTS - 1:N_EXPERTS] * (1.0 / bm)
    blk = jnp.minimum(lax.broadcasted_iota(I32, (1, nbp), 1).astype(F32), nact - 1.0)
    be = jnp.sum((ends_col <= blk * bm).astype(F32), axis=0, keepdims=True)
    be_ref[...] = jnp.minimum(be, N_EXPERTS - 1.0).astype(I32)
    na_ref[...] = jnp.broadcast_to(nact, na_ref.shape).astype(I32)
    ps_ref[...] = _lane_pad(ends_row - _col_to_row(size)).astype(I32)
    pe_ref[...] = _lane_pad(ends_row).astype(I32)
    ct_ref[...] = _lane_pad(_col_to_row(cnt_col)).astype(I32)


def _layout(cnt, nbp, bm):
    row = jax.ShapeDtypeStruct((1, LANES), I32)
    return pl.pallas_call(
        functools.partial(_layout_kernel, bm),
        out_shape=[jax.ShapeDtypeStruct((1, nbp), I32), row, row, row, row],
        name="moe_layout",
    )(cnt)


def _stage_rows(tn):
    return tn * TOP_K + N_EXPERTS * MOE_UNIT


def _run_copies(meta_ref, ps_ref, make_copy, tn):
    top = (tn // MOE_UNIT).bit_length() - 1
    src = jnp.int32(0)
    total = jnp.int32(0)
    for e in range(N_EXPERTS):
        units = lax.shift_right_logical(meta_ref[0, 0, e], 3)
        dst = ps_ref[e] + meta_ref[0, 0, N_EXPERTS + e]
        off = jnp.int32(0)
        for k in range(top, -1, -1):
            rows = MOE_UNIT << k
            take = lax.shift_right_logical(units, k) & 1

            @pl.when(take == 1)
            def _(src=src, dst=dst, off=off, rows=rows):
                make_copy(pl.multiple_of(src + off, MOE_UNIT), pl.multiple_of(dst + off, MOE_UNIT), rows).start()
            off = off + take * rows
        src = src + units * MOE_UNIT
        total = total + units
    return total


def _wait_copies(units, make_copy, max_rows):
    for k in range((max_rows // MOE_UNIT).bit_length()):
        if (MOE_UNIT << k) > max_rows:
            break

        @pl.when(lax.shift_right_logical(units, k) & 1 == 1)
        def _(k=k):
            make_copy(0, 0, MOE_UNIT << k).wait()


def _wait_units(count, make_copy):
    def body(u, carry):
        make_copy(0, 0).wait()
        return carry
    lax.fori_loop(0, count, body, 0)


def _dispatch_kernel(ps_ref, pe_ref, ct_ref, meta_ref, pos_ref, h_ref, xg_ref, stage, zrows, zblock, started, sem,
                     zsem):
    i = pl.program_id(0)
    n = pl.num_programs(0)
    slot = i % 2
    rows, tn = stage.shape[1], pos_ref.shape[1]

    def copy(s):
        return lambda src, dst, nrows: pltpu.make_async_copy(stage.at[s, pl.ds(src, nrows)],
                                                             xg_ref.at[pl.ds(dst, nrows)], sem.at[s])

    @pl.when(i >= 2)
    def _():
        _wait_copies(started[slot], copy(slot), rows)

    srow = lax.broadcasted_iota(I32, (rows, tn), 0)
    hit = srow == pos_ref[0:1, :]
    for k in range(1, TOP_K):
        hit = hit | (srow == pos_ref[k:k + 1, :])
    onehot = jnp.where(hit, 1.0, 0.0).astype(BF16)
    stage[slot] = jnp.dot(onehot, h_ref[...], preferred_element_type=F32)
    started[slot] = _run_copies(meta_ref, ps_ref, copy(slot), tn)

    @pl.when(i == n - 1)
    def _():
        _wait_copies(started[slot], copy(slot), rows)

        @pl.when(n >= 2)
        def _():
            _wait_copies(started[1 - slot], copy(1 - slot), rows)

        zrows[...] = jnp.zeros_like(zrows)
        zero = lambda src, dst: pltpu.make_async_copy(zrows, xg_ref.at[pl.ds(dst, MOE_UNIT)], zsem)
        total = jnp.int32(0)
        for e in range(N_EXPERTS):
            lo = ps_ref[e] + ct_ref[e]
            units = lax.shift_right_logical(pe_ref[e] - lo, 3)

            def body(u, carry, lo=lo):
                zero(0, pl.multiple_of(lo + u * MOE_UNIT, MOE_UNIT)).start()
                return carry
            lax.fori_loop(0, units, body, 0)
            total = total + units
        _wait_units(total, zero)

        zblock[...] = jnp.zeros_like(zblock)
        bm = zblock.shape[0]
        zero_block = lambda blk: pltpu.make_async_copy(zblock, xg_ref.at[pl.ds(pl.multiple_of(blk * bm, bm), bm)],
                                                       zsem)
        first = lax.shift_right_logical(pe_ref[N_EXPERTS - 1], bm.bit_length() - 1)
        last = xg_ref.shape[0] // bm

        def start_block(blk, carry):
            zero_block(blk).start()
            return carry
        lax.fori_loop(first, last, start_block, 0)

        def wait_block(blk, carry):
            zero_block(0).wait()
            return carry
        lax.fori_loop(first, last, wait_block, 0)


def _dispatch(ps, pe, ct, meta, pos, h2, n_slots, tn, bm):
    t = h2.shape[0]
    return pl.pallas_call(
        _dispatch_kernel,
        out_shape=jax.ShapeDtypeStruct((n_slots, D_MODEL), F32),
        grid_spec=pltpu.PrefetchScalarGridSpec(
            num_scalar_prefetch=3,
            grid=(t // tn,),
            in_specs=[pl.BlockSpec((1, 1, LANES), lambda i, *_: (i, 0, 0), memory_space=pltpu.SMEM),
                      pl.BlockSpec((TOP_K, tn), lambda i, *_: (0, i)),
                      pl.BlockSpec((tn, D_MODEL), lambda i, *_: (i, 0))],
            out_specs=pl.BlockSpec(memory_space=pl.ANY),
            scratch_shapes=[pltpu.VMEM((2, _stage_rows(tn), D_MODEL), F32),
                            pltpu.VMEM((MOE_UNIT, D_MODEL), F32),
                            pltpu.VMEM((bm, D_MODEL), F32),
                            pltpu.SMEM((2,), I32),
                            pltpu.SemaphoreType.DMA((2,)),
                            pltpu.SemaphoreType.DMA(())]),
        compiler_params=_cparams(("arbitrary",)),
        name="moe_dispatch",
    )(ps, pe, ct, meta, pos, h2)


def _expert_kernel(be_ref, na_ref, x_ref, w1_ref, b1_ref, w2_ref, b2_ref, y_ref, w1s, w2s):
    i = pl.program_id(0)
    e = be_ref[i]
    prev = be_ref[jnp.maximum(i - 1, 0)]

    @pl.when((i == 0) | (e != prev))
    def _():
        rows = 128
        for r0 in range(0, D_MODEL, rows):
            w1s[r0:r0 + rows, :] = w1_ref[r0:r0 + rows, :].astype(BF16)
        for r0 in range(0, D_FF, rows):
            w2s[r0:r0 + rows, :] = w2_ref[r0:r0 + rows, :].astype(BF16)

    @pl.when(i < na_ref[0])
    def _():
        hmid = jnp.dot(x_ref[...].astype(BF16), w1s[...], preferred_element_type=F32) + b1_ref[...]
        glu = jnp.minimum(hmid[:, :D_FF], SWIGLU_LIMIT)
        lin = jnp.clip(hmid[:, D_FF:], -SWIGLU_LIMIT, SWIGLU_LIMIT)
        act = glu * jax.nn.sigmoid(SWIGLU_ALPHA * glu) * (lin + 1.0)
        y_ref[...] = jnp.dot(act.astype(BF16), w2s[...], preferred_element_type=F32) + b2_ref[...]

    @pl.when(i >= na_ref[0])
    def _():
        y_ref[...] = jnp.zeros_like(y_ref)


def _experts(be, nact, xg, layer, w1, b1, w2, b2, bm):
    n_slots = xg.shape[0]
    nb = n_slots // bm
    blk = lambda i, be_r, na_r: (jnp.minimum(i, na_r[0] - 1), 0)
    wsel = lambda i, be_r, na_r: (layer, be_r[i], 0, 0)
    return pl.pallas_call(
        _expert_kernel,
        out_shape=jax.ShapeDtypeStruct((n_slots, D_MODEL), F32),
        grid_spec=pltpu.PrefetchScalarGridSpec(
            num_scalar_prefetch=2,
            grid=(nb,),
            in_specs=[pl.BlockSpec((bm, D_MODEL), blk),
                      pl.BlockSpec((None, None, D_MODEL, 2 * D_FF), wsel),
                      pl.BlockSpec((None, None, 1, 2 * D_FF), wsel),
                      pl.BlockSpec((None, None, D_FF, D_MODEL), wsel),
                      pl.BlockSpec((None, None, 1, D_MODEL), wsel)],
            out_specs=pl.BlockSpec((bm, D_MODEL), lambda i, be_r, na_r: (i, 0)),
            scratch_shapes=[pltpu.VMEM((D_MODEL, 2 * D_FF), BF16), pltpu.VMEM((D_FF, D_MODEL), BF16)]),
        compiler_params=_cparams(("arbitrary",)),
        name="moe_experts",
    )(be, nact, xg, w1, b1, w2, b2)


def _combine_kernel(n_first, ps_ref, mcur_ref, mnext_ref, pos_ref, gate_ref, x1_ref, yg_ref, o_ref, o2_ref, stage,
                    started, sem):
    i = pl.program_id(0)
    n = pl.num_programs(0)
    slot = i % 2
    rows, tn = stage.shape[1], pos_ref.shape[1]

    @pl.when(i == 0)
    def _():
        stage[...] = jnp.zeros_like(stage)

    def copy(s):
        return lambda src, dst, nrows: pltpu.make_async_copy(yg_ref.at[pl.ds(dst, nrows)],
                                                             stage.at[s, pl.ds(src, nrows)], sem.at[s])

    @pl.when(i == 0)
    def _():
        started[0] = _run_copies(mcur_ref, ps_ref, copy(0), tn)

    @pl.when(i + 1 < n)
    def _():
        started[1 - slot] = _run_copies(mnext_ref, ps_ref, copy(1 - slot), tn)

    _wait_copies(started[slot], copy(slot), rows)

    eye = (lax.broadcasted_iota(I32, (tn, tn), 0) == lax.broadcasted_iota(I32, (tn, tn), 1)).astype(F32)
    cols = _hdot_nt(eye, jnp.concatenate([pos_ref[...].astype(F32), gate_ref[...]], axis=0))
    lane = lax.broadcasted_iota(I32, (tn, rows), 1)
    weight = jnp.zeros((tn, rows), F32)
    for k in range(TOP_K):
        weight = weight + jnp.where(lane == cols[:, k:k + 1].astype(I32), cols[:, TOP_K + k:TOP_K + k + 1], 0.0)
    w_hi = weight.astype(BF16)
    w_lo = (weight - w_hi.astype(F32)).astype(BF16)
    y = stage[slot].astype(BF16)
    out = x1_ref[...] + jnp.dot(w_hi, y, preferred_element_type=F32) + jnp.dot(w_lo, y, preferred_element_type=F32)

    @pl.when(i < n_first)
    def _():
        o_ref[...] = out

    @pl.when(i >= n_first)
    def _():
        o2_ref[...] = out


def _combine(ps, meta, pos, gates, x1, yg, tn, t_first):
    t = x1.shape[0]
    nt = t // tn
    n_first = t_first // tn
    return pl.pallas_call(
        functools.partial(_combine_kernel, n_first),
        out_shape=[jax.ShapeDtypeStruct((t_first, D_MODEL), F32),
                   jax.ShapeDtypeStruct((t - t_first, D_MODEL), F32)],
        grid_spec=pltpu.PrefetchScalarGridSpec(
            num_scalar_prefetch=1,
            grid=(nt,),
            in_specs=[pl.BlockSpec((1, 1, LANES), lambda i, *_: (i, 0, 0), memory_space=pltpu.SMEM),
                      pl.BlockSpec((1, 1, LANES), lambda i, *_: (jnp.minimum(i + 1, nt - 1), 0, 0),
                                   memory_space=pltpu.SMEM),
                      pl.BlockSpec((TOP_K, tn), lambda i, *_: (0, i)),
                      pl.BlockSpec((TOP_K, tn), lambda i, *_: (0, i)),
                      pl.BlockSpec((tn, D_MODEL), lambda i, *_: (i, 0)),
                      pl.BlockSpec(memory_space=pl.ANY)],
            out_specs=[pl.BlockSpec((tn, D_MODEL), lambda i, *_: (jnp.minimum(i, n_first - 1), 0)),
                       pl.BlockSpec((tn, D_MODEL), lambda i, *_: (jnp.maximum(i - n_first, 0), 0))],
            scratch_shapes=[pltpu.VMEM((2, _stage_rows(tn), D_MODEL), F32),
                            pltpu.SMEM((2,), I32),
                            pltpu.SemaphoreType.DMA((2,))]),
        compiler_params=_cparams(("arbitrary",)),
        name="moe_combine",
    )(ps, meta, meta, pos, gates, x1, yg)


def _moe(x1, h2, logits_t, layer, w1, b1, w2, b2, t_first):
    t = x1.shape[0]
    tn = MOE_TOK_TILE
    bm = MOE_BM if t * TOP_K >= N_EXPERTS * MOE_BM else LANES
    n_blocks = (t * TOP_K + (t // tn) * N_EXPERTS * (MOE_UNIT - 1)) // bm + N_EXPERTS + 1
    nbp = -(-n_blocks // LANES) * LANES
    pos, gates, meta, cnt = _route(logits_t, tn)
    be, nact, ps, pe, ct = _layout(cnt, nbp, bm)
    ps, pe, ct = ps[0, :N_EXPERTS], pe[0, :N_EXPERTS], ct[0, :N_EXPERTS]
    xg = _dispatch(ps, pe, ct, meta, pos, h2, n_blocks * bm, tn, bm)
    yg = _experts(be[0, :n_blocks], nact[0, :1], xg, layer, w1, b1, w2, b2, bm)
    return _combine(ps, meta, pos, gates, x1, yg, tn, t_first)


def _pack_w_in(w):
    ab = jnp.pad(w[:, _IN_MAIN:_IN_MAIN + _IN_AB], ((0, 0), (0, LANES - _IN_AB)))
    return jnp.concatenate([w[:, :_IN_MAIN], ab, w[:, _IN_MAIN + _IN_AB:]], axis=1).astype(BF16)


def _lane_row(v):
    return jnp.pad(v.astype(F32), (0, LANES - v.shape[0])).reshape(1, LANES)


def _layer_weights(ln1_gain, w_in, q_norm_swa, k_norm_swa, swa_sinks, conv_w, a_log, dt_bias, gdn_norm,
                   q_norm_mem, w_branch, w_out, ln2_gain, w_router, b_router, w_mlp1, b_mlp1, w_mlp2, b_mlp2, layer):
    depth = w_mlp1.shape[0]
    wr_hi = w_router.T.astype(BF16)
    wr_lo = (w_router.T - wr_hi.astype(F32)).astype(BF16)
    return dict(
        layer=layer, wr_t=jnp.concatenate([wr_hi, wr_lo], axis=0), b1r=b_mlp1.reshape(depth, N_EXPERTS, 1, -1), b2r=b_mlp2.reshape(depth, N_EXPERTS, 1, -1),
        ln1=ln1_gain.reshape(1, -1), wp=_pack_w_in(w_in),
        qn_s=q_norm_swa.reshape(1, -1), kn_s=k_norm_swa.reshape(1, -1), sinks=swa_sinks.reshape(1, -1),
        conv_w=conv_w, alog=_lane_row(a_log), dtb=_lane_row(dt_bias), gnorm=gdn_norm.reshape(1, -1),
        qn_m=q_norm_mem.reshape(1, -1), wb=w_branch.astype(BF16), wo=w_out.astype(BF16),
        ln2=ln2_gain.reshape(1, -1), br=b_router.reshape(-1, 1),
        w1=w_mlp1, w2=w_mlp2)


def _finish(lw, first, second):
    t1, t2 = first[4].shape[0], second[4].shape[0]
    tm1 = MERGE_ROWS if t1 % MERGE_ROWS == 0 else MOE_TOK_TILE
    pad = (-t2) % MOE_TOK_TILE
    second = [jnp.pad(a, ((0, pad), (0, 0))) for a in second]
    wts = (lw["wb"], lw["wo"], lw["ln2"], lw["wr_t"], lw["br"])
    bufs = _merge(*first, *wts, tm1, t_out=t1 + t2 + pad)
    bufs = _merge(*second, *wts, MOE_TOK_TILE, into=bufs)
    y1, y2 = _moe(*bufs, lw["layer"], lw["w1"], lw["b1r"], lw["w2"], lw["b2r"], t1)
    return y1, y2[:t2]


def _prompt_layer(x, mem, lw, mem_norm, w_mem_kv, k_norm_mem):
    b, l, d = x.shape
    x2 = x.reshape(b * l, d)
    tm = 256
    tm_in = 2 * tm if l % (2 * tm) == 0 else tm
    qs, ks, vs, qkvg, z, ab, qm, gate, conv_tail = _inproj(x2, lw["ln1"], lw["wp"], lw["conv_w"], tm_in, seq_len=l)
    mk, mv = _memkv(mem.reshape(b * MEM_LEN, d), mem_norm.reshape(1, -1), w_mem_kv.astype(BF16),
                    k_norm_mem.reshape(1, -1))
    r3 = lambda a: a.reshape(b, l, a.shape[-1])
    o_swa, kwin = _swa_prompt(r3(qs), r3(ks), r3(vs), lw["qn_s"], lw["kn_s"], lw["sinks"])
    o_g, s_fin = _gdn_prompt(r3(qkvg), r3(z), r3(ab), lw["alog"], lw["dtb"], lw["gnorm"])
    o_m = _memattn_prompt(r3(qm), mk.reshape(b, MEM_LEN, BRANCH_W), mv.reshape(b, MEM_LEN, BRANCH_W), lw["qn_m"])
    parts = (o_swa.reshape(b * l, -1), o_g.reshape(b * l, -1), o_m.reshape(b * l, -1), gate, x2)
    new_k = kwin.reshape(b, WINDOW, SWA_KV_HEADS, SWA_HEAD_DIM)
    new_v = r3(vs)[:, l - WINDOW:].reshape(b, WINDOW, SWA_KV_HEADS, SWA_HEAD_DIM)
    new_conv = conv_tail[:, _HALO - (CONV_WIDTH - 1):]
    mk4 = mk.reshape(b, MEM_LEN, MEM_HEADS, MEM_HEAD_DIM)
    mv4 = mv.reshape(b, MEM_LEN, MEM_HEADS, MEM_HEAD_DIM)
    return parts, (new_k, new_v, s_fin, new_conv, mk4, mv4)


def _sample_layer(x, cache_k, cache_v, state, conv_state, mem_k, mem_v, lw):
    b, l, d = x.shape
    x2 = x.reshape(b, d)
    tm = TOK_TILE
    qs, ks, vs, qkvg, z, ab, qm, gate = _inproj(x2, lw["ln1"], lw["wp"], lw["conv_w"], tm)
    o_swa, new_k, new_v = _swa_decode(qs, ks, vs, cache_k, cache_v, lw["qn_s"], lw["kn_s"], lw["sinks"])
    o_g, new_s = _gdn_decode(qkvg, z, ab, state, conv_state, lw["conv_w"], lw["alog"], lw["dtb"], lw["gnorm"])
    o_m = _memattn_decode(qm, mem_k, mem_v, lw["qn_m"])
    new_conv = jnp.concatenate([conv_state[:, 1:], qkvg[:, None, :]], axis=1)
    shp = (b, WINDOW, SWA_KV_HEADS, SWA_HEAD_DIM)
    return (o_swa, o_g, o_m, gate, x2), (new_k.reshape(shp), new_v.reshape(shp), new_s, new_conv)


def _layer(lw, xp, mem, mem_norm, w_mem_kv, k_norm_mem, xs, cache_k, cache_v, state, conv_state, mem_k, mem_v):
    p_parts, p_state = _prompt_layer(xp, mem, lw, mem_norm, w_mem_kv, k_norm_mem)
    s_parts, s_state = _sample_layer(xs, cache_k, cache_v, state, conv_state, mem_k, mem_v, lw)
    yp, ys = _finish(lw, p_parts, s_parts)
    return yp.reshape(xp.shape), ys.reshape(xs.shape), p_state, s_state


def kernel(x_prompt, x_sample, cache_swa_k, cache_swa_v, state_gdn, state_conv, cache_mem_k, cache_mem_v, mem_prompt, ln1_gain, w_in, q_norm_swa, k_norm_swa, swa_sinks, conv_w, a_log, dt_bias, gdn_norm, q_norm_mem, k_norm_mem, mem_norm, w_mem_kv, w_branch, w_out, ln2_gain, w_router, b_router, w_mlp1, b_mlp1, w_mlp2, b_mlp2):
    xp, xs = x_prompt, x_sample
    outs = [[] for _ in range(10)]
    for l in range(ln1_gain.shape[0]):
        lw = _layer_weights(ln1_gain[l], w_in[l], q_norm_swa[l], k_norm_swa[l], swa_sinks[l], conv_w[l], a_log[l],
                            dt_bias[l], gdn_norm[l], q_norm_mem[l], w_branch[l], w_out[l], ln2_gain[l], w_router[l],
                            b_router[l], w_mlp1, b_mlp1, w_mlp2, b_mlp2, l)
        xp, xs, (pk, pv, ps, pc, mk, mv), (sk, sv, ss, sc) = _layer(
            lw, xp, mem_prompt, mem_norm[l], w_mem_kv[l], k_norm_mem[l],
            xs, cache_swa_k[l], cache_swa_v[l], state_gdn[l], state_conv[l], cache_mem_k[l], cache_mem_v[l])
        for acc, v in zip(outs, (pk, pv, sk, sv, ps, ss, pc, sc, mk, mv)):
            acc.append(v)
    return (xp, xs) + tuple(jnp.stack(o) for o in outs)
```

```python
import functools

import numpy as np
import jax
import jax.numpy as jnp
from jax import lax
from jax.experimental import pallas as pl
from jax.experimental.pallas import tpu as pltpu

F32 = jnp.float32
BF16 = jnp.bfloat16
I32 = jnp.int32

D_MODEL = 1024
BRANCH_W = 512
SWA_HEADS = 8
SWA_KV_HEADS = 2
SWA_HEAD_DIM = 64
SWA_GROUP = SWA_HEADS // SWA_KV_HEADS
SWA_KV_W = SWA_KV_HEADS * SWA_HEAD_DIM
WINDOW = 128
GDN_HEADS = 4
GDN_HEAD_DIM = 128
GDN_W = GDN_HEADS * GDN_HEAD_DIM
GDN_CONV_W = 3 * GDN_W
GDN_CHUNK = 64
CONV_WIDTH = 4
MEM_LEN = 256
MEM_HEADS = 4
MEM_HEAD_DIM = 128
N_BRANCH = 3
N_EXPERTS = 32
TOP_K = 4
D_FF = 1024
SWIGLU_LIMIT = 7.0
SWIGLU_ALPHA = 1.702
EPS = 1e-6
PAST_LEN = 16384

LANES = 128
MOE_BM = 512
MOE_TOK_TILE = 256
MOE_UNIT = 8
TOK_TILE = 128
MERGE_ROWS = 512
VMEM_LIMIT = 56 * 1024 * 1024

_SEG_QS = (0, 512)
_SEG_KS = (512, 128)
_SEG_VS = (640, 128)
_SEG_QKVG = (768, 1536)
_SEG_Z = (2304, 512)
_SEG_AB = (2816, 128)
_SEG_QM = (2944, 512)
_SEG_GATE = (3456, 3072)
_PACKED_W = 6528
_IN_MAIN = 2816
_IN_AB = 8

_NEG = -1e30
_HI = lax.Precision.HIGHEST


def _cparams(sem, vmem=VMEM_LIMIT):
    return pltpu.CompilerParams(dimension_semantics=sem, vmem_limit_bytes=vmem)


def _bdot(a, b):
    return jnp.dot(a.astype(BF16), b.astype(BF16), preferred_element_type=F32)


def _bdot_nt(a, b):
    return lax.dot_general(a.astype(BF16), b.astype(BF16), (((1,), (1,)), ((), ())),
                           preferred_element_type=F32)


def _bdot_tn(a, b):
    return lax.dot_general(a.astype(BF16), b.astype(BF16), (((0,), (0,)), ((), ())),
                           preferred_element_type=F32)


def _hdot(a, b):
    return jnp.dot(a, b, preferred_element_type=F32, precision=_HI)


def _hdot_nt(a, b):
    return lax.dot_general(a, b, (((1,), (1,)), ((), ())), preferred_element_type=F32, precision=_HI)


def _rms(x, gain):
    return x * lax.rsqrt(jnp.mean(x * x, axis=-1, keepdims=True) + EPS) * gain


def _l2(x):
    return x * lax.rsqrt(jnp.sum(x * x, axis=-1, keepdims=True) + EPS)


def _sigmoid(x):
    return 0.5 * jnp.tanh(0.5 * x) + 0.5


def _silu(x):
    return x * _sigmoid(x)


def _softplus(x):
    return jnp.maximum(x, 0.0) + jnp.log1p(jnp.exp(-jnp.abs(x)))


_HALO = 8


def _inproj_kernel(seq_steps, x_ref, g_ref, w_ref, cw_ref, qs, ks, vs, qkvg, z, ab, qm, gate, *conv_refs):
    i = pl.program_id(0)
    x = x_ref[...]
    hb = _rms(x, g_ref[...]).astype(BF16)

    def proj(off, width):
        return jnp.dot(hb, w_ref[:, off:off + width], preferred_element_type=F32)

    step = 512

    def plain(ref, c0=0, width=None):
        def finish(r):
            ref[:, c0:c0 + (width or r.shape[1])] = r
        return finish

    def gate_chunk(c0):
        def finish(r):
            gate[:, c0:c0 + step] = _sigmoid(r).astype(BF16)
        return finish

    def conv_part(part):
        tail_ref, carry = conv_refs
        c0 = part * GDN_W
        cols = slice(c0, c0 + GDN_W)

        def finish(r):
            first = (i % seq_steps) == 0
            sub = lax.broadcasted_iota(I32, (_HALO, 1), 0)
            rows = r.shape[0]
            before = jnp.where(first, 0.0, carry[:, cols])
            y = r * cw_ref[CONV_WIDTH - 1:CONV_WIDTH, cols]
            top = r[0:_HALO] * cw_ref[CONV_WIDTH - 1:CONV_WIDTH, cols]
            for jj in range(CONV_WIDTH - 1):
                sh = CONV_WIDTH - 1 - jj
                y = y + pltpu.roll(r, sh, 0) * cw_ref[jj:jj + 1, cols]
                head_rows = jnp.where(sub < sh, pltpu.roll(before, sh, 0), pltpu.roll(r[0:_HALO], sh, 0))
                top = top + head_rows * cw_ref[jj:jj + 1, cols]
            carry[:, cols] = r[rows - _HALO:]
            tail_ref[0, :, cols] = r[rows - _HALO:]
            act = _silu(jnp.concatenate([top, y[_HALO:]], axis=0))
            if part < 2:
                scale = GDN_HEAD_DIM ** -0.5 if part == 0 else 1.0
                for h in range(GDN_HEADS):
                    hs = slice(h * GDN_HEAD_DIM, (h + 1) * GDN_HEAD_DIM)
                    qkvg[:, c0 + h * GDN_HEAD_DIM:c0 + (h + 1) * GDN_HEAD_DIM] = _l2(act[:, hs]) * scale
            else:
                qkvg[:, cols] = act
        return finish

    gates = [(_SEG_GATE[0] + c0, step, gate_chunk(c0)) for c0 in range(0, _SEG_GATE[1], step)]
    light = [(_SEG_QS[0], _SEG_QS[1], plain(qs)), (_SEG_KS[0], _SEG_KS[1], plain(ks)),
             (_SEG_VS[0], _SEG_VS[1], plain(vs)), (_SEG_AB[0], _SEG_AB[1], plain(ab)),
             (_SEG_QM[0], _SEG_QM[1], plain(qm))]
    if seq_steps is None:
        heavy = gates
        light += [(_SEG_Z[0], _SEG_Z[1], plain(z))]
        light += [(_SEG_QKVG[0] + c0, step, plain(qkvg, c0, step)) for c0 in range(0, GDN_CONV_W, step)]
    else:
        def silu_z(r):
            z[...] = _silu(r)
        heavy = gates + [(_SEG_QKVG[0] + p * GDN_W, GDN_W, conv_part(p)) for p in range(3)]
        heavy.append((_SEG_Z[0], _SEG_Z[1], silu_z))
    tasks = []
    while heavy or light:
        if heavy:
            tasks.append(heavy.pop(0))
        if light:
            tasks.append(light.pop(0))
    pending = None
    for off, width, finish in tasks:
        r = proj(off, width)
        if pending is not None:
            pending[1](pending[0])
        pending = (r, finish)
    pending[1](pending[0])


def _inproj(x2, gain, wp, conv_w, tm, seq_len=None):
    t = x2.shape[0]
    segs = (_SEG_QS, _SEG_KS, _SEG_VS, _SEG_QKVG, _SEG_Z, _SEG_AB, _SEG_QM, _SEG_GATE)
    dtypes = (F32,) * 7 + (BF16,)
    out_shape = [jax.ShapeDtypeStruct((t, w), dt) for (_, w), dt in zip(segs, dtypes)]
    out_specs = [pl.BlockSpec((tm, w), lambda i: (i, 0)) for _, w in segs]
    scratch = []
    seq_steps = None
    if seq_len is not None:
        seq_steps = seq_len // tm
        out_shape.append(jax.ShapeDtypeStruct((t // seq_len, _HALO, GDN_CONV_W), F32))
        out_specs.append(pl.BlockSpec((1, _HALO, GDN_CONV_W), lambda i: (i // seq_steps, 0, 0)))
        scratch.append(pltpu.VMEM((_HALO, GDN_CONV_W), F32))
    return pl.pallas_call(
        functools.partial(_inproj_kernel, seq_steps),
        out_shape=out_shape,
        grid=(t // tm,),
        in_specs=[pl.BlockSpec((tm, D_MODEL), lambda i: (i, 0)),
                  pl.BlockSpec((1, D_MODEL), lambda i: (0, 0)),
                  pl.BlockSpec((D_MODEL, _PACKED_W), lambda i: (0, 0), pipeline_mode=pl.Buffered(1)),
                  pl.BlockSpec((CONV_WIDTH, GDN_CONV_W), lambda i: (0, 0))],
        out_specs=out_specs,
        scratch_shapes=scratch,
        compiler_params=_cparams(("arbitrary",)),
        name="inproj",
    )(x2, gain, wp, conv_w)


SWA_STEP_BLOCKS = 4


def _alibi_slopes(n):
    return [float(2.0 ** (-8.0 * (i + 1) / n)) for i in range(n)]


def _swa_prompt_kernel(q_ref, kc_ref, vc_ref, qg_ref, kg_ref, sink_ref, o_ref, kwin_ref, bias, kprev, vprev):
    n = pl.program_id(1)
    half = SWA_HEAD_DIM
    nblk = q_ref.shape[1] // WINDOW
    work = [(j, h) for j in range(nblk) for h in range(SWA_HEADS)]

    @pl.when(n == 0)
    def _():
        kprev[...] = jnp.zeros_like(kprev)
        vprev[...] = jnp.zeros_like(vprev)

    def fill(tbl, first):
        row = lax.broadcasted_iota(I32, (WINDOW, 2 * WINDOW), 0)
        col = lax.broadcasted_iota(I32, (WINDOW, 2 * WINDOW), 1)
        dist = row + WINDOW - col
        valid = (dist >= 0) & (dist <= WINDOW) & ((col >= WINDOW) | jnp.logical_not(first))
        distf = dist.astype(F32)
        for h, slope in enumerate(_alibi_slopes(SWA_HEADS)):
            bias[tbl, h] = jnp.where(valid, -slope * distf, _NEG)

    @pl.when(n <= 1)
    def _():
        fill(0, n == 0)

    if nblk > 1:
        @pl.when(n == 0)
        def _():
            fill(1, False)

    low = lax.broadcasted_iota(I32, (1, 2 * half), 1) < half

    def pair_rms(x, gain):
        sq = x * x
        s_lo = jnp.sum(jnp.where(low, sq, 0.0), axis=-1, keepdims=True)
        s_hi = jnp.sum(jnp.where(low, 0.0, sq), axis=-1, keepdims=True)
        return x * lax.rsqrt(jnp.where(low, s_lo, s_hi) * (1.0 / half) + EPS) * gain

    kcn = pair_rms(kc_ref[0], kg_ref[...])
    kwin_ref[0] = kcn[(nblk - 1) * WINDOW:]
    kall = jnp.concatenate([kprev[...], kcn], axis=0)
    vall = jnp.concatenate([vprev[...], vc_ref[0]], axis=0)
    kprev[...] = kcn[(nblk - 1) * WINDOW:]
    vprev[...] = vc_ref[0, (nblk - 1) * WINDOW:, :]
    ones = jnp.ones(((nblk + 1) * WINDOW, 2 * half), BF16)

    def placed(x, aug):
        sw = pltpu.roll(x, half, 1)
        out = {(0, 0): jnp.where(low, x, 0.0), (1, 1): jnp.where(low, 0.0, x),
               (0, 1): jnp.where(low, 0.0, sw), (1, 0): jnp.where(low, sw, 0.0)}
        out = {key: val.astype(BF16) for key, val in out.items()}
        return {key: jnp.concatenate([val, ones], axis=1) for key, val in out.items()} if aug else out

    kvar = placed(kall, False)
    vvar = placed(vall, True)
    qn = [pair_rms(q_ref[0, :, t * 2 * half:(t + 1) * 2 * half], qg_ref[...]).astype(BF16)
          for t in range(SWA_HEADS // 2)]
    where_of = lambda h: (h // SWA_GROUP, h % 2)
    qrows = lambda j: slice(j * WINDOW, (j + 1) * WINDOW)
    krows = lambda j: slice(j * WINDOW, (j + 2) * WINDOW)
    nt = (((1,), (1,)), ((), ()))
    s = {(j, h): lax.dot_general(qn[h // 2][qrows(j)], kvar[where_of(h)][krows(j)], nt, preferred_element_type=F32)
         + bias[min(j, 1), h] for j, h in work}
    sink = [sink_ref[0:1, h:h + 1] for h in range(SWA_HEADS)]
    m = {(j, h): jnp.maximum(jnp.max(s[j, h], axis=-1, keepdims=True), sink[h]) for j, h in work}
    p = {(j, h): jnp.exp(s[j, h] - m[j, h]).astype(BF16) for j, h in work}
    res = {(j, h): jnp.dot(p[j, h], vvar[where_of(h)][krows(j)], preferred_element_type=F32)
           for j, h in work}
    inv = {(j, h): 1.0 / (res[j, h][:, 2 * half:2 * half + 1] + jnp.exp(sink[h] - m[j, h])) for j, h in work}
    for j in range(nblk):
        for t in range(SWA_HEADS // 2):
            h0, h1 = 2 * t, 2 * t + 1
            num = res[j, h0][:, :2 * half] + res[j, h1][:, :2 * half]
            o_ref[0, qrows(j), t * 2 * half:(t + 1) * 2 * half] = num * jnp.where(low, inv[j, h0], inv[j, h1])


def _swa_prompt(qs, ks, vs, qn, kn, sinks):
    b, l, _ = qs.shape
    rows = SWA_STEP_BLOCKS * WINDOW if l % (SWA_STEP_BLOCKS * WINDOW) == 0 else WINDOW
    cur = lambda i, j: (i, j, 0)
    const2 = lambda i, j: (0, 0)
    q_gain = jnp.tile(qn, (1, 2)) * (SWA_HEAD_DIM ** -0.5)
    k_gain = jnp.tile(kn, (1, 2))
    return pl.pallas_call(
        _swa_prompt_kernel,
        out_shape=[jax.ShapeDtypeStruct((b, l, BRANCH_W), F32),
                   jax.ShapeDtypeStruct((b, WINDOW, SWA_KV_W), F32)],
        grid=(b, l // rows),
        in_specs=[pl.BlockSpec((1, rows, BRANCH_W), cur),
                  pl.BlockSpec((1, rows, SWA_KV_W), cur),
                  pl.BlockSpec((1, rows, SWA_KV_W), cur),
                  pl.BlockSpec((1, 2 * SWA_HEAD_DIM), const2),
                  pl.BlockSpec((1, 2 * SWA_HEAD_DIM), const2),
                  pl.BlockSpec((1, SWA_HEADS), const2)],
        out_specs=[pl.BlockSpec((1, rows, BRANCH_W), cur),
                   pl.BlockSpec((1, WINDOW, SWA_KV_W), lambda i, j: (i, 0, 0))],
        scratch_shapes=[pltpu.VMEM((2, SWA_HEADS, WINDOW, 2 * WINDOW), F32),
                        pltpu.VMEM((WINDOW, SWA_KV_W), F32),
                        pltpu.VMEM((WINDOW, SWA_KV_W), F32)],
        compiler_params=_cparams(("arbitrary", "arbitrary")),
        name="swa_prompt",
    )(qs, ks, vs, q_gain, k_gain, sinks)


def _swa_decode_kernel(bs, q_ref, k3_ref, kf_ref, v3_ref, vf_ref, ck_ref, cv_ref, qn_ref, kn_ref, kn2_ref,
                       sink_ref, slope_ref, o_ref, ok_ref, ov_ref):
    scale = SWA_HEAD_DIM ** -0.5
    lane = lax.broadcasted_iota(I32, (1, SWA_KV_W), 1)
    rowi = lax.broadcasted_iota(I32, (WINDOW, SWA_KV_W), 0)
    keyd = (WINDOW - lax.broadcasted_iota(I32, (1, WINDOW), 1)).astype(F32)
    last = rowi == WINDOW - 1
    for b in range(bs):
        kf = kf_ref[b]
        sq = kf * kf
        ms0 = jnp.sum(jnp.where(lane < SWA_HEAD_DIM, sq, 0.0), axis=-1, keepdims=True) / SWA_HEAD_DIM
        ms1 = jnp.sum(jnp.where(lane >= SWA_HEAD_DIM, sq, 0.0), axis=-1, keepdims=True) / SWA_HEAD_DIM
        knf = kf * lax.rsqrt(jnp.where(lane < SWA_HEAD_DIM, ms0, ms1) + EPS) * kn2_ref[...]
        ok_ref[b] = jnp.where(last, knf, pltpu.roll(ck_ref[b], WINDOW - 1, 0))
        ov_ref[b] = jnp.where(last, vf_ref[b], pltpu.roll(cv_ref[b], WINDOW - 1, 0))
    work = [(b, g) for b in range(bs) for g in range(SWA_KV_HEADS)]
    lanes = lambda g: slice(g * SWA_HEAD_DIM, (g + 1) * SWA_HEAD_DIM)
    heads = lambda g: slice(g * SWA_GROUP, (g + 1) * SWA_GROUP)
    qn = [_rms(q_ref[b], qn_ref[...]) for b in range(bs)]
    kn3 = [_rms(k3_ref[b], kn_ref[...]) for b in range(bs)]
    qg = {(b, g): qn[b][heads(g)] for b, g in work}
    s = {(b, g): _bdot_nt(qg[b, g], ck_ref[b, :, lanes(g)]) * scale - slope_ref[heads(g)] * keyd for b, g in work}
    s_new = {(b, g): jnp.sum(qg[b, g] * kn3[b][g:g + 1], axis=-1, keepdims=True) * scale for b, g in work}
    m = {(b, g): jnp.maximum(jnp.maximum(jnp.max(s[b, g], axis=-1, keepdims=True), s_new[b, g]),
                             sink_ref[heads(g)]) for b, g in work}
    p = {w: jnp.exp(s[w] - m[w]) for w in work}
    p_new = {w: jnp.exp(s_new[w] - m[w]) for w in work}
    denom = {(b, g): jnp.sum(p[b, g], axis=-1, keepdims=True) + p_new[b, g] + jnp.exp(sink_ref[heads(g)] - m[b, g])
             for b, g in work}
    pv = {(b, g): _bdot(p[b, g], cv_ref[b, :, lanes(g)]) for b, g in work}
    for b, g in work:
        o_ref[b, heads(g), :] = (pv[b, g] + p_new[b, g] * v3_ref[b, g:g + 1, :]) / denom[b, g]


def _swa_decode(qs, ks, vs, cache_k, cache_v, qn, kn, sinks, bs=8):
    b = qs.shape[0]
    q3 = qs.reshape(b, SWA_HEADS, SWA_HEAD_DIM)
    k3 = ks.reshape(b, SWA_KV_HEADS, SWA_HEAD_DIM)
    kf = ks.reshape(b, 1, SWA_KV_W)
    v3 = vs.reshape(b, SWA_KV_HEADS, SWA_HEAD_DIM)
    vf = vs.reshape(b, 1, SWA_KV_W)
    ck = cache_k.reshape(b, WINDOW, SWA_KV_W)
    cv = cache_v.reshape(b, WINDOW, SWA_KV_W)
    kn2 = jnp.concatenate([kn, kn], axis=-1)
    sink_col = sinks.reshape(SWA_HEADS, 1)
    slope_col = jnp.asarray(np.asarray(_alibi_slopes(SWA_HEADS), np.float32).reshape(SWA_HEADS, 1))
    blk = lambda *shape: pl.BlockSpec((bs,) + shape, lambda i: (i,) + (0,) * len(shape))
    full = lambda *shape: pl.BlockSpec(shape, lambda i: (0,) * len(shape))
    o, ok, ov = pl.pallas_call(
        functools.partial(_swa_decode_kernel, bs),
        out_shape=[jax.ShapeDtypeStruct((b, SWA_HEADS, SWA_HEAD_DIM), F32),
                   jax.ShapeDtypeStruct((b, WINDOW, SWA_KV_W), F32),
                   jax.ShapeDtypeStruct((b, WINDOW, SWA_KV_W), F32)],
        grid=(b // bs,),
        in_specs=[blk(SWA_HEADS, SWA_HEAD_DIM), blk(SWA_KV_HEADS, SWA_HEAD_DIM), blk(1, SWA_KV_W),
                  blk(SWA_KV_HEADS, SWA_HEAD_DIM), blk(1, SWA_KV_W), blk(WINDOW, SWA_KV_W), blk(WINDOW, SWA_KV_W),
                  full(1, SWA_HEAD_DIM), full(1, SWA_HEAD_DIM), full(1, SWA_KV_W),
                  full(SWA_HEADS, 1), full(SWA_HEADS, 1)],
        out_specs=[blk(SWA_HEADS, SWA_HEAD_DIM), blk(WINDOW, SWA_KV_W), blk(WINDOW, SWA_KV_W)],
        compiler_params=_cparams(("parallel",)),
        name="swa_decode",
    )(q3, k3, kf, v3, vf, ck, cv, qn, kn, kn2, sink_col, slope_col)
    return o.reshape(b, BRANCH_W), ok, ov


def _gate_rows(ab, alog_ref, dtb_ref):
    g = -jnp.exp(alog_ref[...]) * _softplus(ab + dtb_ref[...])
    return g, jax.nn.sigmoid(ab)


GDN_PREP_CHUNKS = 4
GDN_SCAN_BATCH = 8


def _gdn_prep_kernel(ys, ab_ref, alog_ref, dtb_ref, u_ref, w_ref, qg_ref, kd_ref, in_ref, eg_ref):
    C = GDN_CHUNK
    d = GDN_HEAD_DIM
    n = GDN_PREP_CHUNKS * C
    heads = range(GDN_HEADS)
    g_all, beta_all = _gate_rows(ab_ref[0], alog_ref, dtb_ref)
    r = lax.broadcasted_iota(I32, (n, n), 0)
    cc = lax.broadcasted_iota(I32, (n, n), 1)
    same = (r // C) == (cc // C)
    incl = same & (r >= cc)
    strict = same & (r > cc)
    upto = same & (r <= cc)
    eye = r == cc
    q = [ys[0, :, h * d:(h + 1) * d] for h in heads]
    k = [ys[0, :, GDN_W + h * d:GDN_W + (h + 1) * d] for h in heads]
    beta = [beta_all[:, GDN_HEADS + h:GDN_HEADS + h + 1] for h in heads]
    gc_row = [jnp.sum(jnp.where(upto, g_all[:, h:h + 1], 0.0), axis=0, keepdims=True) for h in heads]
    gc_col = [jnp.sum(jnp.where(eye, gc_row[h], 0.0), axis=1, keepdims=True) for h in heads]
    decay = [jnp.where(incl, jnp.exp(jnp.where(incl, gc_col[h] - gc_row[h], 0.0)), 0.0) for h in heads]
    kb = [k[h] * beta[h] for h in heads]
    bp = [-jnp.where(strict, _bdot_nt(kb[h], k[h]) * decay[h], 0.0) for h in heads]
    intra = [jnp.where(incl, _bdot_nt(q[h], k[h]) * decay[h], 0.0) for h in heads]
    p = [eye.astype(F32) + bp[h] for h in heads]
    span = 2
    while span < C:
        bp = [_bdot(bp[h], bp[h]) for h in heads]
        p = [p[h] + _bdot(p[h], bp[h]) for h in heads]
        span *= 2
    uw = [_bdot(p[h], jnp.concatenate([ys[0, :, 2 * GDN_W + h * d:2 * GDN_W + (h + 1) * d] * beta[h],
                                       kb[h] * jnp.exp(gc_col[h])], axis=1)) for h in heads]
    for h in heads:
        hs = slice(h * d, (h + 1) * d)
        u_ref[0, :, hs] = uw[h][:, :d]
        w_ref[0, :, hs] = uw[h][:, d:].astype(BF16)
        qg_ref[0, :, hs] = (q[h] * jnp.exp(gc_col[h])).astype(BF16)
        for ci in range(GDN_PREP_CHUNKS):
            r0 = ci * C
            g_last = gc_row[h][:, r0 + C - 1:r0 + C]
            kd_ref[0, r0:r0 + C, hs] = (k[h][r0:r0 + C] * jnp.exp(g_last - gc_col[h][r0:r0 + C])).astype(BF16)
            in_ref[0, r0:r0 + C, h * C:(h + 1) * C] = intra[h][r0:r0 + C, r0:r0 + C].astype(BF16)
            eg_ref[0, ci, h:h + 1, :] = jnp.broadcast_to(jnp.exp(g_last), (1, LANES))


def _gdn_scan_kernel(u_ref, w_ref, qg_ref, kd_ref, in_ref, eg_ref, z_ref, gn_ref, o_ref, s_ref, st):
    c = pl.program_id(1)
    C = GDN_CHUNK
    d = GDN_HEAD_DIM

    @pl.when(c == 0)
    def _():
        st[...] = jnp.zeros_like(st)

    chains = [(bi, h) for bi in range(u_ref.shape[0]) for h in range(GDN_HEADS)]
    hsl = lambda h: slice(h * d, (h + 1) * d)
    s_f = [st[bi, h] for bi, h in chains]
    s_b = [s.astype(BF16) for s in s_f]
    ws = [jnp.dot(w_ref[bi, :, hsl(h)], s_b[i], preferred_element_type=F32) for i, (bi, h) in enumerate(chains)]
    qs = [jnp.dot(qg_ref[bi, :, hsl(h)], s_b[i], preferred_element_type=F32) for i, (bi, h) in enumerate(chains)]
    vb = [(u_ref[bi, :, hsl(h)] - ws[i]).astype(BF16) for i, (bi, h) in enumerate(chains)]
    iv = [jnp.dot(in_ref[bi, :, h * C:(h + 1) * C], vb[i], preferred_element_type=F32)
          for i, (bi, h) in enumerate(chains)]
    kv = [lax.dot_general(kd_ref[bi, :, hsl(h)], vb[i], (((0,), (0,)), ((), ())), preferred_element_type=F32)
          for i, (bi, h) in enumerate(chains)]
    for i, (bi, h) in enumerate(chains):
        st[bi, h] = s_f[i] * eg_ref[bi, 0, h:h + 1, :] + kv[i]
        o_ref[bi, :, hsl(h)] = _rms(qs[i] + iv[i], gn_ref[...]) * z_ref[bi, :, hsl(h)]
    s_ref[...] = st[...]


def _gdn_prompt(qkvg, z, ab, alog, dtb, gnorm):
    b, l, _ = qkvg.shape
    C = GDN_CHUNK
    nc = l // C
    rows = GDN_PREP_CHUNKS * C
    cur = lambda i, j: (i, j, 0)
    const2 = lambda i, j: (0, 0)
    u, w, qg, kd, intra, eg = pl.pallas_call(
        _gdn_prep_kernel,
        out_shape=[jax.ShapeDtypeStruct((b, l, GDN_W), F32),
                   jax.ShapeDtypeStruct((b, l, GDN_W), BF16),
                   jax.ShapeDtypeStruct((b, l, GDN_W), BF16),
                   jax.ShapeDtypeStruct((b, l, GDN_W), BF16),
                   jax.ShapeDtypeStruct((b, l, GDN_HEADS * C), BF16),
                   jax.ShapeDtypeStruct((b, nc, GDN_HEADS, LANES), F32)],
        grid=(b, l // rows),
        in_specs=[pl.BlockSpec((1, rows, GDN_CONV_W), cur),
                  pl.BlockSpec((1, rows, LANES), cur),
                  pl.BlockSpec((1, LANES), const2),
                  pl.BlockSpec((1, LANES), const2)],
        out_specs=[pl.BlockSpec((1, rows, GDN_W), cur),
                   pl.BlockSpec((1, rows, GDN_W), cur),
                   pl.BlockSpec((1, rows, GDN_W), cur),
                   pl.BlockSpec((1, rows, GDN_W), cur),
                   pl.BlockSpec((1, rows, GDN_HEADS * C), cur),
                   pl.BlockSpec((1, GDN_PREP_CHUNKS, GDN_HEADS, LANES), lambda i, j: (i, j, 0, 0))],
        compiler_params=_cparams(("parallel", "parallel")),
        name="gdn_prep",
    )(qkvg, ab, alog, dtb)
    bb = GDN_SCAN_BATCH if b % GDN_SCAN_BATCH == 0 else 1
    seq = lambda wd: pl.BlockSpec((bb, C, wd), cur)
    return pl.pallas_call(
        _gdn_scan_kernel,
        out_shape=[jax.ShapeDtypeStruct((b, l, GDN_W), F32),
                   jax.ShapeDtypeStruct((b, GDN_HEADS, GDN_HEAD_DIM, GDN_HEAD_DIM), F32)],
        grid=(b // bb, nc),
        in_specs=[seq(GDN_W), seq(GDN_W), seq(GDN_W), seq(GDN_W), seq(GDN_HEADS * C),
                  pl.BlockSpec((bb, 1, GDN_HEADS, LANES), lambda i, j: (i, j, 0, 0)),
                  seq(GDN_W),
                  pl.BlockSpec((1, GDN_HEAD_DIM), const2)],
        out_specs=[seq(GDN_W),
                   pl.BlockSpec((bb, GDN_HEADS, GDN_HEAD_DIM, GDN_HEAD_DIM), lambda i, j: (i, 0, 0, 0))],
        scratch_shapes=[pltpu.VMEM((bb, GDN_HEADS, GDN_HEAD_DIM, GDN_HEAD_DIM), F32)],
        compiler_params=_cparams(("parallel", "arbitrary")),
        name="gdn_scan",
    )(u, w, qg, kd, intra, eg, z, gnorm)


def _gdn_decode_kernel(bs, x_ref, cs_ref, z_ref, ab_ref, s_ref, cw_ref, alog_ref, dtb_ref, gn_ref, o_ref, so_ref):
    x = x_ref[...]
    y = x * cw_ref[CONV_WIDTH - 1:CONV_WIDTH, :]
    for j in range(CONV_WIDTH - 1):
        y = y + cs_ref[:, j, :] * cw_ref[j:j + 1, :]
    y = _silu(y)
    g_all, beta_all = _gate_rows(ab_ref[...], alog_ref, dtb_ref)
    eg_all = jnp.exp(g_all)
    z = z_ref[...]
    d = GDN_HEAD_DIM
    eye = (lax.broadcasted_iota(I32, (d, d), 0) == lax.broadcasted_iota(I32, (d, d), 1)).astype(F32)
    for h in range(GDN_HEADS):
        hs = slice(h * d, (h + 1) * d)
        qh = _l2(y[:, h * d:(h + 1) * d]) * (d ** -0.5)
        kh = _l2(y[:, GDN_W + h * d:GDN_W + (h + 1) * d])
        vh = y[:, 2 * GDN_W + h * d:2 * GDN_W + (h + 1) * d]
        kt = _hdot_nt(eye, kh)
        qt = _hdot_nt(eye, qh)
        rng = range(bs)
        eg = [eg_all[b:b + 1, h:h + 1] for b in rng]
        beta = [beta_all[b:b + 1, GDN_HEADS + h:GDN_HEADS + h + 1] for b in rng]
        sk = [jnp.sum(s_ref[b, h] * kt[:, b:b + 1], axis=0, keepdims=True) for b in rng]
        v_new = [beta[b] * (vh[b:b + 1] - eg[b] * sk[b]) for b in rng]
        s_new = [eg[b] * s_ref[b, h] + kt[:, b:b + 1] * v_new[b] for b in rng]
        for b in rng:
            so_ref[b, h] = s_new[b]
        o = jnp.concatenate([jnp.sum(s_new[b] * qt[:, b:b + 1], axis=0, keepdims=True) for b in rng], axis=0)
        o_ref[:, hs] = _rms(o, gn_ref[...]) * _silu(z[:, hs])


def _gdn_decode(qkvg, z, ab, state, conv_state, conv_w, alog, dtb, gnorm, bs=8):
    b = qkvg.shape[0]
    row = lambda w: pl.BlockSpec((bs, w), lambda i: (i, 0))
    full = lambda *shape: pl.BlockSpec(shape, lambda i: (0,) * len(shape))
    sspec = pl.BlockSpec((bs, GDN_HEADS, GDN_HEAD_DIM, GDN_HEAD_DIM), lambda i: (i, 0, 0, 0))
    return pl.pallas_call(
        functools.partial(_gdn_decode_kernel, bs),
        out_shape=[jax.ShapeDtypeStruct((b, GDN_W), F32),
                   jax.ShapeDtypeStruct(state.shape, F32)],
        grid=(b // bs,),
        in_specs=[row(GDN_CONV_W),
                  pl.BlockSpec((bs, CONV_WIDTH - 1, GDN_CONV_W), lambda i: (i, 0, 0)),
                  row(GDN_W), row(LANES), sspec,
                  full(CONV_WIDTH, GDN_CONV_W), full(1, LANES), full(1, LANES), full(1, GDN_HEAD_DIM)],
        out_specs=[row(GDN_W), sspec],
        compiler_params=_cparams(("parallel",)),
        name="gdn_decode",
    )(qkvg, conv_state, z, ab, state, conv_w, alog, dtb, gnorm)


def _memkv_kernel(x_ref, g_ref, w_ref, kn_ref, k_ref, v_ref):
    hb = _rms(x_ref[...], g_ref[...]).astype(BF16)
    for hd in range(MEM_HEADS):
        hs = slice(hd * MEM_HEAD_DIM, (hd + 1) * MEM_HEAD_DIM)
        k_ref[:, hs] = _rms(jnp.dot(hb, w_ref[:, hs], preferred_element_type=F32), kn_ref[...])
    v_ref[...] = jnp.dot(hb, w_ref[:, BRANCH_W:], preferred_element_type=F32)


def _memkv(mem2, gain, w_kv, kn, tm=512):
    t = mem2.shape[0]
    return pl.pallas_call(
        _memkv_kernel,
        out_shape=[jax.ShapeDtypeStruct((t, BRANCH_W), F32)] * 2,
        grid=(t // tm,),
        in_specs=[pl.BlockSpec((tm, D_MODEL), lambda i: (i, 0)),
                  pl.BlockSpec((1, D_MODEL), lambda i: (0, 0)),
                  pl.BlockSpec((D_MODEL, 2 * BRANCH_W), lambda i: (0, 0)),
                  pl.BlockSpec((1, MEM_HEAD_DIM), lambda i: (0, 0))],
        out_specs=[pl.BlockSpec((tm, BRANCH_W), lambda i: (i, 0))] * 2,
        compiler_params=_cparams(("parallel",)),
        name="memkv",
    )(mem2, gain, w_kv, kn)


def _memattn_prompt_kernel(q_ref, k_ref, v_ref, qn_ref, o_ref):
    q = q_ref[0]
    k = k_ref[0]
    v = v_ref[0]
    scale = MEM_HEAD_DIM ** -0.5
    for hd in range(MEM_HEADS):
        hs = slice(hd * MEM_HEAD_DIM, (hd + 1) * MEM_HEAD_DIM)
        s = _bdot_nt(_rms(q[:, hs], qn_ref[...]), k[:, hs]) * scale
        p = jnp.exp(s - jnp.max(s, axis=-1, keepdims=True))
        o_ref[0, :, hs] = _bdot(p, v[:, hs]) / jnp.sum(p, axis=-1, keepdims=True)


def _memattn_prompt(qm, mk, mv, qn):
    b, l, _ = qm.shape
    tq = 512 if l % 512 == 0 else WINDOW
    return pl.pallas_call(
        _memattn_prompt_kernel,
        out_shape=jax.ShapeDtypeStruct((b, l, BRANCH_W), F32),
        grid=(b, l // tq),
        in_specs=[pl.BlockSpec((1, tq, BRANCH_W), lambda i, j: (i, j, 0)),
                  pl.BlockSpec((1, MEM_LEN, BRANCH_W), lambda i, j: (i, 0, 0)),
                  pl.BlockSpec((1, MEM_LEN, BRANCH_W), lambda i, j: (i, 0, 0)),
                  pl.BlockSpec((1, MEM_HEAD_DIM), lambda i, j: (0, 0))],
        out_specs=pl.BlockSpec((1, tq, BRANCH_W), lambda i, j: (i, j, 0)),
        compiler_params=_cparams(("parallel", "parallel")),
        name="memattn_prompt",
    )(qm, mk, mv, qn)


def _memattn_decode_kernel(bs, q_ref, k_ref, v_ref, qn_ref, o_ref):
    scale = MEM_HEAD_DIM ** -0.5
    for b in range(bs):
        qn = _rms(q_ref[b], qn_ref[...])
        s = jnp.sum(k_ref[b] * qn, axis=-1, keepdims=True) * scale
        p = jnp.exp(s - jnp.max(s, axis=0, keepdims=True))
        den = jnp.sum(p, axis=0)
        o_ref[b] = jnp.sum(p * v_ref[b], axis=0) / den


def _memattn_decode(qm, ck, cv, qn, bs=8):
    b = qm.shape[0]
    q3 = pl.BlockSpec((bs, MEM_HEADS, MEM_HEAD_DIM), lambda i: (i, 0, 0))
    kv = pl.BlockSpec((bs, MEM_LEN, MEM_HEADS, MEM_HEAD_DIM), lambda i: (i, 0, 0, 0))
    o = pl.pallas_call(
        functools.partial(_memattn_decode_kernel, bs),
        out_shape=jax.ShapeDtypeStruct((b, MEM_HEADS, MEM_HEAD_DIM), F32),
        grid=(b // bs,),
        in_specs=[q3, kv, kv, pl.BlockSpec((1, MEM_HEAD_DIM), lambda i: (0, 0))],
        out_specs=q3,
        compiler_params=_cparams(("parallel",)),
        name="memattn_decode",
    )(qm.reshape(b, MEM_HEADS, MEM_HEAD_DIM), ck, cv, qn)
    return o.reshape(b, BRANCH_W)


def _merge_kernel(oa_ref, ob_ref, oc_ref, gate_ref, x_ref, wb_ref, wo_ref, g2_ref, wr_ref, br_ref,
                  x1_ref, h2_ref, lg_ref):
    acc = None
    for i, o_ref in enumerate((oa_ref, ob_ref, oc_ref)):
        mixed = jnp.dot(o_ref[...].astype(BF16), wb_ref[i], preferred_element_type=F32)
        term = gate_ref[:, i * D_MODEL:(i + 1) * D_MODEL].astype(F32) * mixed
        acc = term if acc is None else acc + term
    x1 = x_ref[...] + jnp.dot(acc.astype(BF16), wo_ref[...], preferred_element_type=F32)
    x1_ref[...] = x1
    h2 = _rms(x1, g2_ref[...])
    h2_ref[...] = h2.astype(BF16)
    h_hi = h2.astype(BF16)
    h_lo = (h2 - h_hi.astype(F32)).astype(BF16)
    nt = (((1,), (1,)), ((), ()))
    a = lax.dot_general(wr_ref[...], h_hi, nt, preferred_element_type=F32)
    b = lax.dot_general(wr_ref[0:N_EXPERTS, :], h_lo, nt, preferred_element_type=F32)
    lg_ref[...] = a[0:N_EXPERTS] + a[N_EXPERTS:] + b + br_ref[...]


def _merge_into_kernel(oa_ref, ob_ref, oc_ref, gate_ref, x_ref, wb_ref, wo_ref, g2_ref, wr_ref, br_ref,
                       x1_in, h2_in, lg_in, x1_ref, h2_ref, lg_ref):
    del x1_in, h2_in, lg_in
    _merge_kernel(oa_ref, ob_ref, oc_ref, gate_ref, x_ref, wb_ref, wo_ref, g2_ref, wr_ref, br_ref,
                  x1_ref, h2_ref, lg_ref)


def _merge(oa, ob, oc, gate, x2, wb, wo, g2, wr_t, br_col, tm, t_out=None, into=None):
    t = x2.shape[0]
    t_out = t_out or t
    blk0 = 0 if into is None else (into[0].shape[0] - t) // tm
    row = lambda w: pl.BlockSpec((tm, w), lambda i: (i, 0))
    orow = lambda w: pl.BlockSpec((tm, w), lambda i: (i + blk0, 0))
    full = lambda *shape: pl.BlockSpec(shape, lambda i: (0,) * len(shape))
    in_specs = [row(BRANCH_W), row(BRANCH_W), row(BRANCH_W), row(N_BRANCH * D_MODEL), row(D_MODEL),
                full(N_BRANCH, BRANCH_W, D_MODEL), full(D_MODEL, D_MODEL), full(1, D_MODEL),
                full(2 * N_EXPERTS, D_MODEL), full(N_EXPERTS, 1)]
    args = [oa, ob, oc, gate, x2, wb, wo, g2, wr_t, br_col]
    aliases = {}
    if into is not None:
        t_out = into[0].shape[0]
        in_specs += [pl.BlockSpec(memory_space=pl.ANY)] * 3
        aliases = {len(args) + k: k for k in range(3)}
        args += list(into)
    return pl.pallas_call(
        _merge_kernel if into is None else _merge_into_kernel,
        out_shape=[jax.ShapeDtypeStruct((t_out, D_MODEL), F32),
                   jax.ShapeDtypeStruct((t_out, D_MODEL), BF16),
                   jax.ShapeDtypeStruct((N_EXPERTS, t_out), F32)],
        grid=(t // tm,),
        in_specs=in_specs,
        out_specs=[orow(D_MODEL), orow(D_MODEL), pl.BlockSpec((N_EXPERTS, tm), lambda i: (0, i + blk0))],
        input_output_aliases=aliases,
        compiler_params=_cparams(("parallel",)),
        name="merge",
    )(*args)


def _col_to_row(col):
    n = col.shape[0]
    r = lax.broadcasted_iota(I32, (n, n), 0)
    c = lax.broadcasted_iota(I32, (n, n), 1)
    return jnp.sum(jnp.where(r == c, col, 0.0), axis=0, keepdims=True)


def _row_to_col(row):
    n = row.shape[1]
    r = lax.broadcasted_iota(I32, (n, n), 0)
    c = lax.broadcasted_iota(I32, (n, n), 1)
    return jnp.sum(jnp.where(r == c, row, 0.0), axis=1, keepdims=True)


def _lane_pad(row):
    return jnp.concatenate([row, jnp.zeros((1, LANES - row.shape[1]), row.dtype)], axis=1)


def _route_kernel(lg_ref, pos_ref, gate_ref, meta_ref, cnt_ref, carry):
    i = pl.program_id(0)
    tn = lg_ref.shape[1]

    @pl.when(i == 0)
    def _():
        carry[...] = jnp.zeros_like(carry)

    l = lg_ref[...]
    eio = lax.broadcasted_iota(I32, (N_EXPERTS, tn), 0)
    hot = jnp.zeros((N_EXPERTS, tn), F32)
    vals, idxs = [], []
    for _ in range(TOP_K):
        m = jnp.max(l, axis=0, keepdims=True)
        idx = jnp.min(jnp.where(l == m, eio, N_EXPERTS), axis=0, keepdims=True)
        sel = eio == idx
        vals.append(m)
        idxs.append(idx)
        hot = hot + sel.astype(F32)
        l = jnp.where(sel, -jnp.inf, l)
    ex = [jnp.exp(v - vals[0]) for v in vals]
    tot = ex[0] + ex[1] + ex[2] + ex[3]
    before = (lax.broadcasted_iota(I32, (tn, tn), 0) < lax.broadcasted_iota(I32, (tn, tn), 1)).astype(BF16)
    within = jnp.dot(hot.astype(BF16), before, preferred_element_type=F32)
    cnt_col = jnp.sum(hot, axis=1, keepdims=True)
    room = jnp.floor((cnt_col + (MOE_UNIT - 1)) * (1.0 / MOE_UNIT)) * MOE_UNIT
    r = lax.broadcasted_iota(I32, (N_EXPERTS, N_EXPERTS), 0)
    c = lax.broadcasted_iota(I32, (N_EXPERTS, N_EXPERTS), 1)
    start_col = _row_to_col(jnp.sum(jnp.where(r < c, room, 0.0), axis=0, keepdims=True))
    for k in range(TOP_K):
        gate_ref[k:k + 1, :] = ex[k] / tot
        pos_ref[k:k + 1, :] = jnp.sum(jnp.where(eio == idxs[k], start_col + within, 0.0), axis=0,
                                      keepdims=True).astype(I32)
    meta_ref[0] = _lane_pad(jnp.concatenate([_col_to_row(room), _col_to_row(carry[:, 0:1])], axis=1)).astype(I32)
    carry[...] = carry[...] + room
    cnt_ref[...] = carry[...]


def _route(logits_t, tn):
    t = logits_t.shape[1]
    kt = pl.BlockSpec((TOP_K, tn), lambda i: (0, i))
    return pl.pallas_call(
        _route_kernel,
        out_shape=[jax.ShapeDtypeStruct((TOP_K, t), I32),
                   jax.ShapeDtypeStruct((TOP_K, t), F32),
                   jax.ShapeDtypeStruct((t // tn, 1, LANES), I32),
                   jax.ShapeDtypeStruct((N_EXPERTS, LANES), F32)],
        grid=(t // tn,),
        in_specs=[pl.BlockSpec((N_EXPERTS, tn), lambda i: (0, i))],
        out_specs=[kt, kt, pl.BlockSpec((1, 1, LANES), lambda i: (i, 0, 0)),
                   pl.BlockSpec((N_EXPERTS, LANES), lambda i: (0, 0))],
        scratch_shapes=[pltpu.VMEM((N_EXPERTS, LANES), F32)],
        compiler_params=_cparams(("arbitrary",)),
        name="moe_route",
    )(logits_t)


def _layout_kernel(bm, cnt_ref, be_ref, na_ref, ps_ref, pe_ref, ct_ref):
    nbp = be_ref.shape[1]
    cnt_col = cnt_ref[:, 0:1]
    size = jnp.floor((cnt_col + (bm - 1)) * (1.0 / bm)) * bm
    r = lax.broadcasted_iota(I32, (N_EXPERTS, N_EXPERTS), 0)
    c = lax.broadcasted_iota(I32, (N_EXPERTS, N_EXPERTS), 1)
    ends_row = jnp.sum(jnp.where(r <= c, size, 0.0), axis=0, keepdims=True)
    ends_col = _row_to_col(ends_row)
    nact = ends_row[:, N_EXPERTS - 1:N_EXPERTS] * (1.0 / bm)
    blk = jnp.minimum(lax.broadcasted_iota(I32, (1, nbp), 1).astype(F32), nact - 1.0)
    be = jnp.sum((ends_col <= blk * bm).astype(F32), axis=0, keepdims=True)
    be_ref[...] = jnp.minimum(be, N_EXPERTS - 1.0).astype(I32)
    na_ref[...] = jnp.broadcast_to(nact, na_ref.shape).astype(I32)
    ps_ref[...] = _lane_pad(ends_row - _col_to_row(size)).astype(I32)
    pe_ref[...] = _lane_pad(ends_row).astype(I32)
    ct_ref[...] = _lane_pad(_col_to_row(cnt_col)).astype(I32)


def _layout(cnt, nbp, bm):
    row = jax.ShapeDtypeStruct((1, LANES), I32)
    return pl.pallas_call(
        functools.partial(_layout_kernel, bm),
        out_shape=[jax.ShapeDtypeStruct((1, nbp), I32), row, row, row, row],
        name="moe_layout",
    )(cnt)


def _stage_rows(tn):
    return tn * TOP_K + N_EXPERTS * MOE_UNIT


def _run_copies(meta_ref, ps_ref, make_copy, tn):
    top = (tn // MOE_UNIT).bit_length() - 1
    src = jnp.int32(0)
    total = jnp.int32(0)
    for e in range(N_EXPERTS):
        units = lax.shift_right_logical(meta_ref[0, 0, e], 3)
        dst = ps_ref[e] + meta_ref[0, 0, N_EXPERTS + e]
        off = jnp.int32(0)
        for k in range(top, -1, -1):
            rows = MOE_UNIT << k
            take = lax.shift_right_logical(units, k) & 1

            @pl.when(take == 1)
            def _(src=src, dst=dst, off=off, rows=rows):
                make_copy(pl.multiple_of(src + off, MOE_UNIT), pl.multiple_of(dst + off, MOE_UNIT), rows).start()
            off = off + take * rows
        src = src + units * MOE_UNIT
        total = total + units
    return total


def _wait_copies(units, make_copy, max_rows):
    for k in range((max_rows // MOE_UNIT).bit_length()):
        if (MOE_UNIT << k) > max_rows:
            break

        @pl.when(lax.shift_right_logical(units, k) & 1 == 1)
        def _(k=k):
            make_copy(0, 0, MOE_UNIT << k).wait()


def _wait_units(count, make_copy):
    def body(u, carry):
        make_copy(0, 0).wait()
        return carry
    lax.fori_loop(0, count, body, 0)


def _dispatch_kernel(ps_ref, pe_ref, ct_ref, meta_ref, pos_ref, gate_ref, h_ref, xg_ref, stage, zrows, zblock,
                     started, sem, zsem):
    i = pl.program_id(0)
    n = pl.num_programs(0)
    slot = i % 2
    rows, tn = stage.shape[1], pos_ref.shape[1]

    def copy(s):
        return lambda src, dst, nrows: pltpu.make_async_copy(stage.at[s, pl.ds(src, nrows)],
                                                             xg_ref.at[pl.ds(dst, nrows)], sem.at[s])

    @pl.when(i >= 2)
    def _():
        _wait_copies(started[slot], copy(slot), rows)

    srow = lax.broadcasted_iota(I32, (rows, tn), 0)
    hits = [srow == pos_ref[k:k + 1, :] for k in range(TOP_K)]
    hit = hits[0]
    gate_at = jnp.where(hits[0], gate_ref[0:1, :], 0.0)
    for k in range(1, TOP_K):
        hit = hit | hits[k]
        gate_at = gate_at + jnp.where(hits[k], gate_ref[k:k + 1, :], 0.0)
    onehot = jnp.where(hit, 1.0, 0.0).astype(BF16)
    stage[slot, :, 0:D_MODEL] = jnp.dot(onehot, h_ref[...], preferred_element_type=F32)
    stage[slot, :, D_MODEL:] = jnp.broadcast_to(jnp.sum(gate_at, axis=1, keepdims=True), (rows, LANES))
    started[slot] = _run_copies(meta_ref, ps_ref, copy(slot), tn)

    @pl.when(i == n - 1)
    def _():
        _wait_copies(started[slot], copy(slot), rows)

        @pl.when(n >= 2)
        def _():
            _wait_copies(started[1 - slot], copy(1 - slot), rows)

        zrows[...] = jnp.zeros_like(zrows)
        zero = lambda src, dst: pltpu.make_async_copy(zrows, xg_ref.at[pl.ds(dst, MOE_UNIT)], zsem)
        total = jnp.int32(0)
        for e in range(N_EXPERTS):
            lo = ps_ref[e] + ct_ref[e]
            units = lax.shift_right_logical(pe_ref[e] - lo, 3)

            def body(u, carry, lo=lo):
                zero(0, pl.multiple_of(lo + u * MOE_UNIT, MOE_UNIT)).start()
                return carry
            lax.fori_loop(0, units, body, 0)
            total = total + units
        _wait_units(total, zero)

        zblock[...] = jnp.zeros_like(zblock)
        bm = zblock.shape[0]
        zero_block = lambda blk: pltpu.make_async_copy(zblock, xg_ref.at[pl.ds(pl.multiple_of(blk * bm, bm), bm)],
                                                       zsem)
        first = lax.shift_right_logical(pe_ref[N_EXPERTS - 1], bm.bit_length() - 1)
        last = xg_ref.shape[0] // bm

        def start_block(blk, carry):
            zero_block(blk).start()
            return carry
        lax.fori_loop(first, last, start_block, 0)

        def wait_block(blk, carry):
            zero_block(0).wait()
            return carry
        lax.fori_loop(first, last, wait_block, 0)


def _dispatch(ps, pe, ct, meta, pos, gates, h2, n_slots, tn, bm):
    t = h2.shape[0]
    width = D_MODEL + LANES
    return pl.pallas_call(
        _dispatch_kernel,
        out_shape=jax.ShapeDtypeStruct((n_slots, width), F32),
        grid_spec=pltpu.PrefetchScalarGridSpec(
            num_scalar_prefetch=3,
            grid=(t // tn,),
            in_specs=[pl.BlockSpec((1, 1, LANES), lambda i, *_: (i, 0, 0), memory_space=pltpu.SMEM),
                      pl.BlockSpec((TOP_K, tn), lambda i, *_: (0, i)),
                      pl.BlockSpec((TOP_K, tn), lambda i, *_: (0, i)),
                      pl.BlockSpec((tn, D_MODEL), lambda i, *_: (i, 0))],
            out_specs=pl.BlockSpec(memory_space=pl.ANY),
            scratch_shapes=[pltpu.VMEM((2, _stage_rows(tn), width), F32),
                            pltpu.VMEM((MOE_UNIT, width), F32),
                            pltpu.VMEM((bm, width), F32),
                            pltpu.SMEM((2,), I32),
                            pltpu.SemaphoreType.DMA((2,)),
                            pltpu.SemaphoreType.DMA(())]),
        compiler_params=_cparams(("arbitrary",)),
        name="moe_dispatch",
    )(ps, pe, ct, meta, pos, gates, h2)


def _expert_kernel(be_ref, na_ref, x_ref, w1_ref, b1_ref, w2_ref, b2_ref, y_ref, w1s, w2s):
    i = pl.program_id(0)
    e = be_ref[i]
    prev = be_ref[jnp.maximum(i - 1, 0)]

    @pl.when((i == 0) | (e != prev))
    def _():
        rows = 128
        for r0 in range(0, D_MODEL, rows):
            w1s[r0:r0 + rows, :] = w1_ref[r0:r0 + rows, :].astype(BF16)
        for r0 in range(0, D_FF, rows):
            w2s[r0:r0 + rows, :] = w2_ref[r0:r0 + rows, :].astype(BF16)

    @pl.when(i < na_ref[0])
    def _():
        x = x_ref[:, 0:D_MODEL].astype(BF16)
        hmid = jnp.dot(x, w1s[...], preferred_element_type=F32) + b1_ref[...]
        glu = jnp.minimum(hmid[:, :D_FF], SWIGLU_LIMIT)
        lin = jnp.clip(hmid[:, D_FF:], -SWIGLU_LIMIT, SWIGLU_LIMIT)
        act = glu * _sigmoid(SWIGLU_ALPHA * glu) * (lin + 1.0)
        y = jnp.dot(act.astype(BF16), w2s[...], preferred_element_type=F32) + b2_ref[...]
        y_ref[...] = y * x_ref[:, D_MODEL:D_MODEL + 1]

    @pl.when(i >= na_ref[0])
    def _():
        y_ref[...] = jnp.zeros_like(y_ref)


def _experts(be, nact, xg, layer, w1, b1, w2, b2, bm):
    n_slots = xg.shape[0]
    nb = n_slots // bm
    blk = lambda i, be_r, na_r: (jnp.minimum(i, na_r[0] - 1), 0)
    wsel = lambda i, be_r, na_r: (layer, be_r[i], 0, 0)
    return pl.pallas_call(
        _expert_kernel,
        out_shape=jax.ShapeDtypeStruct((n_slots, D_MODEL), F32),
        grid_spec=pltpu.PrefetchScalarGridSpec(
            num_scalar_prefetch=2,
            grid=(nb,),
            in_specs=[pl.BlockSpec((bm, xg.shape[1]), blk),
                      pl.BlockSpec((None, None, D_MODEL, 2 * D_FF), wsel),
                      pl.BlockSpec((None, None, 1, 2 * D_FF), wsel),
                      pl.BlockSpec((None, None, D_FF, D_MODEL), wsel),
                      pl.BlockSpec((None, None, 1, D_MODEL), wsel)],
            out_specs=pl.BlockSpec((bm, D_MODEL), lambda i, be_r, na_r: (i, 0)),
            scratch_shapes=[pltpu.VMEM((D_MODEL, 2 * D_FF), BF16), pltpu.VMEM((D_FF, D_MODEL), BF16)]),
        compiler_params=_cparams(("arbitrary",)),
        name="moe_experts",
    )(be, nact, xg, w1, b1, w2, b2)


def _combine_kernel(n_first, ps_ref, mcur_ref, mnext_ref, pos_ref, x1_ref, yg_ref, o_ref, o2_ref, stage,
                    started, sem):
    i = pl.program_id(0)
    n = pl.num_programs(0)
    slot = i % 2
    rows, tn = stage.shape[1], pos_ref.shape[1]

    @pl.when(i == 0)
    def _():
        stage[...] = jnp.zeros_like(stage)

    def copy(s):
        return lambda src, dst, nrows: pltpu.make_async_copy(yg_ref.at[pl.ds(dst, nrows)],
                                                             stage.at[s, pl.ds(src, nrows)], sem.at[s])

    @pl.when(i == 0)
    def _():
        started[0] = _run_copies(mcur_ref, ps_ref, copy(0), tn)

    @pl.when(i + 1 < n)
    def _():
        started[1 - slot] = _run_copies(mnext_ref, ps_ref, copy(1 - slot), tn)

    _wait_copies(started[slot], copy(slot), rows)

    eye = (lax.broadcasted_iota(I32, (tn, tn), 0) == lax.broadcasted_iota(I32, (tn, tn), 1)).astype(F32)
    cols = _hdot_nt(eye, pos_ref[...].astype(F32)).astype(I32)
    lane = lax.broadcasted_iota(I32, (tn, rows), 1)
    hit = lane == cols[:, 0:1]
    for k in range(1, TOP_K):
        hit = hit | (lane == cols[:, k:k + 1])
    pick = jnp.where(hit, 1.0, 0.0).astype(BF16)
    out = x1_ref[...] + jnp.dot(pick, stage[slot].astype(BF16), preferred_element_type=F32)

    @pl.when(i < n_first)
    def _():
        o_ref[...] = out

    @pl.when(i >= n_first)
    def _():
        o2_ref[...] = out


def _combine(ps, meta, pos, x1, yg, tn, t_first):
    t = x1.shape[0]
    nt = t // tn
    n_first = t_first // tn
    return pl.pallas_call(
        functools.partial(_combine_kernel, n_first),
        out_shape=[jax.ShapeDtypeStruct((t_first, D_MODEL), F32),
                   jax.ShapeDtypeStruct((t - t_first, D_MODEL), F32)],
        grid_spec=pltpu.PrefetchScalarGridSpec(
            num_scalar_prefetch=1,
            grid=(nt,),
            in_specs=[pl.BlockSpec((1, 1, LANES), lambda i, *_: (i, 0, 0), memory_space=pltpu.SMEM),
                      pl.BlockSpec((1, 1, LANES), lambda i, *_: (jnp.minimum(i + 1, nt - 1), 0, 0),
                                   memory_space=pltpu.SMEM),
                      pl.BlockSpec((TOP_K, tn), lambda i, *_: (0, i)),
                      pl.BlockSpec((tn, D_MODEL), lambda i, *_: (i, 0)),
                      pl.BlockSpec(memory_space=pl.ANY)],
            out_specs=[pl.BlockSpec((tn, D_MODEL), lambda i, *_: (jnp.minimum(i, n_first - 1), 0)),
                       pl.BlockSpec((tn, D_MODEL), lambda i, *_: (jnp.maximum(i - n_first, 0), 0))],
            scratch_shapes=[pltpu.VMEM((2, _stage_rows(tn), D_MODEL), F32),
                            pltpu.SMEM((2,), I32),
                            pltpu.SemaphoreType.DMA((2,))]),
        compiler_params=_cparams(("arbitrary",)),
        name="moe_combine",
    )(ps, meta, meta, pos, x1, yg)


def _moe(x1, h2, logits_t, layer, w1, b1, w2, b2, t_first):
    t = x1.shape[0]
    tn = MOE_TOK_TILE
    bm = MOE_BM if t * TOP_K >= N_EXPERTS * MOE_BM else LANES
    n_blocks = (t * TOP_K + (t // tn) * N_EXPERTS * (MOE_UNIT - 1)) // bm + N_EXPERTS + 1
    nbp = -(-n_blocks // LANES) * LANES
    pos, gates, meta, cnt = _route(logits_t, tn)
    be, nact, ps, pe, ct = _layout(cnt, nbp, bm)
    ps, pe, ct = ps[0, :N_EXPERTS], pe[0, :N_EXPERTS], ct[0, :N_EXPERTS]
    xg = _dispatch(ps, pe, ct, meta, pos, gates, h2, n_blocks * bm, tn, bm)
    yg = _experts(be[0, :n_blocks], nact[0, :1], xg, layer, w1, b1, w2, b2, bm)
    return _combine(ps, meta, pos, x1, yg, tn, t_first)


def _pack_w_in(w):
    ab = jnp.pad(w[:, _IN_MAIN:_IN_MAIN + _IN_AB], ((0, 0), (0, LANES - _IN_AB)))
    return jnp.concatenate([w[:, :_IN_MAIN], ab, w[:, _IN_MAIN + _IN_AB:]], axis=1).astype(BF16)


def _lane_row(v):
    return jnp.pad(v.astype(F32), (0, LANES - v.shape[0])).reshape(1, LANES)


def _layer_weights(ln1_gain, w_in, q_norm_swa, k_norm_swa, swa_sinks, conv_w, a_log, dt_bias, gdn_norm,
                   q_norm_mem, w_branch, w_out, ln2_gain, w_router, b_router, w_mlp1, b_mlp1, w_mlp2, b_mlp2, layer):
    depth = w_mlp1.shape[0]
    wr_hi = w_router.T.astype(BF16)
    wr_lo = (w_router.T - wr_hi.astype(F32)).astype(BF16)
    return dict(
        layer=layer, wr_t=jnp.concatenate([wr_hi, wr_lo], axis=0), b1r=b_mlp1.reshape(depth, N_EXPERTS, 1, -1), b2r=b_mlp2.reshape(depth, N_EXPERTS, 1, -1),
        ln1=ln1_gain.reshape(1, -1), wp=_pack_w_in(w_in),
        qn_s=q_norm_swa.reshape(1, -1), kn_s=k_norm_swa.reshape(1, -1), sinks=swa_sinks.reshape(1, -1),
        conv_w=conv_w, alog=_lane_row(a_log), dtb=_lane_row(dt_bias), gnorm=gdn_norm.reshape(1, -1),
        qn_m=q_norm_mem.reshape(1, -1), wb=w_branch.astype(BF16), wo=w_out.astype(BF16),
        ln2=ln2_gain.reshape(1, -1), br=b_router.reshape(-1, 1),
        w1=w_mlp1, w2=w_mlp2)


def _finish(lw, first, second):
    t1, t2 = first[4].shape[0], second[4].shape[0]
    tm1 = MERGE_ROWS if t1 % MERGE_ROWS == 0 else MOE_TOK_TILE
    pad = (-t2) % MOE_TOK_TILE
    second = [jnp.pad(a, ((0, pad), (0, 0))) for a in second]
    wts = (lw["wb"], lw["wo"], lw["ln2"], lw["wr_t"], lw["br"])
    bufs = _merge(*first, *wts, tm1, t_out=t1 + t2 + pad)
    bufs = _merge(*second, *wts, MOE_TOK_TILE, into=bufs)
    y1, y2 = _moe(*bufs, lw["layer"], lw["w1"], lw["b1r"], lw["w2"], lw["b2r"], t1)
    return y1, y2[:t2]


def _prompt_layer(x, mem, lw, mem_norm, w_mem_kv, k_norm_mem):
    b, l, d = x.shape
    x2 = x.reshape(b * l, d)
    tm = 256
    tm_in = 2 * tm if l % (2 * tm) == 0 else tm
    qs, ks, vs, qkvg, z, ab, qm, gate, conv_tail = _inproj(x2, lw["ln1"], lw["wp"], lw["conv_w"], tm_in, seq_len=l)
    mk, mv = _memkv(mem.reshape(b * MEM_LEN, d), mem_norm.reshape(1, -1), w_mem_kv.astype(BF16),
                    k_norm_mem.reshape(1, -1))
    r3 = lambda a: a.reshape(b, l, a.shape[-1])
    o_swa, kwin = _swa_prompt(r3(qs), r3(ks), r3(vs), lw["qn_s"], lw["kn_s"], lw["sinks"])
    o_g, s_fin = _gdn_prompt(r3(qkvg), r3(z), r3(ab), lw["alog"], lw["dtb"], lw["gnorm"])
    o_m = _memattn_prompt(r3(qm), mk.reshape(b, MEM_LEN, BRANCH_W), mv.reshape(b, MEM_LEN, BRANCH_W), lw["qn_m"])
    parts = (o_swa.reshape(b * l, -1), o_g.reshape(b * l, -1), o_m.reshape(b * l, -1), gate, x2)
    new_k = kwin.reshape(b, WINDOW, SWA_KV_HEADS, SWA_HEAD_DIM)
    new_v = r3(vs)[:, l - WINDOW:].reshape(b, WINDOW, SWA_KV_HEADS, SWA_HEAD_DIM)
    new_conv = conv_tail[:, _HALO - (CONV_WIDTH - 1):]
    mk4 = mk.reshape(b, MEM_LEN, MEM_HEADS, MEM_HEAD_DIM)
    mv4 = mv.reshape(b, MEM_LEN, MEM_HEADS, MEM_HEAD_DIM)
    return parts, (new_k, new_v, s_fin, new_conv, mk4, mv4)


def _sample_layer(x, cache_k, cache_v, state, conv_state, mem_k, mem_v, lw):
    b, l, d = x.shape
    x2 = x.reshape(b, d)
    tm = TOK_TILE
    qs, ks, vs, qkvg, z, ab, qm, gate = _inproj(x2, lw["ln1"], lw["wp"], lw["conv_w"], tm)
    o_swa, new_k, new_v = _swa_decode(qs, ks, vs, cache_k, cache_v, lw["qn_s"], lw["kn_s"], lw["sinks"])
    o_g, new_s = _gdn_decode(qkvg, z, ab, state, conv_state, lw["conv_w"], lw["alog"], lw["dtb"], lw["gnorm"])
    o_m = _memattn_decode(qm, mem_k, mem_v, lw["qn_m"])
    new_conv = jnp.concatenate([conv_state[:, 1:], qkvg[:, None, :]], axis=1)
    shp = (b, WINDOW, SWA_KV_HEADS, SWA_HEAD_DIM)
    return (o_swa, o_g, o_m, gate, x2), (new_k.reshape(shp), new_v.reshape(shp), new_s, new_conv)


def _layer(lw, xp, mem, mem_norm, w_mem_kv, k_norm_mem, xs, cache_k, cache_v, state, conv_state, mem_k, mem_v):
    p_parts, p_state = _prompt_layer(xp, mem, lw, mem_norm, w_mem_kv, k_norm_mem)
    s_parts, s_state = _sample_layer(xs, cache_k, cache_v, state, conv_state, mem_k, mem_v, lw)
    yp, ys = _finish(lw, p_parts, s_parts)
    return yp.reshape(xp.shape), ys.reshape(xs.shape), p_state, s_state


def kernel(x_prompt, x_sample, cache_swa_k, cache_swa_v, state_gdn, state_conv, cache_mem_k, cache_mem_v, mem_prompt, ln1_gain, w_in, q_norm_swa, k_norm_swa, swa_sinks, conv_w, a_log, dt_bias, gdn_norm, q_norm_mem, k_norm_mem, mem_norm, w_mem_kv, w_branch, w_out, ln2_gain, w_router, b_router, w_mlp1, b_mlp1, w_mlp2, b_mlp2):
    xp, xs = x_prompt, x_sample
    outs = [[] for _ in range(10)]
    for l in range(ln1_gain.shape[0]):
        lw = _layer_weights(ln1_gain[l], w_in[l], q_norm_swa[l], k_norm_swa[l], swa_sinks[l], conv_w[l], a_log[l],
                            dt_bias[l], gdn_norm[l], q_norm_mem[l], w_branch[l], w_out[l], ln2_gain[l], w_router[l],
                            b_router[l], w_mlp1, b_mlp1, w_mlp2, b_mlp2, l)
        xp, xs, (pk, pv, ps, pc, mk, mv), (sk, sv, ss, sc) = _layer(
            lw, xp, mem_prompt, mem_norm[l], w_mem_kv[l], k_norm_mem[l],
            xs, cache_swa_k[l], cache_swa_v[l], state_gdn[l], state_conv[l], cache_mem_k[l], cache_mem_v[l])
        for acc, v in zip(outs, (pk, pv, sk, sv, ps, ss, pc, sc, mk, mv)):
            acc.append(v)
    return (xp, xs) + tuple(jnp.stack(o) for o in outs)
```

```python
import functools

import numpy as np
import jax
import jax.numpy as jnp
from jax import lax
from jax.experimental import pallas as pl
from jax.experimental.pallas import tpu as pltpu

F32 = jnp.float32
BF16 = jnp.bfloat16
I32 = jnp.int32

D_MODEL = 1024
BRANCH_W = 512
SWA_HEADS = 8
SWA_KV_HEADS = 2
SWA_HEAD_DIM = 64
SWA_GROUP = SWA_HEADS // SWA_KV_HEADS
SWA_KV_W = SWA_KV_HEADS * SWA_HEAD_DIM
WINDOW = 128
GDN_HEADS = 4
GDN_HEAD_DIM = 128
GDN_W = GDN_HEADS * GDN_HEAD_DIM
GDN_CONV_W = 3 * GDN_W
GDN_CHUNK = 64
CONV_WIDTH = 4
MEM_LEN = 256
MEM_HEADS = 4
MEM_HEAD_DIM = 128
N_BRANCH = 3
N_EXPERTS = 32
TOP_K = 4
D_FF = 1024
SWIGLU_LIMIT = 7.0
SWIGLU_ALPHA = 1.702
EPS = 1e-6

LANES = 128
MOE_BM = 512
MOE_TOK_TILE = 256
MOE_UNIT = 8
TOK_TILE = 128
MERGE_ROWS = 512
VMEM_LIMIT = 56 * 1024 * 1024

_SEG_QS = (0, 512)
_SEG_KS = (512, 128)
_SEG_VS = (640, 128)
_SEG_QKVG = (768, 1536)
_SEG_Z = (2304, 512)
_SEG_AB = (2816, 128)
_SEG_QM = (2944, 512)
_SEG_GATE = (3456, 3072)
_PACKED_W = 6528
_IN_MAIN = 2816
_IN_AB = 8

_NEG = -1e30
_HI = lax.Precision.HIGHEST


def _cparams(sem, vmem=VMEM_LIMIT):
    return pltpu.CompilerParams(dimension_semantics=sem, vmem_limit_bytes=vmem)


def _bdot(a, b):
    return jnp.dot(a.astype(BF16), b.astype(BF16), preferred_element_type=F32)


def _bdot_nt(a, b):
    return lax.dot_general(a.astype(BF16), b.astype(BF16), (((1,), (1,)), ((), ())),
                           preferred_element_type=F32)


def _hdot_nt(a, b):
    return lax.dot_general(a, b, (((1,), (1,)), ((), ())), preferred_element_type=F32, precision=_HI)


def _rms(x, gain):
    return x * lax.rsqrt(jnp.mean(x * x, axis=-1, keepdims=True) + EPS) * gain


def _l2(x):
    return x * lax.rsqrt(jnp.sum(x * x, axis=-1, keepdims=True) + EPS)


def _sigmoid(x):
    return 0.5 * jnp.tanh(0.5 * x) + 0.5


def _silu(x):
    return x * _sigmoid(x)


def _softplus(x):
    return jnp.maximum(x, 0.0) + jnp.log1p(jnp.exp(-jnp.abs(x)))


_HALO = 8


def _inproj_kernel(seq_steps, x_ref, g_ref, w_ref, cw_ref, qs, ks, vs, qkvg, z, ab, qm, gate, *conv_refs):
    i = pl.program_id(0)
    x = x_ref[...]
    hb = _rms(x, g_ref[...]).astype(BF16)

    def proj(off, width):
        return jnp.dot(hb, w_ref[:, off:off + width], preferred_element_type=F32)

    step = 512

    def plain(ref, c0=0, width=None):
        def finish(r):
            ref[:, c0:c0 + (width or r.shape[1])] = r
        return finish

    def gate_chunk(c0):
        def finish(r):
            gate[:, c0:c0 + step] = _sigmoid(r).astype(BF16)
        return finish

    def conv_part(part):
        tail_ref, carry = conv_refs
        c0 = part * GDN_W
        cols = slice(c0, c0 + GDN_W)

        def finish(r):
            first = (i % seq_steps) == 0
            sub = lax.broadcasted_iota(I32, (_HALO, 1), 0)
            rows = r.shape[0]
            before = jnp.where(first, 0.0, carry[:, cols])
            y = r * cw_ref[CONV_WIDTH - 1:CONV_WIDTH, cols]
            top = r[0:_HALO] * cw_ref[CONV_WIDTH - 1:CONV_WIDTH, cols]
            for jj in range(CONV_WIDTH - 1):
                sh = CONV_WIDTH - 1 - jj
                y = y + pltpu.roll(r, sh, 0) * cw_ref[jj:jj + 1, cols]
                head_rows = jnp.where(sub < sh, pltpu.roll(before, sh, 0), pltpu.roll(r[0:_HALO], sh, 0))
                top = top + head_rows * cw_ref[jj:jj + 1, cols]
            carry[:, cols] = r[rows - _HALO:]
            tail_ref[0, :, cols] = r[rows - _HALO:]
            act = _silu(jnp.concatenate([top, y[_HALO:]], axis=0))
            if part < 2:
                scale = GDN_HEAD_DIM ** -0.5 if part == 0 else 1.0
                for h in range(GDN_HEADS):
                    hs = slice(h * GDN_HEAD_DIM, (h + 1) * GDN_HEAD_DIM)
                    qkvg[:, c0 + h * GDN_HEAD_DIM:c0 + (h + 1) * GDN_HEAD_DIM] = _l2(act[:, hs]) * scale
            else:
                qkvg[:, cols] = act
        return finish

    tasks = [(_SEG_QS[0], _SEG_QS[1], plain(qs)), (_SEG_KS[0], _SEG_KS[1], plain(ks)),
             (_SEG_VS[0], _SEG_VS[1], plain(vs)), (_SEG_AB[0], _SEG_AB[1], plain(ab)),
             (_SEG_QM[0], _SEG_QM[1], plain(qm))]
    tasks += [(_SEG_GATE[0] + c0, step, gate_chunk(c0)) for c0 in range(0, _SEG_GATE[1], step)]
    if seq_steps is None:
        tasks += [(_SEG_Z[0], _SEG_Z[1], plain(z))]
        tasks += [(_SEG_QKVG[0] + c0, step, plain(qkvg, c0, step)) for c0 in range(0, GDN_CONV_W, step)]
    else:
        def silu_z(r):
            z[...] = _silu(r)
        tasks += [(_SEG_QKVG[0] + p * GDN_W, GDN_W, conv_part(p)) for p in range(3)]
        tasks.append((_SEG_Z[0], _SEG_Z[1], silu_z))
    for off, width, finish in tasks:
        finish(proj(off, width))


def _inproj(x2, gain, wp, conv_w, tm, seq_len=None):
    t = x2.shape[0]
    segs = (_SEG_QS, _SEG_KS, _SEG_VS, _SEG_QKVG, _SEG_Z, _SEG_AB, _SEG_QM, _SEG_GATE)
    dtypes = (F32,) * 7 + (BF16,)
    out_shape = [jax.ShapeDtypeStruct((t, w), dt) for (_, w), dt in zip(segs, dtypes)]
    out_specs = [pl.BlockSpec((tm, w), lambda i: (i, 0)) for _, w in segs]
    scratch = []
    seq_steps = None
    if seq_len is not None:
        seq_steps = seq_len // tm
        out_shape.append(jax.ShapeDtypeStruct((t // seq_len, _HALO, GDN_CONV_W), F32))
        out_specs.append(pl.BlockSpec((1, _HALO, GDN_CONV_W), lambda i: (i // seq_steps, 0, 0)))
        scratch.append(pltpu.VMEM((_HALO, GDN_CONV_W), F32))
    return pl.pallas_call(
        functools.partial(_inproj_kernel, seq_steps),
        out_shape=out_shape,
        grid=(t // tm,),
        in_specs=[pl.BlockSpec((tm, D_MODEL), lambda i: (i, 0)),
                  pl.BlockSpec((1, D_MODEL), lambda i: (0, 0)),
                  pl.BlockSpec((D_MODEL, _PACKED_W), lambda i: (0, 0), pipeline_mode=pl.Buffered(1)),
                  pl.BlockSpec((CONV_WIDTH, GDN_CONV_W), lambda i: (0, 0))],
        out_specs=out_specs,
        scratch_shapes=scratch,
        compiler_params=_cparams(("arbitrary",)),
        name="inproj",
    )(x2, gain, wp, conv_w)


SWA_STEP_BLOCKS = 8


def _alibi_slopes(n):
    return [float(2.0 ** (-8.0 * (i + 1) / n)) for i in range(n)]


def _swa_prompt_kernel(q_ref, kc_ref, vc_ref, qg_ref, kg_ref, sink_ref, o_ref, kwin_ref, bias, kprev, vprev):
    n = pl.program_id(1)
    half = SWA_HEAD_DIM
    nblk = q_ref.shape[1] // WINDOW
    work = [(j, h) for j in range(nblk) for h in range(SWA_HEADS)]

    @pl.when(n == 0)
    def _():
        kprev[...] = jnp.zeros_like(kprev)
        vprev[...] = jnp.zeros_like(vprev)

    def fill(tbl, first):
        row = lax.broadcasted_iota(I32, (WINDOW, 2 * WINDOW), 0)
        col = lax.broadcasted_iota(I32, (WINDOW, 2 * WINDOW), 1)
        dist = row + WINDOW - col
        valid = (dist >= 0) & (dist <= WINDOW) & ((col >= WINDOW) | jnp.logical_not(first))
        distf = dist.astype(F32)
        for h, slope in enumerate(_alibi_slopes(SWA_HEADS)):
            bias[tbl, h] = jnp.where(valid, -slope * distf, _NEG)

    @pl.when(n <= 1)
    def _():
        fill(0, n == 0)

    if nblk > 1:
        @pl.when(n == 0)
        def _():
            fill(1, False)

    low = lax.broadcasted_iota(I32, (1, 2 * half), 1) < half

    def pair_rms(x, gain):
        sq = x * x
        s_lo = jnp.sum(jnp.where(low, sq, 0.0), axis=-1, keepdims=True)
        s_hi = jnp.sum(jnp.where(low, 0.0, sq), axis=-1, keepdims=True)
        return x * lax.rsqrt(jnp.where(low, s_lo, s_hi) * (1.0 / half) + EPS) * gain

    kcn = pair_rms(kc_ref[0], kg_ref[...])
    kwin_ref[0] = kcn[(nblk - 1) * WINDOW:]
    kall = jnp.concatenate([kprev[...], kcn], axis=0)
    vall = jnp.concatenate([vprev[...], vc_ref[0]], axis=0)
    kprev[...] = kcn[(nblk - 1) * WINDOW:]
    vprev[...] = vc_ref[0, (nblk - 1) * WINDOW:, :]
    ones = jnp.ones(((nblk + 1) * WINDOW, 2 * half), BF16)

    def placed(x, aug):
        sw = pltpu.roll(x, half, 1)
        out = {(0, 0): jnp.where(low, x, 0.0), (1, 1): jnp.where(low, 0.0, x),
               (0, 1): jnp.where(low, 0.0, sw), (1, 0): jnp.where(low, sw, 0.0)}
        out = {key: val.astype(BF16) for key, val in out.items()}
        return {key: jnp.concatenate([val, ones], axis=1) for key, val in out.items()} if aug else out

    kvar = placed(kall, False)
    vvar = placed(vall, True)
    qn = [pair_rms(q_ref[0, :, t * 2 * half:(t + 1) * 2 * half], qg_ref[...]).astype(BF16)
          for t in range(SWA_HEADS // 2)]
    where_of = lambda h: (h // SWA_GROUP, h % 2)
    qrows = lambda j: slice(j * WINDOW, (j + 1) * WINDOW)
    krows = lambda j: slice(j * WINDOW, (j + 2) * WINDOW)
    nt = (((1,), (1,)), ((), ()))
    s = {(j, h): lax.dot_general(qn[h // 2][qrows(j)], kvar[where_of(h)][krows(j)], nt, preferred_element_type=F32)
         + bias[min(j, 1), h] for j, h in work}
    sink = [sink_ref[0:1, h:h + 1] for h in range(SWA_HEADS)]
    m = {(j, h): jnp.maximum(jnp.max(s[j, h], axis=-1, keepdims=True), sink[h]) for j, h in work}
    p = {(j, h): jnp.exp(s[j, h] - m[j, h]).astype(BF16) for j, h in work}
    res = {(j, h): jnp.dot(p[j, h], vvar[where_of(h)][krows(j)], preferred_element_type=F32)
           for j, h in work}
    inv = {(j, h): 1.0 / (res[j, h][:, 2 * half:2 * half + 1] + jnp.exp(sink[h] - m[j, h])) for j, h in work}
    for j in range(nblk):
        for t in range(SWA_HEADS // 2):
            h0, h1 = 2 * t, 2 * t + 1
            num = res[j, h0][:, :2 * half] + res[j, h1][:, :2 * half]
            o_ref[0, qrows(j), t * 2 * half:(t + 1) * 2 * half] = num * jnp.where(low, inv[j, h0], inv[j, h1])


def _swa_prompt(qs, ks, vs, qn, kn, sinks):
    b, l, _ = qs.shape
    rows = SWA_STEP_BLOCKS * WINDOW if l % (SWA_STEP_BLOCKS * WINDOW) == 0 else WINDOW
    cur = lambda i, j: (i, j, 0)
    const2 = lambda i, j: (0, 0)
    q_gain = jnp.tile(qn, (1, 2)) * (SWA_HEAD_DIM ** -0.5)
    k_gain = jnp.tile(kn, (1, 2))
    return pl.pallas_call(
        _swa_prompt_kernel,
        out_shape=[jax.ShapeDtypeStruct((b, l, BRANCH_W), F32),
                   jax.ShapeDtypeStruct((b, WINDOW, SWA_KV_W), F32)],
        grid=(b, l // rows),
        in_specs=[pl.BlockSpec((1, rows, BRANCH_W), cur),
                  pl.BlockSpec((1, rows, SWA_KV_W), cur),
                  pl.BlockSpec((1, rows, SWA_KV_W), cur),
                  pl.BlockSpec((1, 2 * SWA_HEAD_DIM), const2),
                  pl.BlockSpec((1, 2 * SWA_HEAD_DIM), const2),
                  pl.BlockSpec((1, SWA_HEADS), const2)],
        out_specs=[pl.BlockSpec((1, rows, BRANCH_W), cur),
                   pl.BlockSpec((1, WINDOW, SWA_KV_W), lambda i, j: (i, 0, 0))],
        scratch_shapes=[pltpu.VMEM((2, SWA_HEADS, WINDOW, 2 * WINDOW), F32),
                        pltpu.VMEM((WINDOW, SWA_KV_W), F32),
                        pltpu.VMEM((WINDOW, SWA_KV_W), F32)],
        compiler_params=_cparams(("arbitrary", "arbitrary")),
        name="swa_prompt",
    )(qs, ks, vs, q_gain, k_gain, sinks)


def _swa_decode_kernel(bs, q_ref, k3_ref, kf_ref, v3_ref, vf_ref, ck_ref, cv_ref, qn_ref, kn_ref, kn2_ref,
                       sink_ref, slope_ref, o_ref, ok_ref, ov_ref):
    scale = SWA_HEAD_DIM ** -0.5
    lane = lax.broadcasted_iota(I32, (1, SWA_KV_W), 1)
    rowi = lax.broadcasted_iota(I32, (WINDOW, SWA_KV_W), 0)
    keyd = (WINDOW - lax.broadcasted_iota(I32, (1, WINDOW), 1)).astype(F32)
    last = rowi == WINDOW - 1
    for b in range(bs):
        kf = kf_ref[b]
        sq = kf * kf
        ms0 = jnp.sum(jnp.where(lane < SWA_HEAD_DIM, sq, 0.0), axis=-1, keepdims=True) / SWA_HEAD_DIM
        ms1 = jnp.sum(jnp.where(lane >= SWA_HEAD_DIM, sq, 0.0), axis=-1, keepdims=True) / SWA_HEAD_DIM
        knf = kf * lax.rsqrt(jnp.where(lane < SWA_HEAD_DIM, ms0, ms1) + EPS) * kn2_ref[...]
        ok_ref[b] = jnp.where(last, knf, pltpu.roll(ck_ref[b], WINDOW - 1, 0))
        ov_ref[b] = jnp.where(last, vf_ref[b], pltpu.roll(cv_ref[b], WINDOW - 1, 0))
    work = [(b, g) for b in range(bs) for g in range(SWA_KV_HEADS)]
    lanes = lambda g: slice(g * SWA_HEAD_DIM, (g + 1) * SWA_HEAD_DIM)
    heads = lambda g: slice(g * SWA_GROUP, (g + 1) * SWA_GROUP)
    qn = [_rms(q_ref[b], qn_ref[...]) for b in range(bs)]
    kn3 = [_rms(k3_ref[b], kn_ref[...]) for b in range(bs)]
    qg = {(b, g): qn[b][heads(g)] for b, g in work}
    s = {(b, g): _bdot_nt(qg[b, g], ck_ref[b, :, lanes(g)]) * scale - slope_ref[heads(g)] * keyd for b, g in work}
    s_new = {(b, g): jnp.sum(qg[b, g] * kn3[b][g:g + 1], axis=-1, keepdims=True) * scale for b, g in work}
    m = {(b, g): jnp.maximum(jnp.maximum(jnp.max(s[b, g], axis=-1, keepdims=True), s_new[b, g]),
                             sink_ref[heads(g)]) for b, g in work}
    p = {w: jnp.exp(s[w] - m[w]) for w in work}
    p_new = {w: jnp.exp(s_new[w] - m[w]) for w in work}
    denom = {(b, g): jnp.sum(p[b, g], axis=-1, keepdims=True) + p_new[b, g] + jnp.exp(sink_ref[heads(g)] - m[b, g])
             for b, g in work}
    pv = {(b, g): _bdot(p[b, g], cv_ref[b, :, lanes(g)]) for b, g in work}
    for b, g in work:
        o_ref[b, heads(g), :] = (pv[b, g] + p_new[b, g] * v3_ref[b, g:g + 1, :]) / denom[b, g]


def _swa_decode(qs, ks, vs, cache_k, cache_v, qn, kn, sinks, bs=8):
    b = qs.shape[0]
    q3 = qs.reshape(b, SWA_HEADS, SWA_HEAD_DIM)
    k3 = ks.reshape(b, SWA_KV_HEADS, SWA_HEAD_DIM)
    kf = ks.reshape(b, 1, SWA_KV_W)
    v3 = vs.reshape(b, SWA_KV_HEADS, SWA_HEAD_DIM)
    vf = vs.reshape(b, 1, SWA_KV_W)
    ck = cache_k.reshape(b, WINDOW, SWA_KV_W)
    cv = cache_v.reshape(b, WINDOW, SWA_KV_W)
    kn2 = jnp.concatenate([kn, kn], axis=-1)
    sink_col = sinks.reshape(SWA_HEADS, 1)
    slope_col = jnp.asarray(np.asarray(_alibi_slopes(SWA_HEADS), np.float32).reshape(SWA_HEADS, 1))
    blk = lambda *shape: pl.BlockSpec((bs,) + shape, lambda i: (i,) + (0,) * len(shape))
    full = lambda *shape: pl.BlockSpec(shape, lambda i: (0,) * len(shape))
    o, ok, ov = pl.pallas_call(
        functools.partial(_swa_decode_kernel, bs),
        out_shape=[jax.ShapeDtypeStruct((b, SWA_HEADS, SWA_HEAD_DIM), F32),
                   jax.ShapeDtypeStruct((b, WINDOW, SWA_KV_W), F32),
                   jax.ShapeDtypeStruct((b, WINDOW, SWA_KV_W), F32)],
        grid=(b // bs,),
        in_specs=[blk(SWA_HEADS, SWA_HEAD_DIM), blk(SWA_KV_HEADS, SWA_HEAD_DIM), blk(1, SWA_KV_W),
                  blk(SWA_KV_HEADS, SWA_HEAD_DIM), blk(1, SWA_KV_W), blk(WINDOW, SWA_KV_W), blk(WINDOW, SWA_KV_W),
                  full(1, SWA_HEAD_DIM), full(1, SWA_HEAD_DIM), full(1, SWA_KV_W),
                  full(SWA_HEADS, 1), full(SWA_HEADS, 1)],
        out_specs=[blk(SWA_HEADS, SWA_HEAD_DIM), blk(WINDOW, SWA_KV_W), blk(WINDOW, SWA_KV_W)],
        compiler_params=_cparams(("parallel",)),
        name="swa_decode",
    )(q3, k3, kf, v3, vf, ck, cv, qn, kn, kn2, sink_col, slope_col)
    return o.reshape(b, BRANCH_W), ok, ov


def _gate_rows(ab, alog_ref, dtb_ref):
    g = -jnp.exp(alog_ref[...]) * _softplus(ab + dtb_ref[...])
    return g, _sigmoid(ab)


GDN_PREP_CHUNKS = 4
GDN_SCAN_BATCH = 8


def _gdn_prep_kernel(ys, ab_ref, alog_ref, dtb_ref, u_ref, w_ref, qg_ref, kd_ref, in_ref, eg_ref):
    C = GDN_CHUNK
    d = GDN_HEAD_DIM
    n = GDN_PREP_CHUNKS * C
    heads = range(GDN_HEADS)
    g_all, beta_all = _gate_rows(ab_ref[0], alog_ref, dtb_ref)
    r = lax.broadcasted_iota(I32, (n, n), 0)
    cc = lax.broadcasted_iota(I32, (n, n), 1)
    same = (r // C) == (cc // C)
    incl = same & (r >= cc)
    strict = same & (r > cc)
    upto = same & (r <= cc)
    eye = r == cc
    q = [ys[0, :, h * d:(h + 1) * d] for h in heads]
    k = [ys[0, :, GDN_W + h * d:GDN_W + (h + 1) * d] for h in heads]
    beta = [beta_all[:, GDN_HEADS + h:GDN_HEADS + h + 1] for h in heads]
    gc_row = [jnp.sum(jnp.where(upto, g_all[:, h:h + 1], 0.0), axis=0, keepdims=True) for h in heads]
    gc_col = [jnp.sum(jnp.where(eye, gc_row[h], 0.0), axis=1, keepdims=True) for h in heads]
    decay = [jnp.where(incl, jnp.exp(jnp.where(incl, gc_col[h] - gc_row[h], 0.0)), 0.0) for h in heads]
    kb = [k[h] * beta[h] for h in heads]
    bp = [-jnp.where(strict, _bdot_nt(kb[h], k[h]) * decay[h], 0.0) for h in heads]
    intra = [jnp.where(incl, _bdot_nt(q[h], k[h]) * decay[h], 0.0) for h in heads]
    p = [eye.astype(F32) + bp[h] for h in heads]
    span = 2
    while span < C:
        bp = [_bdot(bp[h], bp[h]) for h in heads]
        p = [p[h] + _bdot(p[h], bp[h]) for h in heads]
        span *= 2
    uw = [_bdot(p[h], jnp.concatenate([ys[0, :, 2 * GDN_W + h * d:2 * GDN_W + (h + 1) * d] * beta[h],
                                       kb[h] * jnp.exp(gc_col[h])], axis=1)) for h in heads]
    for h in heads:
        hs = slice(h * d, (h + 1) * d)
        u_ref[0, :, hs] = uw[h][:, :d]
        w_ref[0, :, hs] = uw[h][:, d:].astype(BF16)
        qg_ref[0, :, hs] = (q[h] * jnp.exp(gc_col[h])).astype(BF16)
        for ci in range(GDN_PREP_CHUNKS):
            r0 = ci * C
            g_last = gc_row[h][:, r0 + C - 1:r0 + C]
            kd_ref[0, r0:r0 + C, hs] = (k[h][r0:r0 + C] * jnp.exp(g_last - gc_col[h][r0:r0 + C])).astype(BF16)
            in_ref[0, r0:r0 + C, h * C:(h + 1) * C] = intra[h][r0:r0 + C, r0:r0 + C].astype(BF16)
            eg_ref[0, ci, h:h + 1, :] = jnp.broadcast_to(jnp.exp(g_last), (1, LANES))


def _gdn_scan_kernel(u_ref, w_ref, qg_ref, kd_ref, in_ref, eg_ref, z_ref, gn_ref, o_ref, s_ref, st):
    c = pl.program_id(1)
    C = GDN_CHUNK
    d = GDN_HEAD_DIM

    @pl.when(c == 0)
    def _():
        st[...] = jnp.zeros_like(st)

    chains = [(bi, h) for bi in range(u_ref.shape[0]) for h in range(GDN_HEADS)]
    hsl = lambda h: slice(h * d, (h + 1) * d)
    s_f = [st[bi, h] for bi, h in chains]
    s_b = [s.astype(BF16) for s in s_f]
    ws = [jnp.dot(w_ref[bi, :, hsl(h)], s_b[i], preferred_element_type=F32) for i, (bi, h) in enumerate(chains)]
    qs = [jnp.dot(qg_ref[bi, :, hsl(h)], s_b[i], preferred_element_type=F32) for i, (bi, h) in enumerate(chains)]
    vb = [(u_ref[bi, :, hsl(h)] - ws[i]).astype(BF16) for i, (bi, h) in enumerate(chains)]
    iv = [jnp.dot(in_ref[bi, :, h * C:(h + 1) * C], vb[i], preferred_element_type=F32)
          for i, (bi, h) in enumerate(chains)]
    kv = [lax.dot_general(kd_ref[bi, :, hsl(h)], vb[i], (((0,), (0,)), ((), ())), preferred_element_type=F32)
          for i, (bi, h) in enumerate(chains)]
    for i, (bi, h) in enumerate(chains):
        st[bi, h] = s_f[i] * eg_ref[bi, 0, h:h + 1, :] + kv[i]
        o_ref[bi, :, hsl(h)] = _rms(qs[i] + iv[i], gn_ref[...]) * z_ref[bi, :, hsl(h)]
    s_ref[...] = st[...]


def _gdn_prompt(qkvg, z, ab, alog, dtb, gnorm):
    b, l, _ = qkvg.shape
    C = GDN_CHUNK
    nc = l // C
    rows = GDN_PREP_CHUNKS * C
    cur = lambda i, j: (i, j, 0)
    const2 = lambda i, j: (0, 0)
    u, w, qg, kd, intra, eg = pl.pallas_call(
        _gdn_prep_kernel,
        out_shape=[jax.ShapeDtypeStruct((b, l, GDN_W), F32),
                   jax.ShapeDtypeStruct((b, l, GDN_W), BF16),
                   jax.ShapeDtypeStruct((b, l, GDN_W), BF16),
                   jax.ShapeDtypeStruct((b, l, GDN_W), BF16),
                   jax.ShapeDtypeStruct((b, l, GDN_HEADS * C), BF16),
                   jax.ShapeDtypeStruct((b, nc, GDN_HEADS, LANES), F32)],
        grid=(b, l // rows),
        in_specs=[pl.BlockSpec((1, rows, GDN_CONV_W), cur),
                  pl.BlockSpec((1, rows, LANES), cur),
                  pl.BlockSpec((1, LANES), const2),
                  pl.BlockSpec((1, LANES), const2)],
        out_specs=[pl.BlockSpec((1, rows, GDN_W), cur),
                   pl.BlockSpec((1, rows, GDN_W), cur),
                   pl.BlockSpec((1, rows, GDN_W), cur),
                   pl.BlockSpec((1, rows, GDN_W), cur),
                   pl.BlockSpec((1, rows, GDN_HEADS * C), cur),
                   pl.BlockSpec((1, GDN_PREP_CHUNKS, GDN_HEADS, LANES), lambda i, j: (i, j, 0, 0))],
        compiler_params=_cparams(("parallel", "parallel")),
        name="gdn_prep",
    )(qkvg, ab, alog, dtb)
    bb = GDN_SCAN_BATCH if b % GDN_SCAN_BATCH == 0 else 1
    seq = lambda wd: pl.BlockSpec((bb, C, wd), cur)
    return pl.pallas_call(
        _gdn_scan_kernel,
        out_shape=[jax.ShapeDtypeStruct((b, l, GDN_W), F32),
                   jax.ShapeDtypeStruct((b, GDN_HEADS, GDN_HEAD_DIM, GDN_HEAD_DIM), F32)],
        grid=(b // bb, nc),
        in_specs=[seq(GDN_W), seq(GDN_W), seq(GDN_W), seq(GDN_W), seq(GDN_HEADS * C),
                  pl.BlockSpec((bb, 1, GDN_HEADS, LANES), lambda i, j: (i, j, 0, 0)),
                  seq(GDN_W),
                  pl.BlockSpec((1, GDN_HEAD_DIM), const2)],
        out_specs=[seq(GDN_W),
                   pl.BlockSpec((bb, GDN_HEADS, GDN_HEAD_DIM, GDN_HEAD_DIM), lambda i, j: (i, 0, 0, 0))],
        scratch_shapes=[pltpu.VMEM((bb, GDN_HEADS, GDN_HEAD_DIM, GDN_HEAD_DIM), F32)],
        compiler_params=_cparams(("parallel", "arbitrary")),
        name="gdn_scan",
    )(u, w, qg, kd, intra, eg, z, gnorm)


def _gdn_decode_kernel(bs, x_ref, cs_ref, z_ref, ab_ref, s_ref, cw_ref, alog_ref, dtb_ref, gn_ref, o_ref, so_ref):
    x = x_ref[...]
    y = x * cw_ref[CONV_WIDTH - 1:CONV_WIDTH, :]
    for j in range(CONV_WIDTH - 1):
        y = y + cs_ref[:, j, :] * cw_ref[j:j + 1, :]
    y = _silu(y)
    g_all, beta_all = _gate_rows(ab_ref[...], alog_ref, dtb_ref)
    eg_all = jnp.exp(g_all)
    z = z_ref[...]
    d = GDN_HEAD_DIM
    eye = (lax.broadcasted_iota(I32, (d, d), 0) == lax.broadcasted_iota(I32, (d, d), 1)).astype(F32)
    for h in range(GDN_HEADS):
        hs = slice(h * d, (h + 1) * d)
        qh = _l2(y[:, h * d:(h + 1) * d]) * (d ** -0.5)
        kh = _l2(y[:, GDN_W + h * d:GDN_W + (h + 1) * d])
        vh = y[:, 2 * GDN_W + h * d:2 * GDN_W + (h + 1) * d]
        kt = _hdot_nt(eye, kh)
        qt = _hdot_nt(eye, qh)
        rng = range(bs)
        eg = [eg_all[b:b + 1, h:h + 1] for b in rng]
        beta = [beta_all[b:b + 1, GDN_HEADS + h:GDN_HEADS + h + 1] for b in rng]
        sk = [jnp.sum(s_ref[b, h] * kt[:, b:b + 1], axis=0, keepdims=True) for b in rng]
        v_new = [beta[b] * (vh[b:b + 1] - eg[b] * sk[b]) for b in rng]
        s_new = [eg[b] * s_ref[b, h] + kt[:, b:b + 1] * v_new[b] for b in rng]
        for b in rng:
            so_ref[b, h] = s_new[b]
        o = jnp.concatenate([jnp.sum(s_new[b] * qt[:, b:b + 1], axis=0, keepdims=True) for b in rng], axis=0)
        o_ref[:, hs] = _rms(o, gn_ref[...]) * _silu(z[:, hs])


def _gdn_decode(qkvg, z, ab, state, conv_state, conv_w, alog, dtb, gnorm, bs=8):
    b = qkvg.shape[0]
    row = lambda w: pl.BlockSpec((bs, w), lambda i: (i, 0))
    full = lambda *shape: pl.BlockSpec(shape, lambda i: (0,) * len(shape))
    sspec = pl.BlockSpec((bs, GDN_HEADS, GDN_HEAD_DIM, GDN_HEAD_DIM), lambda i: (i, 0, 0, 0))
    return pl.pallas_call(
        functools.partial(_gdn_decode_kernel, bs),
        out_shape=[jax.ShapeDtypeStruct((b, GDN_W), F32),
                   jax.ShapeDtypeStruct(state.shape, F32)],
        grid=(b // bs,),
        in_specs=[row(GDN_CONV_W),
                  pl.BlockSpec((bs, CONV_WIDTH - 1, GDN_CONV_W), lambda i: (i, 0, 0)),
                  row(GDN_W), row(LANES), sspec,
                  full(CONV_WIDTH, GDN_CONV_W), full(1, LANES), full(1, LANES), full(1, GDN_HEAD_DIM)],
        out_specs=[row(GDN_W), sspec],
        compiler_params=_cparams(("parallel",)),
        name="gdn_decode",
    )(qkvg, conv_state, z, ab, state, conv_w, alog, dtb, gnorm)


def _memkv_kernel(x_ref, g_ref, w_ref, kn_ref, k_ref, v_ref):
    hb = _rms(x_ref[...], g_ref[...]).astype(BF16)
    for hd in range(MEM_HEADS):
        hs = slice(hd * MEM_HEAD_DIM, (hd + 1) * MEM_HEAD_DIM)
        k_ref[:, hs] = _rms(jnp.dot(hb, w_ref[:, hs], preferred_element_type=F32), kn_ref[...])
    v_ref[...] = jnp.dot(hb, w_ref[:, BRANCH_W:], preferred_element_type=F32)


def _memkv(mem2, gain, w_kv, kn, tm=512):
    t = mem2.shape[0]
    return pl.pallas_call(
        _memkv_kernel,
        out_shape=[jax.ShapeDtypeStruct((t, BRANCH_W), F32)] * 2,
        grid=(t // tm,),
        in_specs=[pl.BlockSpec((tm, D_MODEL), lambda i: (i, 0)),
                  pl.BlockSpec((1, D_MODEL), lambda i: (0, 0)),
                  pl.BlockSpec((D_MODEL, 2 * BRANCH_W), lambda i: (0, 0)),
                  pl.BlockSpec((1, MEM_HEAD_DIM), lambda i: (0, 0))],
        out_specs=[pl.BlockSpec((tm, BRANCH_W), lambda i: (i, 0))] * 2,
        compiler_params=_cparams(("parallel",)),
        name="memkv",
    )(mem2, gain, w_kv, kn)


def _memattn_prompt_kernel(q_ref, k_ref, v_ref, qn_ref, o_ref):
    q = q_ref[0]
    k = k_ref[0]
    v = v_ref[0]
    scale = MEM_HEAD_DIM ** -0.5
    for hd in range(MEM_HEADS):
        hs = slice(hd * MEM_HEAD_DIM, (hd + 1) * MEM_HEAD_DIM)
        s = _bdot_nt(_rms(q[:, hs], qn_ref[...]), k[:, hs]) * scale
        p = jnp.exp(s - jnp.max(s, axis=-1, keepdims=True))
        o_ref[0, :, hs] = _bdot(p, v[:, hs]) / jnp.sum(p, axis=-1, keepdims=True)


def _memattn_prompt(qm, mk, mv, qn):
    b, l, _ = qm.shape
    tq = 512 if l % 512 == 0 else WINDOW
    return pl.pallas_call(
        _memattn_prompt_kernel,
        out_shape=jax.ShapeDtypeStruct((b, l, BRANCH_W), F32),
        grid=(b, l // tq),
        in_specs=[pl.BlockSpec((1, tq, BRANCH_W), lambda i, j: (i, j, 0)),
                  pl.BlockSpec((1, MEM_LEN, BRANCH_W), lambda i, j: (i, 0, 0)),
                  pl.BlockSpec((1, MEM_LEN, BRANCH_W), lambda i, j: (i, 0, 0)),
                  pl.BlockSpec((1, MEM_HEAD_DIM), lambda i, j: (0, 0))],
        out_specs=pl.BlockSpec((1, tq, BRANCH_W), lambda i, j: (i, j, 0)),
        compiler_params=_cparams(("parallel", "parallel")),
        name="memattn_prompt",
    )(qm, mk, mv, qn)


def _memattn_decode_kernel(bs, q_ref, k_ref, v_ref, qn_ref, o_ref):
    scale = MEM_HEAD_DIM ** -0.5
    for b in range(bs):
        qn = _rms(q_ref[b], qn_ref[...])
        s = jnp.sum(k_ref[b] * qn, axis=-1, keepdims=True) * scale
        p = jnp.exp(s - jnp.max(s, axis=0, keepdims=True))
        den = jnp.sum(p, axis=0)
        o_ref[b] = jnp.sum(p * v_ref[b], axis=0) / den


def _memattn_decode(qm, ck, cv, qn, bs=8):
    b = qm.shape[0]
    q3 = pl.BlockSpec((bs, MEM_HEADS, MEM_HEAD_DIM), lambda i: (i, 0, 0))
    kv = pl.BlockSpec((bs, MEM_LEN, MEM_HEADS, MEM_HEAD_DIM), lambda i: (i, 0, 0, 0))
    o = pl.pallas_call(
        functools.partial(_memattn_decode_kernel, bs),
        out_shape=jax.ShapeDtypeStruct((b, MEM_HEADS, MEM_HEAD_DIM), F32),
        grid=(b // bs,),
        in_specs=[q3, kv, kv, pl.BlockSpec((1, MEM_HEAD_DIM), lambda i: (0, 0))],
        out_specs=q3,
        compiler_params=_cparams(("parallel",)),
        name="memattn_decode",
    )(qm.reshape(b, MEM_HEADS, MEM_HEAD_DIM), ck, cv, qn)
    return o.reshape(b, BRANCH_W)


def _merge_kernel(oa_ref, ob_ref, oc_ref, gate_ref, x_ref, wb_ref, wo_ref, g2_ref, wr_ref, br_ref,
                  x1_ref, h2_ref, lg_ref):
    acc = None
    for i, o_ref in enumerate((oa_ref, ob_ref, oc_ref)):
        mixed = jnp.dot(o_ref[...].astype(BF16), wb_ref[i], preferred_element_type=F32)
        term = gate_ref[:, i * D_MODEL:(i + 1) * D_MODEL].astype(F32) * mixed
        acc = term if acc is None else acc + term
    x1 = x_ref[...] + jnp.dot(acc.astype(BF16), wo_ref[...], preferred_element_type=F32)
    x1_ref[...] = x1
    h2 = _rms(x1, g2_ref[...])
    h2_ref[...] = h2.astype(BF16)
    h_hi = h2.astype(BF16)
    h_lo = (h2 - h_hi.astype(F32)).astype(BF16)
    nt = (((1,), (1,)), ((), ()))
    a = lax.dot_general(wr_ref[...], h_hi, nt, preferred_element_type=F32)
    b = lax.dot_general(wr_ref[0:N_EXPERTS, :], h_lo, nt, preferred_element_type=F32)
    lg_ref[...] = a[0:N_EXPERTS] + a[N_EXPERTS:] + b + br_ref[...]


def _merge_into_kernel(oa_ref, ob_ref, oc_ref, gate_ref, x_ref, wb_ref, wo_ref, g2_ref, wr_ref, br_ref,
                       x1_in, h2_in, lg_in, x1_ref, h2_ref, lg_ref):
    del x1_in, h2_in, lg_in
    _merge_kernel(oa_ref, ob_ref, oc_ref, gate_ref, x_ref, wb_ref, wo_ref, g2_ref, wr_ref, br_ref,
                  x1_ref, h2_ref, lg_ref)


def _merge(oa, ob, oc, gate, x2, wb, wo, g2, wr_t, br_col, tm, t_out=None, into=None):
    t = x2.shape[0]
    t_out = t_out or t
    blk0 = 0 if into is None else (into[0].shape[0] - t) // tm
    row = lambda w: pl.BlockSpec((tm, w), lambda i: (i, 0))
    orow = lambda w: pl.BlockSpec((tm, w), lambda i: (i + blk0, 0))
    full = lambda *shape: pl.BlockSpec(shape, lambda i: (0,) * len(shape))
    in_specs = [row(BRANCH_W), row(BRANCH_W), row(BRANCH_W), row(N_BRANCH * D_MODEL), row(D_MODEL),
                full(N_BRANCH, BRANCH_W, D_MODEL), full(D_MODEL, D_MODEL), full(1, D_MODEL),
                full(2 * N_EXPERTS, D_MODEL), full(N_EXPERTS, 1)]
    args = [oa, ob, oc, gate, x2, wb, wo, g2, wr_t, br_col]
    aliases = {}
    if into is not None:
        t_out = into[0].shape[0]
        in_specs += [pl.BlockSpec(memory_space=pl.ANY)] * 3
        aliases = {len(args) + k: k for k in range(3)}
        args += list(into)
    return pl.pallas_call(
        _merge_kernel if into is None else _merge_into_kernel,
        out_shape=[jax.ShapeDtypeStruct((t_out, D_MODEL), F32),
                   jax.ShapeDtypeStruct((t_out, D_MODEL), BF16),
                   jax.ShapeDtypeStruct((N_EXPERTS, t_out), F32)],
        grid=(t // tm,),
        in_specs=in_specs,
        out_specs=[orow(D_MODEL), orow(D_MODEL), pl.BlockSpec((N_EXPERTS, tm), lambda i: (0, i + blk0))],
        input_output_aliases=aliases,
        compiler_params=_cparams(("parallel",)),
        name="merge",
    )(*args)


def _col_to_row(col):
    n = col.shape[0]
    r = lax.broadcasted_iota(I32, (n, n), 0)
    c = lax.broadcasted_iota(I32, (n, n), 1)
    return jnp.sum(jnp.where(r == c, col, 0.0), axis=0, keepdims=True)


def _row_to_col(row):
    n = row.shape[1]
    r = lax.broadcasted_iota(I32, (n, n), 0)
    c = lax.broadcasted_iota(I32, (n, n), 1)
    return jnp.sum(jnp.where(r == c, row, 0.0), axis=1, keepdims=True)


def _lane_pad(row):
    return jnp.concatenate([row, jnp.zeros((1, LANES - row.shape[1]), row.dtype)], axis=1)


def _route_kernel(lg_ref, pos_ref, gate_ref, meta_ref, cnt_ref, carry):
    i = pl.program_id(0)
    tn = lg_ref.shape[1]

    @pl.when(i == 0)
    def _():
        carry[...] = jnp.zeros_like(carry)

    l = lg_ref[...]
    eio = lax.broadcasted_iota(I32, (N_EXPERTS, tn), 0)
    hot = jnp.zeros((N_EXPERTS, tn), F32)
    vals, idxs = [], []
    for _ in range(TOP_K):
        m = jnp.max(l, axis=0, keepdims=True)
        idx = jnp.min(jnp.where(l == m, eio, N_EXPERTS), axis=0, keepdims=True)
        sel = eio == idx
        vals.append(m)
        idxs.append(idx)
        hot = hot + sel.astype(F32)
        l = jnp.where(sel, -jnp.inf, l)
    ex = [jnp.exp(v - vals[0]) for v in vals]
    tot = ex[0] + ex[1] + ex[2] + ex[3]
    before = (lax.broadcasted_iota(I32, (tn, tn), 0) < lax.broadcasted_iota(I32, (tn, tn), 1)).astype(BF16)
    within = jnp.dot(hot.astype(BF16), before, preferred_element_type=F32)
    cnt_col = jnp.sum(hot, axis=1, keepdims=True)
    room = jnp.floor((cnt_col + (MOE_UNIT - 1)) * (1.0 / MOE_UNIT)) * MOE_UNIT
    r = lax.broadcasted_iota(I32, (N_EXPERTS, N_EXPERTS), 0)
    c = lax.broadcasted_iota(I32, (N_EXPERTS, N_EXPERTS), 1)
    start_col = _row_to_col(jnp.sum(jnp.where(r < c, room, 0.0), axis=0, keepdims=True))
    for k in range(TOP_K):
        gate_ref[k:k + 1, :] = ex[k] / tot
        pos_ref[k:k + 1, :] = jnp.sum(jnp.where(eio == idxs[k], start_col + within, 0.0), axis=0,
                                      keepdims=True).astype(I32)
    meta_ref[0] = _lane_pad(jnp.concatenate([_col_to_row(room), _col_to_row(carry[:, 0:1])], axis=1)).astype(I32)
    carry[...] = carry[...] + room
    cnt_ref[...] = carry[...]


def _route(logits_t, tn):
    t = logits_t.shape[1]
    kt = pl.BlockSpec((TOP_K, tn), lambda i: (0, i))
    return pl.pallas_call(
        _route_kernel,
        out_shape=[jax.ShapeDtypeStruct((TOP_K, t), I32),
                   jax.ShapeDtypeStruct((TOP_K, t), F32),
                   jax.ShapeDtypeStruct((t // tn, 1, LANES), I32),
                   jax.ShapeDtypeStruct((N_EXPERTS, LANES), F32)],
        grid=(t // tn,),
        in_specs=[pl.BlockSpec((N_EXPERTS, tn), lambda i: (0, i))],
        out_specs=[kt, kt, pl.BlockSpec((1, 1, LANES), lambda i: (i, 0, 0)),
                   pl.BlockSpec((N_EXPERTS, LANES), lambda i: (0, 0))],
        scratch_shapes=[pltpu.VMEM((N_EXPERTS, LANES), F32)],
        compiler_params=_cparams(("arbitrary",)),
        name="moe_route",
    )(logits_t)


def _layout_kernel(bm, cnt_ref, be_ref, na_ref, ps_ref, pe_ref, ct_ref):
    nbp = be_ref.shape[1]
    cnt_col = cnt_ref[:, 0:1]
    size = jnp.floor((cnt_col + (bm - 1)) * (1.0 / bm)) * bm
    r = lax.broadcasted_iota(I32, (N_EXPERTS, N_EXPERTS), 0)
    c = lax.broadcasted_iota(I32, (N_EXPERTS, N_EXPERTS), 1)
    ends_row = jnp.sum(jnp.where(r <= c, size, 0.0), axis=0, keepdims=True)
    ends_col = _row_to_col(ends_row)
    nact = ends_row[:, N_EXPERTS - 1:N_EXPERTS] * (1.0 / bm)
    blk = jnp.minimum(lax.broadcasted_iota(I32, (1, nbp), 1).astype(F32), nact - 1.0)
    be = jnp.sum((ends_col <= blk * bm).astype(F32), axis=0, keepdims=True)
    be_ref[...] = jnp.minimum(be, N_EXPERTS - 1.0).astype(I32)
    na_ref[...] = jnp.broadcast_to(nact, na_ref.shape).astype(I32)
    ps_ref[...] = _lane_pad(ends_row - _col_to_row(size)).astype(I32)
    pe_ref[...] = _lane_pad(ends_row).astype(I32)
    ct_ref[...] = _lane_pad(_col_to_row(cnt_col)).astype(I32)


def _layout(cnt, nbp, bm):
    row = jax.ShapeDtypeStruct((1, LANES), I32)
    return pl.pallas_call(
        functools.partial(_layout_kernel, bm),
        out_shape=[jax.ShapeDtypeStruct((1, nbp), I32), row, row, row, row],
        name="moe_layout",
    )(cnt)


def _stage_rows(tn):
    return tn * TOP_K + N_EXPERTS * MOE_UNIT


def _run_copies(meta_ref, ps_ref, make_copy, tn):
    top = (tn // MOE_UNIT).bit_length() - 1
    src = jnp.int32(0)
    total = jnp.int32(0)
    for e in range(N_EXPERTS):
        units = lax.shift_right_logical(meta_ref[0, 0, e], 3)
        dst = ps_ref[e] + meta_ref[0, 0, N_EXPERTS + e]
        off = jnp.int32(0)
        for k in range(top, -1, -1):
            rows = MOE_UNIT << k
            take = lax.shift_right_logical(units, k) & 1

            @pl.when(take == 1)
            def _(src=src, dst=dst, off=off, rows=rows):
                make_copy(pl.multiple_of(src + off, MOE_UNIT), pl.multiple_of(dst + off, MOE_UNIT), rows).start()
            off = off + take * rows
        src = src + units * MOE_UNIT
        total = total + units
    return total


def _wait_copies(units, make_copy, max_rows):
    for k in range((max_rows // MOE_UNIT).bit_length()):
        if (MOE_UNIT << k) > max_rows:
            break

        @pl.when(lax.shift_right_logical(units, k) & 1 == 1)
        def _(k=k):
            make_copy(0, 0, MOE_UNIT << k).wait()


def _wait_units(count, make_copy):
    def body(u, carry):
        make_copy(0, 0).wait()
        return carry
    lax.fori_loop(0, count, body, 0)


def _dispatch_kernel(ps_ref, pe_ref, ct_ref, meta_ref, pos_ref, gate_ref, h_ref, xg_ref, stage, zrows, zblock,
                     started, sem, zsem):
    i = pl.program_id(0)
    n = pl.num_programs(0)
    slot = i % 2
    rows, tn = stage.shape[1], pos_ref.shape[1]

    def copy(s):
        return lambda src, dst, nrows: pltpu.make_async_copy(stage.at[s, pl.ds(src, nrows)],
                                                             xg_ref.at[pl.ds(dst, nrows)], sem.at[s])

    @pl.when(i >= 2)
    def _():
        _wait_copies(started[slot], copy(slot), rows)

    srow = lax.broadcasted_iota(I32, (rows, tn), 0)
    hits = [srow == pos_ref[k:k + 1, :] for k in range(TOP_K)]
    hit = hits[0]
    gate_at = jnp.where(hits[0], gate_ref[0:1, :], 0.0)
    for k in range(1, TOP_K):
        hit = hit | hits[k]
        gate_at = gate_at + jnp.where(hits[k], gate_ref[k:k + 1, :], 0.0)
    onehot = jnp.where(hit, 1.0, 0.0).astype(BF16)
    stage[slot, :, 0:D_MODEL] = jnp.dot(onehot, h_ref[...], preferred_element_type=F32)
    stage[slot, :, D_MODEL:] = jnp.broadcast_to(jnp.sum(gate_at, axis=1, keepdims=True), (rows, LANES))
    started[slot] = _run_copies(meta_ref, ps_ref, copy(slot), tn)

    @pl.when(i == n - 1)
    def _():
        _wait_copies(started[slot], copy(slot), rows)

        @pl.when(n >= 2)
        def _():
            _wait_copies(started[1 - slot], copy(1 - slot), rows)

        zrows[...] = jnp.zeros_like(zrows)
        zero = lambda src, dst: pltpu.make_async_copy(zrows, xg_ref.at[pl.ds(dst, MOE_UNIT)], zsem)
        total = jnp.int32(0)
        for e in range(N_EXPERTS):
            lo = ps_ref[e] + ct_ref[e]
            units = lax.shift_right_logical(pe_ref[e] - lo, 3)

            def body(u, carry, lo=lo):
                zero(0, pl.multiple_of(lo + u * MOE_UNIT, MOE_UNIT)).start()
                return carry
            lax.fori_loop(0, units, body, 0)
            total = total + units
        _wait_units(total, zero)

        zblock[...] = jnp.zeros_like(zblock)
        bm = zblock.shape[0]
        zero_block = lambda blk: pltpu.make_async_copy(zblock, xg_ref.at[pl.ds(pl.multiple_of(blk * bm, bm), bm)],
                                                       zsem)
        first = lax.shift_right_logical(pe_ref[N_EXPERTS - 1], bm.bit_length() - 1)
        last = xg_ref.shape[0] // bm

        def start_block(blk, carry):
            zero_block(blk).start()
            return carry
        lax.fori_loop(first, last, start_block, 0)

        def wait_block(blk, carry):
            zero_block(0).wait()
            return carry
        lax.fori_loop(first, last, wait_block, 0)


def _dispatch(ps, pe, ct, meta, pos, gates, h2, n_slots, tn, bm):
    t = h2.shape[0]
    width = D_MODEL + LANES
    return pl.pallas_call(
        _dispatch_kernel,
        out_shape=jax.ShapeDtypeStruct((n_slots, width), F32),
        grid_spec=pltpu.PrefetchScalarGridSpec(
            num_scalar_prefetch=3,
            grid=(t // tn,),
            in_specs=[pl.BlockSpec((1, 1, LANES), lambda i, *_: (i, 0, 0), memory_space=pltpu.SMEM),
                      pl.BlockSpec((TOP_K, tn), lambda i, *_: (0, i)),
                      pl.BlockSpec((TOP_K, tn), lambda i, *_: (0, i)),
                      pl.BlockSpec((tn, D_MODEL), lambda i, *_: (i, 0))],
            out_specs=pl.BlockSpec(memory_space=pl.ANY),
            scratch_shapes=[pltpu.VMEM((2, _stage_rows(tn), width), F32),
                            pltpu.VMEM((MOE_UNIT, width), F32),
                            pltpu.VMEM((bm, width), F32),
                            pltpu.SMEM((2,), I32),
                            pltpu.SemaphoreType.DMA((2,)),
                            pltpu.SemaphoreType.DMA(())]),
        compiler_params=_cparams(("arbitrary",)),
        name="moe_dispatch",
    )(ps, pe, ct, meta, pos, gates, h2)


def _expert_kernel(be_ref, na_ref, ps_ref, ct_ref, x_ref, w1_ref, b1_ref, w2_ref, b2_ref, y_ref, w1s, w2s):
    i = pl.program_id(0)
    e = be_ref[i]
    prev = be_ref[jnp.maximum(i - 1, 0)]
    bm = x_ref.shape[0]
    used = ps_ref[e] + ct_ref[e] - i * bm

    @pl.when((i == 0) | (e != prev))
    def _():
        rows = 128
        for r0 in range(0, D_MODEL, rows):
            w1s[r0:r0 + rows, :] = w1_ref[r0:r0 + rows, :].astype(BF16)
        for r0 in range(0, D_FF, rows):
            w2s[r0:r0 + rows, :] = w2_ref[r0:r0 + rows, :].astype(BF16)

    def mlp(rows):
        x = x_ref[0:rows, 0:D_MODEL].astype(BF16)
        hmid = jnp.dot(x, w1s[...], preferred_element_type=F32) + b1_ref[...]
        glu = jnp.minimum(hmid[:, :D_FF], SWIGLU_LIMIT)
        lin = jnp.clip(hmid[:, D_FF:], -SWIGLU_LIMIT, SWIGLU_LIMIT)
        act = glu * _sigmoid(SWIGLU_ALPHA * glu) * (lin + 1.0)
        y = jnp.dot(act.astype(BF16), w2s[...], preferred_element_type=F32) + b2_ref[...]
        y_ref[0:rows, :] = y * x_ref[0:rows, D_MODEL:D_MODEL + 1]

    active = i < na_ref[0]
    half = bm // 2

    @pl.when(active & (used > half))
    def _():
        mlp(bm)

    @pl.when(active & (used <= half))
    def _():
        mlp(half)
        y_ref[half:, :] = jnp.zeros((bm - half, D_MODEL), F32)

    @pl.when(jnp.logical_not(active))
    def _():
        y_ref[...] = jnp.zeros_like(y_ref)


def _experts(be, nact, ps, ct, xg, layer, w1, b1, w2, b2, bm):
    n_slots = xg.shape[0]
    nb = n_slots // bm
    blk = lambda i, be_r, na_r, *_: (jnp.minimum(i, na_r[0] - 1), 0)
    wsel = lambda i, be_r, *_: (layer, be_r[i], 0, 0)
    return pl.pallas_call(
        _expert_kernel,
        out_shape=jax.ShapeDtypeStruct((n_slots, D_MODEL), F32),
        grid_spec=pltpu.PrefetchScalarGridSpec(
            num_scalar_prefetch=4,
            grid=(nb,),
            in_specs=[pl.BlockSpec((bm, xg.shape[1]), blk),
                      pl.BlockSpec((None, None, D_MODEL, 2 * D_FF), wsel),
                      pl.BlockSpec((None, None, 1, 2 * D_FF), wsel),
                      pl.BlockSpec((None, None, D_FF, D_MODEL), wsel),
                      pl.BlockSpec((None, None, 1, D_MODEL), wsel)],
            out_specs=pl.BlockSpec((bm, D_MODEL), lambda i, *_: (i, 0)),
            scratch_shapes=[pltpu.VMEM((D_MODEL, 2 * D_FF), BF16), pltpu.VMEM((D_FF, D_MODEL), BF16)]),
        compiler_params=_cparams(("arbitrary",)),
        name="moe_experts",
    )(be, nact, ps, ct, xg, w1, b1, w2, b2)


def _combine_kernel(n_first, ps_ref, mcur_ref, mnext_ref, pos_ref, x1_ref, yg_ref, o_ref, o2_ref, stage,
                    started, sem):
    i = pl.program_id(0)
    n = pl.num_programs(0)
    slot = i % 2
    rows, tn = stage.shape[1], pos_ref.shape[1]

    @pl.when(i == 0)
    def _():
        stage[...] = jnp.zeros_like(stage)

    def copy(s):
        return lambda src, dst, nrows: pltpu.make_async_copy(yg_ref.at[pl.ds(dst, nrows)],
                                                             stage.at[s, pl.ds(src, nrows)], sem.at[s])

    @pl.when(i == 0)
    def _():
        started[0] = _run_copies(mcur_ref, ps_ref, copy(0), tn)

    @pl.when(i + 1 < n)
    def _():
        started[1 - slot] = _run_copies(mnext_ref, ps_ref, copy(1 - slot), tn)

    _wait_copies(started[slot], copy(slot), rows)

    eye = (lax.broadcasted_iota(I32, (tn, tn), 0) == lax.broadcasted_iota(I32, (tn, tn), 1)).astype(F32)
    cols = _hdot_nt(eye, pos_ref[...].astype(F32)).astype(I32)
    lane = lax.broadcasted_iota(I32, (tn, rows), 1)
    hit = lane == cols[:, 0:1]
    for k in range(1, TOP_K):
        hit = hit | (lane == cols[:, k:k + 1])
    pick = jnp.where(hit, 1.0, 0.0).astype(BF16)
    out = x1_ref[...] + jnp.dot(pick, stage[slot].astype(BF16), preferred_element_type=F32)

    @pl.when(i < n_first)
    def _():
        o_ref[...] = out

    @pl.when(i >= n_first)
    def _():
        o2_ref[...] = out


def _combine(ps, meta, pos, x1, yg, tn, t_first):
    t = x1.shape[0]
    nt = t // tn
    n_first = t_first // tn
    return pl.pallas_call(
        functools.partial(_combine_kernel, n_first),
        out_shape=[jax.ShapeDtypeStruct((t_first, D_MODEL), F32),
                   jax.ShapeDtypeStruct((t - t_first, D_MODEL), F32)],
        grid_spec=pltpu.PrefetchScalarGridSpec(
            num_scalar_prefetch=1,
            grid=(nt,),
            in_specs=[pl.BlockSpec((1, 1, LANES), lambda i, *_: (i, 0, 0), memory_space=pltpu.SMEM),
                      pl.BlockSpec((1, 1, LANES), lambda i, *_: (jnp.minimum(i + 1, nt - 1), 0, 0),
                                   memory_space=pltpu.SMEM),
                      pl.BlockSpec((TOP_K, tn), lambda i, *_: (0, i)),
                      pl.BlockSpec((tn, D_MODEL), lambda i, *_: (i, 0)),
                      pl.BlockSpec(memory_space=pl.ANY)],
            out_specs=[pl.BlockSpec((tn, D_MODEL), lambda i, *_: (jnp.minimum(i, n_first - 1), 0)),
                       pl.BlockSpec((tn, D_MODEL), lambda i, *_: (jnp.maximum(i - n_first, 0), 0))],
            scratch_shapes=[pltpu.VMEM((2, _stage_rows(tn), D_MODEL), F32),
                            pltpu.SMEM((2,), I32),
                            pltpu.SemaphoreType.DMA((2,))]),
        compiler_params=_cparams(("arbitrary",)),
        name="moe_combine",
    )(ps, meta, meta, pos, x1, yg)


def _moe(x1, h2, logits_t, layer, w1, b1, w2, b2, t_first):
    t = x1.shape[0]
    tn = MOE_TOK_TILE
    bm = MOE_BM if t * TOP_K >= N_EXPERTS * MOE_BM else LANES
    n_blocks = (t * TOP_K + (t // tn) * N_EXPERTS * (MOE_UNIT - 1)) // bm + N_EXPERTS + 1
    nbp = -(-n_blocks // LANES) * LANES
    pos, gates, meta, cnt = _route(logits_t, tn)
    be, nact, ps, pe, ct = _layout(cnt, nbp, bm)
    ps, pe, ct = ps[0, :N_EXPERTS], pe[0, :N_EXPERTS], ct[0, :N_EXPERTS]
    xg = _dispatch(ps, pe, ct, meta, pos, gates, h2, n_blocks * bm, tn, bm)
    yg = _experts(be[0, :n_blocks], nact[0, :1], ps, ct, xg, layer, w1, b1, w2, b2, bm)
    return _combine(ps, meta, pos, x1, yg, tn, t_first)


def _pack_w_in(w):
    ab = jnp.pad(w[:, _IN_MAIN:_IN_MAIN + _IN_AB], ((0, 0), (0, LANES - _IN_AB)))
    return jnp.concatenate([w[:, :_IN_MAIN], ab, w[:, _IN_MAIN + _IN_AB:]], axis=1).astype(BF16)


def _lane_row(v):
    return jnp.pad(v.astype(F32), (0, LANES - v.shape[0])).reshape(1, LANES)


def _layer_weights(ln1_gain, w_in, q_norm_swa, k_norm_swa, swa_sinks, conv_w, a_log, dt_bias, gdn_norm,
                   q_norm_mem, w_branch, w_out, ln2_gain, w_router, b_router, w_mlp1, b_mlp1, w_mlp2, b_mlp2, layer):
    depth = w_mlp1.shape[0]
    wr_hi = w_router.T.astype(BF16)
    wr_lo = (w_router.T - wr_hi.astype(F32)).astype(BF16)
    return dict(
        layer=layer, wr_t=jnp.concatenate([wr_hi, wr_lo], axis=0), b1r=b_mlp1.reshape(depth, N_EXPERTS, 1, -1), b2r=b_mlp2.reshape(depth, N_EXPERTS, 1, -1),
        ln1=ln1_gain.reshape(1, -1), wp=_pack_w_in(w_in),
        qn_s=q_norm_swa.reshape(1, -1), kn_s=k_norm_swa.reshape(1, -1), sinks=swa_sinks.reshape(1, -1),
        conv_w=conv_w, alog=_lane_row(a_log), dtb=_lane_row(dt_bias), gnorm=gdn_norm.reshape(1, -1),
        qn_m=q_norm_mem.reshape(1, -1), wb=w_branch.astype(BF16), wo=w_out.astype(BF16),
        ln2=ln2_gain.reshape(1, -1), br=b_router.reshape(-1, 1),
        w1=w_mlp1, w2=w_mlp2)


def _finish(lw, first, second):
    t1, t2 = first[4].shape[0], second[4].shape[0]
    tm1 = MERGE_ROWS if t1 % MERGE_ROWS == 0 else MOE_TOK_TILE
    pad = (-t2) % MOE_TOK_TILE
    second = [jnp.pad(a, ((0, pad), (0, 0))) for a in second]
    wts = (lw["wb"], lw["wo"], lw["ln2"], lw["wr_t"], lw["br"])
    bufs = _merge(*first, *wts, tm1, t_out=t1 + t2 + pad)
    bufs = _merge(*second, *wts, MOE_TOK_TILE, into=bufs)
    y1, y2 = _moe(*bufs, lw["layer"], lw["w1"], lw["b1r"], lw["w2"], lw["b2r"], t1)
    return y1, y2[:t2]


def _prompt_layer(x, mem, lw, mem_norm, w_mem_kv, k_norm_mem):
    b, l, d = x.shape
    x2 = x.reshape(b * l, d)
    tm = 256
    tm_in = 2 * tm if l % (2 * tm) == 0 else tm
    qs, ks, vs, qkvg, z, ab, qm, gate, conv_tail = _inproj(x2, lw["ln1"], lw["wp"], lw["conv_w"], tm_in, seq_len=l)
    mk, mv = _memkv(mem.reshape(b * MEM_LEN, d), mem_norm.reshape(1, -1), w_mem_kv.astype(BF16),
                    k_norm_mem.reshape(1, -1))
    r3 = lambda a: a.reshape(b, l, a.shape[-1])
    o_swa, kwin = _swa_prompt(r3(qs), r3(ks), r3(vs), lw["qn_s"], lw["kn_s"], lw["sinks"])
    o_g, s_fin = _gdn_prompt(r3(qkvg), r3(z), r3(ab), lw["alog"], lw["dtb"], lw["gnorm"])
    o_m = _memattn_prompt(r3(qm), mk.reshape(b, MEM_LEN, BRANCH_W), mv.reshape(b, MEM_LEN, BRANCH_W), lw["qn_m"])
    parts = (o_swa.reshape(b * l, -1), o_g.reshape(b * l, -1), o_m.reshape(b * l, -1), gate, x2)
    new_k = kwin.reshape(b, WINDOW, SWA_KV_HEADS, SWA_HEAD_DIM)
    new_v = r3(vs)[:, l - WINDOW:].reshape(b, WINDOW, SWA_KV_HEADS, SWA_HEAD_DIM)
    new_conv = conv_tail[:, _HALO - (CONV_WIDTH - 1):]
    mk4 = mk.reshape(b, MEM_LEN, MEM_HEADS, MEM_HEAD_DIM)
    mv4 = mv.reshape(b, MEM_LEN, MEM_HEADS, MEM_HEAD_DIM)
    return parts, (new_k, new_v, s_fin, new_conv, mk4, mv4)


def _sample_layer(x, cache_k, cache_v, state, conv_state, mem_k, mem_v, lw):
    b, l, d = x.shape
    x2 = x.reshape(b, d)
    tm = TOK_TILE
    qs, ks, vs, qkvg, z, ab, qm, gate = _inproj(x2, lw["ln1"], lw["wp"], lw["conv_w"], tm)
    o_swa, new_k, new_v = _swa_decode(qs, ks, vs, cache_k, cache_v, lw["qn_s"], lw["kn_s"], lw["sinks"])
    o_g, new_s = _gdn_decode(qkvg, z, ab, state, conv_state, lw["conv_w"], lw["alog"], lw["dtb"], lw["gnorm"])
    o_m = _memattn_decode(qm, mem_k, mem_v, lw["qn_m"])
    new_conv = jnp.concatenate([conv_state[:, 1:], qkvg[:, None, :]], axis=1)
    shp = (b, WINDOW, SWA_KV_HEADS, SWA_HEAD_DIM)
    return (o_swa, o_g, o_m, gate, x2), (new_k.reshape(shp), new_v.reshape(shp), new_s, new_conv)


def _layer(lw, xp, mem, mem_norm, w_mem_kv, k_norm_mem, xs, cache_k, cache_v, state, conv_state, mem_k, mem_v):
    p_parts, p_state = _prompt_layer(xp, mem, lw, mem_norm, w_mem_kv, k_norm_mem)
    s_parts, s_state = _sample_layer(xs, cache_k, cache_v, state, conv_state, mem_k, mem_v, lw)
    yp, ys = _finish(lw, p_parts, s_parts)
    return yp.reshape(xp.shape), ys.reshape(xs.shape), p_state, s_state


def kernel(x_prompt, x_sample, cache_swa_k, cache_swa_v, state_gdn, state_conv, cache_mem_k, cache_mem_v, mem_prompt, ln1_gain, w_in, q_norm_swa, k_norm_swa, swa_sinks, conv_w, a_log, dt_bias, gdn_norm, q_norm_mem, k_norm_mem, mem_norm, w_mem_kv, w_branch, w_out, ln2_gain, w_router, b_router, w_mlp1, b_mlp1, w_mlp2, b_mlp2):
    xp, xs = x_prompt, x_sample
    outs = [[] for _ in range(10)]
    for l in range(ln1_gain.shape[0]):
        lw = _layer_weights(ln1_gain[l], w_in[l], q_norm_swa[l], k_norm_swa[l], swa_sinks[l], conv_w[l], a_log[l],
                            dt_bias[l], gdn_norm[l], q_norm_mem[l], w_branch[l], w_out[l], ln2_gain[l], w_router[l],
                            b_router[l], w_mlp1, b_mlp1, w_mlp2, b_mlp2, l)
        xp, xs, (pk, pv, ps, pc, mk, mv), (sk, sv, ss, sc) = _layer(
            lw, xp, mem_prompt, mem_norm[l], w_mem_kv[l], k_norm_mem[l],
            xs, cache_swa_k[l], cache_swa_v[l], state_gdn[l], state_conv[l], cache_mem_k[l], cache_mem_v[l])
        for acc, v in zip(outs, (pk, pv, sk, sv, ps, ss, pc, sc, mk, mv)):
            acc.append(v)
    return (xp, xs) + tuple(jnp.stack(o) for o in outs)
```

```python
import functools

import numpy as np
import jax
import jax.numpy as jnp
from jax import lax
from jax.experimental import pallas as pl
from jax.experimental.pallas import tpu as pltpu

F32 = jnp.float32
BF16 = jnp.bfloat16
I32 = jnp.int32

D_MODEL = 1024
BRANCH_W = 512
SWA_HEADS = 8
SWA_KV_HEADS = 2
SWA_HEAD_DIM = 64
SWA_GROUP = SWA_HEADS // SWA_KV_HEADS
SWA_KV_W = SWA_KV_HEADS * SWA_HEAD_DIM
WINDOW = 128
GDN_HEADS = 4
GDN_HEAD_DIM = 128
GDN_W = GDN_HEADS * GDN_HEAD_DIM
GDN_CONV_W = 3 * GDN_W
GDN_CHUNK = 64
CONV_WIDTH = 4
MEM_LEN = 256
MEM_HEADS = 4
MEM_HEAD_DIM = 128
N_BRANCH = 3
N_EXPERTS = 32
TOP_K = 4
D_FF = 1024
SWIGLU_LIMIT = 7.0
SWIGLU_ALPHA = 1.702
EPS = 1e-6

LANES = 128
MOE_BM = 512
MOE_TOK_TILE = 256
MOE_UNIT = 8
TOK_TILE = 128
MERGE_ROWS = 512
VMEM_LIMIT = 56 * 1024 * 1024

_SEG_QS = (0, 512)
_SEG_KS = (512, 128)
_SEG_VS = (640, 128)
_SEG_QKVG = (768, 1536)
_SEG_Z = (2304, 512)
_SEG_AB = (2816, 128)
_SEG_QM = (2944, 512)
_SEG_GATE = (3456, 3072)
_PACKED_W = 6528
_IN_MAIN = 2816
_IN_AB = 8

_NEG = -1e30
_HI = lax.Precision.HIGHEST


def _cparams(sem, vmem=VMEM_LIMIT):
    return pltpu.CompilerParams(dimension_semantics=sem, vmem_limit_bytes=vmem)


def _bdot(a, b):
    return jnp.dot(a.astype(BF16), b.astype(BF16), preferred_element_type=F32)


def _bdot_nt(a, b):
    return lax.dot_general(a.astype(BF16), b.astype(BF16), (((1,), (1,)), ((), ())),
                           preferred_element_type=F32)


def _hdot_nt(a, b):
    return lax.dot_general(a, b, (((1,), (1,)), ((), ())), preferred_element_type=F32, precision=_HI)


def _rms(x, gain):
    return x * lax.rsqrt(jnp.mean(x * x, axis=-1, keepdims=True) + EPS) * gain


def _l2(x):
    return x * lax.rsqrt(jnp.sum(x * x, axis=-1, keepdims=True) + EPS)


def _sigmoid(x):
    return 0.5 * jnp.tanh(0.5 * x) + 0.5


def _silu(x):
    return x * _sigmoid(x)


def _softplus(x):
    return jnp.maximum(x, 0.0) + jnp.log1p(jnp.exp(-jnp.abs(x)))


_HALO = 8


def _inproj_kernel(seq_steps, x_ref, g_ref, w_ref, cw_ref, qs, ks, vs, qkvg, z, ab, qm, gate, *conv_refs):
    i = pl.program_id(0)
    x = x_ref[...]
    hb = _rms(x, g_ref[...]).astype(BF16)

    def proj(off, width):
        return jnp.dot(hb, w_ref[:, off:off + width], preferred_element_type=F32)

    step = 512

    def plain(ref, c0=0, width=None):
        def finish(r):
            ref[:, c0:c0 + (width or r.shape[1])] = r
        return finish

    def gate_chunk(c0):
        def finish(r):
            gate[:, c0:c0 + step] = _sigmoid(r).astype(BF16)
        return finish

    def conv_part(part):
        tail_ref, carry = conv_refs
        c0 = part * GDN_W
        cols = slice(c0, c0 + GDN_W)

        def finish(r):
            first = (i % seq_steps) == 0
            sub = lax.broadcasted_iota(I32, (_HALO, 1), 0)
            rows = r.shape[0]
            before = jnp.where(first, 0.0, carry[:, cols])
            y = r * cw_ref[CONV_WIDTH - 1:CONV_WIDTH, cols]
            top = r[0:_HALO] * cw_ref[CONV_WIDTH - 1:CONV_WIDTH, cols]
            for jj in range(CONV_WIDTH - 1):
                sh = CONV_WIDTH - 1 - jj
                y = y + pltpu.roll(r, sh, 0) * cw_ref[jj:jj + 1, cols]
                head_rows = jnp.where(sub < sh, pltpu.roll(before, sh, 0), pltpu.roll(r[0:_HALO], sh, 0))
                top = top + head_rows * cw_ref[jj:jj + 1, cols]
            carry[:, cols] = r[rows - _HALO:]
            tail_ref[0, :, cols] = r[rows - _HALO:]
            act = _silu(jnp.concatenate([top, y[_HALO:]], axis=0))
            if part < 2:
                scale = GDN_HEAD_DIM ** -0.5 if part == 0 else 1.0
                for h in range(GDN_HEADS):
                    hs = slice(h * GDN_HEAD_DIM, (h + 1) * GDN_HEAD_DIM)
                    qkvg[:, c0 + h * GDN_HEAD_DIM:c0 + (h + 1) * GDN_HEAD_DIM] = _l2(act[:, hs]) * scale
            else:
                qkvg[:, cols] = act
        return finish

    tasks = [(_SEG_QS[0], _SEG_QS[1], plain(qs)), (_SEG_KS[0], _SEG_KS[1], plain(ks)),
             (_SEG_VS[0], _SEG_VS[1], plain(vs)), (_SEG_AB[0], _SEG_AB[1], plain(ab)),
             (_SEG_QM[0], _SEG_QM[1], plain(qm))]
    tasks += [(_SEG_GATE[0] + c0, step, gate_chunk(c0)) for c0 in range(0, _SEG_GATE[1], step)]
    if seq_steps is None:
        tasks += [(_SEG_Z[0], _SEG_Z[1], plain(z))]
        tasks += [(_SEG_QKVG[0] + c0, step, plain(qkvg, c0, step)) for c0 in range(0, GDN_CONV_W, step)]
    else:
        def silu_z(r):
            z[...] = _silu(r)
        tasks += [(_SEG_QKVG[0] + p * GDN_W, GDN_W, conv_part(p)) for p in range(3)]
        tasks.append((_SEG_Z[0], _SEG_Z[1], silu_z))
    for off, width, finish in tasks:
        finish(proj(off, width))


def _inproj(x2, gain, wp, conv_w, tm, seq_len=None):
    t = x2.shape[0]
    segs = (_SEG_QS, _SEG_KS, _SEG_VS, _SEG_QKVG, _SEG_Z, _SEG_AB, _SEG_QM, _SEG_GATE)
    dtypes = (F32,) * 7 + (BF16,)
    out_shape = [jax.ShapeDtypeStruct((t, w), dt) for (_, w), dt in zip(segs, dtypes)]
    out_specs = [pl.BlockSpec((tm, w), lambda i: (i, 0)) for _, w in segs]
    scratch = []
    seq_steps = None
    if seq_len is not None:
        seq_steps = seq_len // tm
        out_shape.append(jax.ShapeDtypeStruct((t // seq_len, _HALO, GDN_CONV_W), F32))
        out_specs.append(pl.BlockSpec((1, _HALO, GDN_CONV_W), lambda i: (i // seq_steps, 0, 0)))
        scratch.append(pltpu.VMEM((_HALO, GDN_CONV_W), F32))
    return pl.pallas_call(
        functools.partial(_inproj_kernel, seq_steps),
        out_shape=out_shape,
        grid=(t // tm,),
        in_specs=[pl.BlockSpec((tm, D_MODEL), lambda i: (i, 0)),
                  pl.BlockSpec((1, D_MODEL), lambda i: (0, 0)),
                  pl.BlockSpec((D_MODEL, _PACKED_W), lambda i: (0, 0), pipeline_mode=pl.Buffered(1)),
                  pl.BlockSpec((CONV_WIDTH, GDN_CONV_W), lambda i: (0, 0))],
        out_specs=out_specs,
        scratch_shapes=scratch,
        compiler_params=_cparams(("arbitrary",)),
        name="inproj",
    )(x2, gain, wp, conv_w)


SWA_STEP_BLOCKS = 8


def _alibi_slopes(n):
    return [float(2.0 ** (-8.0 * (i + 1) / n)) for i in range(n)]


def _swa_prompt_kernel(q_ref, kc_ref, vc_ref, qg_ref, kg_ref, sink_ref, o_ref, kwin_ref, bias, kprev, vprev):
    n = pl.program_id(1)
    half = SWA_HEAD_DIM
    nblk = q_ref.shape[1] // WINDOW
    work = [(j, h) for j in range(nblk) for h in range(SWA_HEADS)]

    @pl.when(n == 0)
    def _():
        kprev[...] = jnp.zeros_like(kprev)
        vprev[...] = jnp.zeros_like(vprev)

    @pl.when((pl.program_id(0) == 0) & (n == 0))
    def _():
        row = lax.broadcasted_iota(I32, (WINDOW, 2 * WINDOW), 0)
        col = lax.broadcasted_iota(I32, (WINDOW, 2 * WINDOW), 1)
        dist = row + WINDOW - col
        window = (dist >= 0) & (dist <= WINDOW)
        distf = dist.astype(F32)
        for h, slope in enumerate(_alibi_slopes(SWA_HEADS)):
            bias[0, h] = jnp.where(window & (col >= WINDOW), -slope * distf, _NEG)
            bias[1, h] = jnp.where(window, -slope * distf, _NEG)

    first_table = jnp.where(n == 0, 0, 1)

    low = lax.broadcasted_iota(I32, (1, 2 * half), 1) < half

    def pair_rms(x, gain):
        sq = x * x
        s_lo = jnp.sum(jnp.where(low, sq, 0.0), axis=-1, keepdims=True)
        s_hi = jnp.sum(jnp.where(low, 0.0, sq), axis=-1, keepdims=True)
        return x * lax.rsqrt(jnp.where(low, s_lo, s_hi) * (1.0 / half) + EPS) * gain

    kcn = pair_rms(kc_ref[0], kg_ref[...])
    kwin_ref[0] = kcn[(nblk - 1) * WINDOW:]
    kall = jnp.concatenate([kprev[...], kcn], axis=0)
    vall = jnp.concatenate([vprev[...], vc_ref[0]], axis=0)
    kprev[...] = kcn[(nblk - 1) * WINDOW:]
    vprev[...] = vc_ref[0, (nblk - 1) * WINDOW:, :]
    ones = jnp.ones(((nblk + 1) * WINDOW, 2 * half), BF16)

    def placed(x, aug):
        sw = pltpu.roll(x, half, 1)
        out = {(0, 0): jnp.where(low, x, 0.0), (1, 1): jnp.where(low, 0.0, x),
               (0, 1): jnp.where(low, 0.0, sw), (1, 0): jnp.where(low, sw, 0.0)}
        out = {key: val.astype(BF16) for key, val in out.items()}
        return {key: jnp.concatenate([val, ones], axis=1) for key, val in out.items()} if aug else out

    kvar = placed(kall, False)
    vvar = placed(vall, True)
    qn = [pair_rms(q_ref[0, :, t * 2 * half:(t + 1) * 2 * half], qg_ref[...]).astype(BF16)
          for t in range(SWA_HEADS // 2)]
    where_of = lambda h: (h // SWA_GROUP, h % 2)
    qrows = lambda j: slice(j * WINDOW, (j + 1) * WINDOW)
    krows = lambda j: slice(j * WINDOW, (j + 2) * WINDOW)
    nt = (((1,), (1,)), ((), ()))
    s = {(j, h): lax.dot_general(qn[h // 2][qrows(j)], kvar[where_of(h)][krows(j)], nt, preferred_element_type=F32)
         + bias[first_table if j == 0 else 1, h] for j, h in work}
    sink = [sink_ref[0:1, h:h + 1] for h in range(SWA_HEADS)]
    m = {(j, h): jnp.maximum(jnp.max(s[j, h], axis=-1, keepdims=True), sink[h]) for j, h in work}
    p = {(j, h): jnp.exp(s[j, h] - m[j, h]).astype(BF16) for j, h in work}
    res = {(j, h): jnp.dot(p[j, h], vvar[where_of(h)][krows(j)], preferred_element_type=F32)
           for j, h in work}
    inv = {(j, h): 1.0 / (res[j, h][:, 2 * half:2 * half + 1] + jnp.exp(sink[h] - m[j, h])) for j, h in work}
    for j in range(nblk):
        for t in range(SWA_HEADS // 2):
            h0, h1 = 2 * t, 2 * t + 1
            num = res[j, h0][:, :2 * half] + res[j, h1][:, :2 * half]
            o_ref[0, qrows(j), t * 2 * half:(t + 1) * 2 * half] = num * jnp.where(low, inv[j, h0], inv[j, h1])


def _swa_prompt(qs, ks, vs, qn, kn, sinks):
    b, l, _ = qs.shape
    rows = SWA_STEP_BLOCKS * WINDOW if l % (SWA_STEP_BLOCKS * WINDOW) == 0 else WINDOW
    cur = lambda i, j: (i, j, 0)
    const2 = lambda i, j: (0, 0)
    q_gain = jnp.tile(qn, (1, 2)) * (SWA_HEAD_DIM ** -0.5)
    k_gain = jnp.tile(kn, (1, 2))
    return pl.pallas_call(
        _swa_prompt_kernel,
        out_shape=[jax.ShapeDtypeStruct((b, l, BRANCH_W), F32),
                   jax.ShapeDtypeStruct((b, WINDOW, SWA_KV_W), F32)],
        grid=(b, l // rows),
        in_specs=[pl.BlockSpec((1, rows, BRANCH_W), cur),
                  pl.BlockSpec((1, rows, SWA_KV_W), cur),
                  pl.BlockSpec((1, rows, SWA_KV_W), cur),
                  pl.BlockSpec((1, 2 * SWA_HEAD_DIM), const2),
                  pl.BlockSpec((1, 2 * SWA_HEAD_DIM), const2),
                  pl.BlockSpec((1, SWA_HEADS), const2)],
        out_specs=[pl.BlockSpec((1, rows, BRANCH_W), cur),
                   pl.BlockSpec((1, WINDOW, SWA_KV_W), lambda i, j: (i, 0, 0))],
        scratch_shapes=[pltpu.VMEM((2, SWA_HEADS, WINDOW, 2 * WINDOW), F32),
                        pltpu.VMEM((WINDOW, SWA_KV_W), F32),
                        pltpu.VMEM((WINDOW, SWA_KV_W), F32)],
        compiler_params=_cparams(("arbitrary", "arbitrary")),
        name="swa_prompt",
    )(qs, ks, vs, q_gain, k_gain, sinks)


def _swa_decode_kernel(bs, q_ref, k3_ref, kf_ref, v3_ref, vf_ref, ck_ref, cv_ref, qn_ref, kn_ref, kn2_ref,
                       sink_ref, slope_ref, o_ref, ok_ref, ov_ref):
    scale = SWA_HEAD_DIM ** -0.5
    lane = lax.broadcasted_iota(I32, (1, SWA_KV_W), 1)
    rowi = lax.broadcasted_iota(I32, (WINDOW, SWA_KV_W), 0)
    keyd = (WINDOW - lax.broadcasted_iota(I32, (1, WINDOW), 1)).astype(F32)
    last = rowi == WINDOW - 1
    for b in range(bs):
        kf = kf_ref[b]
        sq = kf * kf
        ms0 = jnp.sum(jnp.where(lane < SWA_HEAD_DIM, sq, 0.0), axis=-1, keepdims=True) / SWA_HEAD_DIM
        ms1 = jnp.sum(jnp.where(lane >= SWA_HEAD_DIM, sq, 0.0), axis=-1, keepdims=True) / SWA_HEAD_DIM
        knf = kf * lax.rsqrt(jnp.where(lane < SWA_HEAD_DIM, ms0, ms1) + EPS) * kn2_ref[...]
        ok_ref[b] = jnp.where(last, knf, pltpu.roll(ck_ref[b], WINDOW - 1, 0))
        ov_ref[b] = jnp.where(last, vf_ref[b], pltpu.roll(cv_ref[b], WINDOW - 1, 0))
    work = [(b, g) for b in range(bs) for g in range(SWA_KV_HEADS)]
    lanes = lambda g: slice(g * SWA_HEAD_DIM, (g + 1) * SWA_HEAD_DIM)
    heads = lambda g: slice(g * SWA_GROUP, (g + 1) * SWA_GROUP)
    qn = [_rms(q_ref[b], qn_ref[...]) for b in range(bs)]
    kn3 = [_rms(k3_ref[b], kn_ref[...]) for b in range(bs)]
    qg = {(b, g): qn[b][heads(g)] for b, g in work}
    s = {(b, g): _bdot_nt(qg[b, g], ck_ref[b, :, lanes(g)]) * scale - slope_ref[heads(g)] * keyd for b, g in work}
    s_new = {(b, g): jnp.sum(qg[b, g] * kn3[b][g:g + 1], axis=-1, keepdims=True) * scale for b, g in work}
    m = {(b, g): jnp.maximum(jnp.maximum(jnp.max(s[b, g], axis=-1, keepdims=True), s_new[b, g]),
                             sink_ref[heads(g)]) for b, g in work}
    p = {w: jnp.exp(s[w] - m[w]) for w in work}
    p_new = {w: jnp.exp(s_new[w] - m[w]) for w in work}
    denom = {(b, g): jnp.sum(p[b, g], axis=-1, keepdims=True) + p_new[b, g] + jnp.exp(sink_ref[heads(g)] - m[b, g])
             for b, g in work}
    pv = {(b, g): _bdot(p[b, g], cv_ref[b, :, lanes(g)]) for b, g in work}
    for b, g in work:
        o_ref[b, heads(g), :] = (pv[b, g] + p_new[b, g] * v3_ref[b, g:g + 1, :]) / denom[b, g]


def _swa_decode(qs, ks, vs, cache_k, cache_v, qn, kn, sinks, bs=8):
    b = qs.shape[0]
    q3 = qs.reshape(b, SWA_HEADS, SWA_HEAD_DIM)
    k3 = ks.reshape(b, SWA_KV_HEADS, SWA_HEAD_DIM)
    kf = ks.reshape(b, 1, SWA_KV_W)
    v3 = vs.reshape(b, SWA_KV_HEADS, SWA_HEAD_DIM)
    vf = vs.reshape(b, 1, SWA_KV_W)
    ck = cache_k.reshape(b, WINDOW, SWA_KV_W)
    cv = cache_v.reshape(b, WINDOW, SWA_KV_W)
    kn2 = jnp.concatenate([kn, kn], axis=-1)
    sink_col = sinks.reshape(SWA_HEADS, 1)
    slope_col = jnp.asarray(np.asarray(_alibi_slopes(SWA_HEADS), np.float32).reshape(SWA_HEADS, 1))
    blk = lambda *shape: pl.BlockSpec((bs,) + shape, lambda i: (i,) + (0,) * len(shape))
    full = lambda *shape: pl.BlockSpec(shape, lambda i: (0,) * len(shape))
    o, ok, ov = pl.pallas_call(
        functools.partial(_swa_decode_kernel, bs),
        out_shape=[jax.ShapeDtypeStruct((b, SWA_HEADS, SWA_HEAD_DIM), F32),
                   jax.ShapeDtypeStruct((b, WINDOW, SWA_KV_W), F32),
                   jax.ShapeDtypeStruct((b, WINDOW, SWA_KV_W), F32)],
        grid=(b // bs,),
        in_specs=[blk(SWA_HEADS, SWA_HEAD_DIM), blk(SWA_KV_HEADS, SWA_HEAD_DIM), blk(1, SWA_KV_W),
                  blk(SWA_KV_HEADS, SWA_HEAD_DIM), blk(1, SWA_KV_W), blk(WINDOW, SWA_KV_W), blk(WINDOW, SWA_KV_W),
                  full(1, SWA_HEAD_DIM), full(1, SWA_HEAD_DIM), full(1, SWA_KV_W),
                  full(SWA_HEADS, 1), full(SWA_HEADS, 1)],
        out_specs=[blk(SWA_HEADS, SWA_HEAD_DIM), blk(WINDOW, SWA_KV_W), blk(WINDOW, SWA_KV_W)],
        compiler_params=_cparams(("parallel",)),
        name="swa_decode",
    )(q3, k3, kf, v3, vf, ck, cv, qn, kn, kn2, sink_col, slope_col)
    return o.reshape(b, BRANCH_W), ok, ov


def _gate_rows(ab, alog_ref, dtb_ref):
    g = -jnp.exp(alog_ref[...]) * _softplus(ab + dtb_ref[...])
    return g, _sigmoid(ab)


GDN_PREP_CHUNKS = 4
GDN_PREP_HEADS = 4
GDN_SCAN_BATCH = 8


def _gdn_prep_kernel(ys, ab_ref, alog_ref, dtb_ref, u_ref, w_ref, qg_ref, kd_ref, in_ref, eg_ref):
    C = GDN_CHUNK
    d = GDN_HEAD_DIM
    n = GDN_PREP_CHUNKS * C
    g_all, beta_all = _gate_rows(ab_ref[0], alog_ref, dtb_ref)
    r = lax.broadcasted_iota(I32, (n, n), 0)
    cc = lax.broadcasted_iota(I32, (n, n), 1)
    same = (r // C) == (cc // C)
    incl = same & (r >= cc)
    strict = same & (r > cc)
    upto = same & (r <= cc)
    eye = r == cc
    for h0 in range(0, GDN_HEADS, GDN_PREP_HEADS):
        heads = range(h0, h0 + GDN_PREP_HEADS)
        q = {h: ys[0, :, h * d:(h + 1) * d] for h in heads}
        k = {h: ys[0, :, GDN_W + h * d:GDN_W + (h + 1) * d] for h in heads}
        beta = {h: beta_all[:, GDN_HEADS + h:GDN_HEADS + h + 1] for h in heads}
        gc_row = {h: jnp.sum(jnp.where(upto, g_all[:, h:h + 1], 0.0), axis=0, keepdims=True) for h in heads}
        gc_col = {h: jnp.sum(jnp.where(eye, gc_row[h], 0.0), axis=1, keepdims=True) for h in heads}
        decay = {h: jnp.where(incl, jnp.exp(jnp.where(incl, gc_col[h] - gc_row[h], 0.0)), 0.0) for h in heads}
        kb = {h: k[h] * beta[h] for h in heads}
        bp = {h: -jnp.where(strict, _bdot_nt(kb[h], k[h]) * decay[h], 0.0) for h in heads}
        intra = {h: jnp.where(incl, _bdot_nt(q[h], k[h]) * decay[h], 0.0) for h in heads}
        p = {h: eye.astype(F32) + bp[h] for h in heads}
        span = 2
        while span < C:
            bp = {h: _bdot(bp[h], bp[h]) for h in heads}
            p = {h: p[h] + _bdot(p[h], bp[h]) for h in heads}
            span *= 2
        uw = {h: _bdot(p[h], jnp.concatenate([ys[0, :, 2 * GDN_W + h * d:2 * GDN_W + (h + 1) * d] * beta[h],
                                              kb[h] * jnp.exp(gc_col[h])], axis=1)) for h in heads}
        for h in heads:
            hs = slice(h * d, (h + 1) * d)
            u_ref[0, :, hs] = uw[h][:, :d]
            w_ref[0, :, hs] = uw[h][:, d:].astype(BF16)
            qg_ref[0, :, hs] = (q[h] * jnp.exp(gc_col[h])).astype(BF16)
            for ci in range(GDN_PREP_CHUNKS):
                r0 = ci * C
                g_last = gc_row[h][:, r0 + C - 1:r0 + C]
                kd_ref[0, r0:r0 + C, hs] = (k[h][r0:r0 + C] * jnp.exp(g_last - gc_col[h][r0:r0 + C])).astype(BF16)
                in_ref[0, r0:r0 + C, h * C:(h + 1) * C] = intra[h][r0:r0 + C, r0:r0 + C].astype(BF16)
                eg_ref[0, ci, h:h + 1, :] = jnp.broadcast_to(jnp.exp(g_last), (1, LANES))


def _gdn_scan_kernel(u_ref, w_ref, qg_ref, kd_ref, in_ref, eg_ref, z_ref, gn_ref, o_ref, s_ref, st):
    c = pl.program_id(1)
    C = GDN_CHUNK
    d = GDN_HEAD_DIM

    @pl.when(c == 0)
    def _():
        st[...] = jnp.zeros_like(st)

    chains = [(bi, h) for bi in range(u_ref.shape[0]) for h in range(GDN_HEADS)]
    hsl = lambda h: slice(h * d, (h + 1) * d)
    s_f = [st[bi, h] for bi, h in chains]
    s_b = [s.astype(BF16) for s in s_f]
    ws = [jnp.dot(w_ref[bi, :, hsl(h)], s_b[i], preferred_element_type=F32) for i, (bi, h) in enumerate(chains)]
    qs = [jnp.dot(qg_ref[bi, :, hsl(h)], s_b[i], preferred_element_type=F32) for i, (bi, h) in enumerate(chains)]
    vb = [(u_ref[bi, :, hsl(h)] - ws[i]).astype(BF16) for i, (bi, h) in enumerate(chains)]
    iv = [jnp.dot(in_ref[bi, :, h * C:(h + 1) * C], vb[i], preferred_element_type=F32)
          for i, (bi, h) in enumerate(chains)]
    kv = [lax.dot_general(kd_ref[bi, :, hsl(h)], vb[i], (((0,), (0,)), ((), ())), preferred_element_type=F32)
          for i, (bi, h) in enumerate(chains)]
    for i, (bi, h) in enumerate(chains):
        st[bi, h] = s_f[i] * eg_ref[bi, 0, h:h + 1, :] + kv[i]
        o_ref[bi, :, hsl(h)] = _rms(qs[i] + iv[i], gn_ref[...]) * z_ref[bi, :, hsl(h)]
    s_ref[...] = st[...]


def _gdn_prompt(qkvg, z, ab, alog, dtb, gnorm):
    b, l, _ = qkvg.shape
    C = GDN_CHUNK
    nc = l // C
    rows = GDN_PREP_CHUNKS * C
    cur = lambda i, j: (i, j, 0)
    const2 = lambda i, j: (0, 0)
    u, w, qg, kd, intra, eg = pl.pallas_call(
        _gdn_prep_kernel,
        out_shape=[jax.ShapeDtypeStruct((b, l, GDN_W), F32),
                   jax.ShapeDtypeStruct((b, l, GDN_W), BF16),
                   jax.ShapeDtypeStruct((b, l, GDN_W), BF16),
                   jax.ShapeDtypeStruct((b, l, GDN_W), BF16),
                   jax.ShapeDtypeStruct((b, l, GDN_HEADS * C), BF16),
                   jax.ShapeDtypeStruct((b, nc, GDN_HEADS, LANES), F32)],
        grid=(b, l // rows),
        in_specs=[pl.BlockSpec((1, rows, GDN_CONV_W), cur),
                  pl.BlockSpec((1, rows, LANES), cur),
                  pl.BlockSpec((1, LANES), const2),
                  pl.BlockSpec((1, LANES), const2)],
        out_specs=[pl.BlockSpec((1, rows, GDN_W), cur),
                   pl.BlockSpec((1, rows, GDN_W), cur),
                   pl.BlockSpec((1, rows, GDN_W), cur),
                   pl.BlockSpec((1, rows, GDN_W), cur),
                   pl.BlockSpec((1, rows, GDN_HEADS * C), cur),
                   pl.BlockSpec((1, GDN_PREP_CHUNKS, GDN_HEADS, LANES), lambda i, j: (i, j, 0, 0))],
        compiler_params=_cparams(("parallel", "parallel")),
        name="gdn_prep",
    )(qkvg, ab, alog, dtb)
    bb = GDN_SCAN_BATCH if b % GDN_SCAN_BATCH == 0 else 1
    seq = lambda wd: pl.BlockSpec((bb, C, wd), cur)
    return pl.pallas_call(
        _gdn_scan_kernel,
        out_shape=[jax.ShapeDtypeStruct((b, l, GDN_W), F32),
                   jax.ShapeDtypeStruct((b, GDN_HEADS, GDN_HEAD_DIM, GDN_HEAD_DIM), F32)],
        grid=(b // bb, nc),
        in_specs=[seq(GDN_W), seq(GDN_W), seq(GDN_W), seq(GDN_W), seq(GDN_HEADS * C),
                  pl.BlockSpec((bb, 1, GDN_HEADS, LANES), lambda i, j: (i, j, 0, 0)),
                  seq(GDN_W),
                  pl.BlockSpec((1, GDN_HEAD_DIM), const2)],
        out_specs=[seq(GDN_W),
                   pl.BlockSpec((bb, GDN_HEADS, GDN_HEAD_DIM, GDN_HEAD_DIM), lambda i, j: (i, 0, 0, 0))],
        scratch_shapes=[pltpu.VMEM((bb, GDN_HEADS, GDN_HEAD_DIM, GDN_HEAD_DIM), F32)],
        compiler_params=_cparams(("parallel", "arbitrary")),
        name="gdn_scan",
    )(u, w, qg, kd, intra, eg, z, gnorm)


def _gdn_decode_kernel(bs, x_ref, cs_ref, z_ref, ab_ref, s_ref, cw_ref, alog_ref, dtb_ref, gn_ref, o_ref, so_ref):
    x = x_ref[...]
    y = x * cw_ref[CONV_WIDTH - 1:CONV_WIDTH, :]
    for j in range(CONV_WIDTH - 1):
        y = y + cs_ref[:, j, :] * cw_ref[j:j + 1, :]
    y = _silu(y)
    g_all, beta_all = _gate_rows(ab_ref[...], alog_ref, dtb_ref)
    eg_all = jnp.exp(g_all)
    z = z_ref[...]
    d = GDN_HEAD_DIM
    eye = (lax.broadcasted_iota(I32, (d, d), 0) == lax.broadcasted_iota(I32, (d, d), 1)).astype(F32)
    for h in range(GDN_HEADS):
        hs = slice(h * d, (h + 1) * d)
        qh = _l2(y[:, h * d:(h + 1) * d]) * (d ** -0.5)
        kh = _l2(y[:, GDN_W + h * d:GDN_W + (h + 1) * d])
        vh = y[:, 2 * GDN_W + h * d:2 * GDN_W + (h + 1) * d]
        kt = _hdot_nt(eye, kh)
        qt = _hdot_nt(eye, qh)
        rng = range(bs)
        eg = [eg_all[b:b + 1, h:h + 1] for b in rng]
        beta = [beta_all[b:b + 1, GDN_HEADS + h:GDN_HEADS + h + 1] for b in rng]
        sk = [jnp.sum(s_ref[b, h] * kt[:, b:b + 1], axis=0, keepdims=True) for b in rng]
        v_new = [beta[b] * (vh[b:b + 1] - eg[b] * sk[b]) for b in rng]
        s_new = [eg[b] * s_ref[b, h] + kt[:, b:b + 1] * v_new[b] for b in rng]
        for b in rng:
            so_ref[b, h] = s_new[b]
        o = jnp.concatenate([jnp.sum(s_new[b] * qt[:, b:b + 1], axis=0, keepdims=True) for b in rng], axis=0)
        o_ref[:, hs] = _rms(o, gn_ref[...]) * _silu(z[:, hs])


def _gdn_decode(qkvg, z, ab, state, conv_state, conv_w, alog, dtb, gnorm, bs=8):
    b = qkvg.shape[0]
    row = lambda w: pl.BlockSpec((bs, w), lambda i: (i, 0))
    full = lambda *shape: pl.BlockSpec(shape, lambda i: (0,) * len(shape))
    sspec = pl.BlockSpec((bs, GDN_HEADS, GDN_HEAD_DIM, GDN_HEAD_DIM), lambda i: (i, 0, 0, 0))
    return pl.pallas_call(
        functools.partial(_gdn_decode_kernel, bs),
        out_shape=[jax.ShapeDtypeStruct((b, GDN_W), F32),
                   jax.ShapeDtypeStruct(state.shape, F32)],
        grid=(b // bs,),
        in_specs=[row(GDN_CONV_W),
                  pl.BlockSpec((bs, CONV_WIDTH - 1, GDN_CONV_W), lambda i: (i, 0, 0)),
                  row(GDN_W), row(LANES), sspec,
                  full(CONV_WIDTH, GDN_CONV_W), full(1, LANES), full(1, LANES), full(1, GDN_HEAD_DIM)],
        out_specs=[row(GDN_W), sspec],
        compiler_params=_cparams(("parallel",)),
        name="gdn_decode",
    )(qkvg, conv_state, z, ab, state, conv_w, alog, dtb, gnorm)


def _memkv_kernel(x_ref, g_ref, w_ref, kn_ref, k_ref, v_ref):
    hb = _rms(x_ref[...], g_ref[...]).astype(BF16)
    for hd in range(MEM_HEADS):
        hs = slice(hd * MEM_HEAD_DIM, (hd + 1) * MEM_HEAD_DIM)
        k_ref[:, hs] = _rms(jnp.dot(hb, w_ref[:, hs], preferred_element_type=F32), kn_ref[...])
    v_ref[...] = jnp.dot(hb, w_ref[:, BRANCH_W:], preferred_element_type=F32)


def _memkv(mem2, gain, w_kv, kn, tm=512):
    t = mem2.shape[0]
    return pl.pallas_call(
        _memkv_kernel,
        out_shape=[jax.ShapeDtypeStruct((t, BRANCH_W), F32)] * 2,
        grid=(t // tm,),
        in_specs=[pl.BlockSpec((tm, D_MODEL), lambda i: (i, 0)),
                  pl.BlockSpec((1, D_MODEL), lambda i: (0, 0)),
                  pl.BlockSpec((D_MODEL, 2 * BRANCH_W), lambda i: (0, 0)),
                  pl.BlockSpec((1, MEM_HEAD_DIM), lambda i: (0, 0))],
        out_specs=[pl.BlockSpec((tm, BRANCH_W), lambda i: (i, 0))] * 2,
        compiler_params=_cparams(("parallel",)),
        name="memkv",
    )(mem2, gain, w_kv, kn)


def _memattn_prompt_kernel(q_ref, k_ref, v_ref, qn_ref, o_ref):
    q = q_ref[0]
    k = k_ref[0]
    v = v_ref[0]
    scale = MEM_HEAD_DIM ** -0.5
    for hd in range(MEM_HEADS):
        hs = slice(hd * MEM_HEAD_DIM, (hd + 1) * MEM_HEAD_DIM)
        s = _bdot_nt(_rms(q[:, hs], qn_ref[...]), k[:, hs]) * scale
        p = jnp.exp(s - jnp.max(s, axis=-1, keepdims=True))
        o_ref[0, :, hs] = _bdot(p, v[:, hs]) / jnp.sum(p, axis=-1, keepdims=True)


def _memattn_prompt(qm, mk, mv, qn):
    b, l, _ = qm.shape
    tq = 512 if l % 512 == 0 else WINDOW
    return pl.pallas_call(
        _memattn_prompt_kernel,
        out_shape=jax.ShapeDtypeStruct((b, l, BRANCH_W), F32),
        grid=(b, l // tq),
        in_specs=[pl.BlockSpec((1, tq, BRANCH_W), lambda i, j: (i, j, 0)),
                  pl.BlockSpec((1, MEM_LEN, BRANCH_W), lambda i, j: (i, 0, 0)),
                  pl.BlockSpec((1, MEM_LEN, BRANCH_W), lambda i, j: (i, 0, 0)),
                  pl.BlockSpec((1, MEM_HEAD_DIM), lambda i, j: (0, 0))],
        out_specs=pl.BlockSpec((1, tq, BRANCH_W), lambda i, j: (i, j, 0)),
        compiler_params=_cparams(("parallel", "parallel")),
        name="memattn_prompt",
    )(qm, mk, mv, qn)


def _memattn_decode_kernel(bs, q_ref, k_ref, v_ref, qn_ref, o_ref):
    scale = MEM_HEAD_DIM ** -0.5
    for b in range(bs):
        qn = _rms(q_ref[b], qn_ref[...])
        s = jnp.sum(k_ref[b] * qn, axis=-1, keepdims=True) * scale
        p = jnp.exp(s - jnp.max(s, axis=0, keepdims=True))
        den = jnp.sum(p, axis=0)
        o_ref[b] = jnp.sum(p * v_ref[b], axis=0) / den


def _memattn_decode(qm, ck, cv, qn, bs=8):
    b = qm.shape[0]
    q3 = pl.BlockSpec((bs, MEM_HEADS, MEM_HEAD_DIM), lambda i: (i, 0, 0))
    kv = pl.BlockSpec((bs, MEM_LEN, MEM_HEADS, MEM_HEAD_DIM), lambda i: (i, 0, 0, 0))
    o = pl.pallas_call(
        functools.partial(_memattn_decode_kernel, bs),
        out_shape=jax.ShapeDtypeStruct((b, MEM_HEADS, MEM_HEAD_DIM), F32),
        grid=(b // bs,),
        in_specs=[q3, kv, kv, pl.BlockSpec((1, MEM_HEAD_DIM), lambda i: (0, 0))],
        out_specs=q3,
        compiler_params=_cparams(("parallel",)),
        name="memattn_decode",
    )(qm.reshape(b, MEM_HEADS, MEM_HEAD_DIM), ck, cv, qn)
    return o.reshape(b, BRANCH_W)


def _merge_kernel(oa_ref, ob_ref, oc_ref, gate_ref, x_ref, wb_ref, wo_ref, g2_ref, wr_ref, br_ref,
                  x1_ref, h2_ref, lg_ref):
    acc = None
    for i, o_ref in enumerate((oa_ref, ob_ref, oc_ref)):
        mixed = jnp.dot(o_ref[...].astype(BF16), wb_ref[i], preferred_element_type=F32)
        term = gate_ref[:, i * D_MODEL:(i + 1) * D_MODEL].astype(F32) * mixed
        acc = term if acc is None else acc + term
    x1 = x_ref[...] + jnp.dot(acc.astype(BF16), wo_ref[...], preferred_element_type=F32)
    x1_ref[...] = x1
    h2 = _rms(x1, g2_ref[...])
    h2_ref[...] = h2.astype(BF16)
    h_hi = h2.astype(BF16)
    h_lo = (h2 - h_hi.astype(F32)).astype(BF16)
    nt = (((1,), (1,)), ((), ()))
    a = lax.dot_general(wr_ref[...], h_hi, nt, preferred_element_type=F32)
    b = lax.dot_general(wr_ref[0:N_EXPERTS, :], h_lo, nt, preferred_element_type=F32)
    lg_ref[...] = a[0:N_EXPERTS] + a[N_EXPERTS:] + b + br_ref[...]


def _merge_into_kernel(oa_ref, ob_ref, oc_ref, gate_ref, x_ref, wb_ref, wo_ref, g2_ref, wr_ref, br_ref,
                       x1_in, h2_in, lg_in, x1_ref, h2_ref, lg_ref):
    del x1_in, h2_in, lg_in
    _merge_kernel(oa_ref, ob_ref, oc_ref, gate_ref, x_ref, wb_ref, wo_ref, g2_ref, wr_ref, br_ref,
                  x1_ref, h2_ref, lg_ref)


def _merge(oa, ob, oc, gate, x2, wb, wo, g2, wr_t, br_col, tm, t_out=None, into=None):
    t = x2.shape[0]
    t_out = t_out or t
    blk0 = 0 if into is None else (into[0].shape[0] - t) // tm
    row = lambda w: pl.BlockSpec((tm, w), lambda i: (i, 0))
    orow = lambda w: pl.BlockSpec((tm, w), lambda i: (i + blk0, 0))
    full = lambda *shape: pl.BlockSpec(shape, lambda i: (0,) * len(shape))
    in_specs = [row(BRANCH_W), row(BRANCH_W), row(BRANCH_W), row(N_BRANCH * D_MODEL), row(D_MODEL),
                full(N_BRANCH, BRANCH_W, D_MODEL), full(D_MODEL, D_MODEL), full(1, D_MODEL),
                full(2 * N_EXPERTS, D_MODEL), full(N_EXPERTS, 1)]
    args = [oa, ob, oc, gate, x2, wb, wo, g2, wr_t, br_col]
    aliases = {}
    if into is not None:
        t_out = into[0].shape[0]
        in_specs += [pl.BlockSpec(memory_space=pl.ANY)] * 3
        aliases = {len(args) + k: k for k in range(3)}
        args += list(into)
    return pl.pallas_call(
        _merge_kernel if into is None else _merge_into_kernel,
        out_shape=[jax.ShapeDtypeStruct((t_out, D_MODEL), F32),
                   jax.ShapeDtypeStruct((t_out, D_MODEL), BF16),
                   jax.ShapeDtypeStruct((N_EXPERTS, t_out), F32)],
        grid=(t // tm,),
        in_specs=in_specs,
        out_specs=[orow(D_MODEL), orow(D_MODEL), pl.BlockSpec((N_EXPERTS, tm), lambda i: (0, i + blk0))],
        input_output_aliases=aliases,
        compiler_params=_cparams(("parallel",)),
        name="merge",
    )(*args)


def _col_to_row(col):
    n = col.shape[0]
    r = lax.broadcasted_iota(I32, (n, n), 0)
    c = lax.broadcasted_iota(I32, (n, n), 1)
    return jnp.sum(jnp.where(r == c, col, 0.0), axis=0, keepdims=True)


def _row_to_col(row):
    n = row.shape[1]
    r = lax.broadcasted_iota(I32, (n, n), 0)
    c = lax.broadcasted_iota(I32, (n, n), 1)
    return jnp.sum(jnp.where(r == c, row, 0.0), axis=1, keepdims=True)


def _lane_pad(row):
    return jnp.concatenate([row, jnp.zeros((1, LANES - row.shape[1]), row.dtype)], axis=1)


def _route_kernel(lg_ref, pos_ref, gate_ref, meta_ref, cnt_ref, carry):
    i = pl.program_id(0)
    tn = lg_ref.shape[1]

    @pl.when(i == 0)
    def _():
        carry[...] = jnp.zeros_like(carry)

    l = lg_ref[...]
    eio = lax.broadcasted_iota(I32, (N_EXPERTS, tn), 0)
    hot = jnp.zeros((N_EXPERTS, tn), F32)
    vals, idxs = [], []
    for _ in range(TOP_K):
        m = jnp.max(l, axis=0, keepdims=True)
        idx = jnp.min(jnp.where(l == m, eio, N_EXPERTS), axis=0, keepdims=True)
        sel = eio == idx
        vals.append(m)
        idxs.append(idx)
        hot = hot + sel.astype(F32)
        l = jnp.where(sel, -jnp.inf, l)
    ex = [jnp.exp(v - vals[0]) for v in vals]
    tot = ex[0] + ex[1] + ex[2] + ex[3]
    before = (lax.broadcasted_iota(I32, (tn, tn), 0) < lax.broadcasted_iota(I32, (tn, tn), 1)).astype(BF16)
    within = jnp.dot(hot.astype(BF16), before, preferred_element_type=F32)
    cnt_col = jnp.sum(hot, axis=1, keepdims=True)
    room = jnp.floor((cnt_col + (MOE_UNIT - 1)) * (1.0 / MOE_UNIT)) * MOE_UNIT
    r = lax.broadcasted_iota(I32, (N_EXPERTS, N_EXPERTS), 0)
    c = lax.broadcasted_iota(I32, (N_EXPERTS, N_EXPERTS), 1)
    start_col = _row_to_col(jnp.sum(jnp.where(r < c, room, 0.0), axis=0, keepdims=True))
    for k in range(TOP_K):
        gate_ref[k:k + 1, :] = ex[k] / tot
        pos_ref[k:k + 1, :] = jnp.sum(jnp.where(eio == idxs[k], start_col + within, 0.0), axis=0,
                                      keepdims=True).astype(I32)
    meta_ref[0] = _lane_pad(jnp.concatenate([_col_to_row(room), _col_to_row(carry[:, 0:1])], axis=1)).astype(I32)
    carry[...] = carry[...] + room
    cnt_ref[...] = carry[...]


def _route(logits_t, tn):
    t = logits_t.shape[1]
    kt = pl.BlockSpec((TOP_K, tn), lambda i: (0, i))
    return pl.pallas_call(
        _route_kernel,
        out_shape=[jax.ShapeDtypeStruct((TOP_K, t), I32),
                   jax.ShapeDtypeStruct((TOP_K, t), F32),
                   jax.ShapeDtypeStruct((t // tn, 1, LANES), I32),
                   jax.ShapeDtypeStruct((N_EXPERTS, LANES), F32)],
        grid=(t // tn,),
        in_specs=[pl.BlockSpec((N_EXPERTS, tn), lambda i: (0, i))],
        out_specs=[kt, kt, pl.BlockSpec((1, 1, LANES), lambda i: (i, 0, 0)),
                   pl.BlockSpec((N_EXPERTS, LANES), lambda i: (0, 0))],
        scratch_shapes=[pltpu.VMEM((N_EXPERTS, LANES), F32)],
        compiler_params=_cparams(("arbitrary",)),
        name="moe_route",
    )(logits_t)


def _layout_kernel(bm, cnt_ref, be_ref, na_ref, ps_ref, pe_ref, ct_ref):
    nbp = be_ref.shape[1]
    cnt_col = cnt_ref[:, 0:1]
    size = jnp.floor((cnt_col + (bm - 1)) * (1.0 / bm)) * bm
    r = lax.broadcasted_iota(I32, (N_EXPERTS, N_EXPERTS), 0)
    c = lax.broadcasted_iota(I32, (N_EXPERTS, N_EXPERTS), 1)
    ends_row = jnp.sum(jnp.where(r <= c, size, 0.0), axis=0, keepdims=True)
    ends_col = _row_to_col(ends_row)
    nact = ends_row[:, N_EXPERTS - 1:N_EXPERTS] * (1.0 / bm)
    blk = jnp.minimum(lax.broadcasted_iota(I32, (1, nbp), 1).astype(F32), nact - 1.0)
    be = jnp.sum((ends_col <= blk * bm).astype(F32), axis=0, keepdims=True)
    be_ref[...] = jnp.minimum(be, N_EXPERTS - 1.0).astype(I32)
    na_ref[...] = jnp.broadcast_to(nact, na_ref.shape).astype(I32)
    ps_ref[...] = _lane_pad(ends_row - _col_to_row(size)).astype(I32)
    pe_ref[...] = _lane_pad(ends_row).astype(I32)
    ct_ref[...] = _lane_pad(_col_to_row(cnt_col)).astype(I32)


def _layout(cnt, nbp, bm):
    row = jax.ShapeDtypeStruct((1, LANES), I32)
    return pl.pallas_call(
        functools.partial(_layout_kernel, bm),
        out_shape=[jax.ShapeDtypeStruct((1, nbp), I32), row, row, row, row],
        name="moe_layout",
    )(cnt)


def _stage_rows(tn):
    return tn * TOP_K + N_EXPERTS * MOE_UNIT


def _run_copies(meta_ref, ps_ref, make_copy, tn):
    top = (tn // MOE_UNIT).bit_length() - 1
    src = jnp.int32(0)
    total = jnp.int32(0)
    for e in range(N_EXPERTS):
        units = lax.shift_right_logical(meta_ref[0, 0, e], 3)
        dst = ps_ref[e] + meta_ref[0, 0, N_EXPERTS + e]
        off = jnp.int32(0)
        for k in range(top, -1, -1):
            rows = MOE_UNIT << k
            take = lax.shift_right_logical(units, k) & 1

            @pl.when(take == 1)
            def _(src=src, dst=dst, off=off, rows=rows):
                make_copy(pl.multiple_of(src + off, MOE_UNIT), pl.multiple_of(dst + off, MOE_UNIT), rows).start()
            off = off + take * rows
        src = src + units * MOE_UNIT
        total = total + units
    return total


def _wait_copies(units, make_copy, max_rows):
    for k in range((max_rows // MOE_UNIT).bit_length()):
        if (MOE_UNIT << k) > max_rows:
            break

        @pl.when(lax.shift_right_logical(units, k) & 1 == 1)
        def _(k=k):
            make_copy(0, 0, MOE_UNIT << k).wait()


def _wait_units(count, make_copy):
    def body(u, carry):
        make_copy(0, 0).wait()
        return carry
    lax.fori_loop(0, count, body, 0)


def _dispatch_kernel(ps_ref, pe_ref, ct_ref, meta_ref, pos_ref, gate_ref, h_ref, xg_ref, stage, zrows, zblock,
                     started, sem, zsem):
    i = pl.program_id(0)
    n = pl.num_programs(0)
    slot = i % 2
    rows, tn = stage.shape[1], pos_ref.shape[1]

    def copy(s):
        return lambda src, dst, nrows: pltpu.make_async_copy(stage.at[s, pl.ds(src, nrows)],
                                                             xg_ref.at[pl.ds(dst, nrows)], sem.at[s])

    @pl.when(i >= 2)
    def _():
        _wait_copies(started[slot], copy(slot), rows)

    srow = lax.broadcasted_iota(I32, (rows, tn), 0)
    hits = [srow == pos_ref[k:k + 1, :] for k in range(TOP_K)]
    hit = hits[0]
    gate_at = jnp.where(hits[0], gate_ref[0:1, :], 0.0)
    for k in range(1, TOP_K):
        hit = hit | hits[k]
        gate_at = gate_at + jnp.where(hits[k], gate_ref[k:k + 1, :], 0.0)
    onehot = jnp.where(hit, 1.0, 0.0).astype(BF16)
    stage[slot, :, 0:D_MODEL] = jnp.dot(onehot, h_ref[...], preferred_element_type=F32)
    stage[slot, :, D_MODEL:] = jnp.broadcast_to(jnp.sum(gate_at, axis=1, keepdims=True), (rows, LANES))
    started[slot] = _run_copies(meta_ref, ps_ref, copy(slot), tn)

    @pl.when(i == n - 1)
    def _():
        _wait_copies(started[slot], copy(slot), rows)

        @pl.when(n >= 2)
        def _():
            _wait_copies(started[1 - slot], copy(1 - slot), rows)

        zrows[...] = jnp.zeros_like(zrows)
        zero = lambda src, dst: pltpu.make_async_copy(zrows, xg_ref.at[pl.ds(dst, MOE_UNIT)], zsem)
        total = jnp.int32(0)
        for e in range(N_EXPERTS):
            lo = ps_ref[e] + ct_ref[e]
            units = lax.shift_right_logical(pe_ref[e] - lo, 3)

            def body(u, carry, lo=lo):
                zero(0, pl.multiple_of(lo + u * MOE_UNIT, MOE_UNIT)).start()
                return carry
            lax.fori_loop(0, units, body, 0)
            total = total + units
        _wait_units(total, zero)

        zblock[...] = jnp.zeros_like(zblock)
        bm = zblock.shape[0]
        zero_block = lambda blk: pltpu.make_async_copy(zblock, xg_ref.at[pl.ds(pl.multiple_of(blk * bm, bm), bm)],
                                                       zsem)
        first = lax.shift_right_logical(pe_ref[N_EXPERTS - 1], bm.bit_length() - 1)
        last = xg_ref.shape[0] // bm

        def start_block(blk, carry):
            zero_block(blk).start()
            return carry
        lax.fori_loop(first, last, start_block, 0)

        def wait_block(blk, carry):
            zero_block(0).wait()
            return carry
        lax.fori_loop(first, last, wait_block, 0)


def _dispatch(ps, pe, ct, meta, pos, gates, h2, n_slots, tn, bm):
    t = h2.shape[0]
    width = D_MODEL + LANES
    return pl.pallas_call(
        _dispatch_kernel,
        out_shape=jax.ShapeDtypeStruct((n_slots, width), F32),
        grid_spec=pltpu.PrefetchScalarGridSpec(
            num_scalar_prefetch=3,
            grid=(t // tn,),
            in_specs=[pl.BlockSpec((1, 1, LANES), lambda i, *_: (i, 0, 0), memory_space=pltpu.SMEM),
                      pl.BlockSpec((TOP_K, tn), lambda i, *_: (0, i)),
                      pl.BlockSpec((TOP_K, tn), lambda i, *_: (0, i)),
                      pl.BlockSpec((tn, D_MODEL), lambda i, *_: (i, 0))],
            out_specs=pl.BlockSpec(memory_space=pl.ANY),
            scratch_shapes=[pltpu.VMEM((2, _stage_rows(tn), width), F32),
                            pltpu.VMEM((MOE_UNIT, width), F32),
                            pltpu.VMEM((bm, width), F32),
                            pltpu.SMEM((2,), I32),
                            pltpu.SemaphoreType.DMA((2,)),
                            pltpu.SemaphoreType.DMA(())]),
        compiler_params=_cparams(("arbitrary",)),
        name="moe_dispatch",
    )(ps, pe, ct, meta, pos, gates, h2)


def _expert_kernel(be_ref, na_ref, x_ref, w1_ref, b1_ref, w2_ref, b2_ref, y_ref, w1s, w2s):
    i = pl.program_id(0)
    e = be_ref[i]
    prev = be_ref[jnp.maximum(i - 1, 0)]

    @pl.when((i == 0) | (e != prev))
    def _():
        rows = 128
        for r0 in range(0, D_MODEL, rows):
            w1s[r0:r0 + rows, :] = w1_ref[r0:r0 + rows, :].astype(BF16)
        for r0 in range(0, D_FF, rows):
            w2s[r0:r0 + rows, :] = w2_ref[r0:r0 + rows, :].astype(BF16)

    @pl.when(i < na_ref[0])
    def _():
        x = x_ref[:, 0:D_MODEL].astype(BF16)
        hmid = jnp.dot(x, w1s[...], preferred_element_type=F32) + b1_ref[...]
        glu = jnp.minimum(hmid[:, :D_FF], SWIGLU_LIMIT)
        lin = jnp.clip(hmid[:, D_FF:], -SWIGLU_LIMIT, SWIGLU_LIMIT)
        act = glu * _sigmoid(SWIGLU_ALPHA * glu) * (lin + 1.0)
        y = jnp.dot(act.astype(BF16), w2s[...], preferred_element_type=F32) + b2_ref[...]
        y_ref[...] = y * x_ref[:, D_MODEL:D_MODEL + 1]

    @pl.when(i >= na_ref[0])
    def _():
        y_ref[...] = jnp.zeros_like(y_ref)


def _experts(be, nact, xg, layer, w1, b1, w2, b2, bm):
    n_slots = xg.shape[0]
    nb = n_slots // bm
    blk = lambda i, be_r, na_r, *_: (jnp.minimum(i, na_r[0] - 1), 0)
    wsel = lambda i, be_r, *_: (layer, be_r[i], 0, 0)
    return pl.pallas_call(
        _expert_kernel,
        out_shape=jax.ShapeDtypeStruct((n_slots, D_MODEL), F32),
        grid_spec=pltpu.PrefetchScalarGridSpec(
            num_scalar_prefetch=2,
            grid=(nb,),
            in_specs=[pl.BlockSpec((bm, xg.shape[1]), blk),
                      pl.BlockSpec((None, None, D_MODEL, 2 * D_FF), wsel),
                      pl.BlockSpec((None, None, 1, 2 * D_FF), wsel),
                      pl.BlockSpec((None, None, D_FF, D_MODEL), wsel),
                      pl.BlockSpec((None, None, 1, D_MODEL), wsel)],
            out_specs=pl.BlockSpec((bm, D_MODEL), lambda i, *_: (i, 0)),
            scratch_shapes=[pltpu.VMEM((D_MODEL, 2 * D_FF), BF16), pltpu.VMEM((D_FF, D_MODEL), BF16)]),
        compiler_params=_cparams(("arbitrary",)),
        name="moe_experts",
    )(be, nact, xg, w1, b1, w2, b2)


def _combine_kernel(n_first, ps_ref, mcur_ref, mnext_ref, pos_ref, x1_ref, yg_ref, o_ref, o2_ref, stage,
                    started, sem):
    i = pl.program_id(0)
    n = pl.num_programs(0)
    slot = i % 2
    rows, tn = stage.shape[1], pos_ref.shape[1]

    @pl.when(i == 0)
    def _():
        stage[...] = jnp.zeros_like(stage)

    def copy(s):
        return lambda src, dst, nrows: pltpu.make_async_copy(yg_ref.at[pl.ds(dst, nrows)],
                                                             stage.at[s, pl.ds(src, nrows)], sem.at[s])

    @pl.when(i == 0)
    def _():
        started[0] = _run_copies(mcur_ref, ps_ref, copy(0), tn)

    @pl.when(i + 1 < n)
    def _():
        started[1 - slot] = _run_copies(mnext_ref, ps_ref, copy(1 - slot), tn)

    _wait_copies(started[slot], copy(slot), rows)

    eye = (lax.broadcasted_iota(I32, (tn, tn), 0) == lax.broadcasted_iota(I32, (tn, tn), 1)).astype(F32)
    cols = _hdot_nt(eye, pos_ref[...].astype(F32)).astype(I32)
    lane = lax.broadcasted_iota(I32, (tn, rows), 1)
    hit = lane == cols[:, 0:1]
    for k in range(1, TOP_K):
        hit = hit | (lane == cols[:, k:k + 1])
    pick = jnp.where(hit, 1.0, 0.0).astype(BF16)
    out = x1_ref[...] + jnp.dot(pick, stage[slot].astype(BF16), preferred_element_type=F32)

    @pl.when(i < n_first)
    def _():
        o_ref[...] = out

    @pl.when(i >= n_first)
    def _():
        o2_ref[...] = out


def _combine(ps, meta, pos, x1, yg, tn, t_first):
    t = x1.shape[0]
    nt = t // tn
    n_first = t_first // tn
    return pl.pallas_call(
        functools.partial(_combine_kernel, n_first),
        out_shape=[jax.ShapeDtypeStruct((t_first, D_MODEL), F32),
                   jax.ShapeDtypeStruct((t - t_first, D_MODEL), F32)],
        grid_spec=pltpu.PrefetchScalarGridSpec(
            num_scalar_prefetch=1,
            grid=(nt,),
            in_specs=[pl.BlockSpec((1, 1, LANES), lambda i, *_: (i, 0, 0), memory_space=pltpu.SMEM),
                      pl.BlockSpec((1, 1, LANES), lambda i, *_: (jnp.minimum(i + 1, nt - 1), 0, 0),
                                   memory_space=pltpu.SMEM),
                      pl.BlockSpec((TOP_K, tn), lambda i, *_: (0, i)),
                      pl.BlockSpec((tn, D_MODEL), lambda i, *_: (i, 0)),
                      pl.BlockSpec(memory_space=pl.ANY)],
            out_specs=[pl.BlockSpec((tn, D_MODEL), lambda i, *_: (jnp.minimum(i, n_first - 1), 0)),
                       pl.BlockSpec((tn, D_MODEL), lambda i, *_: (jnp.maximum(i - n_first, 0), 0))],
            scratch_shapes=[pltpu.VMEM((2, _stage_rows(tn), D_MODEL), F32),
                            pltpu.SMEM((2,), I32),
                            pltpu.SemaphoreType.DMA((2,))]),
        compiler_params=_cparams(("arbitrary",)),
        name="moe_combine",
    )(ps, meta, meta, pos, x1, yg)


def _moe(x1, h2, logits_t, layer, w1, b1, w2, b2, t_first):
    t = x1.shape[0]
    tn = MOE_TOK_TILE
    bm = MOE_BM if t * TOP_K >= N_EXPERTS * MOE_BM else LANES
    n_blocks = (t * TOP_K + (t // tn) * N_EXPERTS * (MOE_UNIT - 1)) // bm + N_EXPERTS + 1
    nbp = -(-n_blocks // LANES) * LANES
    pos, gates, meta, cnt = _route(logits_t, tn)
    be, nact, ps, pe, ct = _layout(cnt, nbp, bm)
    ps, pe, ct = ps[0, :N_EXPERTS], pe[0, :N_EXPERTS], ct[0, :N_EXPERTS]
    xg = _dispatch(ps, pe, ct, meta, pos, gates, h2, n_blocks * bm, tn, bm)
    yg = _experts(be[0, :n_blocks], nact[0, :1], xg, layer, w1, b1, w2, b2, bm)
    return _combine(ps, meta, pos, x1, yg, tn, t_first)


def _pack_w_in(w):
    ab = jnp.pad(w[:, _IN_MAIN:_IN_MAIN + _IN_AB], ((0, 0), (0, LANES - _IN_AB)))
    return jnp.concatenate([w[:, :_IN_MAIN], ab, w[:, _IN_MAIN + _IN_AB:]], axis=1).astype(BF16)


def _lane_row(v):
    return jnp.pad(v.astype(F32), (0, LANES - v.shape[0])).reshape(1, LANES)


def _layer_weights(ln1_gain, w_in, q_norm_swa, k_norm_swa, swa_sinks, conv_w, a_log, dt_bias, gdn_norm,
                   q_norm_mem, w_branch, w_out, ln2_gain, w_router, b_router, w_mlp1, b_mlp1, w_mlp2, b_mlp2, layer):
    depth = w_mlp1.shape[0]
    wr_hi = w_router.T.astype(BF16)
    wr_lo = (w_router.T - wr_hi.astype(F32)).astype(BF16)
    return dict(
        layer=layer, wr_t=jnp.concatenate([wr_hi, wr_lo], axis=0), b1r=b_mlp1.reshape(depth, N_EXPERTS, 1, -1), b2r=b_mlp2.reshape(depth, N_EXPERTS, 1, -1),
        ln1=ln1_gain.reshape(1, -1), wp=_pack_w_in(w_in),
        qn_s=q_norm_swa.reshape(1, -1), kn_s=k_norm_swa.reshape(1, -1), sinks=swa_sinks.reshape(1, -1),
        conv_w=conv_w, alog=_lane_row(a_log), dtb=_lane_row(dt_bias), gnorm=gdn_norm.reshape(1, -1),
        qn_m=q_norm_mem.reshape(1, -1), wb=w_branch.astype(BF16), wo=w_out.astype(BF16),
        ln2=ln2_gain.reshape(1, -1), br=b_router.reshape(-1, 1),
        w1=w_mlp1, w2=w_mlp2)


def _finish(lw, first, second):
    t1, t2 = first[4].shape[0], second[4].shape[0]
    tm1 = MERGE_ROWS if t1 % MERGE_ROWS == 0 else MOE_TOK_TILE
    pad = (-t2) % MOE_TOK_TILE
    second = [jnp.pad(a, ((0, pad), (0, 0))) for a in second]
    wts = (lw["wb"], lw["wo"], lw["ln2"], lw["wr_t"], lw["br"])
    bufs = _merge(*first, *wts, tm1, t_out=t1 + t2 + pad)
    bufs = _merge(*second, *wts, MOE_TOK_TILE, into=bufs)
    y1, y2 = _moe(*bufs, lw["layer"], lw["w1"], lw["b1r"], lw["w2"], lw["b2r"], t1)
    return y1, y2[:t2]


def _prompt_layer(x, mem, lw, mem_norm, w_mem_kv, k_norm_mem):
    b, l, d = x.shape
    x2 = x.reshape(b * l, d)
    tm = 256
    tm_in = 2 * tm if l % (2 * tm) == 0 else tm
    qs, ks, vs, qkvg, z, ab, qm, gate, conv_tail = _inproj(x2, lw["ln1"], lw["wp"], lw["conv_w"], tm_in, seq_len=l)
    mk, mv = _memkv(mem.reshape(b * MEM_LEN, d), mem_norm.reshape(1, -1), w_mem_kv.astype(BF16),
                    k_norm_mem.reshape(1, -1))
    r3 = lambda a: a.reshape(b, l, a.shape[-1])
    o_swa, kwin = _swa_prompt(r3(qs), r3(ks), r3(vs), lw["qn_s"], lw["kn_s"], lw["sinks"])
    o_g, s_fin = _gdn_prompt(r3(qkvg), r3(z), r3(ab), lw["alog"], lw["dtb"], lw["gnorm"])
    o_m = _memattn_prompt(r3(qm), mk.reshape(b, MEM_LEN, BRANCH_W), mv.reshape(b, MEM_LEN, BRANCH_W), lw["qn_m"])
    parts = (o_swa.reshape(b * l, -1), o_g.reshape(b * l, -1), o_m.reshape(b * l, -1), gate, x2)
    new_k = kwin.reshape(b, WINDOW, SWA_KV_HEADS, SWA_HEAD_DIM)
    new_v = r3(vs)[:, l - WINDOW:].reshape(b, WINDOW, SWA_KV_HEADS, SWA_HEAD_DIM)
    new_conv = conv_tail[:, _HALO - (CONV_WIDTH - 1):]
    mk4 = mk.reshape(b, MEM_LEN, MEM_HEADS, MEM_HEAD_DIM)
    mv4 = mv.reshape(b, MEM_LEN, MEM_HEADS, MEM_HEAD_DIM)
    return parts, (new_k, new_v, s_fin, new_conv, mk4, mv4)


def _sample_layer(x, cache_k, cache_v, state, conv_state, mem_k, mem_v, lw):
    b, l, d = x.shape
    x2 = x.reshape(b, d)
    tm = TOK_TILE
    qs, ks, vs, qkvg, z, ab, qm, gate = _inproj(x2, lw["ln1"], lw["wp"], lw["conv_w"], tm)
    o_swa, new_k, new_v = _swa_decode(qs, ks, vs, cache_k, cache_v, lw["qn_s"], lw["kn_s"], lw["sinks"])
    o_g, new_s = _gdn_decode(qkvg, z, ab, state, conv_state, lw["conv_w"], lw["alog"], lw["dtb"], lw["gnorm"])
    o_m = _memattn_decode(qm, mem_k, mem_v, lw["qn_m"])
    new_conv = jnp.concatenate([conv_state[:, 1:], qkvg[:, None, :]], axis=1)
    shp = (b, WINDOW, SWA_KV_HEADS, SWA_HEAD_DIM)
    return (o_swa, o_g, o_m, gate, x2), (new_k.reshape(shp), new_v.reshape(shp), new_s, new_conv)


def _layer(lw, xp, mem, mem_norm, w_mem_kv, k_norm_mem, xs, cache_k, cache_v, state, conv_state, mem_k, mem_v):
    p_parts, p_state = _prompt_layer(xp, mem, lw, mem_norm, w_mem_kv, k_norm_mem)
    s_parts, s_state = _sample_layer(xs, cache_k, cache_v, state, conv_state, mem_k, mem_v, lw)
    yp, ys = _finish(lw, p_parts, s_parts)
    return yp.reshape(xp.shape), ys.reshape(xs.shape), p_state, s_state


def kernel(x_prompt, x_sample, cache_swa_k, cache_swa_v, state_gdn, state_conv, cache_mem_k, cache_mem_v, mem_prompt, ln1_gain, w_in, q_norm_swa, k_norm_swa, swa_sinks, conv_w, a_log, dt_bias, gdn_norm, q_norm_mem, k_norm_mem, mem_norm, w_mem_kv, w_branch, w_out, ln2_gain, w_router, b_router, w_mlp1, b_mlp1, w_mlp2, b_mlp2):
    xp, xs = x_prompt, x_sample
    outs = [[] for _ in range(10)]
    for l in range(ln1_gain.shape[0]):
        lw = _layer_weights(ln1_gain[l], w_in[l], q_norm_swa[l], k_norm_swa[l], swa_sinks[l], conv_w[l], a_log[l],
                            dt_bias[l], gdn_norm[l], q_norm_mem[l], w_branch[l], w_out[l], ln2_gain[l], w_router[l],
                            b_router[l], w_mlp1, b_mlp1, w_mlp2, b_mlp2, l)
        xp, xs, (pk, pv, ps, pc, mk, mv), (sk, sv, ss, sc) = _layer(
            lw, xp, mem_prompt, mem_norm[l], w_mem_kv[l], k_norm_mem[l],
            xs, cache_swa_k[l], cache_swa_v[l], state_gdn[l], state_conv[l], cache_mem_k[l], cache_mem_v[l])
        for acc, v in zip(outs, (pk, pv, sk, sv, ps, ss, pc, sc, mk, mv)):
            acc.append(v)
    return (xp, xs) + tuple(jnp.stack(o) for o in outs)
```

```python
import functools

import numpy as np
import jax
import jax.numpy as jnp
from jax import lax
from jax.experimental import pallas as pl
from jax.experimental.pallas import tpu as pltpu

F32 = jnp.float32
BF16 = jnp.bfloat16
I32 = jnp.int32

D_MODEL = 1024
BRANCH_W = 512
SWA_HEADS = 8
SWA_KV_HEADS = 2
SWA_HEAD_DIM = 64
SWA_GROUP = SWA_HEADS // SWA_KV_HEADS
SWA_KV_W = SWA_KV_HEADS * SWA_HEAD_DIM
WINDOW = 128
GDN_HEADS = 4
GDN_HEAD_DIM = 128
GDN_W = GDN_HEADS * GDN_HEAD_DIM
GDN_CONV_W = 3 * GDN_W
GDN_CHUNK = 64
CONV_WIDTH = 4
MEM_LEN = 256
MEM_HEADS = 4
MEM_HEAD_DIM = 128
N_BRANCH = 3
N_EXPERTS = 32
TOP_K = 4
D_FF = 1024
SWIGLU_LIMIT = 7.0
SWIGLU_ALPHA = 1.702
EPS = 1e-6

LANES = 128
MOE_BM = 512
MOE_TOK_TILE = 256
MOE_UNIT = 8
TOK_TILE = 128
MERGE_ROWS = 512
VMEM_LIMIT = 56 * 1024 * 1024

_SEG_QS = (0, 512)
_SEG_KS = (512, 128)
_SEG_VS = (640, 128)
_SEG_QKVG = (768, 1536)
_SEG_Z = (2304, 512)
_SEG_AB = (2816, 128)
_SEG_QM = (2944, 512)
_SEG_GATE = (3456, 3072)
_PACKED_W = 6528
_IN_MAIN = 2816
_IN_AB = 8

_NEG = -1e30
_HI = lax.Precision.HIGHEST


def _cparams(sem, vmem=VMEM_LIMIT):
    return pltpu.CompilerParams(dimension_semantics=sem, vmem_limit_bytes=vmem)


def _bdot(a, b):
    return jnp.dot(a.astype(BF16), b.astype(BF16), preferred_element_type=F32)


def _bdot_nt(a, b):
    return lax.dot_general(a.astype(BF16), b.astype(BF16), (((1,), (1,)), ((), ())),
                           preferred_element_type=F32)


def _hdot_nt(a, b):
    return lax.dot_general(a, b, (((1,), (1,)), ((), ())), preferred_element_type=F32, precision=_HI)


def _rms(x, gain):
    return x * lax.rsqrt(jnp.mean(x * x, axis=-1, keepdims=True) + EPS) * gain


def _l2(x):
    return x * lax.rsqrt(jnp.sum(x * x, axis=-1, keepdims=True) + EPS)


def _sigmoid(x):
    return 0.5 * jnp.tanh(0.5 * x) + 0.5


def _silu(x):
    return x * _sigmoid(x)


def _softplus(x):
    return jnp.maximum(x, 0.0) + jnp.log1p(jnp.exp(-jnp.abs(x)))


_HALO = 8


def _inproj_kernel(seq_steps, x_ref, g_ref, w_ref, cw_ref, qs, ks, vs, qkvg, z, ab, qm, gate, *conv_refs):
    i = pl.program_id(0)
    x = x_ref[...]
    hb = _rms(x, g_ref[...]).astype(BF16)

    def proj(off, width):
        return jnp.dot(hb, w_ref[:, off:off + width], preferred_element_type=F32)

    step = 512

    def plain(ref, c0=0, width=None):
        def finish(r):
            ref[:, c0:c0 + (width or r.shape[1])] = r
        return finish

    def gate_chunk(c0):
        def finish(r):
            gate[:, c0:c0 + step] = _sigmoid(r).astype(BF16)
        return finish

    def conv_part(part):
        tail_ref, carry = conv_refs
        c0 = part * GDN_W
        cols = slice(c0, c0 + GDN_W)

        def finish(r):
            first = (i % seq_steps) == 0
            sub = lax.broadcasted_iota(I32, (_HALO, 1), 0)
            rows = r.shape[0]
            before = jnp.where(first, 0.0, carry[:, cols])
            y = r * cw_ref[CONV_WIDTH - 1:CONV_WIDTH, cols]
            top = r[0:_HALO] * cw_ref[CONV_WIDTH - 1:CONV_WIDTH, cols]
            for jj in range(CONV_WIDTH - 1):
                sh = CONV_WIDTH - 1 - jj
                y = y + pltpu.roll(r, sh, 0) * cw_ref[jj:jj + 1, cols]
                head_rows = jnp.where(sub < sh, pltpu.roll(before, sh, 0), pltpu.roll(r[0:_HALO], sh, 0))
                top = top + head_rows * cw_ref[jj:jj + 1, cols]
            carry[:, cols] = r[rows - _HALO:]
            tail_ref[0, :, cols] = r[rows - _HALO:]
            act = _silu(jnp.concatenate([top, y[_HALO:]], axis=0))
            if part < 2:
                scale = GDN_HEAD_DIM ** -0.5 if part == 0 else 1.0
                for h in range(GDN_HEADS):
                    hs = slice(h * GDN_HEAD_DIM, (h + 1) * GDN_HEAD_DIM)
                    qkvg[:, c0 + h * GDN_HEAD_DIM:c0 + (h + 1) * GDN_HEAD_DIM] = _l2(act[:, hs]) * scale
            else:
                qkvg[:, cols] = act
        return finish

    tasks = [(_SEG_QS[0], _SEG_QS[1], plain(qs)), (_SEG_KS[0], _SEG_KS[1], plain(ks)),
             (_SEG_VS[0], _SEG_VS[1], plain(vs)), (_SEG_AB[0], _SEG_AB[1], plain(ab)),
             (_SEG_QM[0], _SEG_QM[1], plain(qm))]
    tasks += [(_SEG_GATE[0] + c0, step, gate_chunk(c0)) for c0 in range(0, _SEG_GATE[1], step)]
    if seq_steps is None:
        tasks += [(_SEG_Z[0], _SEG_Z[1], plain(z))]
        tasks += [(_SEG_QKVG[0] + c0, step, plain(qkvg, c0, step)) for c0 in range(0, GDN_CONV_W, step)]
    else:
        def silu_z(r):
            z[...] = _silu(r)
        tasks += [(_SEG_QKVG[0] + p * GDN_W, GDN_W, conv_part(p)) for p in range(3)]
        tasks.append((_SEG_Z[0], _SEG_Z[1], silu_z))
    for off, width, finish in tasks:
        finish(proj(off, width))


def _inproj(x2, gain, wp, conv_w, tm, seq_len=None):
    t = x2.shape[0]
    segs = (_SEG_QS, _SEG_KS, _SEG_VS, _SEG_QKVG, _SEG_Z, _SEG_AB, _SEG_QM, _SEG_GATE)
    dtypes = (F32,) * 7 + (BF16,)
    out_shape = [jax.ShapeDtypeStruct((t, w), dt) for (_, w), dt in zip(segs, dtypes)]
    out_specs = [pl.BlockSpec((tm, w), lambda i: (i, 0)) for _, w in segs]
    scratch = []
    seq_steps = None
    if seq_len is not None:
        seq_steps = seq_len // tm
        out_shape.append(jax.ShapeDtypeStruct((t // seq_len, _HALO, GDN_CONV_W), F32))
        out_specs.append(pl.BlockSpec((1, _HALO, GDN_CONV_W), lambda i: (i // seq_steps, 0, 0)))
        scratch.append(pltpu.VMEM((_HALO, GDN_CONV_W), F32))
    return pl.pallas_call(
        functools.partial(_inproj_kernel, seq_steps),
        out_shape=out_shape,
        grid=(t // tm,),
        in_specs=[pl.BlockSpec((tm, D_MODEL), lambda i: (i, 0)),
                  pl.BlockSpec((1, D_MODEL), lambda i: (0, 0)),
                  pl.BlockSpec((D_MODEL, _PACKED_W), lambda i: (0, 0), pipeline_mode=pl.Buffered(1)),
                  pl.BlockSpec((CONV_WIDTH, GDN_CONV_W), lambda i: (0, 0))],
        out_specs=out_specs,
        scratch_shapes=scratch,
        compiler_params=_cparams(("arbitrary",)),
        name="inproj",
    )(x2, gain, wp, conv_w)


SWA_STEP_BLOCKS = 8


def _alibi_slopes(n):
    return [float(2.0 ** (-8.0 * (i + 1) / n)) for i in range(n)]


def _swa_prompt_kernel(q_ref, kc_ref, vc_ref, qg_ref, kg_ref, sink_ref, o_ref, kwin_ref, bias, kprev, vprev):
    n = pl.program_id(1)
    half = SWA_HEAD_DIM
    nblk = q_ref.shape[1] // WINDOW
    work = [(j, h) for j in range(nblk) for h in range(SWA_HEADS)]

    @pl.when(n == 0)
    def _():
        kprev[...] = jnp.zeros_like(kprev)
        vprev[...] = jnp.zeros_like(vprev)

    @pl.when((pl.program_id(0) == 0) & (n == 0))
    def _():
        row = lax.broadcasted_iota(I32, (WINDOW, 2 * WINDOW), 0)
        col = lax.broadcasted_iota(I32, (WINDOW, 2 * WINDOW), 1)
        dist = row + WINDOW - col
        window = (dist >= 0) & (dist <= WINDOW)
        distf = dist.astype(F32)
        for h, slope in enumerate(_alibi_slopes(SWA_HEADS)):
            bias[0, h] = jnp.where(window & (col >= WINDOW), -slope * distf, _NEG)
            bias[1, h] = jnp.where(window, -slope * distf, _NEG)

    first_table = jnp.where(n == 0, 0, 1)

    low = lax.broadcasted_iota(I32, (1, 2 * half), 1) < half

    def pair_rms(x, gain):
        sq = x * x
        s_lo = jnp.sum(jnp.where(low, sq, 0.0), axis=-1, keepdims=True)
        s_hi = jnp.sum(jnp.where(low, 0.0, sq), axis=-1, keepdims=True)
        return x * lax.rsqrt(jnp.where(low, s_lo, s_hi) * (1.0 / half) + EPS) * gain

    kcn = pair_rms(kc_ref[0], kg_ref[...])
    kwin_ref[0] = kcn[(nblk - 1) * WINDOW:]
    kall = jnp.concatenate([kprev[...], kcn], axis=0)
    vall = jnp.concatenate([vprev[...], vc_ref[0]], axis=0)
    kprev[...] = kcn[(nblk - 1) * WINDOW:]
    vprev[...] = vc_ref[0, (nblk - 1) * WINDOW:, :]
    ones = jnp.ones(((nblk + 1) * WINDOW, 2 * half), BF16)

    def placed(x, aug):
        sw = pltpu.roll(x, half, 1)
        out = {(0, 0): jnp.where(low, x, 0.0), (1, 1): jnp.where(low, 0.0, x),
               (0, 1): jnp.where(low, 0.0, sw), (1, 0): jnp.where(low, sw, 0.0)}
        out = {key: val.astype(BF16) for key, val in out.items()}
        return {key: jnp.concatenate([val, ones], axis=1) for key, val in out.items()} if aug else out

    kvar = placed(kall, False)
    vvar = placed(vall, True)
    qn = [pair_rms(q_ref[0, :, t * 2 * half:(t + 1) * 2 * half], qg_ref[...]).astype(BF16)
          for t in range(SWA_HEADS // 2)]
    where_of = lambda h: (h // SWA_GROUP, h % 2)
    qrows = lambda j: slice(j * WINDOW, (j + 1) * WINDOW)
    krows = lambda j: slice(j * WINDOW, (j + 2) * WINDOW)
    nt = (((1,), (1,)), ((), ()))
    s = {(j, h): lax.dot_general(qn[h // 2][qrows(j)], kvar[where_of(h)][krows(j)], nt, preferred_element_type=F32)
         + bias[first_table if j == 0 else 1, h] for j, h in work}
    sink = [sink_ref[0:1, h:h + 1] for h in range(SWA_HEADS)]
    m = {(j, h): jnp.maximum(jnp.max(s[j, h], axis=-1, keepdims=True), sink[h]) for j, h in work}
    p = {(j, h): jnp.exp(s[j, h] - m[j, h]).astype(BF16) for j, h in work}
    res = {(j, h): jnp.dot(p[j, h], vvar[where_of(h)][krows(j)], preferred_element_type=F32)
           for j, h in work}
    inv = {(j, h): 1.0 / (res[j, h][:, 2 * half:2 * half + 1] + jnp.exp(sink[h] - m[j, h])) for j, h in work}
    for j in range(nblk):
        for t in range(SWA_HEADS // 2):
            h0, h1 = 2 * t, 2 * t + 1
            num = res[j, h0][:, :2 * half] + res[j, h1][:, :2 * half]
            o_ref[0, qrows(j), t * 2 * half:(t + 1) * 2 * half] = num * jnp.where(low, inv[j, h0], inv[j, h1])


def _swa_prompt(qs, ks, vs, qn, kn, sinks):
    b, l, _ = qs.shape
    rows = SWA_STEP_BLOCKS * WINDOW if l % (SWA_STEP_BLOCKS * WINDOW) == 0 else WINDOW
    cur = lambda i, j: (i, j, 0)
    const2 = lambda i, j: (0, 0)
    q_gain = jnp.tile(qn, (1, 2)) * (SWA_HEAD_DIM ** -0.5)
    k_gain = jnp.tile(kn, (1, 2))
    return pl.pallas_call(
        _swa_prompt_kernel,
        out_shape=[jax.ShapeDtypeStruct((b, l, BRANCH_W), F32),
                   jax.ShapeDtypeStruct((b, WINDOW, SWA_KV_W), F32)],
        grid=(b, l // rows),
        in_specs=[pl.BlockSpec((1, rows, BRANCH_W), cur),
                  pl.BlockSpec((1, rows, SWA_KV_W), cur),
                  pl.BlockSpec((1, rows, SWA_KV_W), cur),
                  pl.BlockSpec((1, 2 * SWA_HEAD_DIM), const2),
                  pl.BlockSpec((1, 2 * SWA_HEAD_DIM), const2),
                  pl.BlockSpec((1, SWA_HEADS), const2)],
        out_specs=[pl.BlockSpec((1, rows, BRANCH_W), cur),
                   pl.BlockSpec((1, WINDOW, SWA_KV_W), lambda i, j: (i, 0, 0))],
        scratch_shapes=[pltpu.VMEM((2, SWA_HEADS, WINDOW, 2 * WINDOW), F32),
                        pltpu.VMEM((WINDOW, SWA_KV_W), F32),
                        pltpu.VMEM((WINDOW, SWA_KV_W), F32)],
        compiler_params=_cparams(("arbitrary", "arbitrary")),
        name="swa_prompt",
    )(qs, ks, vs, q_gain, k_gain, sinks)


def _swa_decode_kernel(bs, q_ref, k3_ref, kf_ref, v3_ref, vf_ref, ck_ref, cv_ref, qn_ref, kn_ref, kn2_ref,
                       sink_ref, slope_ref, o_ref, ok_ref, ov_ref):
    scale = SWA_HEAD_DIM ** -0.5
    lane = lax.broadcasted_iota(I32, (1, SWA_KV_W), 1)
    rowi = lax.broadcasted_iota(I32, (WINDOW, SWA_KV_W), 0)
    keyd = (WINDOW - lax.broadcasted_iota(I32, (1, WINDOW), 1)).astype(F32)
    last = rowi == WINDOW - 1
    for b in range(bs):
        kf = kf_ref[b]
        sq = kf * kf
        ms0 = jnp.sum(jnp.where(lane < SWA_HEAD_DIM, sq, 0.0), axis=-1, keepdims=True) / SWA_HEAD_DIM
        ms1 = jnp.sum(jnp.where(lane >= SWA_HEAD_DIM, sq, 0.0), axis=-1, keepdims=True) / SWA_HEAD_DIM
        knf = kf * lax.rsqrt(jnp.where(lane < SWA_HEAD_DIM, ms0, ms1) + EPS) * kn2_ref[...]
        ok_ref[b] = jnp.where(last, knf, pltpu.roll(ck_ref[b], WINDOW - 1, 0))
        ov_ref[b] = jnp.where(last, vf_ref[b], pltpu.roll(cv_ref[b], WINDOW - 1, 0))
    work = [(b, g) for b in range(bs) for g in range(SWA_KV_HEADS)]
    lanes = lambda g: slice(g * SWA_HEAD_DIM, (g + 1) * SWA_HEAD_DIM)
    heads = lambda g: slice(g * SWA_GROUP, (g + 1) * SWA_GROUP)
    qn = [_rms(q_ref[b], qn_ref[...]) for b in range(bs)]
    kn3 = [_rms(k3_ref[b], kn_ref[...]) for b in range(bs)]
    qg = {(b, g): qn[b][heads(g)] for b, g in work}
    s = {(b, g): _bdot_nt(qg[b, g], ck_ref[b, :, lanes(g)]) * scale - slope_ref[heads(g)] * keyd for b, g in work}
    s_new = {(b, g): jnp.sum(qg[b, g] * kn3[b][g:g + 1], axis=-1, keepdims=True) * scale for b, g in work}
    m = {(b, g): jnp.maximum(jnp.maximum(jnp.max(s[b, g], axis=-1, keepdims=True), s_new[b, g]),
                             sink_ref[heads(g)]) for b, g in work}
    p = {w: jnp.exp(s[w] - m[w]) for w in work}
    p_new = {w: jnp.exp(s_new[w] - m[w]) for w in work}
    denom = {(b, g): jnp.sum(p[b, g], axis=-1, keepdims=True) + p_new[b, g] + jnp.exp(sink_ref[heads(g)] - m[b, g])
             for b, g in work}
    pv = {(b, g): _bdot(p[b, g], cv_ref[b, :, lanes(g)]) for b, g in work}
    for b, g in work:
        o_ref[b, heads(g), :] = (pv[b, g] + p_new[b, g] * v3_ref[b, g:g + 1, :]) / denom[b, g]


def _swa_decode(qs, ks, vs, cache_k, cache_v, qn, kn, sinks, bs=8):
    b = qs.shape[0]
    q3 = qs.reshape(b, SWA_HEADS, SWA_HEAD_DIM)
    k3 = ks.reshape(b, SWA_KV_HEADS, SWA_HEAD_DIM)
    kf = ks.reshape(b, 1, SWA_KV_W)
    v3 = vs.reshape(b, SWA_KV_HEADS, SWA_HEAD_DIM)
    vf = vs.reshape(b, 1, SWA_KV_W)
    ck = cache_k.reshape(b, WINDOW, SWA_KV_W)
    cv = cache_v.reshape(b, WINDOW, SWA_KV_W)
    kn2 = jnp.concatenate([kn, kn], axis=-1)
    sink_col = sinks.reshape(SWA_HEADS, 1)
    slope_col = jnp.asarray(np.asarray(_alibi_slopes(SWA_HEADS), np.float32).reshape(SWA_HEADS, 1))
    blk = lambda *shape: pl.BlockSpec((bs,) + shape, lambda i: (i,) + (0,) * len(shape))
    full = lambda *shape: pl.BlockSpec(shape, lambda i: (0,) * len(shape))
    o, ok, ov = pl.pallas_call(
        functools.partial(_swa_decode_kernel, bs),
        out_shape=[jax.ShapeDtypeStruct((b, SWA_HEADS, SWA_HEAD_DIM), F32),
                   jax.ShapeDtypeStruct((b, WINDOW, SWA_KV_W), F32),
                   jax.ShapeDtypeStruct((b, WINDOW, SWA_KV_W), F32)],
        grid=(b // bs,),
        in_specs=[blk(SWA_HEADS, SWA_HEAD_DIM), blk(SWA_KV_HEADS, SWA_HEAD_DIM), blk(1, SWA_KV_W),
                  blk(SWA_KV_HEADS, SWA_HEAD_DIM), blk(1, SWA_KV_W), blk(WINDOW, SWA_KV_W), blk(WINDOW, SWA_KV_W),
                  full(1, SWA_HEAD_DIM), full(1, SWA_HEAD_DIM), full(1, SWA_KV_W),
                  full(SWA_HEADS, 1), full(SWA_HEADS, 1)],
        out_specs=[blk(SWA_HEADS, SWA_HEAD_DIM), blk(WINDOW, SWA_KV_W), blk(WINDOW, SWA_KV_W)],
        compiler_params=_cparams(("parallel",)),
        name="swa_decode",
    )(q3, k3, kf, v3, vf, ck, cv, qn, kn, kn2, sink_col, slope_col)
    return o.reshape(b, BRANCH_W), ok, ov


def _gate_rows(ab, alog_ref, dtb_ref):
    g = -jnp.exp(alog_ref[...]) * _softplus(ab + dtb_ref[...])
    return g, _sigmoid(ab)


GDN_PREP_CHUNKS = 4
GDN_PREP_HEADS = 4
GDN_SCAN_BATCH = 8


def _gdn_prep_kernel(ys, ab_ref, alog_ref, dtb_ref, u_ref, w_ref, qg_ref, kd_ref, in_ref, eg_ref):
    C = GDN_CHUNK
    d = GDN_HEAD_DIM
    n = GDN_PREP_CHUNKS * C
    g_all, beta_all = _gate_rows(ab_ref[0], alog_ref, dtb_ref)
    r = lax.broadcasted_iota(I32, (n, n), 0)
    cc = lax.broadcasted_iota(I32, (n, n), 1)
    same = (r // C) == (cc // C)
    incl = same & (r >= cc)
    strict = same & (r > cc)
    upto = same & (r <= cc)
    eye = r == cc
    for h0 in range(0, GDN_HEADS, GDN_PREP_HEADS):
        heads = range(h0, h0 + GDN_PREP_HEADS)
        q = {h: ys[0, :, h * d:(h + 1) * d] for h in heads}
        k = {h: ys[0, :, GDN_W + h * d:GDN_W + (h + 1) * d] for h in heads}
        beta = {h: beta_all[:, GDN_HEADS + h:GDN_HEADS + h + 1] for h in heads}
        gc_row = {h: jnp.sum(jnp.where(upto, g_all[:, h:h + 1], 0.0), axis=0, keepdims=True) for h in heads}
        gc_col = {h: jnp.sum(jnp.where(eye, gc_row[h], 0.0), axis=1, keepdims=True) for h in heads}
        decay = {h: jnp.where(incl, jnp.exp(jnp.where(incl, gc_col[h] - gc_row[h], 0.0)), 0.0) for h in heads}
        kb = {h: k[h] * beta[h] for h in heads}
        a = {h: jnp.where(strict, _bdot_nt(kb[h], k[h]) * decay[h], 0.0) for h in heads}
        intra = {h: jnp.where(incl, _bdot_nt(q[h], k[h]) * decay[h], 0.0) for h in heads}
        base = 8
        blk = lambda s: (r // s) == (cc // s)
        bp = {h: -jnp.where(blk(base), a[h], 0.0) for h in heads}
        p = {h: eye.astype(F32) + bp[h] for h in heads}
        span = 2
        while span < base:
            bp = {h: _bdot(bp[h], bp[h]) for h in heads}
            p = {h: p[h] + _bdot(p[h], bp[h]) for h in heads}
            span *= 2
        size = base
        while size < C:
            off = {h: jnp.where(blk(2 * size) & jnp.logical_not(blk(size)), a[h], 0.0) for h in heads}
            p = {h: p[h] - _bdot(p[h], _bdot(off[h], p[h])) for h in heads}
            size *= 2
        uw = {h: _bdot(p[h], jnp.concatenate([ys[0, :, 2 * GDN_W + h * d:2 * GDN_W + (h + 1) * d] * beta[h],
                                              kb[h] * jnp.exp(gc_col[h])], axis=1)) for h in heads}
        for h in heads:
            hs = slice(h * d, (h + 1) * d)
            u_ref[0, :, hs] = uw[h][:, :d]
            w_ref[0, :, hs] = uw[h][:, d:].astype(BF16)
            qg_ref[0, :, hs] = (q[h] * jnp.exp(gc_col[h])).astype(BF16)
            for ci in range(GDN_PREP_CHUNKS):
                r0 = ci * C
                g_last = gc_row[h][:, r0 + C - 1:r0 + C]
                kd_ref[0, r0:r0 + C, hs] = (k[h][r0:r0 + C] * jnp.exp(g_last - gc_col[h][r0:r0 + C])).astype(BF16)
                in_ref[0, r0:r0 + C, h * C:(h + 1) * C] = intra[h][r0:r0 + C, r0:r0 + C].astype(BF16)
                eg_ref[0, ci, h:h + 1, :] = jnp.broadcast_to(jnp.exp(g_last), (1, LANES))


def _gdn_scan_kernel(u_ref, w_ref, qg_ref, kd_ref, in_ref, eg_ref, z_ref, gn_ref, o_ref, s_ref, st):
    c = pl.program_id(1)
    C = GDN_CHUNK
    d = GDN_HEAD_DIM

    @pl.when(c == 0)
    def _():
        st[...] = jnp.zeros_like(st)

    chains = [(bi, h) for bi in range(u_ref.shape[0]) for h in range(GDN_HEADS)]
    hsl = lambda h: slice(h * d, (h + 1) * d)
    s_f = [st[bi, h] for bi, h in chains]
    s_b = [s.astype(BF16) for s in s_f]
    ws = [jnp.dot(w_ref[bi, :, hsl(h)], s_b[i], preferred_element_type=F32) for i, (bi, h) in enumerate(chains)]
    qs = [jnp.dot(qg_ref[bi, :, hsl(h)], s_b[i], preferred_element_type=F32) for i, (bi, h) in enumerate(chains)]
    vb = [(u_ref[bi, :, hsl(h)] - ws[i]).astype(BF16) for i, (bi, h) in enumerate(chains)]
    iv = [jnp.dot(in_ref[bi, :, h * C:(h + 1) * C], vb[i], preferred_element_type=F32)
          for i, (bi, h) in enumerate(chains)]
    kv = [lax.dot_general(kd_ref[bi, :, hsl(h)], vb[i], (((0,), (0,)), ((), ())), preferred_element_type=F32)
          for i, (bi, h) in enumerate(chains)]
    for i, (bi, h) in enumerate(chains):
        st[bi, h] = s_f[i] * eg_ref[bi, 0, h:h + 1, :] + kv[i]
        o_ref[bi, :, hsl(h)] = _rms(qs[i] + iv[i], gn_ref[...]) * z_ref[bi, :, hsl(h)]
    s_ref[...] = st[...]


def _gdn_prompt(qkvg, z, ab, alog, dtb, gnorm):
    b, l, _ = qkvg.shape
    C = GDN_CHUNK
    nc = l // C
    rows = GDN_PREP_CHUNKS * C
    cur = lambda i, j: (i, j, 0)
    const2 = lambda i, j: (0, 0)
    u, w, qg, kd, intra, eg = pl.pallas_call(
        _gdn_prep_kernel,
        out_shape=[jax.ShapeDtypeStruct((b, l, GDN_W), F32),
                   jax.ShapeDtypeStruct((b, l, GDN_W), BF16),
                   jax.ShapeDtypeStruct((b, l, GDN_W), BF16),
                   jax.ShapeDtypeStruct((b, l, GDN_W), BF16),
                   jax.ShapeDtypeStruct((b, l, GDN_HEADS * C), BF16),
                   jax.ShapeDtypeStruct((b, nc, GDN_HEADS, LANES), F32)],
        grid=(b, l // rows),
        in_specs=[pl.BlockSpec((1, rows, GDN_CONV_W), cur),
                  pl.BlockSpec((1, rows, LANES), cur),
                  pl.BlockSpec((1, LANES), const2),
                  pl.BlockSpec((1, LANES), const2)],
        out_specs=[pl.BlockSpec((1, rows, GDN_W), cur),
                   pl.BlockSpec((1, rows, GDN_W), cur),
                   pl.BlockSpec((1, rows, GDN_W), cur),
                   pl.BlockSpec((1, rows, GDN_W), cur),
                   pl.BlockSpec((1, rows, GDN_HEADS * C), cur),
                   pl.BlockSpec((1, GDN_PREP_CHUNKS, GDN_HEADS, LANES), lambda i, j: (i, j, 0, 0))],
        compiler_params=_cparams(("parallel", "parallel")),
        name="gdn_prep",
    )(qkvg, ab, alog, dtb)
    bb = GDN_SCAN_BATCH if b % GDN_SCAN_BATCH == 0 else 1
    seq = lambda wd: pl.BlockSpec((bb, C, wd), cur)
    return pl.pallas_call(
        _gdn_scan_kernel,
        out_shape=[jax.ShapeDtypeStruct((b, l, GDN_W), F32),
                   jax.ShapeDtypeStruct((b, GDN_HEADS, GDN_HEAD_DIM, GDN_HEAD_DIM), F32)],
        grid=(b // bb, nc),
        in_specs=[seq(GDN_W), seq(GDN_W), seq(GDN_W), seq(GDN_W), seq(GDN_HEADS * C),
                  pl.BlockSpec((bb, 1, GDN_HEADS, LANES), lambda i, j: (i, j, 0, 0)),
                  seq(GDN_W),
                  pl.BlockSpec((1, GDN_HEAD_DIM), const2)],
        out_specs=[seq(GDN_W),
                   pl.BlockSpec((bb, GDN_HEADS, GDN_HEAD_DIM, GDN_HEAD_DIM), lambda i, j: (i, 0, 0, 0))],
        scratch_shapes=[pltpu.VMEM((bb, GDN_HEADS, GDN_HEAD_DIM, GDN_HEAD_DIM), F32)],
        compiler_params=_cparams(("parallel", "arbitrary")),
        name="gdn_scan",
    )(u, w, qg, kd, intra, eg, z, gnorm)


def _gdn_decode_kernel(bs, x_ref, cs_ref, z_ref, ab_ref, s_ref, cw_ref, alog_ref, dtb_ref, gn_ref, o_ref, so_ref):
    x = x_ref[...]
    y = x * cw_ref[CONV_WIDTH - 1:CONV_WIDTH, :]
    for j in range(CONV_WIDTH - 1):
        y = y + cs_ref[:, j, :] * cw_ref[j:j + 1, :]
    y = _silu(y)
    g_all, beta_all = _gate_rows(ab_ref[...], alog_ref, dtb_ref)
    eg_all = jnp.exp(g_all)
    z = z_ref[...]
    d = GDN_HEAD_DIM
    eye = (lax.broadcasted_iota(I32, (d, d), 0) == lax.broadcasted_iota(I32, (d, d), 1)).astype(F32)
    for h in range(GDN_HEADS):
        hs = slice(h * d, (h + 1) * d)
        qh = _l2(y[:, h * d:(h + 1) * d]) * (d ** -0.5)
        kh = _l2(y[:, GDN_W + h * d:GDN_W + (h + 1) * d])
        vh = y[:, 2 * GDN_W + h * d:2 * GDN_W + (h + 1) * d]
        kt = _hdot_nt(eye, kh)
        qt = _hdot_nt(eye, qh)
        rng = range(bs)
        eg = [eg_all[b:b + 1, h:h + 1] for b in rng]
        beta = [beta_all[b:b + 1, GDN_HEADS + h:GDN_HEADS + h + 1] for b in rng]
        sk = [jnp.sum(s_ref[b, h] * kt[:, b:b + 1], axis=0, keepdims=True) for b in rng]
        v_new = [beta[b] * (vh[b:b + 1] - eg[b] * sk[b]) for b in rng]
        s_new = [eg[b] * s_ref[b, h] + kt[:, b:b + 1] * v_new[b] for b in rng]
        for b in rng:
            so_ref[b, h] = s_new[b]
        o = jnp.concatenate([jnp.sum(s_new[b] * qt[:, b:b + 1], axis=0, keepdims=True) for b in rng], axis=0)
        o_ref[:, hs] = _rms(o, gn_ref[...]) * _silu(z[:, hs])


def _gdn_decode(qkvg, z, ab, state, conv_state, conv_w, alog, dtb, gnorm, bs=8):
    b = qkvg.shape[0]
    row = lambda w: pl.BlockSpec((bs, w), lambda i: (i, 0))
    full = lambda *shape: pl.BlockSpec(shape, lambda i: (0,) * len(shape))
    sspec = pl.BlockSpec((bs, GDN_HEADS, GDN_HEAD_DIM, GDN_HEAD_DIM), lambda i: (i, 0, 0, 0))
    return pl.pallas_call(
        functools.partial(_gdn_decode_kernel, bs),
        out_shape=[jax.ShapeDtypeStruct((b, GDN_W), F32),
                   jax.ShapeDtypeStruct(state.shape, F32)],
        grid=(b // bs,),
        in_specs=[row(GDN_CONV_W),
                  pl.BlockSpec((bs, CONV_WIDTH - 1, GDN_CONV_W), lambda i: (i, 0, 0)),
                  row(GDN_W), row(LANES), sspec,
                  full(CONV_WIDTH, GDN_CONV_W), full(1, LANES), full(1, LANES), full(1, GDN_HEAD_DIM)],
        out_specs=[row(GDN_W), sspec],
        compiler_params=_cparams(("parallel",)),
        name="gdn_decode",
    )(qkvg, conv_state, z, ab, state, conv_w, alog, dtb, gnorm)


def _memkv_kernel(x_ref, g_ref, w_ref, kn_ref, k_ref, v_ref):
    hb = _rms(x_ref[...], g_ref[...]).astype(BF16)
    for hd in range(MEM_HEADS):
        hs = slice(hd * MEM_HEAD_DIM, (hd + 1) * MEM_HEAD_DIM)
        k_ref[:, hs] = _rms(jnp.dot(hb, w_ref[:, hs], preferred_element_type=F32), kn_ref[...])
    v_ref[...] = jnp.dot(hb, w_ref[:, BRANCH_W:], preferred_element_type=F32)


def _memkv(mem2, gain, w_kv, kn, tm=512):
    t = mem2.shape[0]
    return pl.pallas_call(
        _memkv_kernel,
        out_shape=[jax.ShapeDtypeStruct((t, BRANCH_W), F32)] * 2,
        grid=(t // tm,),
        in_specs=[pl.BlockSpec((tm, D_MODEL), lambda i: (i, 0)),
                  pl.BlockSpec((1, D_MODEL), lambda i: (0, 0)),
                  pl.BlockSpec((D_MODEL, 2 * BRANCH_W), lambda i: (0, 0)),
                  pl.BlockSpec((1, MEM_HEAD_DIM), lambda i: (0, 0))],
        out_specs=[pl.BlockSpec((tm, BRANCH_W), lambda i: (i, 0))] * 2,
        compiler_params=_cparams(("parallel",)),
        name="memkv",
    )(mem2, gain, w_kv, kn)


def _memattn_prompt_kernel(q_ref, k_ref, v_ref, qn_ref, o_ref):
    q = q_ref[0]
    k = k_ref[0]
    v = v_ref[0]
    scale = MEM_HEAD_DIM ** -0.5
    for hd in range(MEM_HEADS):
        hs = slice(hd * MEM_HEAD_DIM, (hd + 1) * MEM_HEAD_DIM)
        s = _bdot_nt(_rms(q[:, hs], qn_ref[...]), k[:, hs]) * scale
        p = jnp.exp(s - jnp.max(s, axis=-1, keepdims=True))
        o_ref[0, :, hs] = _bdot(p, v[:, hs]) / jnp.sum(p, axis=-1, keepdims=True)


def _memattn_prompt(qm, mk, mv, qn):
    b, l, _ = qm.shape
    tq = 512 if l % 512 == 0 else WINDOW
    return pl.pallas_call(
        _memattn_prompt_kernel,
        out_shape=jax.ShapeDtypeStruct((b, l, BRANCH_W), F32),
        grid=(b, l // tq),
        in_specs=[pl.BlockSpec((1, tq, BRANCH_W), lambda i, j: (i, j, 0)),
                  pl.BlockSpec((1, MEM_LEN, BRANCH_W), lambda i, j: (i, 0, 0)),
                  pl.BlockSpec((1, MEM_LEN, BRANCH_W), lambda i, j: (i, 0, 0)),
                  pl.BlockSpec((1, MEM_HEAD_DIM), lambda i, j: (0, 0))],
        out_specs=pl.BlockSpec((1, tq, BRANCH_W), lambda i, j: (i, j, 0)),
        compiler_params=_cparams(("parallel", "parallel")),
        name="memattn_prompt",
    )(qm, mk, mv, qn)


def _memattn_decode_kernel(bs, q_ref, k_ref, v_ref, qn_ref, o_ref):
    scale = MEM_HEAD_DIM ** -0.5
    for b in range(bs):
        qn = _rms(q_ref[b], qn_ref[...])
        s = jnp.sum(k_ref[b] * qn, axis=-1, keepdims=True) * scale
        p = jnp.exp(s - jnp.max(s, axis=0, keepdims=True))
        den = jnp.sum(p, axis=0)
        o_ref[b] = jnp.sum(p * v_ref[b], axis=0) / den


def _memattn_decode(qm, ck, cv, qn, bs=8):
    b = qm.shape[0]
    q3 = pl.BlockSpec((bs, MEM_HEADS, MEM_HEAD_DIM), lambda i: (i, 0, 0))
    kv = pl.BlockSpec((bs, MEM_LEN, MEM_HEADS, MEM_HEAD_DIM), lambda i: (i, 0, 0, 0))
    o = pl.pallas_call(
        functools.partial(_memattn_decode_kernel, bs),
        out_shape=jax.ShapeDtypeStruct((b, MEM_HEADS, MEM_HEAD_DIM), F32),
        grid=(b // bs,),
        in_specs=[q3, kv, kv, pl.BlockSpec((1, MEM_HEAD_DIM), lambda i: (0, 0))],
        out_specs=q3,
        compiler_params=_cparams(("parallel",)),
        name="memattn_decode",
    )(qm.reshape(b, MEM_HEADS, MEM_HEAD_DIM), ck, cv, qn)
    return o.reshape(b, BRANCH_W)


def _merge_kernel(oa_ref, ob_ref, oc_ref, gate_ref, x_ref, wb_ref, wo_ref, g2_ref, wr_ref, br_ref,
                  x1_ref, h2_ref, lg_ref):
    acc = None
    for i, o_ref in enumerate((oa_ref, ob_ref, oc_ref)):
        mixed = jnp.dot(o_ref[...].astype(BF16), wb_ref[i], preferred_element_type=F32)
        term = gate_ref[:, i * D_MODEL:(i + 1) * D_MODEL].astype(F32) * mixed
        acc = term if acc is None else acc + term
    x1 = x_ref[...] + jnp.dot(acc.astype(BF16), wo_ref[...], preferred_element_type=F32)
    x1_ref[...] = x1
    h2 = _rms(x1, g2_ref[...])
    h2_ref[...] = h2.astype(BF16)
    h_hi = h2.astype(BF16)
    h_lo = (h2 - h_hi.astype(F32)).astype(BF16)
    nt = (((1,), (1,)), ((), ()))
    a = lax.dot_general(wr_ref[...], h_hi, nt, preferred_element_type=F32)
    b = lax.dot_general(wr_ref[0:N_EXPERTS, :], h_lo, nt, preferred_element_type=F32)
    lg_ref[...] = a[0:N_EXPERTS] + a[N_EXPERTS:] + b + br_ref[...]


def _merge_into_kernel(oa_ref, ob_ref, oc_ref, gate_ref, x_ref, wb_ref, wo_ref, g2_ref, wr_ref, br_ref,
                       x1_in, h2_in, lg_in, x1_ref, h2_ref, lg_ref):
    del x1_in, h2_in, lg_in
    _merge_kernel(oa_ref, ob_ref, oc_ref, gate_ref, x_ref, wb_ref, wo_ref, g2_ref, wr_ref, br_ref,
                  x1_ref, h2_ref, lg_ref)


def _merge(oa, ob, oc, gate, x2, wb, wo, g2, wr_t, br_col, tm, t_out=None, into=None):
    t = x2.shape[0]
    t_out = t_out or t
    blk0 = 0 if into is None else (into[0].shape[0] - t) // tm
    row = lambda w: pl.BlockSpec((tm, w), lambda i: (i, 0))
    orow = lambda w: pl.BlockSpec((tm, w), lambda i: (i + blk0, 0))
    full = lambda *shape: pl.BlockSpec(shape, lambda i: (0,) * len(shape))
    in_specs = [row(BRANCH_W), row(BRANCH_W), row(BRANCH_W), row(N_BRANCH * D_MODEL), row(D_MODEL),
                full(N_BRANCH, BRANCH_W, D_MODEL), full(D_MODEL, D_MODEL), full(1, D_MODEL),
                full(2 * N_EXPERTS, D_MODEL), full(N_EXPERTS, 1)]
    args = [oa, ob, oc, gate, x2, wb, wo, g2, wr_t, br_col]
    aliases = {}
    if into is not None:
        t_out = into[0].shape[0]
        in_specs += [pl.BlockSpec(memory_space=pl.ANY)] * 3
        aliases = {len(args) + k: k for k in range(3)}
        args += list(into)
    return pl.pallas_call(
        _merge_kernel if into is None else _merge_into_kernel,
        out_shape=[jax.ShapeDtypeStruct((t_out, D_MODEL), F32),
                   jax.ShapeDtypeStruct((t_out, D_MODEL), BF16),
                   jax.ShapeDtypeStruct((N_EXPERTS, t_out), F32)],
        grid=(t // tm,),
        in_specs=in_specs,
        out_specs=[orow(D_MODEL), orow(D_MODEL), pl.BlockSpec((N_EXPERTS, tm), lambda i: (0, i + blk0))],
        input_output_aliases=aliases,
        compiler_params=_cparams(("parallel",)),
        name="merge",
    )(*args)


def _col_to_row(col):
    n = col.shape[0]
    r = lax.broadcasted_iota(I32, (n, n), 0)
    c = lax.broadcasted_iota(I32, (n, n), 1)
    return jnp.sum(jnp.where(r == c, col, 0.0), axis=0, keepdims=True)


def _row_to_col(row):
    n = row.shape[1]
    r = lax.broadcasted_iota(I32, (n, n), 0)
    c = lax.broadcasted_iota(I32, (n, n), 1)
    return jnp.sum(jnp.where(r == c, row, 0.0), axis=1, keepdims=True)


def _lane_pad(row):
    return jnp.concatenate([row, jnp.zeros((1, LANES - row.shape[1]), row.dtype)], axis=1)


def _route_kernel(lg_ref, pos_ref, gate_ref, meta_ref, cnt_ref, carry):
    i = pl.program_id(0)
    tn = lg_ref.shape[1]

    @pl.when(i == 0)
    def _():
        carry[...] = jnp.zeros_like(carry)

    l = lg_ref[...]
    eio = lax.broadcasted_iota(I32, (N_EXPERTS, tn), 0)
    hot = jnp.zeros((N_EXPERTS, tn), F32)
    vals, idxs = [], []
    for _ in range(TOP_K):
        m = jnp.max(l, axis=0, keepdims=True)
        idx = jnp.min(jnp.where(l == m, eio, N_EXPERTS), axis=0, keepdims=True)
        sel = eio == idx
        vals.append(m)
        idxs.append(idx)
        hot = hot + sel.astype(F32)
        l = jnp.where(sel, -jnp.inf, l)
    ex = [jnp.exp(v - vals[0]) for v in vals]
    tot = ex[0] + ex[1] + ex[2] + ex[3]
    before = (lax.broadcasted_iota(I32, (tn, tn), 0) < lax.broadcasted_iota(I32, (tn, tn), 1)).astype(BF16)
    within = jnp.dot(hot.astype(BF16), before, preferred_element_type=F32)
    cnt_col = jnp.sum(hot, axis=1, keepdims=True)
    room = jnp.floor((cnt_col + (MOE_UNIT - 1)) * (1.0 / MOE_UNIT)) * MOE_UNIT
    r = lax.broadcasted_iota(I32, (N_EXPERTS, N_EXPERTS), 0)
    c = lax.broadcasted_iota(I32, (N_EXPERTS, N_EXPERTS), 1)
    start_col = _row_to_col(jnp.sum(jnp.where(r < c, room, 0.0), axis=0, keepdims=True))
    for k in range(TOP_K):
        gate_ref[k:k + 1, :] = ex[k] / tot
        pos_ref[k:k + 1, :] = jnp.sum(jnp.where(eio == idxs[k], start_col + within, 0.0), axis=0,
                                      keepdims=True).astype(I32)
    meta_ref[0] = _lane_pad(jnp.concatenate([_col_to_row(room), _col_to_row(carry[:, 0:1])], axis=1)).astype(I32)
    carry[...] = carry[...] + room
    cnt_ref[...] = carry[...]


def _route(logits_t, tn):
    t = logits_t.shape[1]
    kt = pl.BlockSpec((TOP_K, tn), lambda i: (0, i))
    return pl.pallas_call(
        _route_kernel,
        out_shape=[jax.ShapeDtypeStruct((TOP_K, t), I32),
                   jax.ShapeDtypeStruct((TOP_K, t), F32),
                   jax.ShapeDtypeStruct((t // tn, 1, LANES), I32),
                   jax.ShapeDtypeStruct((N_EXPERTS, LANES), F32)],
        grid=(t // tn,),
        in_specs=[pl.BlockSpec((N_EXPERTS, tn), lambda i: (0, i))],
        out_specs=[kt, kt, pl.BlockSpec((1, 1, LANES), lambda i: (i, 0, 0)),
                   pl.BlockSpec((N_EXPERTS, LANES), lambda i: (0, 0))],
        scratch_shapes=[pltpu.VMEM((N_EXPERTS, LANES), F32)],
        compiler_params=_cparams(("arbitrary",)),
        name="moe_route",
    )(logits_t)


def _layout_kernel(bm, cnt_ref, be_ref, na_ref, ps_ref, pe_ref, ct_ref):
    nbp = be_ref.shape[1]
    cnt_col = cnt_ref[:, 0:1]
    size = jnp.floor((cnt_col + (bm - 1)) * (1.0 / bm)) * bm
    r = lax.broadcasted_iota(I32, (N_EXPERTS, N_EXPERTS), 0)
    c = lax.broadcasted_iota(I32, (N_EXPERTS, N_EXPERTS), 1)
    ends_row = jnp.sum(jnp.where(r <= c, size, 0.0), axis=0, keepdims=True)
    ends_col = _row_to_col(ends_row)
    nact = ends_row[:, N_EXPERTS - 1:N_EXPERTS] * (1.0 / bm)
    blk = jnp.minimum(lax.broadcasted_iota(I32, (1, nbp), 1).astype(F32), nact - 1.0)
    be = jnp.sum((ends_col <= blk * bm).astype(F32), axis=0, keepdims=True)
    be_ref[...] = jnp.minimum(be, N_EXPERTS - 1.0).astype(I32)
    na_ref[...] = jnp.broadcast_to(nact, na_ref.shape).astype(I32)
    ps_ref[...] = _lane_pad(ends_row - _col_to_row(size)).astype(I32)
    pe_ref[...] = _lane_pad(ends_row).astype(I32)
    ct_ref[...] = _lane_pad(_col_to_row(cnt_col)).astype(I32)


def _layout(cnt, nbp, bm):
    row = jax.ShapeDtypeStruct((1, LANES), I32)
    return pl.pallas_call(
        functools.partial(_layout_kernel, bm),
        out_shape=[jax.ShapeDtypeStruct((1, nbp), I32), row, row, row, row],
        name="moe_layout",
    )(cnt)


def _stage_rows(tn):
    return tn * TOP_K + N_EXPERTS * MOE_UNIT


def _run_copies(meta_ref, ps_ref, make_copy, tn):
    top = (tn // MOE_UNIT).bit_length() - 1
    src = jnp.int32(0)
    total = jnp.int32(0)
    for e in range(N_EXPERTS):
        units = lax.shift_right_logical(meta_ref[0, 0, e], 3)
        dst = ps_ref[e] + meta_ref[0, 0, N_EXPERTS + e]
        off = jnp.int32(0)
        for k in range(top, -1, -1):
            rows = MOE_UNIT << k
            take = lax.shift_right_logical(units, k) & 1

            @pl.when(take == 1)
            def _(src=src, dst=dst, off=off, rows=rows):
                make_copy(pl.multiple_of(src + off, MOE_UNIT), pl.multiple_of(dst + off, MOE_UNIT), rows).start()
            off = off + take * rows
        src = src + units * MOE_UNIT
        total = total + units
    return total


def _wait_copies(units, make_copy, max_rows):
    for k in range((max_rows // MOE_UNIT).bit_length()):
        if (MOE_UNIT << k) > max_rows:
            break

        @pl.when(lax.shift_right_logical(units, k) & 1 == 1)
        def _(k=k):
            make_copy(0, 0, MOE_UNIT << k).wait()


def _wait_units(count, make_copy):
    def body(u, carry):
        make_copy(0, 0).wait()
        return carry
    lax.fori_loop(0, count, body, 0)


def _dispatch_kernel(ps_ref, pe_ref, ct_ref, meta_ref, pos_ref, gate_ref, h_ref, xg_ref, stage, zrows, zblock,
                     started, sem, zsem):
    i = pl.program_id(0)
    n = pl.num_programs(0)
    slot = i % 2
    rows, tn = stage.shape[1], pos_ref.shape[1]

    def copy(s):
        return lambda src, dst, nrows: pltpu.make_async_copy(stage.at[s, pl.ds(src, nrows)],
                                                             xg_ref.at[pl.ds(dst, nrows)], sem.at[s])

    @pl.when(i >= 2)
    def _():
        _wait_copies(started[slot], copy(slot), rows)

    srow = lax.broadcasted_iota(I32, (rows, tn), 0)
    hits = [srow == pos_ref[k:k + 1, :] for k in range(TOP_K)]
    hit = hits[0]
    gate_at = jnp.where(hits[0], gate_ref[0:1, :], 0.0)
    for k in range(1, TOP_K):
        hit = hit | hits[k]
        gate_at = gate_at + jnp.where(hits[k], gate_ref[k:k + 1, :], 0.0)
    onehot = jnp.where(hit, 1.0, 0.0).astype(BF16)
    stage[slot, :, 0:D_MODEL] = jnp.dot(onehot, h_ref[...], preferred_element_type=F32)
    stage[slot, :, D_MODEL:] = jnp.broadcast_to(jnp.sum(gate_at, axis=1, keepdims=True), (rows, LANES))
    started[slot] = _run_copies(meta_ref, ps_ref, copy(slot), tn)

    @pl.when(i == n - 1)
    def _():
        _wait_copies(started[slot], copy(slot), rows)

        @pl.when(n >= 2)
        def _():
            _wait_copies(started[1 - slot], copy(1 - slot), rows)

        zrows[...] = jnp.zeros_like(zrows)
        zero = lambda src, dst: pltpu.make_async_copy(zrows, xg_ref.at[pl.ds(dst, MOE_UNIT)], zsem)
        total = jnp.int32(0)
        for e in range(N_EXPERTS):
            lo = ps_ref[e] + ct_ref[e]
            units = lax.shift_right_logical(pe_ref[e] - lo, 3)

            def body(u, carry, lo=lo):
                zero(0, pl.multiple_of(lo + u * MOE_UNIT, MOE_UNIT)).start()
                return carry
            lax.fori_loop(0, units, body, 0)
            total = total + units
        _wait_units(total, zero)

        zblock[...] = jnp.zeros_like(zblock)
        bm = zblock.shape[0]
        zero_block = lambda blk: pltpu.make_async_copy(zblock, xg_ref.at[pl.ds(pl.multiple_of(blk * bm, bm), bm)],
                                                       zsem)
        first = lax.shift_right_logical(pe_ref[N_EXPERTS - 1], bm.bit_length() - 1)
        last = xg_ref.shape[0] // bm

        def start_block(blk, carry):
            zero_block(blk).start()
            return carry
        lax.fori_loop(first, last, start_block, 0)

        def wait_block(blk, carry):
            zero_block(0).wait()
            return carry
        lax.fori_loop(first, last, wait_block, 0)


def _dispatch(ps, pe, ct, meta, pos, gates, h2, n_slots, tn, bm):
    t = h2.shape[0]
    width = D_MODEL + LANES
    return pl.pallas_call(
        _dispatch_kernel,
        out_shape=jax.ShapeDtypeStruct((n_slots, width), F32),
        grid_spec=pltpu.PrefetchScalarGridSpec(
            num_scalar_prefetch=3,
            grid=(t // tn,),
            in_specs=[pl.BlockSpec((1, 1, LANES), lambda i, *_: (i, 0, 0), memory_space=pltpu.SMEM),
                      pl.BlockSpec((TOP_K, tn), lambda i, *_: (0, i)),
                      pl.BlockSpec((TOP_K, tn), lambda i, *_: (0, i)),
                      pl.BlockSpec((tn, D_MODEL), lambda i, *_: (i, 0))],
            out_specs=pl.BlockSpec(memory_space=pl.ANY),
            scratch_shapes=[pltpu.VMEM((2, _stage_rows(tn), width), F32),
                            pltpu.VMEM((MOE_UNIT, width), F32),
                            pltpu.VMEM((bm, width), F32),
                            pltpu.SMEM((2,), I32),
                            pltpu.SemaphoreType.DMA((2,)),
                            pltpu.SemaphoreType.DMA(())]),
        compiler_params=_cparams(("arbitrary",)),
        name="moe_dispatch",
    )(ps, pe, ct, meta, pos, gates, h2)


def _expert_kernel(be_ref, na_ref, x_ref, w1_ref, b1_ref, w2_ref, b2_ref, y_ref, w1s, w2s):
    i = pl.program_id(0)
    e = be_ref[i]
    prev = be_ref[jnp.maximum(i - 1, 0)]

    @pl.when((i == 0) | (e != prev))
    def _():
        rows = 128
        for r0 in range(0, D_MODEL, rows):
            w1s[r0:r0 + rows, :] = w1_ref[r0:r0 + rows, :].astype(BF16)
        for r0 in range(0, D_FF, rows):
            w2s[r0:r0 + rows, :] = w2_ref[r0:r0 + rows, :].astype(BF16)

    @pl.when(i < na_ref[0])
    def _():
        x = x_ref[:, 0:D_MODEL].astype(BF16)
        hmid = jnp.dot(x, w1s[...], preferred_element_type=F32) + b1_ref[...]
        glu = jnp.minimum(hmid[:, :D_FF], SWIGLU_LIMIT)
        lin = jnp.clip(hmid[:, D_FF:], -SWIGLU_LIMIT, SWIGLU_LIMIT)
        act = glu * _sigmoid(SWIGLU_ALPHA * glu) * (lin + 1.0)
        y = jnp.dot(act.astype(BF16), w2s[...], preferred_element_type=F32) + b2_ref[...]
        y_ref[...] = y * x_ref[:, D_MODEL:D_MODEL + 1]

    @pl.when(i >= na_ref[0])
    def _():
        y_ref[...] = jnp.zeros_like(y_ref)


def _experts(be, nact, xg, layer, w1, b1, w2, b2, bm):
    n_slots = xg.shape[0]
    nb = n_slots // bm
    blk = lambda i, be_r, na_r, *_: (jnp.minimum(i, na_r[0] - 1), 0)
    wsel = lambda i, be_r, *_: (layer, be_r[i], 0, 0)
    return pl.pallas_call(
        _expert_kernel,
        out_shape=jax.ShapeDtypeStruct((n_slots, D_MODEL), F32),
        grid_spec=pltpu.PrefetchScalarGridSpec(
            num_scalar_prefetch=2,
            grid=(nb,),
            in_specs=[pl.BlockSpec((bm, xg.shape[1]), blk),
                      pl.BlockSpec((None, None, D_MODEL, 2 * D_FF), wsel),
                      pl.BlockSpec((None, None, 1, 2 * D_FF), wsel),
                      pl.BlockSpec((None, None, D_FF, D_MODEL), wsel),
                      pl.BlockSpec((None, None, 1, D_MODEL), wsel)],
            out_specs=pl.BlockSpec((bm, D_MODEL), lambda i, *_: (i, 0)),
            scratch_shapes=[pltpu.VMEM((D_MODEL, 2 * D_FF), BF16), pltpu.VMEM((D_FF, D_MODEL), BF16)]),
        compiler_params=_cparams(("arbitrary",)),
        name="moe_experts",
    )(be, nact, xg, w1, b1, w2, b2)


def _combine_kernel(n_first, ps_ref, mcur_ref, mnext_ref, pos_ref, x1_ref, yg_ref, o_ref, o2_ref, stage,
                    started, sem):
    i = pl.program_id(0)
    n = pl.num_programs(0)
    slot = i % 2
    rows, tn = stage.shape[1], pos_ref.shape[1]

    @pl.when(i == 0)
    def _():
        stage[...] = jnp.zeros_like(stage)

    def copy(s):
        return lambda src, dst, nrows: pltpu.make_async_copy(yg_ref.at[pl.ds(dst, nrows)],
                                                             stage.at[s, pl.ds(src, nrows)], sem.at[s])

    @pl.when(i == 0)
    def _():
        started[0] = _run_copies(mcur_ref, ps_ref, copy(0), tn)

    @pl.when(i + 1 < n)
    def _():
        started[1 - slot] = _run_copies(mnext_ref, ps_ref, copy(1 - slot), tn)

    _wait_copies(started[slot], copy(slot), rows)

    eye = (lax.broadcasted_iota(I32, (tn, tn), 0) == lax.broadcasted_iota(I32, (tn, tn), 1)).astype(F32)
    cols = _hdot_nt(eye, pos_ref[...].astype(F32)).astype(I32)
    lane = lax.broadcasted_iota(I32, (tn, rows), 1)
    hit = lane == cols[:, 0:1]
    for k in range(1, TOP_K):
        hit = hit | (lane == cols[:, k:k + 1])
    pick = jnp.where(hit, 1.0, 0.0).astype(BF16)
    out = x1_ref[...] + jnp.dot(pick, stage[slot].astype(BF16), preferred_element_type=F32)

    @pl.when(i < n_first)
    def _():
        o_ref[...] = out

    @pl.when(i >= n_first)
    def _():
        o2_ref[...] = out


def _combine(ps, meta, pos, x1, yg, tn, t_first):
    t = x1.shape[0]
    nt = t // tn
    n_first = t_first // tn
    return pl.pallas_call(
        functools.partial(_combine_kernel, n_first),
        out_shape=[jax.ShapeDtypeStruct((t_first, D_MODEL), F32),
                   jax.ShapeDtypeStruct((t - t_first, D_MODEL), F32)],
        grid_spec=pltpu.PrefetchScalarGridSpec(
            num_scalar_prefetch=1,
            grid=(nt,),
            in_specs=[pl.BlockSpec((1, 1, LANES), lambda i, *_: (i, 0, 0), memory_space=pltpu.SMEM),
                      pl.BlockSpec((1, 1, LANES), lambda i, *_: (jnp.minimum(i + 1, nt - 1), 0, 0),
                                   memory_space=pltpu.SMEM),
                      pl.BlockSpec((TOP_K, tn), lambda i, *_: (0, i)),
                      pl.BlockSpec((tn, D_MODEL), lambda i, *_: (i, 0)),
                      pl.BlockSpec(memory_space=pl.ANY)],
            out_specs=[pl.BlockSpec((tn, D_MODEL), lambda i, *_: (jnp.minimum(i, n_first - 1), 0)),
                       pl.BlockSpec((tn, D_MODEL), lambda i, *_: (jnp.maximum(i - n_first, 0), 0))],
            scratch_shapes=[pltpu.VMEM((2, _stage_rows(tn), D_MODEL), F32),
                            pltpu.SMEM((2,), I32),
                            pltpu.SemaphoreType.DMA((2,))]),
        compiler_params=_cparams(("arbitrary",)),
        name="moe_combine",
    )(ps, meta, meta, pos, x1, yg)


def _moe(x1, h2, logits_t, layer, w1, b1, w2, b2, t_first):
    t = x1.shape[0]
    tn = MOE_TOK_TILE
    bm = MOE_BM if t * TOP_K >= N_EXPERTS * MOE_BM else LANES
    n_blocks = (t * TOP_K + (t // tn) * N_EXPERTS * (MOE_UNIT - 1)) // bm + N_EXPERTS + 1
    nbp = -(-n_blocks // LANES) * LANES
    pos, gates, meta, cnt = _route(logits_t, tn)
    be, nact, ps, pe, ct = _layout(cnt, nbp, bm)
    ps, pe, ct = ps[0, :N_EXPERTS], pe[0, :N_EXPERTS], ct[0, :N_EXPERTS]
    xg = _dispatch(ps, pe, ct, meta, pos, gates, h2, n_blocks * bm, tn, bm)
    yg = _experts(be[0, :n_blocks], nact[0, :1], xg, layer, w1, b1, w2, b2, bm)
    return _combine(ps, meta, pos, x1, yg, tn, t_first)


def _pack_w_in(w):
    ab = jnp.pad(w[:, _IN_MAIN:_IN_MAIN + _IN_AB], ((0, 0), (0, LANES - _IN_AB)))
    return jnp.concatenate([w[:, :_IN_MAIN], ab, w[:, _IN_MAIN + _IN_AB:]], axis=1).astype(BF16)


def _lane_row(v):
    return jnp.pad(v.astype(F32), (0, LANES - v.shape[0])).reshape(1, LANES)


def _layer_weights(ln1_gain, w_in, q_norm_swa, k_norm_swa, swa_sinks, conv_w, a_log, dt_bias, gdn_norm,
                   q_norm_mem, w_branch, w_out, ln2_gain, w_router, b_router, w_mlp1, b_mlp1, w_mlp2, b_mlp2, layer):
    depth = w_mlp1.shape[0]
    wr_hi = w_router.T.astype(BF16)
    wr_lo = (w_router.T - wr_hi.astype(F32)).astype(BF16)
    return dict(
        layer=layer, wr_t=jnp.concatenate([wr_hi, wr_lo], axis=0), b1r=b_mlp1.reshape(depth, N_EXPERTS, 1, -1), b2r=b_mlp2.reshape(depth, N_EXPERTS, 1, -1),
        ln1=ln1_gain.reshape(1, -1), wp=_pack_w_in(w_in),
        qn_s=q_norm_swa.reshape(1, -1), kn_s=k_norm_swa.reshape(1, -1), sinks=swa_sinks.reshape(1, -1),
        conv_w=conv_w, alog=_lane_row(a_log), dtb=_lane_row(dt_bias), gnorm=gdn_norm.reshape(1, -1),
        qn_m=q_norm_mem.reshape(1, -1), wb=w_branch.astype(BF16), wo=w_out.astype(BF16),
        ln2=ln2_gain.reshape(1, -1), br=b_router.reshape(-1, 1),
        w1=w_mlp1, w2=w_mlp2)


def _finish(lw, first, second):
    t1, t2 = first[4].shape[0], second[4].shape[0]
    tm1 = MERGE_ROWS if t1 % MERGE_ROWS == 0 else MOE_TOK_TILE
    pad = (-t2) % MOE_TOK_TILE
    second = [jnp.pad(a, ((0, pad), (0, 0))) for a in second]
    wts = (lw["wb"], lw["wo"], lw["ln2"], lw["wr_t"], lw["br"])
    bufs = _merge(*first, *wts, tm1, t_out=t1 + t2 + pad)
    bufs = _merge(*second, *wts, MOE_TOK_TILE, into=bufs)
    y1, y2 = _moe(*bufs, lw["layer"], lw["w1"], lw["b1r"], lw["w2"], lw["b2r"], t1)
    return y1, y2[:t2]


def _prompt_layer(x, mem, lw, mem_norm, w_mem_kv, k_norm_mem):
    b, l, d = x.shape
    x2 = x.reshape(b * l, d)
    tm = 256
    tm_in = 2 * tm if l % (2 * tm) == 0 else tm
    qs, ks, vs, qkvg, z, ab, qm, gate, conv_tail = _inproj(x2, lw["ln1"], lw["wp"], lw["conv_w"], tm_in, seq_len=l)
    mk, mv = _memkv(mem.reshape(b * MEM_LEN, d), mem_norm.reshape(1, -1), w_mem_kv.astype(BF16),
                    k_norm_mem.reshape(1, -1))
    r3 = lambda a: a.reshape(b, l, a.shape[-1])
    o_swa, kwin = _swa_prompt(r3(qs), r3(ks), r3(vs), lw["qn_s"], lw["kn_s"], lw["sinks"])
    o_g, s_fin = _gdn_prompt(r3(qkvg), r3(z), r3(ab), lw["alog"], lw["dtb"], lw["gnorm"])
    o_m = _memattn_prompt(r3(qm), mk.reshape(b, MEM_LEN, BRANCH_W), mv.reshape(b, MEM_LEN, BRANCH_W), lw["qn_m"])
    parts = (o_swa.reshape(b * l, -1), o_g.reshape(b * l, -1), o_m.reshape(b * l, -1), gate, x2)
    new_k = kwin.reshape(b, WINDOW, SWA_KV_HEADS, SWA_HEAD_DIM)
    new_v = r3(vs)[:, l - WINDOW:].reshape(b, WINDOW, SWA_KV_HEADS, SWA_HEAD_DIM)
    new_conv = conv_tail[:, _HALO - (CONV_WIDTH - 1):]
    mk4 = mk.reshape(b, MEM_LEN, MEM_HEADS, MEM_HEAD_DIM)
    mv4 = mv.reshape(b, MEM_LEN, MEM_HEADS, MEM_HEAD_DIM)
    return parts, (new_k, new_v, s_fin, new_conv, mk4, mv4)


def _sample_layer(x, cache_k, cache_v, state, conv_state, mem_k, mem_v, lw):
    b, l, d = x.shape
    x2 = x.reshape(b, d)
    tm = TOK_TILE
    qs, ks, vs, qkvg, z, ab, qm, gate = _inproj(x2, lw["ln1"], lw["wp"], lw["conv_w"], tm)
    o_swa, new_k, new_v = _swa_decode(qs, ks, vs, cache_k, cache_v, lw["qn_s"], lw["kn_s"], lw["sinks"])
    o_g, new_s = _gdn_decode(qkvg, z, ab, state, conv_state, lw["conv_w"], lw["alog"], lw["dtb"], lw["gnorm"])
    o_m = _memattn_decode(qm, mem_k, mem_v, lw["qn_m"])
    new_conv = jnp.concatenate([conv_state[:, 1:], qkvg[:, None, :]], axis=1)
    shp = (b, WINDOW, SWA_KV_HEADS, SWA_HEAD_DIM)
    return (o_swa, o_g, o_m, gate, x2), (new_k.reshape(shp), new_v.reshape(shp), new_s, new_conv)


def _layer(lw, xp, mem, mem_norm, w_mem_kv, k_norm_mem, xs, cache_k, cache_v, state, conv_state, mem_k, mem_v):
    p_parts, p_state = _prompt_layer(xp, mem, lw, mem_norm, w_mem_kv, k_norm_mem)
    s_parts, s_state = _sample_layer(xs, cache_k, cache_v, state, conv_state, mem_k, mem_v, lw)
    yp, ys = _finish(lw, p_parts, s_parts)
    return yp.reshape(xp.shape), ys.reshape(xs.shape), p_state, s_state


def kernel(x_prompt, x_sample, cache_swa_k, cache_swa_v, state_gdn, state_conv, cache_mem_k, cache_mem_v, mem_prompt, ln1_gain, w_in, q_norm_swa, k_norm_swa, swa_sinks, conv_w, a_log, dt_bias, gdn_norm, q_norm_mem, k_norm_mem, mem_norm, w_mem_kv, w_branch, w_out, ln2_gain, w_router, b_router, w_mlp1, b_mlp1, w_mlp2, b_mlp2):
    xp, xs = x_prompt, x_sample
    outs = [[] for _ in range(10)]
    for l in range(ln1_gain.shape[0]):
        lw = _layer_weights(ln1_gain[l], w_in[l], q_norm_swa[l], k_norm_swa[l], swa_sinks[l], conv_w[l], a_log[l],
                            dt_bias[l], gdn_norm[l], q_norm_mem[l], w_branch[l], w_out[l], ln2_gain[l], w_router[l],
                            b_router[l], w_mlp1, b_mlp1, w_mlp2, b_mlp2, l)
        xp, xs, (pk, pv, ps, pc, mk, mv), (sk, sv, ss, sc) = _layer(
            lw, xp, mem_prompt, mem_norm[l], w_mem_kv[l], k_norm_mem[l],
            xs, cache_swa_k[l], cache_swa_v[l], state_gdn[l], state_conv[l], cache_mem_k[l], cache_mem_v[l])
        for acc, v in zip(outs, (pk, pv, sk, sv, ps, ss, pc, sc, mk, mv)):
            acc.append(v)
    return (xp, xs) + tuple(jnp.stack(o) for o in outs)
```

```python
import functools

import numpy as np
import jax
import jax.numpy as jnp
from jax import lax
from jax.experimental import pallas as pl
from jax.experimental.pallas import tpu as pltpu

F32 = jnp.float32
BF16 = jnp.bfloat16
I32 = jnp.int32

D_MODEL = 1024
BRANCH_W = 512
SWA_HEADS = 8
SWA_KV_HEADS = 2
SWA_HEAD_DIM = 64
SWA_GROUP = SWA_HEADS // SWA_KV_HEADS
SWA_KV_W = SWA_KV_HEADS * SWA_HEAD_DIM
WINDOW = 128
GDN_HEADS = 4
GDN_HEAD_DIM = 128
GDN_W = GDN_HEADS * GDN_HEAD_DIM
GDN_CONV_W = 3 * GDN_W
GDN_CHUNK = 64
CONV_WIDTH = 4
MEM_LEN = 256
MEM_HEADS = 4
MEM_HEAD_DIM = 128
N_BRANCH = 3
N_EXPERTS = 32
TOP_K = 4
D_FF = 1024
SWIGLU_LIMIT = 7.0
SWIGLU_ALPHA = 1.702
EPS = 1e-6

LANES = 128
MOE_BM = 512
MOE_TOK_TILE = 256
MOE_UNIT = 8
TOK_TILE = 128
MERGE_ROWS = 512
VMEM_LIMIT = 56 * 1024 * 1024

_SEG_QS = (0, 512)
_SEG_KS = (512, 128)
_SEG_VS = (640, 128)
_SEG_QKVG = (768, 1536)
_SEG_Z = (2304, 512)
_SEG_AB = (2816, 128)
_SEG_QM = (2944, 512)
_SEG_GATE = (3456, 3072)
_PACKED_W = 6528
_IN_MAIN = 2816
_IN_AB = 8

_NEG = -1e30
_HI = lax.Precision.HIGHEST


def _cparams(sem, vmem=VMEM_LIMIT):
    return pltpu.CompilerParams(dimension_semantics=sem, vmem_limit_bytes=vmem)


def _bdot(a, b):
    return jnp.dot(a.astype(BF16), b.astype(BF16), preferred_element_type=F32)


def _bdot_nt(a, b):
    return lax.dot_general(a.astype(BF16), b.astype(BF16), (((1,), (1,)), ((), ())),
                           preferred_element_type=F32)


def _hdot_nt(a, b):
    return lax.dot_general(a, b, (((1,), (1,)), ((), ())), preferred_element_type=F32, precision=_HI)


def _rms(x, gain):
    return x * lax.rsqrt(jnp.mean(x * x, axis=-1, keepdims=True) + EPS) * gain


def _l2(x):
    return x * lax.rsqrt(jnp.sum(x * x, axis=-1, keepdims=True) + EPS)


def _sigmoid(x):
    return 0.5 * jnp.tanh(0.5 * x) + 0.5


def _silu(x):
    return x * _sigmoid(x)


def _softplus(x):
    return jnp.maximum(x, 0.0) + jnp.log1p(jnp.exp(-jnp.abs(x)))


_HALO = 8


def _inproj_kernel(seq_steps, x_ref, g_ref, w_ref, cw_ref, qs, ks, vs, qkvg, z, ab, qm, gate, *conv_refs):
    i = pl.program_id(0)
    x = x_ref[...]
    hb = _rms(x, g_ref[...]).astype(BF16)

    def proj(off, width):
        return jnp.dot(hb, w_ref[:, off:off + width], preferred_element_type=F32)

    step = 512

    def plain(ref, c0=0, width=None):
        def finish(r):
            ref[:, c0:c0 + (width or r.shape[1])] = r
        return finish

    def gate_chunk(c0):
        def finish(r):
            gate[:, c0:c0 + step] = _sigmoid(r).astype(BF16)
        return finish

    def conv_part(part):
        tail_ref, carry = conv_refs
        c0 = part * GDN_W
        cols = slice(c0, c0 + GDN_W)

        def finish(r):
            first = (i % seq_steps) == 0
            sub = lax.broadcasted_iota(I32, (_HALO, 1), 0)
            rows = r.shape[0]
            before = jnp.where(first, 0.0, carry[:, cols])
            y = r * cw_ref[CONV_WIDTH - 1:CONV_WIDTH, cols]
            top = r[0:_HALO] * cw_ref[CONV_WIDTH - 1:CONV_WIDTH, cols]
            for jj in range(CONV_WIDTH - 1):
                sh = CONV_WIDTH - 1 - jj
                y = y + pltpu.roll(r, sh, 0) * cw_ref[jj:jj + 1, cols]
                head_rows = jnp.where(sub < sh, pltpu.roll(before, sh, 0), pltpu.roll(r[0:_HALO], sh, 0))
                top = top + head_rows * cw_ref[jj:jj + 1, cols]
            carry[:, cols] = r[rows - _HALO:]
            tail_ref[0, :, cols] = r[rows - _HALO:]
            act = _silu(jnp.concatenate([top, y[_HALO:]], axis=0))
            if part < 2:
                scale = GDN_HEAD_DIM ** -0.5 if part == 0 else 1.0
                for h in range(GDN_HEADS):
                    hs = slice(h * GDN_HEAD_DIM, (h + 1) * GDN_HEAD_DIM)
                    qkvg[:, c0 + h * GDN_HEAD_DIM:c0 + (h + 1) * GDN_HEAD_DIM] = _l2(act[:, hs]) * scale
            else:
                qkvg[:, cols] = act
        return finish

    tasks = [(_SEG_QS[0], _SEG_QS[1], plain(qs)), (_SEG_KS[0], _SEG_KS[1], plain(ks)),
             (_SEG_VS[0], _SEG_VS[1], plain(vs)), (_SEG_AB[0], _SEG_AB[1], plain(ab)),
             (_SEG_QM[0], _SEG_QM[1], plain(qm))]
    tasks += [(_SEG_GATE[0] + c0, step, gate_chunk(c0)) for c0 in range(0, _SEG_GATE[1], step)]
    if seq_steps is None:
        tasks += [(_SEG_Z[0], _SEG_Z[1], plain(z))]
        tasks += [(_SEG_QKVG[0] + c0, step, plain(qkvg, c0, step)) for c0 in range(0, GDN_CONV_W, step)]
    else:
        def silu_z(r):
            z[...] = _silu(r)
        tasks += [(_SEG_QKVG[0] + p * GDN_W, GDN_W, conv_part(p)) for p in range(3)]
        tasks.append((_SEG_Z[0], _SEG_Z[1], silu_z))
    for off, width, finish in tasks:
        finish(proj(off, width))


def _inproj(x2, gain, wp, conv_w, tm, seq_len=None):
    t = x2.shape[0]
    segs = (_SEG_QS, _SEG_KS, _SEG_VS, _SEG_QKVG, _SEG_Z, _SEG_AB, _SEG_QM, _SEG_GATE)
    dtypes = (F32,) * 7 + (BF16,)
    out_shape = [jax.ShapeDtypeStruct((t, w), dt) for (_, w), dt in zip(segs, dtypes)]
    out_specs = [pl.BlockSpec((tm, w), lambda i: (i, 0)) for _, w in segs]
    scratch = []
    seq_steps = None
    if seq_len is not None:
        seq_steps = seq_len // tm
        out_shape.append(jax.ShapeDtypeStruct((t // seq_len, _HALO, GDN_CONV_W), F32))
        out_specs.append(pl.BlockSpec((1, _HALO, GDN_CONV_W), lambda i: (i // seq_steps, 0, 0)))
        scratch.append(pltpu.VMEM((_HALO, GDN_CONV_W), F32))
    return pl.pallas_call(
        functools.partial(_inproj_kernel, seq_steps),
        out_shape=out_shape,
        grid=(t // tm,),
        in_specs=[pl.BlockSpec((tm, D_MODEL), lambda i: (i, 0)),
                  pl.BlockSpec((1, D_MODEL), lambda i: (0, 0)),
                  pl.BlockSpec((D_MODEL, _PACKED_W), lambda i: (0, 0), pipeline_mode=pl.Buffered(1)),
                  pl.BlockSpec((CONV_WIDTH, GDN_CONV_W), lambda i: (0, 0))],
        out_specs=out_specs,
        scratch_shapes=scratch,
        compiler_params=_cparams(("arbitrary",)),
        name="inproj",
    )(x2, gain, wp, conv_w)


SWA_STEP_BLOCKS = 8


def _alibi_slopes(n):
    return [float(2.0 ** (-8.0 * (i + 1) / n)) for i in range(n)]


def _swa_prompt_kernel(q_ref, kc_ref, vc_ref, qg_ref, kg_ref, sink_ref, o_ref, kwin_ref, bias, kprev, vprev):
    n = pl.program_id(1)
    half = SWA_HEAD_DIM
    nblk = q_ref.shape[1] // WINDOW
    work = [(j, h) for j in range(nblk) for h in range(SWA_HEADS)]

    @pl.when(n == 0)
    def _():
        kprev[...] = jnp.zeros_like(kprev)
        vprev[...] = jnp.zeros_like(vprev)

    @pl.when((pl.program_id(0) == 0) & (n == 0))
    def _():
        row = lax.broadcasted_iota(I32, (WINDOW, 2 * WINDOW), 0)
        col = lax.broadcasted_iota(I32, (WINDOW, 2 * WINDOW), 1)
        dist = row + WINDOW - col
        window = (dist >= 0) & (dist <= WINDOW)
        distf = dist.astype(F32)
        for h, slope in enumerate(_alibi_slopes(SWA_HEADS)):
            bias[0, h] = jnp.where(window & (col >= WINDOW), -slope * distf, _NEG)
            bias[1, h] = jnp.where(window, -slope * distf, _NEG)

    first_table = jnp.where(n == 0, 0, 1)

    low = lax.broadcasted_iota(I32, (1, 2 * half), 1) < half

    def pair_rms(x, gain):
        sq = x * x
        s_lo = jnp.sum(jnp.where(low, sq, 0.0), axis=-1, keepdims=True)
        s_hi = jnp.sum(jnp.where(low, 0.0, sq), axis=-1, keepdims=True)
        return x * lax.rsqrt(jnp.where(low, s_lo, s_hi) * (1.0 / half) + EPS) * gain

    kcn = pair_rms(kc_ref[0], kg_ref[...])
    kwin_ref[0] = kcn[(nblk - 1) * WINDOW:]
    kall = jnp.concatenate([kprev[...], kcn], axis=0)
    vall = jnp.concatenate([vprev[...], vc_ref[0]], axis=0)
    kprev[...] = kcn[(nblk - 1) * WINDOW:]
    vprev[...] = vc_ref[0, (nblk - 1) * WINDOW:, :]
    ones = jnp.ones(((nblk + 1) * WINDOW, 2 * half), BF16)

    def placed(x, aug):
        sw = pltpu.roll(x, half, 1)
        out = {(0, 0): jnp.where(low, x, 0.0), (1, 1): jnp.where(low, 0.0, x),
               (0, 1): jnp.where(low, 0.0, sw), (1, 0): jnp.where(low, sw, 0.0)}
        out = {key: val.astype(BF16) for key, val in out.items()}
        return {key: jnp.concatenate([val, ones], axis=1) for key, val in out.items()} if aug else out

    kvar = placed(kall, False)
    vvar = placed(vall, True)
    qn = [pair_rms(q_ref[0, :, t * 2 * half:(t + 1) * 2 * half], qg_ref[...]).astype(BF16)
          for t in range(SWA_HEADS // 2)]
    where_of = lambda h: (h // SWA_GROUP, h % 2)
    qrows = lambda j: slice(j * WINDOW, (j + 1) * WINDOW)
    krows = lambda j: slice(j * WINDOW, (j + 2) * WINDOW)
    nt = (((1,), (1,)), ((), ()))
    s = {(j, h): lax.dot_general(qn[h // 2][qrows(j)], kvar[where_of(h)][krows(j)], nt, preferred_element_type=F32)
         + bias[first_table if j == 0 else 1, h] for j, h in work}
    sink = [sink_ref[0:1, h:h + 1] for h in range(SWA_HEADS)]
    m = {(j, h): jnp.maximum(jnp.max(s[j, h], axis=-1, keepdims=True), sink[h]) for j, h in work}
    p = {(j, h): jnp.exp(s[j, h] - m[j, h]).astype(BF16) for j, h in work}
    res = {(j, h): jnp.dot(p[j, h], vvar[where_of(h)][krows(j)], preferred_element_type=F32)
           for j, h in work}
    inv = {(j, h): 1.0 / (res[j, h][:, 2 * half:2 * half + 1] + jnp.exp(sink[h] - m[j, h])) for j, h in work}
    for j in range(nblk):
        for t in range(SWA_HEADS // 2):
            h0, h1 = 2 * t, 2 * t + 1
            num = res[j, h0][:, :2 * half] + res[j, h1][:, :2 * half]
            o_ref[0, qrows(j), t * 2 * half:(t + 1) * 2 * half] = num * jnp.where(low, inv[j, h0], inv[j, h1])


def _swa_prompt(qs, ks, vs, qn, kn, sinks):
    b, l, _ = qs.shape
    rows = SWA_STEP_BLOCKS * WINDOW if l % (SWA_STEP_BLOCKS * WINDOW) == 0 else WINDOW
    cur = lambda i, j: (i, j, 0)
    const2 = lambda i, j: (0, 0)
    q_gain = jnp.tile(qn, (1, 2)) * (SWA_HEAD_DIM ** -0.5)
    k_gain = jnp.tile(kn, (1, 2))
    return pl.pallas_call(
        _swa_prompt_kernel,
        out_shape=[jax.ShapeDtypeStruct((b, l, BRANCH_W), F32),
                   jax.ShapeDtypeStruct((b, WINDOW, SWA_KV_W), F32)],
        grid=(b, l // rows),
        in_specs=[pl.BlockSpec((1, rows, BRANCH_W), cur),
                  pl.BlockSpec((1, rows, SWA_KV_W), cur),
                  pl.BlockSpec((1, rows, SWA_KV_W), cur),
                  pl.BlockSpec((1, 2 * SWA_HEAD_DIM), const2),
                  pl.BlockSpec((1, 2 * SWA_HEAD_DIM), const2),
                  pl.BlockSpec((1, SWA_HEADS), const2)],
        out_specs=[pl.BlockSpec((1, rows, BRANCH_W), cur),
                   pl.BlockSpec((1, WINDOW, SWA_KV_W), lambda i, j: (i, 0, 0))],
        scratch_shapes=[pltpu.VMEM((2, SWA_HEADS, WINDOW, 2 * WINDOW), F32),
                        pltpu.VMEM((WINDOW, SWA_KV_W), F32),
                        pltpu.VMEM((WINDOW, SWA_KV_W), F32)],
        compiler_params=_cparams(("arbitrary", "arbitrary")),
        name="swa_prompt",
    )(qs, ks, vs, q_gain, k_gain, sinks)


def _swa_decode_kernel(bs, q_ref, k3_ref, kf_ref, v3_ref, vf_ref, ck_ref, cv_ref, qn_ref, kn_ref, kn2_ref,
                       sink_ref, slope_ref, o_ref, ok_ref, ov_ref):
    scale = SWA_HEAD_DIM ** -0.5
    lane = lax.broadcasted_iota(I32, (1, SWA_KV_W), 1)
    rowi = lax.broadcasted_iota(I32, (WINDOW, SWA_KV_W), 0)
    keyd = (WINDOW - lax.broadcasted_iota(I32, (1, WINDOW), 1)).astype(F32)
    last = rowi == WINDOW - 1
    for b in range(bs):
        kf = kf_ref[b]
        sq = kf * kf
        ms0 = jnp.sum(jnp.where(lane < SWA_HEAD_DIM, sq, 0.0), axis=-1, keepdims=True) / SWA_HEAD_DIM
        ms1 = jnp.sum(jnp.where(lane >= SWA_HEAD_DIM, sq, 0.0), axis=-1, keepdims=True) / SWA_HEAD_DIM
        knf = kf * lax.rsqrt(jnp.where(lane < SWA_HEAD_DIM, ms0, ms1) + EPS) * kn2_ref[...]
        ok_ref[b] = jnp.where(last, knf, pltpu.roll(ck_ref[b], WINDOW - 1, 0))
        ov_ref[b] = jnp.where(last, vf_ref[b], pltpu.roll(cv_ref[b], WINDOW - 1, 0))
    work = [(b, g) for b in range(bs) for g in range(SWA_KV_HEADS)]
    lanes = lambda g: slice(g * SWA_HEAD_DIM, (g + 1) * SWA_HEAD_DIM)
    heads = lambda g: slice(g * SWA_GROUP, (g + 1) * SWA_GROUP)
    qn = [_rms(q_ref[b], qn_ref[...]) for b in range(bs)]
    kn3 = [_rms(k3_ref[b], kn_ref[...]) for b in range(bs)]
    qg = {(b, g): qn[b][heads(g)] for b, g in work}
    s = {(b, g): _bdot_nt(qg[b, g], ck_ref[b, :, lanes(g)]) * scale - slope_ref[heads(g)] * keyd for b, g in work}
    s_new = {(b, g): jnp.sum(qg[b, g] * kn3[b][g:g + 1], axis=-1, keepdims=True) * scale for b, g in work}
    m = {(b, g): jnp.maximum(jnp.maximum(jnp.max(s[b, g], axis=-1, keepdims=True), s_new[b, g]),
                             sink_ref[heads(g)]) for b, g in work}
    p = {w: jnp.exp(s[w] - m[w]) for w in work}
    p_new = {w: jnp.exp(s_new[w] - m[w]) for w in work}
    denom = {(b, g): jnp.sum(p[b, g], axis=-1, keepdims=True) + p_new[b, g] + jnp.exp(sink_ref[heads(g)] - m[b, g])
             for b, g in work}
    pv = {(b, g): _bdot(p[b, g], cv_ref[b, :, lanes(g)]) for b, g in work}
    for b, g in work:
        o_ref[b, heads(g), :] = (pv[b, g] + p_new[b, g] * v3_ref[b, g:g + 1, :]) / denom[b, g]


def _swa_decode(qs, ks, vs, cache_k, cache_v, qn, kn, sinks, bs=8):
    b = qs.shape[0]
    q3 = qs.reshape(b, SWA_HEADS, SWA_HEAD_DIM)
    k3 = ks.reshape(b, SWA_KV_HEADS, SWA_HEAD_DIM)
    kf = ks.reshape(b, 1, SWA_KV_W)
    v3 = vs.reshape(b, SWA_KV_HEADS, SWA_HEAD_DIM)
    vf = vs.reshape(b, 1, SWA_KV_W)
    ck = cache_k.reshape(b, WINDOW, SWA_KV_W)
    cv = cache_v.reshape(b, WINDOW, SWA_KV_W)
    kn2 = jnp.concatenate([kn, kn], axis=-1)
    sink_col = sinks.reshape(SWA_HEADS, 1)
    slope_col = jnp.asarray(np.asarray(_alibi_slopes(SWA_HEADS), np.float32).reshape(SWA_HEADS, 1))
    blk = lambda *shape: pl.BlockSpec((bs,) + shape, lambda i: (i,) + (0,) * len(shape))
    full = lambda *shape: pl.BlockSpec(shape, lambda i: (0,) * len(shape))
    o, ok, ov = pl.pallas_call(
        functools.partial(_swa_decode_kernel, bs),
        out_shape=[jax.ShapeDtypeStruct((b, SWA_HEADS, SWA_HEAD_DIM), F32),
                   jax.ShapeDtypeStruct((b, WINDOW, SWA_KV_W), F32),
                   jax.ShapeDtypeStruct((b, WINDOW, SWA_KV_W), F32)],
        grid=(b // bs,),
        in_specs=[blk(SWA_HEADS, SWA_HEAD_DIM), blk(SWA_KV_HEADS, SWA_HEAD_DIM), blk(1, SWA_KV_W),
                  blk(SWA_KV_HEADS, SWA_HEAD_DIM), blk(1, SWA_KV_W), blk(WINDOW, SWA_KV_W), blk(WINDOW, SWA_KV_W),
                  full(1, SWA_HEAD_DIM), full(1, SWA_HEAD_DIM), full(1, SWA_KV_W),
                  full(SWA_HEADS, 1), full(SWA_HEADS, 1)],
        out_specs=[blk(SWA_HEADS, SWA_HEAD_DIM), blk(WINDOW, SWA_KV_W), blk(WINDOW, SWA_KV_W)],
        compiler_params=_cparams(("parallel",)),
        name="swa_decode",
    )(q3, k3, kf, v3, vf, ck, cv, qn, kn, kn2, sink_col, slope_col)
    return o.reshape(b, BRANCH_W), ok, ov


def _gate_rows(ab, alog_ref, dtb_ref):
    g = -jnp.exp(alog_ref[...]) * _softplus(ab + dtb_ref[...])
    return g, _sigmoid(ab)


GDN_PREP_CHUNKS = 4
GDN_PREP_HEADS = 4
GDN_SCAN_BATCH = 8


def _gdn_prep_kernel(ys, ab_ref, alog_ref, dtb_ref, u_ref, w_ref, qg_ref, kd_ref, in_ref, eg_ref):
    C = GDN_CHUNK
    d = GDN_HEAD_DIM
    n = GDN_PREP_CHUNKS * C
    g_all, beta_all = _gate_rows(ab_ref[0], alog_ref, dtb_ref)
    r = lax.broadcasted_iota(I32, (n, n), 0)
    cc = lax.broadcasted_iota(I32, (n, n), 1)
    same = (r // C) == (cc // C)
    incl = same & (r >= cc)
    strict = same & (r > cc)
    upto = same & (r <= cc)
    eye = r == cc
    for h0 in range(0, GDN_HEADS, GDN_PREP_HEADS):
        heads = range(h0, h0 + GDN_PREP_HEADS)
        q = {h: ys[0, :, h * d:(h + 1) * d] for h in heads}
        k = {h: ys[0, :, GDN_W + h * d:GDN_W + (h + 1) * d] for h in heads}
        beta = {h: beta_all[:, GDN_HEADS + h:GDN_HEADS + h + 1] for h in heads}
        gc_row = {h: jnp.sum(jnp.where(upto, g_all[:, h:h + 1], 0.0), axis=0, keepdims=True) for h in heads}
        gc_col = {h: jnp.sum(jnp.where(eye, gc_row[h], 0.0), axis=1, keepdims=True) for h in heads}
        decay = {h: jnp.where(incl, jnp.exp(jnp.where(incl, gc_col[h] - gc_row[h], 0.0)), 0.0) for h in heads}
        kb = {h: k[h] * beta[h] for h in heads}
        a = {h: jnp.where(strict, _bdot_nt(kb[h], k[h]) * decay[h], 0.0) for h in heads}
        intra = {h: jnp.where(incl, _bdot_nt(q[h], k[h]) * decay[h], 0.0) for h in heads}
        base = 8
        blk = lambda s: (r // s) == (cc // s)
        bp = {h: -jnp.where(blk(base), a[h], 0.0) for h in heads}
        p = {h: eye.astype(F32) + bp[h] for h in heads}
        span = 2
        while span < base:
            bp = {h: _bdot(bp[h], bp[h]) for h in heads}
            p = {h: p[h] + _bdot(p[h], bp[h]) for h in heads}
            span *= 2
        size = base
        while size < C:
            off = {h: jnp.where(blk(2 * size) & jnp.logical_not(blk(size)), a[h], 0.0) for h in heads}
            p = {h: p[h] - _bdot(p[h], _bdot(off[h], p[h])) for h in heads}
            size *= 2
        uw = {h: _bdot(p[h], jnp.concatenate([ys[0, :, 2 * GDN_W + h * d:2 * GDN_W + (h + 1) * d] * beta[h],
                                              kb[h] * jnp.exp(gc_col[h])], axis=1)) for h in heads}
        for h in heads:
            hs = slice(h * d, (h + 1) * d)
            u_ref[0, :, hs] = uw[h][:, :d]
            w_ref[0, :, hs] = uw[h][:, d:].astype(BF16)
            qg_ref[0, :, hs] = (q[h] * jnp.exp(gc_col[h])).astype(BF16)
            for ci in range(GDN_PREP_CHUNKS):
                r0 = ci * C
                g_last = gc_row[h][:, r0 + C - 1:r0 + C]
                kd_ref[0, r0:r0 + C, hs] = (k[h][r0:r0 + C] * jnp.exp(g_last - gc_col[h][r0:r0 + C])).astype(BF16)
                in_ref[0, r0:r0 + C, h * C:(h + 1) * C] = intra[h][r0:r0 + C, r0:r0 + C].astype(BF16)
                eg_ref[0, ci, h:h + 1, :] = jnp.broadcast_to(jnp.exp(g_last), (1, LANES))


def _gdn_scan_kernel(u_ref, w_ref, qg_ref, kd_ref, in_ref, eg_ref, z_ref, gn_ref, o_ref, s_ref, st):
    c = pl.program_id(1)
    C = GDN_CHUNK
    d = GDN_HEAD_DIM

    @pl.when(c == 0)
    def _():
        st[...] = jnp.zeros_like(st)

    chains = [(bi, h) for bi in range(u_ref.shape[0]) for h in range(GDN_HEADS)]
    hsl = lambda h: slice(h * d, (h + 1) * d)
    s_f = [st[bi, h] for bi, h in chains]
    s_b = [s.astype(BF16) for s in s_f]
    ws = [jnp.dot(w_ref[bi, :, hsl(h)], s_b[i], preferred_element_type=F32) for i, (bi, h) in enumerate(chains)]
    qs = [jnp.dot(qg_ref[bi, :, hsl(h)], s_b[i], preferred_element_type=F32) for i, (bi, h) in enumerate(chains)]
    vb = [(u_ref[bi, :, hsl(h)] - ws[i]).astype(BF16) for i, (bi, h) in enumerate(chains)]
    iv = [jnp.dot(in_ref[bi, :, h * C:(h + 1) * C], vb[i], preferred_element_type=F32)
          for i, (bi, h) in enumerate(chains)]
    kv = [lax.dot_general(kd_ref[bi, :, hsl(h)], vb[i], (((0,), (0,)), ((), ())), preferred_element_type=F32)
          for i, (bi, h) in enumerate(chains)]
    for i, (bi, h) in enumerate(chains):
        st[bi, h] = s_f[i] * eg_ref[bi, 0, h:h + 1, :] + kv[i]
        o_ref[bi, :, hsl(h)] = _rms(qs[i] + iv[i], gn_ref[...]) * z_ref[bi, :, hsl(h)]
    s_ref[...] = st[...]


def _gdn_prompt(qkvg, z, ab, alog, dtb, gnorm):
    b, l, _ = qkvg.shape
    C = GDN_CHUNK
    nc = l // C
    rows = GDN_PREP_CHUNKS * C
    cur = lambda i, j: (i, j, 0)
    const2 = lambda i, j: (0, 0)
    u, w, qg, kd, intra, eg = pl.pallas_call(
        _gdn_prep_kernel,
        out_shape=[jax.ShapeDtypeStruct((b, l, GDN_W), F32),
                   jax.ShapeDtypeStruct((b, l, GDN_W), BF16),
                   jax.ShapeDtypeStruct((b, l, GDN_W), BF16),
                   jax.ShapeDtypeStruct((b, l, GDN_W), BF16),
                   jax.ShapeDtypeStruct((b, l, GDN_HEADS * C), BF16),
                   jax.ShapeDtypeStruct((b, nc, GDN_HEADS, LANES), F32)],
        grid=(b, l // rows),
        in_specs=[pl.BlockSpec((1, rows, GDN_CONV_W), cur),
                  pl.BlockSpec((1, rows, LANES), cur),
                  pl.BlockSpec((1, LANES), const2),
                  pl.BlockSpec((1, LANES), const2)],
        out_specs=[pl.BlockSpec((1, rows, GDN_W), cur),
                   pl.BlockSpec((1, rows, GDN_W), cur),
                   pl.BlockSpec((1, rows, GDN_W), cur),
                   pl.BlockSpec((1, rows, GDN_W), cur),
                   pl.BlockSpec((1, rows, GDN_HEADS * C), cur),
                   pl.BlockSpec((1, GDN_PREP_CHUNKS, GDN_HEADS, LANES), lambda i, j: (i, j, 0, 0))],
        compiler_params=_cparams(("parallel", "parallel")),
        name="gdn_prep",
    )(qkvg, ab, alog, dtb)
    bb = GDN_SCAN_BATCH if b % GDN_SCAN_BATCH == 0 else 1
    seq = lambda wd: pl.BlockSpec((bb, C, wd), cur)
    return pl.pallas_call(
        _gdn_scan_kernel,
        out_shape=[jax.ShapeDtypeStruct((b, l, GDN_W), F32),
                   jax.ShapeDtypeStruct((b, GDN_HEADS, GDN_HEAD_DIM, GDN_HEAD_DIM), F32)],
        grid=(b // bb, nc),
        in_specs=[seq(GDN_W), seq(GDN_W), seq(GDN_W), seq(GDN_W), seq(GDN_HEADS * C),
                  pl.BlockSpec((bb, 1, GDN_HEADS, LANES), lambda i, j: (i, j, 0, 0)),
                  seq(GDN_W),
                  pl.BlockSpec((1, GDN_HEAD_DIM), const2)],
        out_specs=[seq(GDN_W),
                   pl.BlockSpec((bb, GDN_HEADS, GDN_HEAD_DIM, GDN_HEAD_DIM), lambda i, j: (i, 0, 0, 0))],
        scratch_shapes=[pltpu.VMEM((bb, GDN_HEADS, GDN_HEAD_DIM, GDN_HEAD_DIM), F32)],
        compiler_params=_cparams(("parallel", "arbitrary")),
        name="gdn_scan",
    )(u, w, qg, kd, intra, eg, z, gnorm)


def _gdn_decode_kernel(bs, x_ref, cs_ref, z_ref, ab_ref, s_ref, cw_ref, alog_ref, dtb_ref, gn_ref, o_ref, so_ref):
    x = x_ref[...]
    y = x * cw_ref[CONV_WIDTH - 1:CONV_WIDTH, :]
    for j in range(CONV_WIDTH - 1):
        y = y + cs_ref[:, j, :] * cw_ref[j:j + 1, :]
    y = _silu(y)
    g_all, beta_all = _gate_rows(ab_ref[...], alog_ref, dtb_ref)
    eg_all = jnp.exp(g_all)
    z = z_ref[...]
    d = GDN_HEAD_DIM
    eye = (lax.broadcasted_iota(I32, (d, d), 0) == lax.broadcasted_iota(I32, (d, d), 1)).astype(F32)
    for h in range(GDN_HEADS):
        hs = slice(h * d, (h + 1) * d)
        qh = _l2(y[:, h * d:(h + 1) * d]) * (d ** -0.5)
        kh = _l2(y[:, GDN_W + h * d:GDN_W + (h + 1) * d])
        vh = y[:, 2 * GDN_W + h * d:2 * GDN_W + (h + 1) * d]
        kt = _hdot_nt(eye, kh)
        qt = _hdot_nt(eye, qh)
        rng = range(bs)
        eg = [eg_all[b:b + 1, h:h + 1] for b in rng]
        beta = [beta_all[b:b + 1, GDN_HEADS + h:GDN_HEADS + h + 1] for b in rng]
        sk = [jnp.sum(s_ref[b, h] * kt[:, b:b + 1], axis=0, keepdims=True) for b in rng]
        v_new = [beta[b] * (vh[b:b + 1] - eg[b] * sk[b]) for b in rng]
        s_new = [eg[b] * s_ref[b, h] + kt[:, b:b + 1] * v_new[b] for b in rng]
        for b in rng:
            so_ref[b, h] = s_new[b]
        o = jnp.concatenate([jnp.sum(s_new[b] * qt[:, b:b + 1], axis=0, keepdims=True) for b in rng], axis=0)
        o_ref[:, hs] = _rms(o, gn_ref[...]) * _silu(z[:, hs])


def _gdn_decode(qkvg, z, ab, state, conv_state, conv_w, alog, dtb, gnorm, bs=8):
    b = qkvg.shape[0]
    row = lambda w: pl.BlockSpec((bs, w), lambda i: (i, 0))
    full = lambda *shape: pl.BlockSpec(shape, lambda i: (0,) * len(shape))
    sspec = pl.BlockSpec((bs, GDN_HEADS, GDN_HEAD_DIM, GDN_HEAD_DIM), lambda i: (i, 0, 0, 0))
    return pl.pallas_call(
        functools.partial(_gdn_decode_kernel, bs),
        out_shape=[jax.ShapeDtypeStruct((b, GDN_W), F32),
                   jax.ShapeDtypeStruct(state.shape, F32)],
        grid=(b // bs,),
        in_specs=[row(GDN_CONV_W),
                  pl.BlockSpec((bs, CONV_WIDTH - 1, GDN_CONV_W), lambda i: (i, 0, 0)),
                  row(GDN_W), row(LANES), sspec,
                  full(CONV_WIDTH, GDN_CONV_W), full(1, LANES), full(1, LANES), full(1, GDN_HEAD_DIM)],
        out_specs=[row(GDN_W), sspec],
        compiler_params=_cparams(("parallel",)),
        name="gdn_decode",
    )(qkvg, conv_state, z, ab, state, conv_w, alog, dtb, gnorm)


def _memkv_kernel(x_ref, g_ref, w_ref, kn_ref, k_ref, v_ref):
    hb = _rms(x_ref[...], g_ref[...]).astype(BF16)
    for hd in range(MEM_HEADS):
        hs = slice(hd * MEM_HEAD_DIM, (hd + 1) * MEM_HEAD_DIM)
        k_ref[:, hs] = _rms(jnp.dot(hb, w_ref[:, hs], preferred_element_type=F32), kn_ref[...])
    v_ref[...] = jnp.dot(hb, w_ref[:, BRANCH_W:], preferred_element_type=F32)


def _memkv(mem2, gain, w_kv, kn, tm=512):
    t = mem2.shape[0]
    return pl.pallas_call(
        _memkv_kernel,
        out_shape=[jax.ShapeDtypeStruct((t, BRANCH_W), F32)] * 2,
        grid=(t // tm,),
        in_specs=[pl.BlockSpec((tm, D_MODEL), lambda i: (i, 0)),
                  pl.BlockSpec((1, D_MODEL), lambda i: (0, 0)),
                  pl.BlockSpec((D_MODEL, 2 * BRANCH_W), lambda i: (0, 0)),
                  pl.BlockSpec((1, MEM_HEAD_DIM), lambda i: (0, 0))],
        out_specs=[pl.BlockSpec((tm, BRANCH_W), lambda i: (i, 0))] * 2,
        compiler_params=_cparams(("parallel",)),
        name="memkv",
    )(mem2, gain, w_kv, kn)


def _memattn_prompt_kernel(q_ref, k_ref, v_ref, qn_ref, o_ref):
    q = q_ref[0]
    k = k_ref[0]
    v = v_ref[0]
    scale = MEM_HEAD_DIM ** -0.5
    for hd in range(MEM_HEADS):
        hs = slice(hd * MEM_HEAD_DIM, (hd + 1) * MEM_HEAD_DIM)
        s = _bdot_nt(_rms(q[:, hs], qn_ref[...]), k[:, hs]) * scale
        p = jnp.exp(s - jnp.max(s, axis=-1, keepdims=True))
        o_ref[0, :, hs] = _bdot(p, v[:, hs]) / jnp.sum(p, axis=-1, keepdims=True)


def _memattn_prompt(qm, mk, mv, qn):
    b, l, _ = qm.shape
    tq = 512 if l % 512 == 0 else WINDOW
    return pl.pallas_call(
        _memattn_prompt_kernel,
        out_shape=jax.ShapeDtypeStruct((b, l, BRANCH_W), F32),
        grid=(b, l // tq),
        in_specs=[pl.BlockSpec((1, tq, BRANCH_W), lambda i, j: (i, j, 0)),
                  pl.BlockSpec((1, MEM_LEN, BRANCH_W), lambda i, j: (i, 0, 0)),
                  pl.BlockSpec((1, MEM_LEN, BRANCH_W), lambda i, j: (i, 0, 0)),
                  pl.BlockSpec((1, MEM_HEAD_DIM), lambda i, j: (0, 0))],
        out_specs=pl.BlockSpec((1, tq, BRANCH_W), lambda i, j: (i, j, 0)),
        compiler_params=_cparams(("parallel", "parallel")),
        name="memattn_prompt",
    )(qm, mk, mv, qn)


def _memattn_decode_kernel(bs, q_ref, k_ref, v_ref, qn_ref, o_ref):
    scale = MEM_HEAD_DIM ** -0.5
    for b in range(bs):
        qn = _rms(q_ref[b], qn_ref[...])
        s = jnp.sum(k_ref[b] * qn, axis=-1, keepdims=True) * scale
        p = jnp.exp(s - jnp.max(s, axis=0, keepdims=True))
        den = jnp.sum(p, axis=0)
        o_ref[b] = jnp.sum(p * v_ref[b], axis=0) / den


def _memattn_decode(qm, ck, cv, qn, bs=8):
    b = qm.shape[0]
    q3 = pl.BlockSpec((bs, MEM_HEADS, MEM_HEAD_DIM), lambda i: (i, 0, 0))
    kv = pl.BlockSpec((bs, MEM_LEN, MEM_HEADS, MEM_HEAD_DIM), lambda i: (i, 0, 0, 0))
    o = pl.pallas_call(
        functools.partial(_memattn_decode_kernel, bs),
        out_shape=jax.ShapeDtypeStruct((b, MEM_HEADS, MEM_HEAD_DIM), F32),
        grid=(b // bs,),
        in_specs=[q3, kv, kv, pl.BlockSpec((1, MEM_HEAD_DIM), lambda i: (0, 0))],
        out_specs=q3,
        compiler_params=_cparams(("parallel",)),
        name="memattn_decode",
    )(qm.reshape(b, MEM_HEADS, MEM_HEAD_DIM), ck, cv, qn)
    return o.reshape(b, BRANCH_W)


def _merge_kernel(oa_ref, ob_ref, oc_ref, gate_ref, x_ref, wb_ref, wo_ref, g2_ref, wr_ref, br_ref,
                  x1_ref, h2_ref, lg_ref):
    acc = None
    for i, o_ref in enumerate((oa_ref, ob_ref, oc_ref)):
        mixed = jnp.dot(o_ref[...].astype(BF16), wb_ref[i], preferred_element_type=F32)
        term = gate_ref[:, i * D_MODEL:(i + 1) * D_MODEL].astype(F32) * mixed
        acc = term if acc is None else acc + term
    x1 = x_ref[...] + jnp.dot(acc.astype(BF16), wo_ref[...], preferred_element_type=F32)
    x1_ref[...] = x1
    h2 = _rms(x1, g2_ref[...])
    h2_ref[...] = h2.astype(BF16)
    h_hi = h2.astype(BF16)
    h_lo = (h2 - h_hi.astype(F32)).astype(BF16)
    nt = (((1,), (1,)), ((), ()))
    a = lax.dot_general(wr_ref[...], h_hi, nt, preferred_element_type=F32)
    b = lax.dot_general(wr_ref[0:N_EXPERTS, :], h_lo, nt, preferred_element_type=F32)
    lg_ref[...] = a[0:N_EXPERTS] + a[N_EXPERTS:] + b + br_ref[...]


def _merge_into_kernel(oa_ref, ob_ref, oc_ref, gate_ref, x_ref, wb_ref, wo_ref, g2_ref, wr_ref, br_ref,
                       x1_in, h2_in, lg_in, x1_ref, h2_ref, lg_ref):
    del x1_in, h2_in, lg_in
    _merge_kernel(oa_ref, ob_ref, oc_ref, gate_ref, x_ref, wb_ref, wo_ref, g2_ref, wr_ref, br_ref,
                  x1_ref, h2_ref, lg_ref)


def _merge(oa, ob, oc, gate, x2, wb, wo, g2, wr_t, br_col, tm, t_out=None, into=None):
    t = x2.shape[0]
    t_out = t_out or t
    blk0 = 0 if into is None else (into[0].shape[0] - t) // tm
    row = lambda w: pl.BlockSpec((tm, w), lambda i: (i, 0))
    orow = lambda w: pl.BlockSpec((tm, w), lambda i: (i + blk0, 0))
    full = lambda *shape: pl.BlockSpec(shape, lambda i: (0,) * len(shape))
    in_specs = [row(BRANCH_W), row(BRANCH_W), row(BRANCH_W), row(N_BRANCH * D_MODEL), row(D_MODEL),
                full(N_BRANCH, BRANCH_W, D_MODEL), full(D_MODEL, D_MODEL), full(1, D_MODEL),
                full(2 * N_EXPERTS, D_MODEL), full(N_EXPERTS, 1)]
    args = [oa, ob, oc, gate, x2, wb, wo, g2, wr_t, br_col]
    aliases = {}
    if into is not None:
        t_out = into[0].shape[0]
        in_specs += [pl.BlockSpec(memory_space=pl.ANY)] * 3
        aliases = {len(args) + k: k for k in range(3)}
        args += list(into)
    return pl.pallas_call(
        _merge_kernel if into is None else _merge_into_kernel,
        out_shape=[jax.ShapeDtypeStruct((t_out, D_MODEL), F32),
                   jax.ShapeDtypeStruct((t_out, D_MODEL), BF16),
                   jax.ShapeDtypeStruct((N_EXPERTS, t_out), F32)],
        grid=(t // tm,),
        in_specs=in_specs,
        out_specs=[orow(D_MODEL), orow(D_MODEL), pl.BlockSpec((N_EXPERTS, tm), lambda i: (0, i + blk0))],
        input_output_aliases=aliases,
        compiler_params=_cparams(("parallel",)),
        name="merge",
    )(*args)


def _col_to_row(col):
    n = col.shape[0]
    r = lax.broadcasted_iota(I32, (n, n), 0)
    c = lax.broadcasted_iota(I32, (n, n), 1)
    return jnp.sum(jnp.where(r == c, col, 0.0), axis=0, keepdims=True)


def _row_to_col(row):
    n = row.shape[1]
    r = lax.broadcasted_iota(I32, (n, n), 0)
    c = lax.broadcasted_iota(I32, (n, n), 1)
    return jnp.sum(jnp.where(r == c, row, 0.0), axis=1, keepdims=True)


def _lane_pad(row):
    return jnp.concatenate([row, jnp.zeros((1, LANES - row.shape[1]), row.dtype)], axis=1)


def _route_kernel(lg_ref, pos_ref, gate_ref, meta_ref, cnt_ref, carry):
    i = pl.program_id(0)
    tn = lg_ref.shape[1]

    @pl.when(i == 0)
    def _():
        carry[...] = jnp.zeros_like(carry)

    l = lg_ref[...]
    eio = lax.broadcasted_iota(I32, (N_EXPERTS, tn), 0)
    hot = jnp.zeros((N_EXPERTS, tn), F32)
    vals, idxs = [], []
    for _ in range(TOP_K):
        m = jnp.max(l, axis=0, keepdims=True)
        idx = jnp.min(jnp.where(l == m, eio, N_EXPERTS), axis=0, keepdims=True)
        sel = eio == idx
        vals.append(m)
        idxs.append(idx)
        hot = hot + sel.astype(F32)
        l = jnp.where(sel, -jnp.inf, l)
    ex = [jnp.exp(v - vals[0]) for v in vals]
    tot = ex[0] + ex[1] + ex[2] + ex[3]
    before = (lax.broadcasted_iota(I32, (tn, tn), 0) < lax.broadcasted_iota(I32, (tn, tn), 1)).astype(BF16)
    within = jnp.dot(hot.astype(BF16), before, preferred_element_type=F32)
    cnt_col = jnp.sum(hot, axis=1, keepdims=True)
    room = jnp.floor((cnt_col + (MOE_UNIT - 1)) * (1.0 / MOE_UNIT)) * MOE_UNIT
    r = lax.broadcasted_iota(I32, (N_EXPERTS, N_EXPERTS), 0)
    c = lax.broadcasted_iota(I32, (N_EXPERTS, N_EXPERTS), 1)
    start_col = _row_to_col(jnp.sum(jnp.where(r < c, room, 0.0), axis=0, keepdims=True))
    for k in range(TOP_K):
        gate_ref[k:k + 1, :] = ex[k] / tot
        pos_ref[k:k + 1, :] = jnp.sum(jnp.where(eio == idxs[k], start_col + within, 0.0), axis=0,
                                      keepdims=True).astype(I32)
    meta_ref[0] = _lane_pad(jnp.concatenate([_col_to_row(room), _col_to_row(carry[:, 0:1])], axis=1)).astype(I32)
    carry[...] = carry[...] + room
    cnt_ref[...] = carry[...]


def _route(logits_t, tn):
    t = logits_t.shape[1]
    kt = pl.BlockSpec((TOP_K, tn), lambda i: (0, i))
    return pl.pallas_call(
        _route_kernel,
        out_shape=[jax.ShapeDtypeStruct((TOP_K, t), I32),
                   jax.ShapeDtypeStruct((TOP_K, t), F32),
                   jax.ShapeDtypeStruct((t // tn, 1, LANES), I32),
                   jax.ShapeDtypeStruct((N_EXPERTS, LANES), F32)],
        grid=(t // tn,),
        in_specs=[pl.BlockSpec((N_EXPERTS, tn), lambda i: (0, i))],
        out_specs=[kt, kt, pl.BlockSpec((1, 1, LANES), lambda i: (i, 0, 0)),
                   pl.BlockSpec((N_EXPERTS, LANES), lambda i: (0, 0))],
        scratch_shapes=[pltpu.VMEM((N_EXPERTS, LANES), F32)],
        compiler_params=_cparams(("arbitrary",)),
        name="moe_route",
    )(logits_t)


def _layout_kernel(bm, cnt_ref, be_ref, na_ref, ps_ref, pe_ref, ct_ref):
    nbp = be_ref.shape[1]
    cnt_col = cnt_ref[:, 0:1]
    size = jnp.floor((cnt_col + (bm - 1)) * (1.0 / bm)) * bm
    r = lax.broadcasted_iota(I32, (N_EXPERTS, N_EXPERTS), 0)
    c = lax.broadcasted_iota(I32, (N_EXPERTS, N_EXPERTS), 1)
    ends_row = jnp.sum(jnp.where(r <= c, size, 0.0), axis=0, keepdims=True)
    ends_col = _row_to_col(ends_row)
    nact = ends_row[:, N_EXPERTS - 1:N_EXPERTS] * (1.0 / bm)
    blk = jnp.minimum(lax.broadcasted_iota(I32, (1, nbp), 1).astype(F32), nact - 1.0)
    be = jnp.sum((ends_col <= blk * bm).astype(F32), axis=0, keepdims=True)
    be_ref[...] = jnp.minimum(be, N_EXPERTS - 1.0).astype(I32)
    na_ref[...] = jnp.broadcast_to(nact, na_ref.shape).astype(I32)
    ps_ref[...] = _lane_pad(ends_row - _col_to_row(size)).astype(I32)
    pe_ref[...] = _lane_pad(ends_row).astype(I32)
    ct_ref[...] = _lane_pad(_col_to_row(cnt_col)).astype(I32)


def _layout(cnt, nbp, bm):
    row = jax.ShapeDtypeStruct((1, LANES), I32)
    return pl.pallas_call(
        functools.partial(_layout_kernel, bm),
        out_shape=[jax.ShapeDtypeStruct((1, nbp), I32), row, row, row, row],
        name="moe_layout",
    )(cnt)


def _stage_rows(tn):
    return tn * TOP_K + N_EXPERTS * MOE_UNIT


def _run_copies(meta_ref, ps_ref, make_copy):
    low_bits = 3
    chunk = MOE_UNIT << low_bits
    src = jnp.int32(0)
    total = jnp.int32(0)
    for e in range(N_EXPERTS):
        units = lax.shift_right_logical(meta_ref[0, 0, e], 3)
        dst = ps_ref[e] + meta_ref[0, 0, N_EXPERTS + e]
        chunks = lax.shift_right_logical(units, low_bits)

        def body(c, carry, src=src, dst=dst):
            make_copy(pl.multiple_of(src + c * chunk, MOE_UNIT), pl.multiple_of(dst + c * chunk, MOE_UNIT),
                      chunk).start()
            return carry
        lax.fori_loop(0, chunks, body, 0)
        off = chunks * chunk
        for k in range(low_bits - 1, -1, -1):
            rows = MOE_UNIT << k
            take = lax.shift_right_logical(units, k) & 1

            @pl.when(take == 1)
            def _(src=src, dst=dst, off=off, rows=rows):
                make_copy(pl.multiple_of(src + off, MOE_UNIT), pl.multiple_of(dst + off, MOE_UNIT), rows).start()
            off = off + take * rows
        src = src + units * MOE_UNIT
        total = total + units
    return total


def _wait_copies(units, make_copy, max_rows):
    for k in range((max_rows // MOE_UNIT).bit_length()):
        if (MOE_UNIT << k) > max_rows:
            break

        @pl.when(lax.shift_right_logical(units, k) & 1 == 1)
        def _(k=k):
            make_copy(0, 0, MOE_UNIT << k).wait()


def _wait_units(count, make_copy):
    def body(u, carry):
        make_copy(0, 0).wait()
        return carry
    lax.fori_loop(0, count, body, 0)


def _dispatch_kernel(ps_ref, pe_ref, ct_ref, meta_ref, pos_ref, gate_ref, h_ref, xg_ref, stage, zrows, zblock,
                     started, sem, zsem):
    i = pl.program_id(0)
    n = pl.num_programs(0)
    slot = i % 2
    rows, tn = stage.shape[1], pos_ref.shape[1]

    def copy(s):
        return lambda src, dst, nrows: pltpu.make_async_copy(stage.at[s, pl.ds(src, nrows)],
                                                             xg_ref.at[pl.ds(dst, nrows)], sem.at[s])

    @pl.when(i >= 2)
    def _():
        _wait_copies(started[slot], copy(slot), rows)

    srow = lax.broadcasted_iota(I32, (rows, tn), 0)
    hits = [srow == pos_ref[k:k + 1, :] for k in range(TOP_K)]
    hit = hits[0]
    gate_at = jnp.where(hits[0], gate_ref[0:1, :], 0.0)
    for k in range(1, TOP_K):
        hit = hit | hits[k]
        gate_at = gate_at + jnp.where(hits[k], gate_ref[k:k + 1, :], 0.0)
    onehot = jnp.where(hit, 1.0, 0.0).astype(BF16)
    stage[slot, :, 0:D_MODEL] = jnp.dot(onehot, h_ref[...], preferred_element_type=F32)
    stage[slot, :, D_MODEL:] = jnp.broadcast_to(jnp.sum(gate_at, axis=1, keepdims=True), (rows, LANES))
    started[slot] = _run_copies(meta_ref, ps_ref, copy(slot))

    @pl.when(i == n - 1)
    def _():
        _wait_copies(started[slot], copy(slot), rows)

        @pl.when(n >= 2)
        def _():
            _wait_copies(started[1 - slot], copy(1 - slot), rows)

        zrows[...] = jnp.zeros_like(zrows)
        zero = lambda src, dst: pltpu.make_async_copy(zrows, xg_ref.at[pl.ds(dst, MOE_UNIT)], zsem)
        total = jnp.int32(0)
        for e in range(N_EXPERTS):
            lo = ps_ref[e] + ct_ref[e]
            units = lax.shift_right_logical(pe_ref[e] - lo, 3)

            def body(u, carry, lo=lo):
                zero(0, pl.multiple_of(lo + u * MOE_UNIT, MOE_UNIT)).start()
                return carry
            lax.fori_loop(0, units, body, 0)
            total = total + units
        _wait_units(total, zero)

        zblock[...] = jnp.zeros_like(zblock)
        bm = zblock.shape[0]
        zero_block = lambda blk: pltpu.make_async_copy(zblock, xg_ref.at[pl.ds(pl.multiple_of(blk * bm, bm), bm)],
                                                       zsem)
        first = lax.shift_right_logical(pe_ref[N_EXPERTS - 1], bm.bit_length() - 1)
        last = xg_ref.shape[0] // bm

        def start_block(blk, carry):
            zero_block(blk).start()
            return carry
        lax.fori_loop(first, last, start_block, 0)

        def wait_block(blk, carry):
            zero_block(0).wait()
            return carry
        lax.fori_loop(first, last, wait_block, 0)


def _dispatch(ps, pe, ct, meta, pos, gates, h2, n_slots, tn, bm):
    t = h2.shape[0]
    width = D_MODEL + LANES
    return pl.pallas_call(
        _dispatch_kernel,
        out_shape=jax.ShapeDtypeStruct((n_slots, width), F32),
        grid_spec=pltpu.PrefetchScalarGridSpec(
            num_scalar_prefetch=3,
            grid=(t // tn,),
            in_specs=[pl.BlockSpec((1, 1, LANES), lambda i, *_: (i, 0, 0), memory_space=pltpu.SMEM),
                      pl.BlockSpec((TOP_K, tn), lambda i, *_: (0, i)),
                      pl.BlockSpec((TOP_K, tn), lambda i, *_: (0, i)),
                      pl.BlockSpec((tn, D_MODEL), lambda i, *_: (i, 0))],
            out_specs=pl.BlockSpec(memory_space=pl.ANY),
            scratch_shapes=[pltpu.VMEM((2, _stage_rows(tn), width), F32),
                            pltpu.VMEM((MOE_UNIT, width), F32),
                            pltpu.VMEM((bm, width), F32),
                            pltpu.SMEM((2,), I32),
                            pltpu.SemaphoreType.DMA((2,)),
                            pltpu.SemaphoreType.DMA(())]),
        compiler_params=_cparams(("arbitrary",)),
        name="moe_dispatch",
    )(ps, pe, ct, meta, pos, gates, h2)


def _expert_kernel(be_ref, na_ref, x_ref, w1_ref, b1_ref, w2_ref, b2_ref, y_ref, w1s, w2s):
    i = pl.program_id(0)
    e = be_ref[i]
    prev = be_ref[jnp.maximum(i - 1, 0)]

    @pl.when((i == 0) | (e != prev))
    def _():
        rows = 128
        for r0 in range(0, D_MODEL, rows):
            w1s[r0:r0 + rows, :] = w1_ref[r0:r0 + rows, :].astype(BF16)
        for r0 in range(0, D_FF, rows):
            w2s[r0:r0 + rows, :] = w2_ref[r0:r0 + rows, :].astype(BF16)

    @pl.when(i < na_ref[0])
    def _():
        x = x_ref[:, 0:D_MODEL].astype(BF16)
        hmid = jnp.dot(x, w1s[...], preferred_element_type=F32) + b1_ref[...]
        glu = jnp.minimum(hmid[:, :D_FF], SWIGLU_LIMIT)
        lin = jnp.clip(hmid[:, D_FF:], -SWIGLU_LIMIT, SWIGLU_LIMIT)
        act = glu * _sigmoid(SWIGLU_ALPHA * glu) * (lin + 1.0)
        y = jnp.dot(act.astype(BF16), w2s[...], preferred_element_type=F32) + b2_ref[...]
        y_ref[...] = y * x_ref[:, D_MODEL:D_MODEL + 1]

    @pl.when(i >= na_ref[0])
    def _():
        y_ref[...] = jnp.zeros_like(y_ref)


def _experts(be, nact, xg, layer, w1, b1, w2, b2, bm):
    n_slots = xg.shape[0]
    nb = n_slots // bm
    blk = lambda i, be_r, na_r, *_: (jnp.minimum(i, na_r[0] - 1), 0)
    wsel = lambda i, be_r, *_: (layer, be_r[i], 0, 0)
    return pl.pallas_call(
        _expert_kernel,
        out_shape=jax.ShapeDtypeStruct((n_slots, D_MODEL), F32),
        grid_spec=pltpu.PrefetchScalarGridSpec(
            num_scalar_prefetch=2,
            grid=(nb,),
            in_specs=[pl.BlockSpec((bm, xg.shape[1]), blk),
                      pl.BlockSpec((None, None, D_MODEL, 2 * D_FF), wsel),
                      pl.BlockSpec((None, None, 1, 2 * D_FF), wsel),
                      pl.BlockSpec((None, None, D_FF, D_MODEL), wsel),
                      pl.BlockSpec((None, None, 1, D_MODEL), wsel)],
            out_specs=pl.BlockSpec((bm, D_MODEL), lambda i, *_: (i, 0)),
            scratch_shapes=[pltpu.VMEM((D_MODEL, 2 * D_FF), BF16), pltpu.VMEM((D_FF, D_MODEL), BF16)]),
        compiler_params=_cparams(("arbitrary",)),
        name="moe_experts",
    )(be, nact, xg, w1, b1, w2, b2)


def _combine_kernel(n_first, ps_ref, mcur_ref, mnext_ref, pos_ref, x1_ref, yg_ref, o_ref, o2_ref, stage,
                    started, sem):
    i = pl.program_id(0)
    n = pl.num_programs(0)
    slot = i % 2
    rows, tn = stage.shape[1], pos_ref.shape[1]

    @pl.when(i == 0)
    def _():
        stage[...] = jnp.zeros_like(stage)

    def copy(s):
        return lambda src, dst, nrows: pltpu.make_async_copy(yg_ref.at[pl.ds(dst, nrows)],
                                                             stage.at[s, pl.ds(src, nrows)], sem.at[s])

    @pl.when(i == 0)
    def _():
        started[0] = _run_copies(mcur_ref, ps_ref, copy(0))

    @pl.when(i + 1 < n)
    def _():
        started[1 - slot] = _run_copies(mnext_ref, ps_ref, copy(1 - slot))

    _wait_copies(started[slot], copy(slot), rows)

    eye = (lax.broadcasted_iota(I32, (tn, tn), 0) == lax.broadcasted_iota(I32, (tn, tn), 1)).astype(F32)
    cols = _hdot_nt(eye, pos_ref[...].astype(F32)).astype(I32)
    lane = lax.broadcasted_iota(I32, (tn, rows), 1)
    hit = lane == cols[:, 0:1]
    for k in range(1, TOP_K):
        hit = hit | (lane == cols[:, k:k + 1])
    pick = jnp.where(hit, 1.0, 0.0).astype(BF16)
    out = x1_ref[...] + jnp.dot(pick, stage[slot].astype(BF16), preferred_element_type=F32)

    @pl.when(i < n_first)
    def _():
        o_ref[...] = out

    @pl.when(i >= n_first)
    def _():
        o2_ref[...] = out


def _combine(ps, meta, pos, x1, yg, tn, t_first):
    t = x1.shape[0]
    nt = t // tn
    n_first = t_first // tn
    return pl.pallas_call(
        functools.partial(_combine_kernel, n_first),
        out_shape=[jax.ShapeDtypeStruct((t_first, D_MODEL), F32),
                   jax.ShapeDtypeStruct((t - t_first, D_MODEL), F32)],
        grid_spec=pltpu.PrefetchScalarGridSpec(
            num_scalar_prefetch=1,
            grid=(nt,),
            in_specs=[pl.BlockSpec((1, 1, LANES), lambda i, *_: (i, 0, 0), memory_space=pltpu.SMEM),
                      pl.BlockSpec((1, 1, LANES), lambda i, *_: (jnp.minimum(i + 1, nt - 1), 0, 0),
                                   memory_space=pltpu.SMEM),
                      pl.BlockSpec((TOP_K, tn), lambda i, *_: (0, i)),
                      pl.BlockSpec((tn, D_MODEL), lambda i, *_: (i, 0)),
                      pl.BlockSpec(memory_space=pl.ANY)],
            out_specs=[pl.BlockSpec((tn, D_MODEL), lambda i, *_: (jnp.minimum(i, n_first - 1), 0)),
                       pl.BlockSpec((tn, D_MODEL), lambda i, *_: (jnp.maximum(i - n_first, 0), 0))],
            scratch_shapes=[pltpu.VMEM((2, _stage_rows(tn), D_MODEL), F32),
                            pltpu.SMEM((2,), I32),
                            pltpu.SemaphoreType.DMA((2,))]),
        compiler_params=_cparams(("arbitrary",)),
        name="moe_combine",
    )(ps, meta, meta, pos, x1, yg)


def _moe(x1, h2, logits_t, layer, w1, b1, w2, b2, t_first):
    t = x1.shape[0]
    tn = MOE_TOK_TILE
    bm = MOE_BM if t * TOP_K >= N_EXPERTS * MOE_BM else LANES
    n_blocks = (t * TOP_K + (t // tn) * N_EXPERTS * (MOE_UNIT - 1)) // bm + N_EXPERTS + 1
    nbp = -(-n_blocks // LANES) * LANES
    pos, gates, meta, cnt = _route(logits_t, tn)
    be, nact, ps, pe, ct = _layout(cnt, nbp, bm)
    ps, pe, ct = ps[0, :N_EXPERTS], pe[0, :N_EXPERTS], ct[0, :N_EXPERTS]
    xg = _dispatch(ps, pe, ct, meta, pos, gates, h2, n_blocks * bm, tn, bm)
    yg = _experts(be[0, :n_blocks], nact[0, :1], xg, layer, w1, b1, w2, b2, bm)
    return _combine(ps, meta, pos, x1, yg, tn, t_first)


def _pack_w_in(w):
    ab = jnp.pad(w[:, _IN_MAIN:_IN_MAIN + _IN_AB], ((0, 0), (0, LANES - _IN_AB)))
    return jnp.concatenate([w[:, :_IN_MAIN], ab, w[:, _IN_MAIN + _IN_AB:]], axis=1).astype(BF16)


def _lane_row(v):
    return jnp.pad(v.astype(F32), (0, LANES - v.shape[0])).reshape(1, LANES)


def _layer_weights(ln1_gain, w_in, q_norm_swa, k_norm_swa, swa_sinks, conv_w, a_log, dt_bias, gdn_norm,
                   q_norm_mem, w_branch, w_out, ln2_gain, w_router, b_router, w_mlp1, b_mlp1, w_mlp2, b_mlp2, layer):
    depth = w_mlp1.shape[0]
    wr_hi = w_router.T.astype(BF16)
    wr_lo = (w_router.T - wr_hi.astype(F32)).astype(BF16)
    return dict(
        layer=layer, wr_t=jnp.concatenate([wr_hi, wr_lo], axis=0), b1r=b_mlp1.reshape(depth, N_EXPERTS, 1, -1), b2r=b_mlp2.reshape(depth, N_EXPERTS, 1, -1),
        ln1=ln1_gain.reshape(1, -1), wp=_pack_w_in(w_in),
        qn_s=q_norm_swa.reshape(1, -1), kn_s=k_norm_swa.reshape(1, -1), sinks=swa_sinks.reshape(1, -1),
        conv_w=conv_w, alog=_lane_row(a_log), dtb=_lane_row(dt_bias), gnorm=gdn_norm.reshape(1, -1),
        qn_m=q_norm_mem.reshape(1, -1), wb=w_branch.astype(BF16), wo=w_out.astype(BF16),
        ln2=ln2_gain.reshape(1, -1), br=b_router.reshape(-1, 1),
        w1=w_mlp1, w2=w_mlp2)


def _finish(lw, first, second):
    t1, t2 = first[4].shape[0], second[4].shape[0]
    tm1 = MERGE_ROWS if t1 % MERGE_ROWS == 0 else MOE_TOK_TILE
    pad = (-t2) % MOE_TOK_TILE
    second = [jnp.pad(a, ((0, pad), (0, 0))) for a in second]
    wts = (lw["wb"], lw["wo"], lw["ln2"], lw["wr_t"], lw["br"])
    bufs = _merge(*first, *wts, tm1, t_out=t1 + t2 + pad)
    bufs = _merge(*second, *wts, MOE_TOK_TILE, into=bufs)
    y1, y2 = _moe(*bufs, lw["layer"], lw["w1"], lw["b1r"], lw["w2"], lw["b2r"], t1)
    return y1, y2[:t2]


def _prompt_layer(x, mem, lw, mem_norm, w_mem_kv, k_norm_mem):
    b, l, d = x.shape
    x2 = x.reshape(b * l, d)
    tm = 256
    tm_in = 2 * tm if l % (2 * tm) == 0 else tm
    qs, ks, vs, qkvg, z, ab, qm, gate, conv_tail = _inproj(x2, lw["ln1"], lw["wp"], lw["conv_w"], tm_in, seq_len=l)
    mk, mv = _memkv(mem.reshape(b * MEM_LEN, d), mem_norm.reshape(1, -1), w_mem_kv.astype(BF16),
                    k_norm_mem.reshape(1, -1))
    r3 = lambda a: a.reshape(b, l, a.shape[-1])
    o_swa, kwin = _swa_prompt(r3(qs), r3(ks), r3(vs), lw["qn_s"], lw["kn_s"], lw["sinks"])
    o_g, s_fin = _gdn_prompt(r3(qkvg), r3(z), r3(ab), lw["alog"], lw["dtb"], lw["gnorm"])
    o_m = _memattn_prompt(r3(qm), mk.reshape(b, MEM_LEN, BRANCH_W), mv.reshape(b, MEM_LEN, BRANCH_W), lw["qn_m"])
    parts = (o_swa.reshape(b * l, -1), o_g.reshape(b * l, -1), o_m.reshape(b * l, -1), gate, x2)
    new_k = kwin.reshape(b, WINDOW, SWA_KV_HEADS, SWA_HEAD_DIM)
    new_v = r3(vs)[:, l - WINDOW:].reshape(b, WINDOW, SWA_KV_HEADS, SWA_HEAD_DIM)
    new_conv = conv_tail[:, _HALO - (CONV_WIDTH - 1):]
    mk4 = mk.reshape(b, MEM_LEN, MEM_HEADS, MEM_HEAD_DIM)
    mv4 = mv.reshape(b, MEM_LEN, MEM_HEADS, MEM_HEAD_DIM)
    return parts, (new_k, new_v, s_fin, new_conv, mk4, mv4)


def _sample_layer(x, cache_k, cache_v, state, conv_state, mem_k, mem_v, lw):
    b, l, d = x.shape
    x2 = x.reshape(b, d)
    tm = TOK_TILE
    qs, ks, vs, qkvg, z, ab, qm, gate = _inproj(x2, lw["ln1"], lw["wp"], lw["conv_w"], tm)
    o_swa, new_k, new_v = _swa_decode(qs, ks, vs, cache_k, cache_v, lw["qn_s"], lw["kn_s"], lw["sinks"])
    o_g, new_s = _gdn_decode(qkvg, z, ab, state, conv_state, lw["conv_w"], lw["alog"], lw["dtb"], lw["gnorm"])
    o_m = _memattn_decode(qm, mem_k, mem_v, lw["qn_m"])
    new_conv = jnp.concatenate([conv_state[:, 1:], qkvg[:, None, :]], axis=1)
    shp = (b, WINDOW, SWA_KV_HEADS, SWA_HEAD_DIM)
    return (o_swa, o_g, o_m, gate, x2), (new_k.reshape(shp), new_v.reshape(shp), new_s, new_conv)


def _layer(lw, xp, mem, mem_norm, w_mem_kv, k_norm_mem, xs, cache_k, cache_v, state, conv_state, mem_k, mem_v):
    p_parts, p_state = _prompt_layer(xp, mem, lw, mem_norm, w_mem_kv, k_norm_mem)
    s_parts, s_state = _sample_layer(xs, cache_k, cache_v, state, conv_state, mem_k, mem_v, lw)
    yp, ys = _finish(lw, p_parts, s_parts)
    return yp.reshape(xp.shape), ys.reshape(xs.shape), p_state, s_state


def kernel(x_prompt, x_sample, cache_swa_k, cache_swa_v, state_gdn, state_conv, cache_mem_k, cache_mem_v, mem_prompt, ln1_gain, w_in, q_norm_swa, k_norm_swa, swa_sinks, conv_w, a_log, dt_bias, gdn_norm, q_norm_mem, k_norm_mem, mem_norm, w_mem_kv, w_branch, w_out, ln2_gain, w_router, b_router, w_mlp1, b_mlp1, w_mlp2, b_mlp2):
    xp, xs = x_prompt, x_sample
    outs = [[] for _ in range(10)]
    for l in range(ln1_gain.shape[0]):
        lw = _layer_weights(ln1_gain[l], w_in[l], q_norm_swa[l], k_norm_swa[l], swa_sinks[l], conv_w[l], a_log[l],
                            dt_bias[l], gdn_norm[l], q_norm_mem[l], w_branch[l], w_out[l], ln2_gain[l], w_router[l],
                            b_router[l], w_mlp1, b_mlp1, w_mlp2, b_mlp2, l)
        xp, xs, (pk, pv, ps, pc, mk, mv), (sk, sv, ss, sc) = _layer(
            lw, xp, mem_prompt, mem_norm[l], w_mem_kv[l], k_norm_mem[l],
            xs, cache_swa_k[l], cache_swa_v[l], state_gdn[l], state_conv[l], cache_mem_k[l], cache_mem_v[l])
        for acc, v in zip(outs, (pk, pv, sk, sv, ps, ss, pc, sc, mk, mv)):
            acc.append(v)
    return (xp, xs) + tuple(jnp.stack(o) for o in outs)
```

```python
import functools

import numpy as np
import jax
import jax.numpy as jnp
from jax import lax
from jax.experimental import pallas as pl
from jax.experimental.pallas import tpu as pltpu

F32 = jnp.float32
BF16 = jnp.bfloat16
I32 = jnp.int32

D_MODEL = 1024
BRANCH_W = 512
SWA_HEADS = 8
SWA_KV_HEADS = 2
SWA_HEAD_DIM = 64
SWA_GROUP = SWA_HEADS // SWA_KV_HEADS
SWA_KV_W = SWA_KV_HEADS * SWA_HEAD_DIM
WINDOW = 128
GDN_HEADS = 4
GDN_HEAD_DIM = 128
GDN_W = GDN_HEADS * GDN_HEAD_DIM
GDN_CONV_W = 3 * GDN_W
GDN_CHUNK = 64
CONV_WIDTH = 4
MEM_LEN = 256
MEM_HEADS = 4
MEM_HEAD_DIM = 128
N_BRANCH = 3
N_EXPERTS = 32
TOP_K = 4
D_FF = 1024
SWIGLU_LIMIT = 7.0
SWIGLU_ALPHA = 1.702
EPS = 1e-6

LANES = 128
MOE_BM = 512
MOE_TOK_TILE = 256
MOE_UNIT = 8
TOK_TILE = 128
MERGE_ROWS = 512
VMEM_LIMIT = 56 * 1024 * 1024

_SEG_QS = (0, 512)
_SEG_KS = (512, 128)
_SEG_VS = (640, 128)
_SEG_QKVG = (768, 1536)
_SEG_Z = (2304, 512)
_SEG_AB = (2816, 128)
_SEG_QM = (2944, 512)
_SEG_GATE = (3456, 3072)
_PACKED_W = 6528
_IN_MAIN = 2816
_IN_AB = 8

_NEG = -1e30
_HI = lax.Precision.HIGHEST


def _cparams(sem, vmem=VMEM_LIMIT):
    return pltpu.CompilerParams(dimension_semantics=sem, vmem_limit_bytes=vmem)


def _bdot(a, b):
    return jnp.dot(a.astype(BF16), b.astype(BF16), preferred_element_type=F32)


def _bdot_nt(a, b):
    return lax.dot_general(a.astype(BF16), b.astype(BF16), (((1,), (1,)), ((), ())),
                           preferred_element_type=F32)


def _hdot_nt(a, b):
    return lax.dot_general(a, b, (((1,), (1,)), ((), ())), preferred_element_type=F32, precision=_HI)


def _rms(x, gain):
    return x * lax.rsqrt(jnp.mean(x * x, axis=-1, keepdims=True) + EPS) * gain


def _l2(x):
    return x * lax.rsqrt(jnp.sum(x * x, axis=-1, keepdims=True) + EPS)


def _sigmoid(x):
    return 0.5 * jnp.tanh(0.5 * x) + 0.5


def _silu(x):
    return x * _sigmoid(x)


def _softplus(x):
    return jnp.maximum(x, 0.0) + jnp.log1p(jnp.exp(-jnp.abs(x)))


_HALO = 8


def _inproj_kernel(seq_steps, x_ref, g_ref, w_ref, cw_ref, qs, ks, vs, qkvg, z, ab, qm, gate, *conv_refs):
    i = pl.program_id(0)
    x = x_ref[...]
    hb = _rms(x, g_ref[...]).astype(BF16)

    def proj(off, width):
        return jnp.dot(hb, w_ref[:, off:off + width], preferred_element_type=F32)

    step = 512

    def plain(ref, c0=0, width=None):
        def finish(r):
            ref[:, c0:c0 + (width or r.shape[1])] = r
        return finish

    def gate_chunk(c0):
        def finish(r):
            gate[:, c0:c0 + step] = _sigmoid(r).astype(BF16)
        return finish

    def conv_part(part):
        tail_ref, carry = conv_refs
        c0 = part * GDN_W
        cols = slice(c0, c0 + GDN_W)

        def finish(r):
            first = (i % seq_steps) == 0
            sub = lax.broadcasted_iota(I32, (_HALO, 1), 0)
            rows = r.shape[0]
            before = jnp.where(first, 0.0, carry[:, cols])
            y = r * cw_ref[CONV_WIDTH - 1:CONV_WIDTH, cols]
            top = r[0:_HALO] * cw_ref[CONV_WIDTH - 1:CONV_WIDTH, cols]
            for jj in range(CONV_WIDTH - 1):
                sh = CONV_WIDTH - 1 - jj
                y = y + pltpu.roll(r, sh, 0) * cw_ref[jj:jj + 1, cols]
                head_rows = jnp.where(sub < sh, pltpu.roll(before, sh, 0), pltpu.roll(r[0:_HALO], sh, 0))
                top = top + head_rows * cw_ref[jj:jj + 1, cols]
            carry[:, cols] = r[rows - _HALO:]
            tail_ref[0, :, cols] = r[rows - _HALO:]
            act = _silu(jnp.concatenate([top, y[_HALO:]], axis=0))
            if part < 2:
                scale = GDN_HEAD_DIM ** -0.5 if part == 0 else 1.0
                for h in range(GDN_HEADS):
                    hs = slice(h * GDN_HEAD_DIM, (h + 1) * GDN_HEAD_DIM)
                    qkvg[:, c0 + h * GDN_HEAD_DIM:c0 + (h + 1) * GDN_HEAD_DIM] = _l2(act[:, hs]) * scale
            else:
                qkvg[:, cols] = act
        return finish

    tasks = [(_SEG_QS[0], _SEG_QS[1], plain(qs)), (_SEG_KS[0], _SEG_KS[1], plain(ks)),
             (_SEG_VS[0], _SEG_VS[1], plain(vs)), (_SEG_AB[0], _SEG_AB[1], plain(ab)),
             (_SEG_QM[0], _SEG_QM[1], plain(qm))]
    tasks += [(_SEG_GATE[0] + c0, step, gate_chunk(c0)) for c0 in range(0, _SEG_GATE[1], step)]
    if seq_steps is None:
        tasks += [(_SEG_Z[0], _SEG_Z[1], plain(z))]
        tasks += [(_SEG_QKVG[0] + c0, step, plain(qkvg, c0, step)) for c0 in range(0, GDN_CONV_W, step)]
    else:
        def silu_z(r):
            z[...] = _silu(r)
        tasks += [(_SEG_QKVG[0] + p * GDN_W, GDN_W, conv_part(p)) for p in range(3)]
        tasks.append((_SEG_Z[0], _SEG_Z[1], silu_z))
    for off, width, finish in tasks:
        finish(proj(off, width))


def _inproj(x2, gain, wp, conv_w, tm, seq_len=None):
    t = x2.shape[0]
    segs = (_SEG_QS, _SEG_KS, _SEG_VS, _SEG_QKVG, _SEG_Z, _SEG_AB, _SEG_QM, _SEG_GATE)
    dtypes = (F32,) * 7 + (BF16,)
    out_shape = [jax.ShapeDtypeStruct((t, w), dt) for (_, w), dt in zip(segs, dtypes)]
    out_specs = [pl.BlockSpec((tm, w), lambda i: (i, 0)) for _, w in segs]
    scratch = []
    seq_steps = None
    if seq_len is not None:
        seq_steps = seq_len // tm
        out_shape.append(jax.ShapeDtypeStruct((t // seq_len, _HALO, GDN_CONV_W), F32))
        out_specs.append(pl.BlockSpec((1, _HALO, GDN_CONV_W), lambda i: (i // seq_steps, 0, 0)))
        scratch.append(pltpu.VMEM((_HALO, GDN_CONV_W), F32))
    return pl.pallas_call(
        functools.partial(_inproj_kernel, seq_steps),
        out_shape=out_shape,
        grid=(t // tm,),
        in_specs=[pl.BlockSpec((tm, D_MODEL), lambda i: (i, 0)),
                  pl.BlockSpec((1, D_MODEL), lambda i: (0, 0)),
                  pl.BlockSpec((D_MODEL, _PACKED_W), lambda i: (0, 0), pipeline_mode=pl.Buffered(1)),
                  pl.BlockSpec((CONV_WIDTH, GDN_CONV_W), lambda i: (0, 0))],
        out_specs=out_specs,
        scratch_shapes=scratch,
        compiler_params=_cparams(("arbitrary",)),
        name="inproj",
    )(x2, gain, wp, conv_w)


SWA_STEP_BLOCKS = 8


def _alibi_slopes(n):
    return [float(2.0 ** (-8.0 * (i + 1) / n)) for i in range(n)]


def _swa_prompt_kernel(q_ref, kc_ref, vc_ref, qg_ref, kg_ref, sink_ref, o_ref, kwin_ref, bias, kprev, vprev):
    n = pl.program_id(1)
    half = SWA_HEAD_DIM
    nblk = q_ref.shape[1] // WINDOW
    work = [(j, h) for j in range(nblk) for h in range(SWA_HEADS)]

    @pl.when(n == 0)
    def _():
        kprev[...] = jnp.zeros_like(kprev)
        vprev[...] = jnp.zeros_like(vprev)

    @pl.when((pl.program_id(0) == 0) & (n == 0))
    def _():
        row = lax.broadcasted_iota(I32, (WINDOW, 2 * WINDOW), 0)
        col = lax.broadcasted_iota(I32, (WINDOW, 2 * WINDOW), 1)
        dist = row + WINDOW - col
        window = (dist >= 0) & (dist <= WINDOW)
        distf = dist.astype(F32)
        for h, slope in enumerate(_alibi_slopes(SWA_HEADS)):
            bias[0, h] = jnp.where(window & (col >= WINDOW), -slope * distf, _NEG)
            bias[1, h] = jnp.where(window, -slope * distf, _NEG)

    first_table = jnp.where(n == 0, 0, 1)

    low = lax.broadcasted_iota(I32, (1, 2 * half), 1) < half

    def pair_rms(x, gain):
        sq = x * x
        s_lo = jnp.sum(jnp.where(low, sq, 0.0), axis=-1, keepdims=True)
        s_hi = jnp.sum(jnp.where(low, 0.0, sq), axis=-1, keepdims=True)
        return x * lax.rsqrt(jnp.where(low, s_lo, s_hi) * (1.0 / half) + EPS) * gain

    kcn = pair_rms(kc_ref[0], kg_ref[...])
    kwin_ref[0] = kcn[(nblk - 1) * WINDOW:]
    kall = jnp.concatenate([kprev[...], kcn], axis=0)
    vall = jnp.concatenate([vprev[...], vc_ref[0]], axis=0)
    kprev[...] = kcn[(nblk - 1) * WINDOW:]
    vprev[...] = vc_ref[0, (nblk - 1) * WINDOW:, :]
    ones = jnp.ones(((nblk + 1) * WINDOW, 2 * half), BF16)

    def placed(x, aug):
        sw = pltpu.roll(x, half, 1)
        out = {(0, 0): jnp.where(low, x, 0.0), (1, 1): jnp.where(low, 0.0, x),
               (0, 1): jnp.where(low, 0.0, sw), (1, 0): jnp.where(low, sw, 0.0)}
        out = {key: val.astype(BF16) for key, val in out.items()}
        return {key: jnp.concatenate([val, ones], axis=1) for key, val in out.items()} if aug else out

    kvar = placed(kall, False)
    vvar = placed(vall, True)
    qn = [pair_rms(q_ref[0, :, t * 2 * half:(t + 1) * 2 * half], qg_ref[...]).astype(BF16)
          for t in range(SWA_HEADS // 2)]
    where_of = lambda h: (h // SWA_GROUP, h % 2)
    qrows = lambda j: slice(j * WINDOW, (j + 1) * WINDOW)
    krows = lambda j: slice(j * WINDOW, (j + 2) * WINDOW)
    nt = (((1,), (1,)), ((), ()))
    s = {(j, h): lax.dot_general(qn[h // 2][qrows(j)], kvar[where_of(h)][krows(j)], nt, preferred_element_type=F32)
         + bias[first_table if j == 0 else 1, h] for j, h in work}
    sink = [sink_ref[0:1, h:h + 1] for h in range(SWA_HEADS)]
    m = {(j, h): jnp.maximum(jnp.max(s[j, h], axis=-1, keepdims=True), sink[h]) for j, h in work}
    p = {(j, h): jnp.exp(s[j, h] - m[j, h]).astype(BF16) for j, h in work}
    res = {(j, h): jnp.dot(p[j, h], vvar[where_of(h)][krows(j)], preferred_element_type=F32)
           for j, h in work}
    inv = {(j, h): 1.0 / (res[j, h][:, 2 * half:2 * half + 1] + jnp.exp(sink[h] - m[j, h])) for j, h in work}
    for j in range(nblk):
        for t in range(SWA_HEADS // 2):
            h0, h1 = 2 * t, 2 * t + 1
            num = res[j, h0][:, :2 * half] + res[j, h1][:, :2 * half]
            o_ref[0, qrows(j), t * 2 * half:(t + 1) * 2 * half] = num * jnp.where(low, inv[j, h0], inv[j, h1])


def _swa_prompt(qs, ks, vs, qn, kn, sinks):
    b, l, _ = qs.shape
    rows = SWA_STEP_BLOCKS * WINDOW if l % (SWA_STEP_BLOCKS * WINDOW) == 0 else WINDOW
    cur = lambda i, j: (i, j, 0)
    const2 = lambda i, j: (0, 0)
    q_gain = jnp.tile(qn, (1, 2)) * (SWA_HEAD_DIM ** -0.5)
    k_gain = jnp.tile(kn, (1, 2))
    return pl.pallas_call(
        _swa_prompt_kernel,
        out_shape=[jax.ShapeDtypeStruct((b, l, BRANCH_W), F32),
                   jax.ShapeDtypeStruct((b, WINDOW, SWA_KV_W), F32)],
        grid=(b, l // rows),
        in_specs=[pl.BlockSpec((1, rows, BRANCH_W), cur),
                  pl.BlockSpec((1, rows, SWA_KV_W), cur),
                  pl.BlockSpec((1, rows, SWA_KV_W), cur),
                  pl.BlockSpec((1, 2 * SWA_HEAD_DIM), const2),
                  pl.BlockSpec((1, 2 * SWA_HEAD_DIM), const2),
                  pl.BlockSpec((1, SWA_HEADS), const2)],
        out_specs=[pl.BlockSpec((1, rows, BRANCH_W), cur),
                   pl.BlockSpec((1, WINDOW, SWA_KV_W), lambda i, j: (i, 0, 0))],
        scratch_shapes=[pltpu.VMEM((2, SWA_HEADS, WINDOW, 2 * WINDOW), F32),
                        pltpu.VMEM((WINDOW, SWA_KV_W), F32),
                        pltpu.VMEM((WINDOW, SWA_KV_W), F32)],
        compiler_params=_cparams(("arbitrary", "arbitrary")),
        name="swa_prompt",
    )(qs, ks, vs, q_gain, k_gain, sinks)


def _swa_decode_kernel(bs, q_ref, k3_ref, kf_ref, v3_ref, vf_ref, ck_ref, cv_ref, qn_ref, kn_ref, kn2_ref,
                       sink_ref, slope_ref, o_ref, ok_ref, ov_ref):
    scale = SWA_HEAD_DIM ** -0.5
    lane = lax.broadcasted_iota(I32, (1, SWA_KV_W), 1)
    rowi = lax.broadcasted_iota(I32, (WINDOW, SWA_KV_W), 0)
    keyd = (WINDOW - lax.broadcasted_iota(I32, (1, WINDOW), 1)).astype(F32)
    last = rowi == WINDOW - 1
    for b in range(bs):
        kf = kf_ref[b]
        sq = kf * kf
        ms0 = jnp.sum(jnp.where(lane < SWA_HEAD_DIM, sq, 0.0), axis=-1, keepdims=True) / SWA_HEAD_DIM
        ms1 = jnp.sum(jnp.where(lane >= SWA_HEAD_DIM, sq, 0.0), axis=-1, keepdims=True) / SWA_HEAD_DIM
        knf = kf * lax.rsqrt(jnp.where(lane < SWA_HEAD_DIM, ms0, ms1) + EPS) * kn2_ref[...]
        ok_ref[b] = jnp.where(last, knf, pltpu.roll(ck_ref[b], WINDOW - 1, 0))
        ov_ref[b] = jnp.where(last, vf_ref[b], pltpu.roll(cv_ref[b], WINDOW - 1, 0))
    work = [(b, g) for b in range(bs) for g in range(SWA_KV_HEADS)]
    lanes = lambda g: slice(g * SWA_HEAD_DIM, (g + 1) * SWA_HEAD_DIM)
    heads = lambda g: slice(g * SWA_GROUP, (g + 1) * SWA_GROUP)
    qn = [_rms(q_ref[b], qn_ref[...]) for b in range(bs)]
    kn3 = [_rms(k3_ref[b], kn_ref[...]) for b in range(bs)]
    qg = {(b, g): qn[b][heads(g)] for b, g in work}
    s = {(b, g): _bdot_nt(qg[b, g], ck_ref[b, :, lanes(g)]) * scale - slope_ref[heads(g)] * keyd for b, g in work}
    s_new = {(b, g): jnp.sum(qg[b, g] * kn3[b][g:g + 1], axis=-1, keepdims=True) * scale for b, g in work}
    m = {(b, g): jnp.maximum(jnp.maximum(jnp.max(s[b, g], axis=-1, keepdims=True), s_new[b, g]),
                             sink_ref[heads(g)]) for b, g in work}
    p = {w: jnp.exp(s[w] - m[w]) for w in work}
    p_new = {w: jnp.exp(s_new[w] - m[w]) for w in work}
    denom = {(b, g): jnp.sum(p[b, g], axis=-1, keepdims=True) + p_new[b, g] + jnp.exp(sink_ref[heads(g)] - m[b, g])
             for b, g in work}
    pv = {(b, g): _bdot(p[b, g], cv_ref[b, :, lanes(g)]) for b, g in work}
    for b, g in work:
        o_ref[b, heads(g), :] = (pv[b, g] + p_new[b, g] * v3_ref[b, g:g + 1, :]) / denom[b, g]


def _swa_decode(qs, ks, vs, cache_k, cache_v, qn, kn, sinks, bs=8):
    b = qs.shape[0]
    q3 = qs.reshape(b, SWA_HEADS, SWA_HEAD_DIM)
    k3 = ks.reshape(b, SWA_KV_HEADS, SWA_HEAD_DIM)
    kf = ks.reshape(b, 1, SWA_KV_W)
    v3 = vs.reshape(b, SWA_KV_HEADS, SWA_HEAD_DIM)
    vf = vs.reshape(b, 1, SWA_KV_W)
    ck = cache_k.reshape(b, WINDOW, SWA_KV_W)
    cv = cache_v.reshape(b, WINDOW, SWA_KV_W)
    kn2 = jnp.concatenate([kn, kn], axis=-1)
    sink_col = sinks.reshape(SWA_HEADS, 1)
    slope_col = jnp.asarray(np.asarray(_alibi_slopes(SWA_HEADS), np.float32).reshape(SWA_HEADS, 1))
    blk = lambda *shape: pl.BlockSpec((bs,) + shape, lambda i: (i,) + (0,) * len(shape))
    full = lambda *shape: pl.BlockSpec(shape, lambda i: (0,) * len(shape))
    o, ok, ov = pl.pallas_call(
        functools.partial(_swa_decode_kernel, bs),
        out_shape=[jax.ShapeDtypeStruct((b, SWA_HEADS, SWA_HEAD_DIM), F32),
                   jax.ShapeDtypeStruct((b, WINDOW, SWA_KV_W), F32),
                   jax.ShapeDtypeStruct((b, WINDOW, SWA_KV_W), F32)],
        grid=(b // bs,),
        in_specs=[blk(SWA_HEADS, SWA_HEAD_DIM), blk(SWA_KV_HEADS, SWA_HEAD_DIM), blk(1, SWA_KV_W),
                  blk(SWA_KV_HEADS, SWA_HEAD_DIM), blk(1, SWA_KV_W), blk(WINDOW, SWA_KV_W), blk(WINDOW, SWA_KV_W),
                  full(1, SWA_HEAD_DIM), full(1, SWA_HEAD_DIM), full(1, SWA_KV_W),
                  full(SWA_HEADS, 1), full(SWA_HEADS, 1)],
        out_specs=[blk(SWA_HEADS, SWA_HEAD_DIM), blk(WINDOW, SWA_KV_W), blk(WINDOW, SWA_KV_W)],
        compiler_params=_cparams(("parallel",)),
        name="swa_decode",
    )(q3, k3, kf, v3, vf, ck, cv, qn, kn, kn2, sink_col, slope_col)
    return o.reshape(b, BRANCH_W), ok, ov


def _gate_rows(ab, alog_ref, dtb_ref):
    g = -jnp.exp(alog_ref[...]) * _softplus(ab + dtb_ref[...])
    return g, _sigmoid(ab)


GDN_PREP_CHUNKS = 4
GDN_PREP_HEADS = 4
GDN_SCAN_BATCH = 16


def _gdn_prep_kernel(ys, ab_ref, alog_ref, dtb_ref, u_ref, w_ref, qg_ref, kd_ref, in_ref, eg_ref):
    C = GDN_CHUNK
    d = GDN_HEAD_DIM
    n = GDN_PREP_CHUNKS * C
    g_all, beta_all = _gate_rows(ab_ref[0], alog_ref, dtb_ref)
    r = lax.broadcasted_iota(I32, (n, n), 0)
    cc = lax.broadcasted_iota(I32, (n, n), 1)
    same = (r // C) == (cc // C)
    incl = same & (r >= cc)
    strict = same & (r > cc)
    upto = same & (r <= cc)
    eye = r == cc
    for h0 in range(0, GDN_HEADS, GDN_PREP_HEADS):
        heads = range(h0, h0 + GDN_PREP_HEADS)
        q = {h: ys[0, :, h * d:(h + 1) * d] for h in heads}
        k = {h: ys[0, :, GDN_W + h * d:GDN_W + (h + 1) * d] for h in heads}
        beta = {h: beta_all[:, GDN_HEADS + h:GDN_HEADS + h + 1] for h in heads}
        gc_row = {h: jnp.sum(jnp.where(upto, g_all[:, h:h + 1], 0.0), axis=0, keepdims=True) for h in heads}
        gc_col = {h: jnp.sum(jnp.where(eye, gc_row[h], 0.0), axis=1, keepdims=True) for h in heads}
        decay = {h: jnp.where(incl, jnp.exp(jnp.where(incl, gc_col[h] - gc_row[h], 0.0)), 0.0) for h in heads}
        kb = {h: k[h] * beta[h] for h in heads}
        a = {h: jnp.where(strict, _bdot_nt(kb[h], k[h]) * decay[h], 0.0) for h in heads}
        intra = {h: jnp.where(incl, _bdot_nt(q[h], k[h]) * decay[h], 0.0) for h in heads}
        base = 8
        blk = lambda s: (r // s) == (cc // s)
        bp = {h: -jnp.where(blk(base), a[h], 0.0) for h in heads}
        p = {h: eye.astype(F32) + bp[h] for h in heads}
        span = 2
        while span < base:
            bp = {h: _bdot(bp[h], bp[h]) for h in heads}
            p = {h: p[h] + _bdot(p[h], bp[h]) for h in heads}
            span *= 2
        size = base
        while size < C:
            off = {h: jnp.where(blk(2 * size) & jnp.logical_not(blk(size)), a[h], 0.0) for h in heads}
            p = {h: p[h] - _bdot(p[h], _bdot(off[h], p[h])) for h in heads}
            size *= 2
        uw = {h: _bdot(p[h], jnp.concatenate([ys[0, :, 2 * GDN_W + h * d:2 * GDN_W + (h + 1) * d] * beta[h],
                                              kb[h] * jnp.exp(gc_col[h])], axis=1)) for h in heads}
        for h in heads:
            hs = slice(h * d, (h + 1) * d)
            u_ref[0, :, hs] = uw[h][:, :d]
            w_ref[0, :, hs] = uw[h][:, d:].astype(BF16)
            qg_ref[0, :, hs] = (q[h] * jnp.exp(gc_col[h])).astype(BF16)
            for ci in range(GDN_PREP_CHUNKS):
                r0 = ci * C
                g_last = gc_row[h][:, r0 + C - 1:r0 + C]
                kd_ref[0, r0:r0 + C, hs] = (k[h][r0:r0 + C] * jnp.exp(g_last - gc_col[h][r0:r0 + C])).astype(BF16)
                in_ref[0, r0:r0 + C, h * C:(h + 1) * C] = intra[h][r0:r0 + C, r0:r0 + C].astype(BF16)
                eg_ref[0, ci, h:h + 1, :] = jnp.broadcast_to(jnp.exp(g_last), (1, LANES))


def _gdn_scan_kernel(u_ref, w_ref, qg_ref, kd_ref, in_ref, eg_ref, z_ref, gn_ref, o_ref, s_ref, st):
    c = pl.program_id(1)
    C = GDN_CHUNK
    d = GDN_HEAD_DIM

    @pl.when(c == 0)
    def _():
        st[...] = jnp.zeros_like(st)

    chains = [(bi, h) for bi in range(u_ref.shape[0]) for h in range(GDN_HEADS)]
    hsl = lambda h: slice(h * d, (h + 1) * d)
    s_f = [st[bi, h] for bi, h in chains]
    s_b = [s.astype(BF16) for s in s_f]
    ws = [jnp.dot(w_ref[bi, :, hsl(h)], s_b[i], preferred_element_type=F32) for i, (bi, h) in enumerate(chains)]
    qs = [jnp.dot(qg_ref[bi, :, hsl(h)], s_b[i], preferred_element_type=F32) for i, (bi, h) in enumerate(chains)]
    vb = [(u_ref[bi, :, hsl(h)] - ws[i]).astype(BF16) for i, (bi, h) in enumerate(chains)]
    iv = [jnp.dot(in_ref[bi, :, h * C:(h + 1) * C], vb[i], preferred_element_type=F32)
          for i, (bi, h) in enumerate(chains)]
    kv = [lax.dot_general(kd_ref[bi, :, hsl(h)], vb[i], (((0,), (0,)), ((), ())), preferred_element_type=F32)
          for i, (bi, h) in enumerate(chains)]
    for i, (bi, h) in enumerate(chains):
        st[bi, h] = s_f[i] * eg_ref[bi, 0, h:h + 1, :] + kv[i]
        o_ref[bi, :, hsl(h)] = _rms(qs[i] + iv[i], gn_ref[...]) * z_ref[bi, :, hsl(h)]
    s_ref[...] = st[...]


def _gdn_prompt(qkvg, z, ab, alog, dtb, gnorm):
    b, l, _ = qkvg.shape
    C = GDN_CHUNK
    nc = l // C
    rows = GDN_PREP_CHUNKS * C
    cur = lambda i, j: (i, j, 0)
    const2 = lambda i, j: (0, 0)
    u, w, qg, kd, intra, eg = pl.pallas_call(
        _gdn_prep_kernel,
        out_shape=[jax.ShapeDtypeStruct((b, l, GDN_W), F32),
                   jax.ShapeDtypeStruct((b, l, GDN_W), BF16),
                   jax.ShapeDtypeStruct((b, l, GDN_W), BF16),
                   jax.ShapeDtypeStruct((b, l, GDN_W), BF16),
                   jax.ShapeDtypeStruct((b, l, GDN_HEADS * C), BF16),
                   jax.ShapeDtypeStruct((b, nc, GDN_HEADS, LANES), F32)],
        grid=(b, l // rows),
        in_specs=[pl.BlockSpec((1, rows, GDN_CONV_W), cur),
                  pl.BlockSpec((1, rows, LANES), cur),
                  pl.BlockSpec((1, LANES), const2),
                  pl.BlockSpec((1, LANES), const2)],
        out_specs=[pl.BlockSpec((1, rows, GDN_W), cur),
                   pl.BlockSpec((1, rows, GDN_W), cur),
                   pl.BlockSpec((1, rows, GDN_W), cur),
                   pl.BlockSpec((1, rows, GDN_W), cur),
                   pl.BlockSpec((1, rows, GDN_HEADS * C), cur),
                   pl.BlockSpec((1, GDN_PREP_CHUNKS, GDN_HEADS, LANES), lambda i, j: (i, j, 0, 0))],
        compiler_params=_cparams(("parallel", "parallel")),
        name="gdn_prep",
    )(qkvg, ab, alog, dtb)
    bb = GDN_SCAN_BATCH if b % GDN_SCAN_BATCH == 0 else 1
    seq = lambda wd: pl.BlockSpec((bb, C, wd), cur)
    return pl.pallas_call(
        _gdn_scan_kernel,
        out_shape=[jax.ShapeDtypeStruct((b, l, GDN_W), F32),
                   jax.ShapeDtypeStruct((b, GDN_HEADS, GDN_HEAD_DIM, GDN_HEAD_DIM), F32)],
        grid=(b // bb, nc),
        in_specs=[seq(GDN_W), seq(GDN_W), seq(GDN_W), seq(GDN_W), seq(GDN_HEADS * C),
                  pl.BlockSpec((bb, 1, GDN_HEADS, LANES), lambda i, j: (i, j, 0, 0)),
                  seq(GDN_W),
                  pl.BlockSpec((1, GDN_HEAD_DIM), const2)],
        out_specs=[seq(GDN_W),
                   pl.BlockSpec((bb, GDN_HEADS, GDN_HEAD_DIM, GDN_HEAD_DIM), lambda i, j: (i, 0, 0, 0))],
        scratch_shapes=[pltpu.VMEM((bb, GDN_HEADS, GDN_HEAD_DIM, GDN_HEAD_DIM), F32)],
        compiler_params=_cparams(("parallel", "arbitrary")),
        name="gdn_scan",
    )(u, w, qg, kd, intra, eg, z, gnorm)


def _gdn_decode_kernel(bs, x_ref, cs_ref, z_ref, ab_ref, s_ref, cw_ref, alog_ref, dtb_ref, gn_ref, o_ref, so_ref):
    x = x_ref[...]
    y = x * cw_ref[CONV_WIDTH - 1:CONV_WIDTH, :]
    for j in range(CONV_WIDTH - 1):
        y = y + cs_ref[:, j, :] * cw_ref[j:j + 1, :]
    y = _silu(y)
    g_all, beta_all = _gate_rows(ab_ref[...], alog_ref, dtb_ref)
    eg_all = jnp.exp(g_all)
    z = z_ref[...]
    d = GDN_HEAD_DIM
    eye = (lax.broadcasted_iota(I32, (d, d), 0) == lax.broadcasted_iota(I32, (d, d), 1)).astype(F32)
    for h in range(GDN_HEADS):
        hs = slice(h * d, (h + 1) * d)
        qh = _l2(y[:, h * d:(h + 1) * d]) * (d ** -0.5)
        kh = _l2(y[:, GDN_W + h * d:GDN_W + (h + 1) * d])
        vh = y[:, 2 * GDN_W + h * d:2 * GDN_W + (h + 1) * d]
        kt = _hdot_nt(eye, kh)
        qt = _hdot_nt(eye, qh)
        rng = range(bs)
        eg = [eg_all[b:b + 1, h:h + 1] for b in rng]
        beta = [beta_all[b:b + 1, GDN_HEADS + h:GDN_HEADS + h + 1] for b in rng]
        sk = [jnp.sum(s_ref[b, h] * kt[:, b:b + 1], axis=0, keepdims=True) for b in rng]
        v_new = [beta[b] * (vh[b:b + 1] - eg[b] * sk[b]) for b in rng]
        s_new = [eg[b] * s_ref[b, h] + kt[:, b:b + 1] * v_new[b] for b in rng]
        for b in rng:
            so_ref[b, h] = s_new[b]
        o = jnp.concatenate([jnp.sum(s_new[b] * qt[:, b:b + 1], axis=0, keepdims=True) for b in rng], axis=0)
        o_ref[:, hs] = _rms(o, gn_ref[...]) * _silu(z[:, hs])


def _gdn_decode(qkvg, z, ab, state, conv_state, conv_w, alog, dtb, gnorm, bs=8):
    b = qkvg.shape[0]
    row = lambda w: pl.BlockSpec((bs, w), lambda i: (i, 0))
    full = lambda *shape: pl.BlockSpec(shape, lambda i: (0,) * len(shape))
    sspec = pl.BlockSpec((bs, GDN_HEADS, GDN_HEAD_DIM, GDN_HEAD_DIM), lambda i: (i, 0, 0, 0))
    return pl.pallas_call(
        functools.partial(_gdn_decode_kernel, bs),
        out_shape=[jax.ShapeDtypeStruct((b, GDN_W), F32),
                   jax.ShapeDtypeStruct(state.shape, F32)],
        grid=(b // bs,),
        in_specs=[row(GDN_CONV_W),
                  pl.BlockSpec((bs, CONV_WIDTH - 1, GDN_CONV_W), lambda i: (i, 0, 0)),
                  row(GDN_W), row(LANES), sspec,
                  full(CONV_WIDTH, GDN_CONV_W), full(1, LANES), full(1, LANES), full(1, GDN_HEAD_DIM)],
        out_specs=[row(GDN_W), sspec],
        compiler_params=_cparams(("parallel",)),
        name="gdn_decode",
    )(qkvg, conv_state, z, ab, state, conv_w, alog, dtb, gnorm)


def _memkv_kernel(x_ref, g_ref, w_ref, kn_ref, k_ref, v_ref):
    hb = _rms(x_ref[...], g_ref[...]).astype(BF16)
    for hd in range(MEM_HEADS):
        hs = slice(hd * MEM_HEAD_DIM, (hd + 1) * MEM_HEAD_DIM)
        k_ref[:, hs] = _rms(jnp.dot(hb, w_ref[:, hs], preferred_element_type=F32), kn_ref[...])
    v_ref[...] = jnp.dot(hb, w_ref[:, BRANCH_W:], preferred_element_type=F32)


def _memkv(mem2, gain, w_kv, kn, tm=512):
    t = mem2.shape[0]
    return pl.pallas_call(
        _memkv_kernel,
        out_shape=[jax.ShapeDtypeStruct((t, BRANCH_W), F32)] * 2,
        grid=(t // tm,),
        in_specs=[pl.BlockSpec((tm, D_MODEL), lambda i: (i, 0)),
                  pl.BlockSpec((1, D_MODEL), lambda i: (0, 0)),
                  pl.BlockSpec((D_MODEL, 2 * BRANCH_W), lambda i: (0, 0)),
                  pl.BlockSpec((1, MEM_HEAD_DIM), lambda i: (0, 0))],
        out_specs=[pl.BlockSpec((tm, BRANCH_W), lambda i: (i, 0))] * 2,
        compiler_params=_cparams(("parallel",)),
        name="memkv",
    )(mem2, gain, w_kv, kn)


def _memattn_prompt_kernel(q_ref, k_ref, v_ref, qn_ref, o_ref):
    q = q_ref[0]
    k = k_ref[0]
    v = v_ref[0]
    scale = MEM_HEAD_DIM ** -0.5
    for hd in range(MEM_HEADS):
        hs = slice(hd * MEM_HEAD_DIM, (hd + 1) * MEM_HEAD_DIM)
        s = _bdot_nt(_rms(q[:, hs], qn_ref[...]), k[:, hs]) * scale
        p = jnp.exp(s - jnp.max(s, axis=-1, keepdims=True))
        o_ref[0, :, hs] = _bdot(p, v[:, hs]) / jnp.sum(p, axis=-1, keepdims=True)


def _memattn_prompt(qm, mk, mv, qn):
    b, l, _ = qm.shape
    tq = 512 if l % 512 == 0 else WINDOW
    return pl.pallas_call(
        _memattn_prompt_kernel,
        out_shape=jax.ShapeDtypeStruct((b, l, BRANCH_W), F32),
        grid=(b, l // tq),
        in_specs=[pl.BlockSpec((1, tq, BRANCH_W), lambda i, j: (i, j, 0)),
                  pl.BlockSpec((1, MEM_LEN, BRANCH_W), lambda i, j: (i, 0, 0)),
                  pl.BlockSpec((1, MEM_LEN, BRANCH_W), lambda i, j: (i, 0, 0)),
                  pl.BlockSpec((1, MEM_HEAD_DIM), lambda i, j: (0, 0))],
        out_specs=pl.BlockSpec((1, tq, BRANCH_W), lambda i, j: (i, j, 0)),
        compiler_params=_cparams(("parallel", "parallel")),
        name="memattn_prompt",
    )(qm, mk, mv, qn)


def _memattn_decode_kernel(bs, q_ref, k_ref, v_ref, qn_ref, o_ref):
    scale = MEM_HEAD_DIM ** -0.5
    for b in range(bs):
        qn = _rms(q_ref[b], qn_ref[...])
        s = jnp.sum(k_ref[b] * qn, axis=-1, keepdims=True) * scale
        p = jnp.exp(s - jnp.max(s, axis=0, keepdims=True))
        den = jnp.sum(p, axis=0)
        o_ref[b] = jnp.sum(p * v_ref[b], axis=0) / den


def _memattn_decode(qm, ck, cv, qn, bs=8):
    b = qm.shape[0]
    q3 = pl.BlockSpec((bs, MEM_HEADS, MEM_HEAD_DIM), lambda i: (i, 0, 0))
    kv = pl.BlockSpec((bs, MEM_LEN, MEM_HEADS, MEM_HEAD_DIM), lambda i: (i, 0, 0, 0))
    o = pl.pallas_call(
        functools.partial(_memattn_decode_kernel, bs),
        out_shape=jax.ShapeDtypeStruct((b, MEM_HEADS, MEM_HEAD_DIM), F32),
        grid=(b // bs,),
        in_specs=[q3, kv, kv, pl.BlockSpec((1, MEM_HEAD_DIM), lambda i: (0, 0))],
        out_specs=q3,
        compiler_params=_cparams(("parallel",)),
        name="memattn_decode",
    )(qm.reshape(b, MEM_HEADS, MEM_HEAD_DIM), ck, cv, qn)
    return o.reshape(b, BRANCH_W)


def _merge_kernel(oa_ref, ob_ref, oc_ref, gate_ref, x_ref, wb_ref, wo_ref, g2_ref, wr_ref, br_ref,
                  x1_ref, h2_ref, lg_ref):
    acc = None
    for i, o_ref in enumerate((oa_ref, ob_ref, oc_ref)):
        mixed = jnp.dot(o_ref[...].astype(BF16), wb_ref[i], preferred_element_type=F32)
        term = gate_ref[:, i * D_MODEL:(i + 1) * D_MODEL].astype(F32) * mixed
        acc = term if acc is None else acc + term
    x1 = x_ref[...] + jnp.dot(acc.astype(BF16), wo_ref[...], preferred_element_type=F32)
    x1_ref[...] = x1
    h2 = _rms(x1, g2_ref[...])
    h2_ref[...] = h2.astype(BF16)
    h_hi = h2.astype(BF16)
    h_lo = (h2 - h_hi.astype(F32)).astype(BF16)
    nt = (((1,), (1,)), ((), ()))
    a = lax.dot_general(wr_ref[...], h_hi, nt, preferred_element_type=F32)
    b = lax.dot_general(wr_ref[0:N_EXPERTS, :], h_lo, nt, preferred_element_type=F32)
    lg_ref[...] = a[0:N_EXPERTS] + a[N_EXPERTS:] + b + br_ref[...]


def _merge_into_kernel(oa_ref, ob_ref, oc_ref, gate_ref, x_ref, wb_ref, wo_ref, g2_ref, wr_ref, br_ref,
                       x1_in, h2_in, lg_in, x1_ref, h2_ref, lg_ref):
    del x1_in, h2_in, lg_in
    _merge_kernel(oa_ref, ob_ref, oc_ref, gate_ref, x_ref, wb_ref, wo_ref, g2_ref, wr_ref, br_ref,
                  x1_ref, h2_ref, lg_ref)


def _merge(oa, ob, oc, gate, x2, wb, wo, g2, wr_t, br_col, tm, t_out=None, into=None):
    t = x2.shape[0]
    t_out = t_out or t
    blk0 = 0 if into is None else (into[0].shape[0] - t) // tm
    row = lambda w: pl.BlockSpec((tm, w), lambda i: (i, 0))
    orow = lambda w: pl.BlockSpec((tm, w), lambda i: (i + blk0, 0))
    full = lambda *shape: pl.BlockSpec(shape, lambda i: (0,) * len(shape))
    in_specs = [row(BRANCH_W), row(BRANCH_W), row(BRANCH_W), row(N_BRANCH * D_MODEL), row(D_MODEL),
                full(N_BRANCH, BRANCH_W, D_MODEL), full(D_MODEL, D_MODEL), full(1, D_MODEL),
                full(2 * N_EXPERTS, D_MODEL), full(N_EXPERTS, 1)]
    args = [oa, ob, oc, gate, x2, wb, wo, g2, wr_t, br_col]
    aliases = {}
    if into is not None:
        t_out = into[0].shape[0]
        in_specs += [pl.BlockSpec(memory_space=pl.ANY)] * 3
        aliases = {len(args) + k: k for k in range(3)}
        args += list(into)
    return pl.pallas_call(
        _merge_kernel if into is None else _merge_into_kernel,
        out_shape=[jax.ShapeDtypeStruct((t_out, D_MODEL), F32),
                   jax.ShapeDtypeStruct((t_out, D_MODEL), BF16),
                   jax.ShapeDtypeStruct((N_EXPERTS, t_out), F32)],
        grid=(t // tm,),
        in_specs=in_specs,
        out_specs=[orow(D_MODEL), orow(D_MODEL), pl.BlockSpec((N_EXPERTS, tm), lambda i: (0, i + blk0))],
        input_output_aliases=aliases,
        compiler_params=_cparams(("parallel",)),
        name="merge",
    )(*args)


def _col_to_row(col):
    n = col.shape[0]
    r = lax.broadcasted_iota(I32, (n, n), 0)
    c = lax.broadcasted_iota(I32, (n, n), 1)
    return jnp.sum(jnp.where(r == c, col, 0.0), axis=0, keepdims=True)


def _row_to_col(row):
    n = row.shape[1]
    r = lax.broadcasted_iota(I32, (n, n), 0)
    c = lax.broadcasted_iota(I32, (n, n), 1)
    return jnp.sum(jnp.where(r == c, row, 0.0), axis=1, keepdims=True)


def _lane_pad(row):
    return jnp.concatenate([row, jnp.zeros((1, LANES - row.shape[1]), row.dtype)], axis=1)


def _route_kernel(lg_ref, pos_ref, gate_ref, meta_ref, cnt_ref, carry):
    i = pl.program_id(0)
    tn = lg_ref.shape[1]

    @pl.when(i == 0)
    def _():
        carry[...] = jnp.zeros_like(carry)

    l = lg_ref[...]
    eio = lax.broadcasted_iota(I32, (N_EXPERTS, tn), 0)
    hot = jnp.zeros((N_EXPERTS, tn), F32)
    vals, idxs = [], []
    for _ in range(TOP_K):
        m = jnp.max(l, axis=0, keepdims=True)
        idx = jnp.min(jnp.where(l == m, eio, N_EXPERTS), axis=0, keepdims=True)
        sel = eio == idx
        vals.append(m)
        idxs.append(idx)
        hot = hot + sel.astype(F32)
        l = jnp.where(sel, -jnp.inf, l)
    ex = [jnp.exp(v - vals[0]) for v in vals]
    tot = ex[0] + ex[1] + ex[2] + ex[3]
    before = (lax.broadcasted_iota(I32, (tn, tn), 0) < lax.broadcasted_iota(I32, (tn, tn), 1)).astype(BF16)
    within = jnp.dot(hot.astype(BF16), before, preferred_element_type=F32)
    cnt_col = jnp.sum(hot, axis=1, keepdims=True)
    room = jnp.floor((cnt_col + (MOE_UNIT - 1)) * (1.0 / MOE_UNIT)) * MOE_UNIT
    r = lax.broadcasted_iota(I32, (N_EXPERTS, N_EXPERTS), 0)
    c = lax.broadcasted_iota(I32, (N_EXPERTS, N_EXPERTS), 1)
    start_col = _row_to_col(jnp.sum(jnp.where(r < c, room, 0.0), axis=0, keepdims=True))
    for k in range(TOP_K):
        gate_ref[k:k + 1, :] = ex[k] / tot
        pos_ref[k:k + 1, :] = jnp.sum(jnp.where(eio == idxs[k], start_col + within, 0.0), axis=0,
                                      keepdims=True).astype(I32)
    meta_ref[0] = _lane_pad(jnp.concatenate([_col_to_row(room), _col_to_row(carry[:, 0:1])], axis=1)).astype(I32)
    carry[...] = carry[...] + room
    cnt_ref[...] = carry[...]


def _route(logits_t, tn):
    t = logits_t.shape[1]
    kt = pl.BlockSpec((TOP_K, tn), lambda i: (0, i))
    return pl.pallas_call(
        _route_kernel,
        out_shape=[jax.ShapeDtypeStruct((TOP_K, t), I32),
                   jax.ShapeDtypeStruct((TOP_K, t), F32),
                   jax.ShapeDtypeStruct((t // tn, 1, LANES), I32),
                   jax.ShapeDtypeStruct((N_EXPERTS, LANES), F32)],
        grid=(t // tn,),
        in_specs=[pl.BlockSpec((N_EXPERTS, tn), lambda i: (0, i))],
        out_specs=[kt, kt, pl.BlockSpec((1, 1, LANES), lambda i: (i, 0, 0)),
                   pl.BlockSpec((N_EXPERTS, LANES), lambda i: (0, 0))],
        scratch_shapes=[pltpu.VMEM((N_EXPERTS, LANES), F32)],
        compiler_params=_cparams(("arbitrary",)),
        name="moe_route",
    )(logits_t)


def _layout_kernel(bm, cnt_ref, be_ref, na_ref, ps_ref, pe_ref, ct_ref):
    nbp = be_ref.shape[1]
    cnt_col = cnt_ref[:, 0:1]
    size = jnp.floor((cnt_col + (bm - 1)) * (1.0 / bm)) * bm
    r = lax.broadcasted_iota(I32, (N_EXPERTS, N_EXPERTS), 0)
    c = lax.broadcasted_iota(I32, (N_EXPERTS, N_EXPERTS), 1)
    ends_row = jnp.sum(jnp.where(r <= c, size, 0.0), axis=0, keepdims=True)
    ends_col = _row_to_col(ends_row)
    nact = ends_row[:, N_EXPERTS - 1:N_EXPERTS] * (1.0 / bm)
    blk = jnp.minimum(lax.broadcasted_iota(I32, (1, nbp), 1).astype(F32), nact - 1.0)
    be = jnp.sum((ends_col <= blk * bm).astype(F32), axis=0, keepdims=True)
    be_ref[...] = jnp.minimum(be, N_EXPERTS - 1.0).astype(I32)
    na_ref[...] = jnp.broadcast_to(nact, na_ref.shape).astype(I32)
    ps_ref[...] = _lane_pad(ends_row - _col_to_row(size)).astype(I32)
    pe_ref[...] = _lane_pad(ends_row).astype(I32)
    ct_ref[...] = _lane_pad(_col_to_row(cnt_col)).astype(I32)


def _layout(cnt, nbp, bm):
    row = jax.ShapeDtypeStruct((1, LANES), I32)
    return pl.pallas_call(
        functools.partial(_layout_kernel, bm),
        out_shape=[jax.ShapeDtypeStruct((1, nbp), I32), row, row, row, row],
        name="moe_layout",
    )(cnt)


def _stage_rows(tn):
    return tn * TOP_K + N_EXPERTS * MOE_UNIT


def _run_copies(meta_ref, ps_ref, make_copy):
    low_bits = 3
    chunk = MOE_UNIT << low_bits
    src = jnp.int32(0)
    total = jnp.int32(0)
    for e in range(N_EXPERTS):
        units = lax.shift_right_logical(meta_ref[0, 0, e], 3)
        dst = ps_ref[e] + meta_ref[0, 0, N_EXPERTS + e]
        chunks = lax.shift_right_logical(units, low_bits)

        def body(c, carry, src=src, dst=dst):
            make_copy(pl.multiple_of(src + c * chunk, MOE_UNIT), pl.multiple_of(dst + c * chunk, MOE_UNIT),
                      chunk).start()
            return carry
        lax.fori_loop(0, chunks, body, 0)
        off = chunks * chunk
        for k in range(low_bits - 1, -1, -1):
            rows = MOE_UNIT << k
            take = lax.shift_right_logical(units, k) & 1

            @pl.when(take == 1)
            def _(src=src, dst=dst, off=off, rows=rows):
                make_copy(pl.multiple_of(src + off, MOE_UNIT), pl.multiple_of(dst + off, MOE_UNIT), rows).start()
            off = off + take * rows
        src = src + units * MOE_UNIT
        total = total + units
    return total


def _wait_copies(units, make_copy, max_rows):
    for k in range((max_rows // MOE_UNIT).bit_length()):
        if (MOE_UNIT << k) > max_rows:
            break

        @pl.when(lax.shift_right_logical(units, k) & 1 == 1)
        def _(k=k):
            make_copy(0, 0, MOE_UNIT << k).wait()


def _wait_units(count, make_copy):
    def body(u, carry):
        make_copy(0, 0).wait()
        return carry
    lax.fori_loop(0, count, body, 0)


def _dispatch_kernel(ps_ref, pe_ref, ct_ref, meta_ref, pos_ref, gate_ref, h_ref, xg_ref, stage, zrows, zblock,
                     started, sem, zsem):
    i = pl.program_id(0)
    n = pl.num_programs(0)
    slot = i % 2
    rows, tn = stage.shape[1], pos_ref.shape[1]

    def copy(s):
        return lambda src, dst, nrows: pltpu.make_async_copy(stage.at[s, pl.ds(src, nrows)],
                                                             xg_ref.at[pl.ds(dst, nrows)], sem.at[s])

    @pl.when(i >= 2)
    def _():
        _wait_copies(started[slot], copy(slot), rows)

    srow = lax.broadcasted_iota(I32, (rows, tn), 0)
    hits = [srow == pos_ref[k:k + 1, :] for k in range(TOP_K)]
    hit = hits[0]
    gate_at = jnp.where(hits[0], gate_ref[0:1, :], 0.0)
    for k in range(1, TOP_K):
        hit = hit | hits[k]
        gate_at = gate_at + jnp.where(hits[k], gate_ref[k:k + 1, :], 0.0)
    onehot = jnp.where(hit, 1.0, 0.0).astype(BF16)
    stage[slot, :, 0:D_MODEL] = jnp.dot(onehot, h_ref[...], preferred_element_type=F32)
    stage[slot, :, D_MODEL:] = jnp.broadcast_to(jnp.sum(gate_at, axis=1, keepdims=True), (rows, LANES))
    started[slot] = _run_copies(meta_ref, ps_ref, copy(slot))

    @pl.when(i == n - 1)
    def _():
        _wait_copies(started[slot], copy(slot), rows)

        @pl.when(n >= 2)
        def _():
            _wait_copies(started[1 - slot], copy(1 - slot), rows)

        zrows[...] = jnp.zeros_like(zrows)
        zero = lambda src, dst: pltpu.make_async_copy(zrows, xg_ref.at[pl.ds(dst, MOE_UNIT)], zsem)
        total = jnp.int32(0)
        for e in range(N_EXPERTS):
            lo = ps_ref[e] + ct_ref[e]
            units = lax.shift_right_logical(pe_ref[e] - lo, 3)

            def body(u, carry, lo=lo):
                zero(0, pl.multiple_of(lo + u * MOE_UNIT, MOE_UNIT)).start()
                return carry
            lax.fori_loop(0, units, body, 0)
            total = total + units
        _wait_units(total, zero)

        zblock[...] = jnp.zeros_like(zblock)
        bm = zblock.shape[0]
        zero_block = lambda blk: pltpu.make_async_copy(zblock, xg_ref.at[pl.ds(pl.multiple_of(blk * bm, bm), bm)],
                                                       zsem)
        first = lax.shift_right_logical(pe_ref[N_EXPERTS - 1], bm.bit_length() - 1)
        last = xg_ref.shape[0] // bm

        def start_block(blk, carry):
            zero_block(blk).start()
            return carry
        lax.fori_loop(first, last, start_block, 0)

        def wait_block(blk, carry):
            zero_block(0).wait()
            return carry
        lax.fori_loop(first, last, wait_block, 0)


def _dispatch(ps, pe, ct, meta, pos, gates, h2, n_slots, tn, bm):
    t = h2.shape[0]
    width = D_MODEL + LANES
    return pl.pallas_call(
        _dispatch_kernel,
        out_shape=jax.ShapeDtypeStruct((n_slots, width), F32),
        grid_spec=pltpu.PrefetchScalarGridSpec(
            num_scalar_prefetch=3,
            grid=(t // tn,),
            in_specs=[pl.BlockSpec((1, 1, LANES), lambda i, *_: (i, 0, 0), memory_space=pltpu.SMEM),
                      pl.BlockSpec((TOP_K, tn), lambda i, *_: (0, i)),
                      pl.BlockSpec((TOP_K, tn), lambda i, *_: (0, i)),
                      pl.BlockSpec((tn, D_MODEL), lambda i, *_: (i, 0))],
            out_specs=pl.BlockSpec(memory_space=pl.ANY),
            scratch_shapes=[pltpu.VMEM((2, _stage_rows(tn), width), F32),
                            pltpu.VMEM((MOE_UNIT, width), F32),
                            pltpu.VMEM((bm, width), F32),
                            pltpu.SMEM((2,), I32),
                            pltpu.SemaphoreType.DMA((2,)),
                            pltpu.SemaphoreType.DMA(())]),
        compiler_params=_cparams(("arbitrary",)),
        name="moe_dispatch",
    )(ps, pe, ct, meta, pos, gates, h2)


def _expert_kernel(be_ref, na_ref, x_ref, w1_ref, b1_ref, w2_ref, b2_ref, y_ref, w1s, w2s):
    i = pl.program_id(0)
    e = be_ref[i]
    prev = be_ref[jnp.maximum(i - 1, 0)]

    @pl.when((i == 0) | (e != prev))
    def _():
        rows = 128
        for r0 in range(0, D_MODEL, rows):
            w1s[r0:r0 + rows, :] = w1_ref[r0:r0 + rows, :].astype(BF16)
        for r0 in range(0, D_FF, rows):
            w2s[r0:r0 + rows, :] = w2_ref[r0:r0 + rows, :].astype(BF16)

    @pl.when(i < na_ref[0])
    def _():
        x = x_ref[:, 0:D_MODEL].astype(BF16)
        hmid = jnp.dot(x, w1s[...], preferred_element_type=F32) + b1_ref[...]
        glu = jnp.minimum(hmid[:, :D_FF], SWIGLU_LIMIT)
        lin = jnp.clip(hmid[:, D_FF:], -SWIGLU_LIMIT, SWIGLU_LIMIT)
        act = glu * _sigmoid(SWIGLU_ALPHA * glu) * (lin + 1.0)
        y = jnp.dot(act.astype(BF16), w2s[...], preferred_element_type=F32) + b2_ref[...]
        y_ref[...] = y * x_ref[:, D_MODEL:D_MODEL + 1]

    @pl.when(i >= na_ref[0])
    def _():
        y_ref[...] = jnp.zeros_like(y_ref)


def _experts(be, nact, xg, layer, w1, b1, w2, b2, bm):
    n_slots = xg.shape[0]
    nb = n_slots // bm
    blk = lambda i, be_r, na_r, *_: (jnp.minimum(i, na_r[0] - 1), 0)
    wsel = lambda i, be_r, *_: (layer, be_r[i], 0, 0)
    return pl.pallas_call(
        _expert_kernel,
        out_shape=jax.ShapeDtypeStruct((n_slots, D_MODEL), F32),
        grid_spec=pltpu.PrefetchScalarGridSpec(
            num_scalar_prefetch=2,
            grid=(nb,),
            in_specs=[pl.BlockSpec((bm, xg.shape[1]), blk),
                      pl.BlockSpec((None, None, D_MODEL, 2 * D_FF), wsel),
                      pl.BlockSpec((None, None, 1, 2 * D_FF), wsel),
                      pl.BlockSpec((None, None, D_FF, D_MODEL), wsel),
                      pl.BlockSpec((None, None, 1, D_MODEL), wsel)],
            out_specs=pl.BlockSpec((bm, D_MODEL), lambda i, *_: (i, 0)),
            scratch_shapes=[pltpu.VMEM((D_MODEL, 2 * D_FF), BF16), pltpu.VMEM((D_FF, D_MODEL), BF16)]),
        compiler_params=_cparams(("arbitrary",)),
        name="moe_experts",
    )(be, nact, xg, w1, b1, w2, b2)


def _combine_kernel(n_first, ps_ref, mcur_ref, mnext_ref, pos_ref, x1_ref, yg_ref, o_ref, o2_ref, stage,
                    started, sem):
    i = pl.program_id(0)
    n = pl.num_programs(0)
    slot = i % 2
    rows, tn = stage.shape[1], pos_ref.shape[1]

    @pl.when(i == 0)
    def _():
        stage[...] = jnp.zeros_like(stage)

    def copy(s):
        return lambda src, dst, nrows: pltpu.make_async_copy(yg_ref.at[pl.ds(dst, nrows)],
                                                             stage.at[s, pl.ds(src, nrows)], sem.at[s])

    @pl.when(i == 0)
    def _():
        started[0] = _run_copies(mcur_ref, ps_ref, copy(0))

    @pl.when(i + 1 < n)
    def _():
        started[1 - slot] = _run_copies(mnext_ref, ps_ref, copy(1 - slot))

    _wait_copies(started[slot], copy(slot), rows)

    eye = (lax.broadcasted_iota(I32, (tn, tn), 0) == lax.broadcasted_iota(I32, (tn, tn), 1)).astype(F32)
    cols = _hdot_nt(eye, pos_ref[...].astype(F32)).astype(I32)
    lane = lax.broadcasted_iota(I32, (tn, rows), 1)
    hit = lane == cols[:, 0:1]
    for k in range(1, TOP_K):
        hit = hit | (lane == cols[:, k:k + 1])
    pick = jnp.where(hit, 1.0, 0.0).astype(BF16)
    out = x1_ref[...] + jnp.dot(pick, stage[slot].astype(BF16), preferred_element_type=F32)

    @pl.when(i < n_first)
    def _():
        o_ref[...] = out

    @pl.when(i >= n_first)
    def _():
        o2_ref[...] = out


def _combine(ps, meta, pos, x1, yg, tn, t_first):
    t = x1.shape[0]
    nt = t // tn
    n_first = t_first // tn
    return pl.pallas_call(
        functools.partial(_combine_kernel, n_first),
        out_shape=[jax.ShapeDtypeStruct((t_first, D_MODEL), F32),
                   jax.ShapeDtypeStruct((t - t_first, D_MODEL), F32)],
        grid_spec=pltpu.PrefetchScalarGridSpec(
            num_scalar_prefetch=1,
            grid=(nt,),
            in_specs=[pl.BlockSpec((1, 1, LANES), lambda i, *_: (i, 0, 0), memory_space=pltpu.SMEM),
                      pl.BlockSpec((1, 1, LANES), lambda i, *_: (jnp.minimum(i + 1, nt - 1), 0, 0),
                                   memory_space=pltpu.SMEM),
                      pl.BlockSpec((TOP_K, tn), lambda i, *_: (0, i)),
                      pl.BlockSpec((tn, D_MODEL), lambda i, *_: (i, 0)),
                      pl.BlockSpec(memory_space=pl.ANY)],
            out_specs=[pl.BlockSpec((tn, D_MODEL), lambda i, *_: (jnp.minimum(i, n_first - 1), 0)),
                       pl.BlockSpec((tn, D_MODEL), lambda i, *_: (jnp.maximum(i - n_first, 0), 0))],
            scratch_shapes=[pltpu.VMEM((2, _stage_rows(tn), D_MODEL), F32),
                            pltpu.SMEM((2,), I32),
                            pltpu.SemaphoreType.DMA((2,))]),
        compiler_params=_cparams(("arbitrary",)),
        name="moe_combine",
    )(ps, meta, meta, pos, x1, yg)


def _moe(x1, h2, logits_t, layer, w1, b1, w2, b2, t_first):
    t = x1.shape[0]
    tn = MOE_TOK_TILE
    bm = MOE_BM if t * TOP_K >= N_EXPERTS * MOE_BM else LANES
    n_blocks = (t * TOP_K + (t // tn) * N_EXPERTS * (MOE_UNIT - 1)) // bm + N_EXPERTS + 1
    nbp = -(-n_blocks // LANES) * LANES
    pos, gates, meta, cnt = _route(logits_t, tn)
    be, nact, ps, pe, ct = _layout(cnt, nbp, bm)
    ps, pe, ct = ps[0, :N_EXPERTS], pe[0, :N_EXPERTS], ct[0, :N_EXPERTS]
    xg = _dispatch(ps, pe, ct, meta, pos, gates, h2, n_blocks * bm, tn, bm)
    yg = _experts(be[0, :n_blocks], nact[0, :1], xg, layer, w1, b1, w2, b2, bm)
    return _combine(ps, meta, pos, x1, yg, tn, t_first)


def _pack_w_in(w):
    ab = jnp.pad(w[:, _IN_MAIN:_IN_MAIN + _IN_AB], ((0, 0), (0, LANES - _IN_AB)))
    return jnp.concatenate([w[:, :_IN_MAIN], ab, w[:, _IN_MAIN + _IN_AB:]], axis=1).astype(BF16)


def _lane_row(v):
    return jnp.pad(v.astype(F32), (0, LANES - v.shape[0])).reshape(1, LANES)


def _layer_weights(ln1_gain, w_in, q_norm_swa, k_norm_swa, swa_sinks, conv_w, a_log, dt_bias, gdn_norm,
                   q_norm_mem, w_branch, w_out, ln2_gain, w_router, b_router, w_mlp1, b_mlp1, w_mlp2, b_mlp2, layer):
    depth = w_mlp1.shape[0]
    wr_hi = w_router.T.astype(BF16)
    wr_lo = (w_router.T - wr_hi.astype(F32)).astype(BF16)
    return dict(
        layer=layer, wr_t=jnp.concatenate([wr_hi, wr_lo], axis=0), b1r=b_mlp1.reshape(depth, N_EXPERTS, 1, -1), b2r=b_mlp2.reshape(depth, N_EXPERTS, 1, -1),
        ln1=ln1_gain.reshape(1, -1), wp=_pack_w_in(w_in),
        qn_s=q_norm_swa.reshape(1, -1), kn_s=k_norm_swa.reshape(1, -1), sinks=swa_sinks.reshape(1, -1),
        conv_w=conv_w, alog=_lane_row(a_log), dtb=_lane_row(dt_bias), gnorm=gdn_norm.reshape(1, -1),
        qn_m=q_norm_mem.reshape(1, -1), wb=w_branch.astype(BF16), wo=w_out.astype(BF16),
        ln2=ln2_gain.reshape(1, -1), br=b_router.reshape(-1, 1),
        w1=w_mlp1, w2=w_mlp2)


def _finish(lw, first, second):
    t1, t2 = first[4].shape[0], second[4].shape[0]
    tm1 = MERGE_ROWS if t1 % MERGE_ROWS == 0 else MOE_TOK_TILE
    pad = (-t2) % MOE_TOK_TILE
    second = [jnp.pad(a, ((0, pad), (0, 0))) for a in second]
    wts = (lw["wb"], lw["wo"], lw["ln2"], lw["wr_t"], lw["br"])
    bufs = _merge(*first, *wts, tm1, t_out=t1 + t2 + pad)
    bufs = _merge(*second, *wts, MOE_TOK_TILE, into=bufs)
    y1, y2 = _moe(*bufs, lw["layer"], lw["w1"], lw["b1r"], lw["w2"], lw["b2r"], t1)
    return y1, y2[:t2]


def _prompt_layer(x, mem, lw, mem_norm, w_mem_kv, k_norm_mem):
    b, l, d = x.shape
    x2 = x.reshape(b * l, d)
    tm = 256
    tm_in = 2 * tm if l % (2 * tm) == 0 else tm
    qs, ks, vs, qkvg, z, ab, qm, gate, conv_tail = _inproj(x2, lw["ln1"], lw["wp"], lw["conv_w"], tm_in, seq_len=l)
    mk, mv = _memkv(mem.reshape(b * MEM_LEN, d), mem_norm.reshape(1, -1), w_mem_kv.astype(BF16),
                    k_norm_mem.reshape(1, -1))
    r3 = lambda a: a.reshape(b, l, a.shape[-1])
    o_swa, kwin = _swa_prompt(r3(qs), r3(ks), r3(vs), lw["qn_s"], lw["kn_s"], lw["sinks"])
    o_g, s_fin = _gdn_prompt(r3(qkvg), r3(z), r3(ab), lw["alog"], lw["dtb"], lw["gnorm"])
    o_m = _memattn_prompt(r3(qm), mk.reshape(b, MEM_LEN, BRANCH_W), mv.reshape(b, MEM_LEN, BRANCH_W), lw["qn_m"])
    parts = (o_swa.reshape(b * l, -1), o_g.reshape(b * l, -1), o_m.reshape(b * l, -1), gate, x2)
    new_k = kwin.reshape(b, WINDOW, SWA_KV_HEADS, SWA_HEAD_DIM)
    new_v = r3(vs)[:, l - WINDOW:].reshape(b, WINDOW, SWA_KV_HEADS, SWA_HEAD_DIM)
    new_conv = conv_tail[:, _HALO - (CONV_WIDTH - 1):]
    mk4 = mk.reshape(b, MEM_LEN, MEM_HEADS, MEM_HEAD_DIM)
    mv4 = mv.reshape(b, MEM_LEN, MEM_HEADS, MEM_HEAD_DIM)
    return parts, (new_k, new_v, s_fin, new_conv, mk4, mv4)


def _sample_layer(x, cache_k, cache_v, state, conv_state, mem_k, mem_v, lw):
    b, l, d = x.shape
    x2 = x.reshape(b, d)
    tm = TOK_TILE
    qs, ks, vs, qkvg, z, ab, qm, gate = _inproj(x2, lw["ln1"], lw["wp"], lw["conv_w"], tm)
    o_swa, new_k, new_v = _swa_decode(qs, ks, vs, cache_k, cache_v, lw["qn_s"], lw["kn_s"], lw["sinks"])
    o_g, new_s = _gdn_decode(qkvg, z, ab, state, conv_state, lw["conv_w"], lw["alog"], lw["dtb"], lw["gnorm"])
    o_m = _memattn_decode(qm, mem_k, mem_v, lw["qn_m"])
    new_conv = jnp.concatenate([conv_state[:, 1:], qkvg[:, None, :]], axis=1)
    shp = (b, WINDOW, SWA_KV_HEADS, SWA_HEAD_DIM)
    return (o_swa, o_g, o_m, gate, x2), (new_k.reshape(shp), new_v.reshape(shp), new_s, new_conv)


def _layer(lw, xp, mem, mem_norm, w_mem_kv, k_norm_mem, xs, cache_k, cache_v, state, conv_state, mem_k, mem_v):
    p_parts, p_state = _prompt_layer(xp, mem, lw, mem_norm, w_mem_kv, k_norm_mem)
    s_parts, s_state = _sample_layer(xs, cache_k, cache_v, state, conv_state, mem_k, mem_v, lw)
    yp, ys = _finish(lw, p_parts, s_parts)
    return yp.reshape(xp.shape), ys.reshape(xs.shape), p_state, s_state


def kernel(x_prompt, x_sample, cache_swa_k, cache_swa_v, state_gdn, state_conv, cache_mem_k, cache_mem_v, mem_prompt, ln1_gain, w_in, q_norm_swa, k_norm_swa, swa_sinks, conv_w, a_log, dt_bias, gdn_norm, q_norm_mem, k_norm_mem, mem_norm, w_mem_kv, w_branch, w_out, ln2_gain, w_router, b_router, w_mlp1, b_mlp1, w_mlp2, b_mlp2):
    xp, xs = x_prompt, x_sample
    outs = [[] for _ in range(10)]
    for l in range(ln1_gain.shape[0]):
        lw = _layer_weights(ln1_gain[l], w_in[l], q_norm_swa[l], k_norm_swa[l], swa_sinks[l], conv_w[l], a_log[l],
                            dt_bias[l], gdn_norm[l], q_norm_mem[l], w_branch[l], w_out[l], ln2_gain[l], w_router[l],
                            b_router[l], w_mlp1, b_mlp1, w_mlp2, b_mlp2, l)
        xp, xs, (pk, pv, ps, pc, mk, mv), (sk, sv, ss, sc) = _layer(
            lw, xp, mem_prompt, mem_norm[l], w_mem_kv[l], k_norm_mem[l],
            xs, cache_swa_k[l], cache_swa_v[l], state_gdn[l], state_conv[l], cache_mem_k[l], cache_mem_v[l])
        for acc, v in zip(outs, (pk, pv, sk, sv, ps, ss, pc, sc, mk, mv)):
            acc.append(v)
    return (xp, xs) + tuple(jnp.stack(o) for o in outs)
```
